```python
import math
import numpy as np
import jax
import jax.numpy as jnp
from jax import lax

D_MODEL = 2048
BATCH = 2
SEQ = 8192
DEPTH = 1
DEC_BATCH = 32
DEC_SEQ = 8
PAST_LEN = 16384
PAGE_SIZE = 128

D_MIX = D_MODEL
D_ATT = D_MIX // 2
D_GDN = D_MIX - D_ATT
HEAD_DIM = 128
N_HEADS_A = D_ATT // HEAD_DIM
N_KV_A = N_HEADS_A // 4
GQA = N_HEADS_A // N_KV_A
CMP_BLOCK = 64
N_SELECT = 16
WINDOW = 512
SWA_QBLOCK = 128
SEL_QBLOCK = 64
N_HEADS_B = D_GDN // HEAD_DIM
CONV_W = 4
CONV_DIM = 3 * D_GDN
CHUNK = 64
NORM_EPS = 1e-6
NEG_INF = -1e30
FORCE_SCORE = 1e4
KV_BRANCH = 2 * N_KV_A * HEAD_DIM
SPLITS = (D_ATT, 3 * KV_BRANCH, 3 * N_HEADS_A, D_ATT, CONV_DIM, N_HEADS_B, N_HEADS_B, D_GDN)
D_IN = sum(SPLITS)

kernel_name = "nsa_gdn_parallel_hybrid_step"


def _rmsnorm(x, w):
    xf = x.astype(jnp.float32)
    y = xf * lax.rsqrt(jnp.mean(xf * xf, axis=-1, keepdims=True) + NORM_EPS)
    return (y * w.astype(jnp.float32)).astype(x.dtype)


def _l2norm(x):
    return x * lax.rsqrt(jnp.sum(x * x, axis=-1, keepdims=True) + NORM_EPS)


def _alibi_slopes():
    h = jnp.arange(1, N_HEADS_A + 1, dtype=jnp.float32)
    return jnp.exp2(-8.0 * h / N_HEADS_A).reshape(N_KV_A, GQA)


def _in_proj(x, norm_w, w_in):
    B, T = x.shape[:2]
    h = _rmsnorm(x, norm_w)
    p = jnp.einsum('btd,de->bte', h, w_in)
    offs = [int(o) for o in np.cumsum(SPLITS)[:-1]]
    q_a, kv_a, g_a, z_a, qkv_b, a_b, b_b, z_b = jnp.split(p, offs, axis=-1)
    q_a = q_a.reshape(B, T, N_KV_A, GQA, HEAD_DIM)
    kv_a = kv_a.reshape(B, T, 3, 2, N_KV_A, HEAD_DIM)
    g_a = jax.nn.sigmoid(g_a.astype(jnp.float32)).reshape(B, T, 3, N_KV_A, GQA)
    return q_a, kv_a, g_a, z_a, qkv_b, a_b, b_b, z_b


def _attend(q, k, v, dist, ok, slopes):
    s = jnp.einsum('...qhgd,...shd->...qhgs', q, k).astype(jnp.float32) * (HEAD_DIM ** -0.5)
    s = jnp.where(ok, s - slopes[:, :, None] * dist, NEG_INF)
    p = jnp.where(ok, jax.nn.softmax(s, axis=-1), 0.0)
    o = jnp.einsum('...qhgs,...shd->...qhgd', p.astype(v.dtype), v)
    return o, p


def _compress(rows, pe_cmp, w_cmp):
    B, L = rows.shape[:2]
    blk = rows.reshape(B, L // CMP_BLOCK, CMP_BLOCK, 2, N_KV_A, HEAD_DIM) + pe_cmp[:, :, None, :]
    return jnp.einsum('bnlchd,lcde->bnche', blk, w_cmp)


def _cmp_branch(q, q_pos, kc, vc, slopes):
    nc = kc.shape[1]
    end = (jnp.arange(nc) + 1) * CMP_BLOCK - 1
    d = q_pos[:, None] - end[None, :]
    o, p = _attend(q, kc, vc, d.astype(jnp.float32)[:, None, None, :], (d >= 0)[:, None, None, :], slopes)
    return o, p.sum(axis=3)


def _select(imp, q_pos):
    nc = imp.shape[-1]
    j = jnp.arange(nc)[None, :]
    cur = (q_pos // CMP_BLOCK)[:, None]
    forced = ((j == cur) | (j == 0))[None, :, None, :]
    allowed = (j <= cur)[None, :, None, :]
    score = jnp.where(forced, FORCE_SCORE, jnp.where(allowed, imp, -1.0))
    top, idx = lax.top_k(score, min(N_SELECT, nc))
    return idx, top >= 0.0


def _slc_attend(q, q_pos, blocks, idx, valid, slopes):
    B, Q, H, S = idx.shape
    k = blocks[..., 0, :].reshape(B, Q, H, S * CMP_BLOCK, HEAD_DIM)
    v = blocks[..., 1, :].reshape(B, Q, H, S * CMP_BLOCK, HEAD_DIM)
    kpos = (idx[..., None] * CMP_BLOCK + jnp.arange(CMP_BLOCK)).reshape(B, Q, H, S * CMP_BLOCK)
    d = q_pos[None, :, None, None] - kpos
    ok = jnp.repeat(valid, CMP_BLOCK, axis=-1) & (d >= 0)
    s = jnp.einsum('bqhgd,bqhsd->bqhgs', q, k).astype(jnp.float32) * (HEAD_DIM ** -0.5)
    s = s - slopes[None, None, :, :, None] * d[:, :, :, None, :].astype(jnp.float32)
    s = jnp.where(ok[:, :, :, None, :], s, NEG_INF)
    p = jax.nn.softmax(s, axis=-1)
    return jnp.einsum('bqhgs,bqhsd->bqhgd', p.astype(v.dtype), v)


def _slc_prompt(q, idx, valid, src, slopes):
    B, T = q.shape[:2]
    nq = T // SEL_QBLOCK
    bi = jnp.arange(B)[:, None, None, None]
    hi = jnp.arange(N_KV_A)[None, None, :, None]

    def one_block(args):
        qc, pc, ic, vc = args
        g = src[bi, ic, :, :, hi]
        return _slc_attend(qc, pc, g, ic, vc, slopes)

    split = lambda a: jnp.swapaxes(a.reshape((B, nq, SEL_QBLOCK) + a.shape[2:]), 0, 1)
    pos = jnp.arange(T, dtype=jnp.int32).reshape(nq, SEL_QBLOCK)
    o = lax.map(one_block, (split(q), pos, split(idx), split(valid)))
    return jnp.swapaxes(o, 0, 1).reshape(B, T, N_KV_A, GQA, HEAD_DIM)


def _swa_prompt(q, kv, slopes):
    B, T = q.shape[:2]
    nb = T // SWA_QBLOCK
    span = WINDOW + SWA_QBLOCK
    kv_pad = jnp.pad(kv, ((0, 0), (WINDOW, 0), (0, 0), (0, 0), (0, 0)))
    kidx = jnp.arange(nb)[:, None] * SWA_QBLOCK + jnp.arange(span)[None, :]
    kvb = kv_pad[:, kidx]
    kpos = kidx - WINDOW
    qpos = jnp.arange(T).reshape(nb, SWA_QBLOCK)
    d = qpos[:, :, None] - kpos[:, None, :]
    ok = (d >= 0) & (d < WINDOW) & (kpos >= 0)[:, None, :]
    qb = q.reshape(B, nb, SWA_QBLOCK, N_KV_A, GQA, HEAD_DIM)
    o, _ = _attend(qb, kvb[:, :, :, 0], kvb[:, :, :, 1], d[:, :, None, None, :].astype(jnp.float32),
                   ok[:, :, None, None, :], slopes)
    return o.reshape(B, T, N_KV_A, GQA, HEAD_DIM)


def _short_conv(x, buf, w):
    T = x.shape[1]
    xp = jnp.concatenate([buf.astype(x.dtype), x], axis=1)
    y = xp[:, 0:T] * w[0]
    for i in range(1, CONV_W):
        y = y + xp[:, i:i + T] * w[i]
    return jax.nn.silu(y), xp[:, T:]


def _gated_delta(q, k, v, g, beta, s0):
    B, T, H, D = q.shape
    n = -(-T // CHUNK)
    pad = n * CHUNK - T

    def blk(a):
        a = jnp.pad(a, [(0, 0), (0, pad)] + [(0, 0)] * (a.ndim - 2))
        return jnp.swapaxes(a.reshape((B, n, CHUNK) + a.shape[2:]), 2, 3)

    q, k, v, g, beta = blk(q), blk(k), blk(v), blk(g), blk(beta)
    gc = jnp.cumsum(g, axis=-1)
    i = jnp.arange(CHUNK)
    lower = i[:, None] >= i[None, :]
    strict = i[:, None] > i[None, :]
    gam = jnp.where(lower, jnp.exp(jnp.where(lower, gc[..., :, None] - gc[..., None, :], 0.0)), 0.0)
    kb = k * beta[..., None]
    a = jnp.where(strict, jnp.einsum('bnhid,bnhjd->bnhij', kb, k) * gam, 0.0) + jnp.eye(CHUNK, dtype=q.dtype)
    u = lax.linalg.triangular_solve(a, v * beta[..., None], left_side=True, lower=True, unit_diagonal=True)
    w = lax.linalg.triangular_solve(a, kb * jnp.exp(gc)[..., None], left_side=True, lower=True, unit_diagonal=True)
    qk = jnp.einsum('bnhid,bnhjd->bnhij', q, k) * gam
    qg = q * jnp.exp(gc)[..., None]
    kg = k * jnp.exp(gc[..., -1:] - gc)[..., None]
    decay = jnp.exp(gc[..., -1])

    def step(S, xs):
        u_c, w_c, qk_c, qg_c, kg_c, dec = xs
        v_new = u_c - jnp.einsum('bhcd,bhde->bhce', w_c, S)
        o = jnp.einsum('bhcd,bhde->bhce', qg_c, S) + jnp.einsum('bhij,bhje->bhie', qk_c, v_new)
        S = S * dec[..., None, None] + jnp.einsum('bhcd,bhce->bhde', kg_c, v_new)
        return S, o

    xs = tuple(jnp.moveaxis(t, 1, 0) for t in (u, w, qk, qg, kg, decay))
    s_final, o = lax.scan(step, s0, xs)
    o = jnp.swapaxes(jnp.moveaxis(o, 0, 1), 2, 3).reshape(B, n * CHUNK, H, D)
    return o[:, :T], s_final


def _gdn(qkv_b, a_b, b_b, conv_buf, s0, conv_w, a_log, dt_bias):
    B, T = qkv_b.shape[:2]
    y, conv_new = _short_conv(qkv_b, conv_buf, conv_w)
    y4 = y.astype(jnp.float32).reshape(B, T, 3, N_HEADS_B, HEAD_DIM)
    q = _l2norm(y4[:, :, 0]) * (HEAD_DIM ** -0.5)
    k = _l2norm(y4[:, :, 1])
    v = y4[:, :, 2]
    g = -jnp.exp(a_log.astype(jnp.float32)) * jax.nn.softplus(a_b.astype(jnp.float32) + dt_bias.astype(jnp.float32))
    beta = jax.nn.sigmoid(b_b.astype(jnp.float32))
    o, s_new = _gated_delta(q, k, v, g, beta, s0.astype(jnp.float32))
    return o.astype(qkv_b.dtype), conv_new, s_new.astype(s0.dtype)


def _merge(x, o_cmp, o_slc, o_swa, gates, z_a, o_gdn, z_b, gdn_norm_w, w_out):
    B, T = x.shape[:2]
    g = gates[..., None]
    o_a = g[:, :, 0] * o_cmp + g[:, :, 1] * o_slc + g[:, :, 2] * o_swa
    o_a = o_a.reshape(B, T, D_ATT).astype(x.dtype) * jax.nn.silu(z_a)
    o_b = _rmsnorm(o_gdn, gdn_norm_w).reshape(B, T, D_GDN) * jax.nn.silu(z_b)
    mix = jnp.concatenate([o_a, o_b], axis=-1)
    return x + jnp.einsum('bte,ed->btd', mix, w_out)


def _layer_prompt(x, norm_w, w_in, pe_cmp, w_cmp, conv_w, a_log, dt_bias, gdn_norm_w, w_out):
    B, T = x.shape[:2]
    pos = jnp.arange(T, dtype=jnp.int32)
    slopes = _alibi_slopes()
    q_a, kv_a, gates, z_a, qkv_b, a_b, b_b, z_b = _in_proj(x, norm_w, w_in)
    kv_c = _compress(kv_a[:, :, 0], pe_cmp, w_cmp)
    o_cmp, imp = _cmp_branch(q_a, pos, kv_c[:, :, 0], kv_c[:, :, 1], slopes)
    idx, valid = _select(imp, pos)
    src = kv_a[:, :, 1].reshape(B, T // CMP_BLOCK, CMP_BLOCK, 2, N_KV_A, HEAD_DIM)
    o_slc = _slc_prompt(q_a, idx, valid, src, slopes)
    o_swa = _swa_prompt(q_a, kv_a[:, :, 2], slopes)
    conv0 = jnp.zeros((B, CONV_W - 1, CONV_DIM), x.dtype)
    s0 = jnp.zeros((B, N_HEADS_B, HEAD_DIM, HEAD_DIM), x.dtype)
    o_gdn, conv_new, s_new = _gdn(qkv_b, a_b, b_b, conv0, s0, conv_w, a_log, dt_bias)
    y = _merge(x, o_cmp, o_slc, o_swa, gates, z_a, o_gdn, z_b, gdn_norm_w, w_out)
    return y, kv_a[:, :, 0], kv_a[:, :, 1], kv_a[:, -min(WINDOW, T):, 2], conv_new, s_new


def _layer_sample(x, cache_cmp, cache_slc, cache_swa, conv_buf, s0, page_table,
                  norm_w, w_in, pe_cmp, w_cmp, conv_w, a_log, dt_bias, gdn_norm_w, w_out):
    B, T = x.shape[:2]
    past_len = page_table.shape[1] * PAGE_SIZE
    pos = past_len + jnp.arange(T, dtype=jnp.int32)
    slopes = _alibi_slopes()
    q_a, kv_a, gates, z_a, qkv_b, a_b, b_b, z_b = _in_proj(x, norm_w, w_in)
    n_tail = -(-T // CMP_BLOCK) * CMP_BLOCK
    pad_tail = lambda r: jnp.pad(r, ((0, 0), (0, n_tail - T), (0, 0), (0, 0), (0, 0)))
    past_cmp = cache_cmp[page_table].reshape(B, past_len, 2, N_KV_A, HEAD_DIM)
    rows_cmp = jnp.concatenate([past_cmp, pad_tail(kv_a[:, :, 0])], axis=1)
    kv_c = _compress(rows_cmp, pe_cmp, w_cmp)
    o_cmp, imp = _cmp_branch(q_a, pos, kv_c[:, :, 0], kv_c[:, :, 1], slopes)
    idx, valid = _select(imp, pos)
    ppb = PAGE_SIZE // CMP_BLOCK
    n_pb = past_len // CMP_BLOCK
    pool = cache_slc.reshape(cache_slc.shape[0], ppb, CMP_BLOCK, 2, N_KV_A, HEAD_DIM)
    tail = pad_tail(kv_a[:, :, 1]).reshape(B, n_tail // CMP_BLOCK, CMP_BLOCK, 2, N_KV_A, HEAD_DIM)
    bi = jnp.arange(B)[:, None, None, None]
    hi = jnp.arange(N_KV_A)[None, None, :, None]
    jp = jnp.minimum(idx, n_pb - 1)
    phys = page_table[bi, jp // ppb]
    g_past = pool[phys, jp % ppb, :, :, hi]
    g_new = tail[bi, jnp.clip(idx - n_pb, 0, tail.shape[1] - 1), :, :, hi]
    blocks = jnp.where((idx >= n_pb)[..., None, None, None], g_new, g_past)
    o_slc = _slc_attend(q_a, pos, blocks, idx, valid, slopes)
    w_buf = cache_swa.shape[1]
    kv_win = jnp.concatenate([cache_swa, kv_a[:, :, 2]], axis=1)
    kpos = past_len - w_buf + jnp.arange(w_buf + T)
    d = pos[:, None] - kpos[None, :]
    ok = (d >= 0) & (d < WINDOW)
    o_swa, _ = _attend(q_a, kv_win[:, :, 0], kv_win[:, :, 1], d[:, None, None, :].astype(jnp.float32),
                       ok[:, None, None, :], slopes)
    o_gdn, conv_new, s_new = _gdn(qkv_b, a_b, b_b, conv_buf, s0, conv_w, a_log, dt_bias)
    y = _merge(x, o_cmp, o_slc, o_swa, gates, z_a, o_gdn, z_b, gdn_norm_w, w_out)
    return y, kv_a[:, :, 0], kv_a[:, :, 1], kv_win[:, -w_buf:], conv_new, s_new


def setup_inputs(seed: int = 0) -> dict:
    key = jax.random.key(seed)
    ks = jax.random.split(key, 18)
    f32 = jnp.float32
    nrm = lambda k, s: jax.random.normal(k, s, f32)
    n_pages = PAST_LEN // PAGE_SIZE
    n_phys = (DEC_BATCH * n_pages * 5) // 4
    w_buf = min(WINDOW, PAST_LEN)
    perm = jax.random.permutation(ks[7], n_phys)
    page_table = perm[: DEC_BATCH * n_pages].reshape(DEC_BATCH, n_pages).astype(jnp.int32)
    dt = jnp.exp(jax.random.uniform(ks[13], (DEPTH, N_HEADS_B), f32, math.log(1e-3), math.log(1e-1)))
    return {
        'x_prompt': nrm(ks[0], (BATCH, SEQ, D_MODEL)),
        'x_sample': nrm(ks[1], (DEC_BATCH, DEC_SEQ, D_MODEL)),
        'cache_cmp': nrm(ks[2], (DEPTH, n_phys, PAGE_SIZE, 2, N_KV_A, HEAD_DIM)),
        'cache_slc': nrm(ks[3], (DEPTH, n_phys, PAGE_SIZE, 2, N_KV_A, HEAD_DIM)),
        'cache_swa': nrm(ks[4], (DEPTH, DEC_BATCH, w_buf, 2, N_KV_A, HEAD_DIM)),
        'state_conv': nrm(ks[5], (DEPTH, DEC_BATCH, CONV_W - 1, CONV_DIM)),
        'state_gdn': 0.1 * nrm(ks[6], (DEPTH, DEC_BATCH, N_HEADS_B, HEAD_DIM, HEAD_DIM)),
        'page_table': page_table,
        'norm_w': 1.0 + 0.02 * nrm(ks[8], (DEPTH, D_MODEL)),
        'w_in': nrm(ks[9], (DEPTH, D_MODEL, D_IN)) * D_MODEL ** -0.5,
        'pe_cmp': 0.1 * nrm(ks[10], (DEPTH, CMP_BLOCK, 2, HEAD_DIM)),
        'w_cmp': nrm(ks[11], (DEPTH, CMP_BLOCK, 2, HEAD_DIM, HEAD_DIM)) * (CMP_BLOCK * HEAD_DIM) ** -0.5,
        'conv_w': nrm(ks[12], (DEPTH, CONV_W, CONV_DIM)) * CONV_W ** -0.5,
        'a_log': jnp.log(jax.random.uniform(ks[14], (DEPTH, N_HEADS_B), f32, 1.0, 16.0)),
        'dt_bias': dt + jnp.log(-jnp.expm1(-dt)),
        'gdn_norm_w': 1.0 + 0.02 * nrm(ks[15], (DEPTH, HEAD_DIM)),
        'w_out': nrm(ks[16], (DEPTH, D_MIX, D_MODEL)) * D_MIX ** -0.5,
        'final_norm_w': 1.0 + 0.02 * nrm(ks[17], (D_MODEL,)),
    }


def reference(x_prompt, x_sample, cache_cmp, cache_slc, cache_swa, state_conv, state_gdn, page_table,
              norm_w, w_in, pe_cmp, w_cmp, conv_w, a_log, dt_bias, gdn_norm_w, w_out, final_norm_w):
    hp, hs = x_prompt, x_sample
    p_cmp, p_slc, p_swa, p_conv, p_gdn = [], [], [], [], []
    s_cmp, s_slc, s_swa, s_conv, s_gdn = [], [], [], [], []
    for l in range(DEPTH):
        params = (norm_w[l], w_in[l], pe_cmp[l], w_cmp[l], conv_w[l], a_log[l], dt_bias[l],
                  gdn_norm_w[l], w_out[l])
        hp, kc, kslc, sw, cv, st = _layer_prompt(hp, *params)
        p_cmp.append(kc); p_slc.append(kslc); p_swa.append(sw); p_conv.append(cv); p_gdn.append(st)
        hs, kc, kslc, sw, cv, st = _layer_sample(hs, cache_cmp[l], cache_slc[l], cache_swa[l], state_conv[l],
                                                 state_gdn[l], page_table, *params)
        s_cmp.append(kc); s_slc.append(kslc); s_swa.append(sw); s_conv.append(cv); s_gdn.append(st)
    y_prompt = _rmsnorm(hp, final_norm_w)
    y_sample = _rmsnorm(hs, final_norm_w)
    return (y_prompt, y_sample,
            jnp.stack(p_cmp), jnp.stack(p_slc), jnp.stack(p_swa), jnp.stack(p_conv), jnp.stack(p_gdn),
            jnp.stack(s_cmp), jnp.stack(s_slc), jnp.stack(s_swa), jnp.stack(s_conv), jnp.stack(s_gdn))
```

```python
import functools
import math

import jax
import jax.numpy as jnp
from jax import lax
from jax.experimental import pallas as pl
from jax.experimental.pallas import tpu as pltpu

f32 = jnp.float32
bf16 = jnp.bfloat16

D_MODEL = 2048
HEAD_DIM = 128
N_HEADS_A = 8
N_KV_A = 2
GQA = 4
D_ATT = N_HEADS_A * HEAD_DIM
CMP_BLOCK = 64
N_SELECT = 16
WINDOW = 512
N_HEADS_B = 8
D_GDN = N_HEADS_B * HEAD_DIM
CONV_W = 4
CONV_DIM = 3 * D_GDN
PAGE_SIZE = 128
NORM_EPS = 1e-6
NEG_INF = -1e30
FORCE_SCORE = 1e4
KV_BRANCH = 2 * N_KV_A * HEAD_DIM
N_CH = 2 * N_KV_A
ROWS_PER_PAGE4 = PAGE_SIZE * N_CH
SCALE = HEAD_DIM ** -0.5
G_OFF, A_OFF, B_OFF = 0, 3 * N_HEADS_A, 3 * N_HEADS_A + N_HEADS_B
LANES = 128
VMEM_LIMIT = 56 * 1024 * 1024

IN_TN = 512
N_QZ_TILES = 3 * D_ATT // IN_TN
N_B_TILES = CONV_DIM // IN_TN
N_IN_TILES = N_QZ_TILES + N_B_TILES + 3


def _cparams(sem):
    return pltpu.CompilerParams(dimension_semantics=sem, vmem_limit_bytes=VMEM_LIMIT)


def _dot(a, b):
    return jnp.dot(a, b, preferred_element_type=f32)


def _dot_nt(a, b):
    return lax.dot_general(a, b, (((1,), (1,)), ((), ())), preferred_element_type=f32)


def _dot_tn(a, b):
    return lax.dot_general(a, b, (((0,), (0,)), ((), ())), preferred_element_type=f32)


def _in_proj_kernel(x_ref, nw_ref, w_ref, ws_ref, qz_ref, b_ref, cmp_ref, slc_ref, swa_ref, small_ref, xn_ref):
    n = pl.program_id(1)

    @pl.when(n == 0)
    def _():
        x = x_ref[...]
        ms = jnp.mean(x * x, axis=-1, keepdims=True)
        xn = (x * lax.rsqrt(ms + NORM_EPS) * nw_ref[...]).astype(bf16)
        xn_ref[...] = xn
        small_ref[...] = _dot(xn, ws_ref[...])

    acc = _dot(xn_ref[...], w_ref[...])

    @pl.when(n < N_QZ_TILES)
    def _():
        qz_ref[...] = acc

    @pl.when((n >= N_QZ_TILES) & (n < N_QZ_TILES + N_B_TILES))
    def _():
        b_ref[...] = acc

    @pl.when(n == N_QZ_TILES + N_B_TILES)
    def _():
        cmp_ref[...] = acc

    @pl.when(n == N_QZ_TILES + N_B_TILES + 1)
    def _():
        slc_ref[...] = acc

    @pl.when(n == N_QZ_TILES + N_B_TILES + 2)
    def _():
        swa_ref[...] = acc


def _in_proj(x2d, norm_w, w_main, w_small, tm):
    m = x2d.shape[0]
    S = jax.ShapeDtypeStruct
    nb0 = N_QZ_TILES
    return pl.pallas_call(
        _in_proj_kernel,
        grid=(m // tm, N_IN_TILES),
        in_specs=[
            pl.BlockSpec((tm, D_MODEL), lambda i, n: (i, 0)),
            pl.BlockSpec((1, D_MODEL), lambda i, n: (0, 0)),
            pl.BlockSpec((D_MODEL, IN_TN), lambda i, n: (0, n)),
            pl.BlockSpec((D_MODEL, LANES), lambda i, n: (0, 0)),
        ],
        out_specs=[
            pl.BlockSpec((tm, IN_TN), lambda i, n: (i, jnp.minimum(n, nb0 - 1))),
            pl.BlockSpec((tm, IN_TN), lambda i, n: (i, jnp.clip(n - nb0, 0, N_B_TILES - 1))),
            pl.BlockSpec((tm, KV_BRANCH), lambda i, n: (i, 0)),
            pl.BlockSpec((tm, KV_BRANCH), lambda i, n: (i, 0)),
            pl.BlockSpec((tm, KV_BRANCH), lambda i, n: (i, 0)),
            pl.BlockSpec((tm, LANES), lambda i, n: (i, 0)),
        ],
        out_shape=[S((m, 3 * D_ATT), f32), S((m, CONV_DIM), f32), S((m, KV_BRANCH), f32),
                   S((m, KV_BRANCH), f32), S((m, KV_BRANCH), f32), S((m, LANES), f32)],
        scratch_shapes=[pltpu.VMEM((tm, D_MODEL), bf16)],
        compiler_params=_cparams(("parallel", "arbitrary")),
        name="in_proj",
    )(x2d, norm_w.reshape(1, D_MODEL), w_main, w_small)


def _prep_w_in(w_in):
    o = [0]
    for s in (D_ATT, 3 * KV_BRANCH, 3 * N_HEADS_A, D_ATT, CONV_DIM, N_HEADS_B, N_HEADS_B, D_GDN):
        o.append(o[-1] + s)
    q_a, kv_a, g_a, z_a, qkv_b, a_b, b_b, z_b = (w_in[:, o[i]:o[i + 1]] for i in range(8))
    w_main = jnp.concatenate([q_a, z_a, z_b, qkv_b, kv_a], axis=1).astype(bf16)
    pad = jnp.zeros((w_in.shape[0], LANES - (3 * N_HEADS_A + 2 * N_HEADS_B)), w_in.dtype)
    w_small = jnp.concatenate([g_a, a_b, b_b, pad], axis=1).astype(bf16)
    return w_main, w_small


def _compress_kernel(x0, x1, x2, x3, pe_ref, w_ref, o_ref, *, nbk):
    xs = (x0, x1, x2, x3)

    def body(l, accs):
        new = []
        for c in range(2):
            pe_row = pe_ref[c, pl.ds(l, 1), :]
            rows = [xs[c * N_KV_A + h][pl.ds(l, nbk, stride=CMP_BLOCK), :] + pe_row for h in range(N_KV_A)]
            lhs = jnp.concatenate(rows, axis=0).astype(bf16)
            new.append(accs[c] + _dot(lhs, w_ref[l, c]))
        return tuple(new)

    z = jnp.zeros((N_KV_A * nbk, HEAD_DIM), f32)
    accs = lax.fori_loop(0, CMP_BLOCK, body, (z, z))
    for c in range(2):
        for h in range(N_KV_A):
            o_ref[0, c * N_KV_A + h] = accs[c][h * nbk:(h + 1) * nbk]


def _compress(rows2d, nbatch, pe2d, w_cmp_bf, nbk):
    nblk = rows2d.shape[0] // (nbatch * CMP_BLOCK)
    nj = nblk // nbk
    rows = nbk * CMP_BLOCK
    xspec = [pl.BlockSpec((rows, HEAD_DIM), functools.partial(lambda b, j, ch: (b * nj + j, ch), ch=ch))
             for ch in range(N_CH)]
    return pl.pallas_call(
        functools.partial(_compress_kernel, nbk=nbk),
        grid=(nbatch, nj),
        in_specs=xspec + [
            pl.BlockSpec((2, CMP_BLOCK, HEAD_DIM), lambda b, j: (0, 0, 0)),
            pl.BlockSpec((CMP_BLOCK, 2, HEAD_DIM, HEAD_DIM), lambda b, j: (0, 0, 0, 0)),
        ],
        out_specs=pl.BlockSpec((1, N_CH, nbk, HEAD_DIM), lambda b, j: (b, 0, j, 0)),
        out_shape=jax.ShapeDtypeStruct((nbatch, N_CH, nblk, HEAD_DIM), f32),
        compiler_params=_cparams(("parallel", "parallel")),
        name="compress",
    )(rows2d, rows2d, rows2d, rows2d, pe2d, w_cmp_bf)


def _cmp_kernel(slopes_ref, q_ref, kc_ref, vc_ref, o_ref, sel_ref, sc_ref, *, tq, tqp, ncp, pos0, nsel):
    kvh = pl.program_id(1)
    qt = pl.program_id(2)
    q = q_ref[...]
    if tqp > tq:
        q = jnp.concatenate([q, jnp.zeros((tqp - tq, q.shape[1]), f32)], axis=0)
    kc = kc_ref[0, 0].astype(bf16)
    vc = vc_ref[0, 0].astype(bf16)
    qpos = pos0 + qt * tq + lax.broadcasted_iota(jnp.int32, (1, tqp), 1)
    jblk = lax.broadcasted_iota(jnp.int32, (ncp, 1), 0)
    d = qpos - ((jblk + 1) * CMP_BLOCK - 1)
    ok = d >= 0
    df = d.astype(f32)
    imp = jnp.zeros((ncp, tqp), f32)
    for g in range(GQA):
        qg = q[:, g * HEAD_DIM:(g + 1) * HEAD_DIM].astype(bf16)
        s = _dot_nt(kc, qg) * SCALE
        s = jnp.where(ok, s - slopes_ref[kvh, g] * df, NEG_INF)
        mx = jnp.max(s, axis=0, keepdims=True)
        e = jnp.exp(s - mx)
        p = jnp.where(ok, e / jnp.sum(e, axis=0, keepdims=True), 0.0)
        imp = imp + p
        og = _dot_tn(p.astype(bf16), vc)
        o_ref[:, g * HEAD_DIM:(g + 1) * HEAD_DIM] = og[:tq]
    cur = qpos // CMP_BLOCK
    forced = (jblk == cur) | (jblk == 0)
    score = jnp.where(forced, FORCE_SCORE, jnp.where(jblk <= cur, imp, -1.0))
    for cg in range(tqp // LANES):
        sc = score[:, cg * LANES:(cg + 1) * LANES]
        sc_ref[...] = sc

        def body(i, cnt, sc=sc):
            row = sc_ref[pl.ds(i, 1), :]
            ahead = (row > sc) | ((row == sc) & (jblk > i))
            return cnt + ahead.astype(jnp.int32)

        cnt = lax.fori_loop(0, ncp, body, jnp.zeros((ncp, LANES), jnp.int32))
        sel = ((cnt < nsel) & (sc >= 0.0)).astype(f32)
        lo = cg * LANES
        hi = min(tq, lo + LANES)
        sel_ref[0, 0, lo:hi, :] = sel.T[:hi - lo]


def _cmp_attend(slopes, qz, kcv, nbatch, t, tq, pos0, nsel):
    ncp = kcv.shape[2]
    tqp = max(tq, LANES)
    nqt = t // tq
    gw = GQA * HEAD_DIM
    return pl.pallas_call(
        functools.partial(_cmp_kernel, tq=tq, tqp=tqp, ncp=ncp, pos0=pos0, nsel=nsel),
        grid=(nbatch, N_KV_A, nqt),
        in_specs=[
            pl.BlockSpec(memory_space=pltpu.SMEM),
            pl.BlockSpec((tq, gw), lambda b, h, i: (b * nqt + i, h)),
            pl.BlockSpec((1, 1, ncp, HEAD_DIM), lambda b, h, i: (b, h, 0, 0)),
            pl.BlockSpec((1, 1, ncp, HEAD_DIM), lambda b, h, i: (b, N_KV_A + h, 0, 0)),
        ],
        out_specs=[
            pl.BlockSpec((tq, gw), lambda b, h, i: (b * nqt + i, h)),
            pl.BlockSpec((1, 1, tq, ncp), lambda b, h, i: (b, h, i, 0)),
        ],
        out_shape=[jax.ShapeDtypeStruct((nbatch * t, D_ATT), f32),
                   jax.ShapeDtypeStruct((nbatch, N_KV_A, t, ncp), f32)],
        scratch_shapes=[pltpu.VMEM((ncp, LANES), f32)],
        compiler_params=_cparams(("parallel", "parallel", "parallel")),
        name="cmp_attend",
    )(slopes, qz, kcv, kcv)


def _flash_step(qg, k, v, bias, ok, m_ref, l_ref, acc_ref, rows):
    s = _dot_nt(qg, k) * SCALE - bias
    s = jnp.where(ok, s, NEG_INF)
    m_prev = m_ref[rows, :]
    m_new = jnp.maximum(m_prev, jnp.max(s, axis=-1, keepdims=True))
    p = jnp.where(ok, jnp.exp(s - m_new), 0.0)
    alpha = jnp.exp(m_prev - m_new)
    l_ref[rows, :] = alpha * l_ref[rows, :] + jnp.sum(p, axis=-1, keepdims=True)
    acc_ref[rows, :] = alpha * acc_ref[rows, :] + _dot(p.astype(bf16), v)
    m_ref[rows, :] = m_new


def _flash_init(m_ref, l_ref, acc_ref):
    m_ref[...] = jnp.full(m_ref.shape, NEG_INF, f32)
    l_ref[...] = jnp.zeros(l_ref.shape, f32)
    acc_ref[...] = jnp.zeros(acc_ref.shape, f32)


def _slc_kernel(slopes_ref, q_ref, k_ref, v_ref, sel_ref, e_ref, o_ref, m_ref, l_ref, acc_ref, *, tq, tk):
    kvh = pl.program_id(1)
    qt = pl.program_id(2)
    kt = pl.program_id(3)
    nkt = pl.num_programs(3)

    @pl.when(kt == 0)
    def _():
        _flash_init(m_ref, l_ref, acc_ref)

    @pl.when(kt * tk <= qt * tq + tq - 1)
    def _():
        k = k_ref[...].astype(bf16)
        v = v_ref[...].astype(bf16)
        maskf = _dot(sel_ref[0, 0].astype(bf16), e_ref[...])
        qpos = qt * tq + lax.broadcasted_iota(jnp.int32, (tq, 1), 0)
        kpos = kt * tk + lax.broadcasted_iota(jnp.int32, (1, tk), 1)
        d = qpos - kpos
        ok = (maskf > 0.5) & (d >= 0)
        df = d.astype(f32)
        for g in range(GQA):
            qg = q_ref[:, g * HEAD_DIM:(g + 1) * HEAD_DIM].astype(bf16)
            _flash_step(qg, k, v, slopes_ref[kvh, g] * df, ok, m_ref, l_ref, acc_ref, pl.ds(g * tq, tq))

    @pl.when(kt == nkt - 1)
    def _():
        for g in range(GQA):
            rows = pl.ds(g * tq, tq)
            o_ref[:, g * HEAD_DIM:(g + 1) * HEAD_DIM] = acc_ref[rows, :] / l_ref[rows, :]


def _slc_prompt(slopes, qz, kv_slc, sel, expand, nbatch, t, tq, tk):
    nqt, nkt = t // tq, t // tk
    gw = GQA * HEAD_DIM
    ncp = sel.shape[-1]

    def kidx(b, h, i, j, c):
        return (b * nkt + jnp.minimum(j, (i * tq + tq - 1) // tk), c * N_KV_A + h)

    return pl.pallas_call(
        functools.partial(_slc_kernel, tq=tq, tk=tk),
        grid=(nbatch, N_KV_A, nqt, nkt),
        in_specs=[
            pl.BlockSpec(memory_space=pltpu.SMEM),
            pl.BlockSpec((tq, gw), lambda b, h, i, j: (b * nqt + i, h)),
            pl.BlockSpec((tk, HEAD_DIM), functools.partial(kidx, c=0)),
            pl.BlockSpec((tk, HEAD_DIM), functools.partial(kidx, c=1)),
            pl.BlockSpec((1, 1, tq, ncp), lambda b, h, i, j: (b, h, i, 0)),
            pl.BlockSpec((ncp, tk), lambda b, h, i, j: (0, jnp.minimum(j, (i * tq + tq - 1) // tk))),
        ],
        out_specs=pl.BlockSpec((tq, gw), lambda b, h, i, j: (b * nqt + i, h)),
        out_shape=jax.ShapeDtypeStruct((nbatch * t, D_ATT), f32),
        scratch_shapes=[pltpu.VMEM((GQA * tq, 1), f32), pltpu.VMEM((GQA * tq, 1), f32),
                        pltpu.VMEM((GQA * tq, HEAD_DIM), f32)],
        compiler_params=_cparams(("parallel", "parallel", "parallel", "arbitrary")),
        name="slc_prompt",
    )(slopes, qz, kv_slc, kv_slc, sel, expand)


def _swa_kernel(slopes_ref, q_ref, kp_ref, vp_ref, kc_ref, vc_ref, o_ref, *, tq):
    kvh = pl.program_id(1)
    qt = pl.program_id(2)
    qpos = qt * tq + lax.broadcasted_iota(jnp.int32, (tq, 1), 0)
    kpos = (qt - 1) * tq + lax.broadcasted_iota(jnp.int32, (1, 2 * tq), 1)
    d = qpos - kpos
    ok = (d >= 0) & (d < WINDOW) & (kpos >= 0)
    df = d.astype(f32)
    k = jnp.concatenate([kp_ref[...], kc_ref[...]], axis=0).astype(bf16)
    v = jnp.concatenate([vp_ref[...], vc_ref[...]], axis=0).astype(bf16)
    for g in range(GQA):
        qg = q_ref[:, g * HEAD_DIM:(g + 1) * HEAD_DIM].astype(bf16)
        s = _dot_nt(qg, k) * SCALE - slopes_ref[kvh, g] * df
        s = jnp.where(ok, s, NEG_INF)
        e = jnp.exp(s - jnp.max(s, axis=-1, keepdims=True))
        p = e / jnp.sum(e, axis=-1, keepdims=True)
        o_ref[:, g * HEAD_DIM:(g + 1) * HEAD_DIM] = _dot(p.astype(bf16), v)


def _swa_prompt(slopes, qz, kv_swa, nbatch, t):
    tq = WINDOW
    nqt = t // tq
    gw = GQA * HEAD_DIM
    prev = lambda b, h, i, c: (b * nqt + jnp.maximum(i - 1, 0), c * N_KV_A + h)
    curr = lambda b, h, i, c: (b * nqt + i, c * N_KV_A + h)
    return pl.pallas_call(
        functools.partial(_swa_kernel, tq=tq),
        grid=(nbatch, N_KV_A, nqt),
        in_specs=[
            pl.BlockSpec(memory_space=pltpu.SMEM),
            pl.BlockSpec((tq, gw), lambda b, h, i: (b * nqt + i, h)),
            pl.BlockSpec((tq, HEAD_DIM), functools.partial(prev, c=0)),
            pl.BlockSpec((tq, HEAD_DIM), functools.partial(prev, c=1)),
            pl.BlockSpec((tq, HEAD_DIM), functools.partial(curr, c=0)),
            pl.BlockSpec((tq, HEAD_DIM), functools.partial(curr, c=1)),
        ],
        out_specs=pl.BlockSpec((tq, gw), lambda b, h, i: (b * nqt + i, h)),
        out_shape=jax.ShapeDtypeStruct((nbatch * t, D_ATT), f32),
        compiler_params=_cparams(("parallel", "parallel", "parallel")),
        name="swa_prompt",
    )(slopes, qz, kv_swa, kv_swa, kv_swa, kv_swa)


def _softplus(x):
    return jnp.maximum(x, 0.0) + jnp.log1p(jnp.exp(-jnp.abs(x)))


def _sigmoid(x):
    return 1.0 / (1.0 + jnp.exp(-x))


def _silu(x):
    return x * _sigmoid(x)


def _dot_hi(a, b):
    return jnp.dot(a, b, precision=lax.Precision.HIGHEST, preferred_element_type=f32)


def _g_kk(a, b):
    return _dot_nt(a.astype(bf16), b.astype(bf16))


def _g_inv(a, b):
    return _dot_hi(a, b)


def _g_app(a, b):
    return _dot(a.astype(bf16), b.astype(bf16))


def _g_rec(a, b):
    return _dot(a.astype(bf16), b.astype(bf16))


def _g_rec_tn(a, b):
    return _dot_tn(a.astype(bf16), b.astype(bf16))


def _gdn_kernel(x_ref, small_ref, conv0_ref, s0_ref, cw_ref, alog_ref, dtb_ref, nw_ref,
                o_ref, s_out_ref, xp_ref, st_ref, *, c, tv, halo):
    ci = pl.program_id(1)
    nci = pl.num_programs(1)

    @pl.when(ci == 0)
    def _():
        xp_ref[...] = jnp.zeros(xp_ref.shape, f32)
        xp_ref[halo - (CONV_W - 1):halo, :] = conv0_ref[0]
        st_ref[...] = s0_ref[0]

    xp_ref[halo:halo + tv, :] = x_ref[...]
    y = xp_ref[pl.ds(halo - (CONV_W - 1), c), :] * cw_ref[0:1, :]
    for i in range(1, CONV_W):
        y = y + xp_ref[pl.ds(halo - (CONV_W - 1) + i, c), :] * cw_ref[i:i + 1, :]
    y = _silu(y)
    xp_ref[halo - (CONV_W - 1):halo, :] = xp_ref[halo + tv - (CONV_W - 1):halo + tv, :]

    ri = lax.broadcasted_iota(jnp.int32, (c, 1), 0)
    rowi = lax.broadcasted_iota(jnp.int32, (c, c), 0)
    coli = lax.broadcasted_iota(jnp.int32, (c, c), 1)
    lower = rowi >= coli
    strict = rowi > coli
    tril = lower.astype(f32)
    eye = (rowi == coli).astype(f32)

    sm = small_ref[...]
    if tv < c:
        sm = jnp.concatenate([sm, jnp.zeros((c - tv, LANES), f32)], axis=0)
    g_all = -jnp.exp(alog_ref[...]) * _softplus(sm + dtb_ref[...])
    beta_all = _sigmoid(sm)
    if tv < c:
        live = ri < tv
        g_all = jnp.where(live, g_all, 0.0)
        beta_all = jnp.where(live, beta_all, 0.0)
    gc_all = _dot_hi(tril, g_all)

    for h in range(N_HEADS_B):
        yq = y[:, h * HEAD_DIM:(h + 1) * HEAD_DIM]
        yk = y[:, D_GDN + h * HEAD_DIM:D_GDN + (h + 1) * HEAD_DIM]
        v = y[:, 2 * D_GDN + h * HEAD_DIM:2 * D_GDN + (h + 1) * HEAD_DIM]
        q = yq * lax.rsqrt(jnp.sum(yq * yq, axis=-1, keepdims=True) + NORM_EPS) * SCALE
        k = yk * lax.rsqrt(jnp.sum(yk * yk, axis=-1, keepdims=True) + NORM_EPS)
        if tv < c:
            q = jnp.where(live, q, 0.0)
            k = jnp.where(live, k, 0.0)
            v = jnp.where(live, v, 0.0)
        beta = beta_all[:, B_OFF + h:B_OFF + h + 1]
        gcol = gc_all[:, A_OFF + h:A_OFF + h + 1]
        gcol_b = jnp.broadcast_to(gcol, (c, c))
        grow_b = _dot_hi(jnp.ones((c, c), f32), jnp.where(rowi == coli, gcol_b, 0.0))
        gam = jnp.where(lower, jnp.exp(jnp.where(lower, gcol_b - grow_b, 0.0)), 0.0)
        eg = jnp.exp(gcol)
        g_last = gc_all[c - 1:c, A_OFF + h:A_OFF + h + 1]
        kb = k * beta
        nmat = jnp.where(strict, _g_kk(kb, k) * gam, 0.0)
        inv = eye - nmat
        pw = nmat
        steps = max(1, int(math.ceil(math.log2(c))) - 1)
        for _ in range(steps):
            pw = _g_inv(pw, pw)
            inv = inv + _g_inv(inv, pw)
        u = _g_app(inv, v * beta)
        w = _g_app(inv, kb * eg)
        qk = _g_kk(q, k) * gam
        s_prev = st_ref[h]
        v_new = u - _g_rec(w, s_prev)
        o = _g_rec(q * eg, s_prev) + _g_rec(qk, v_new)
        kg = k * jnp.exp(g_last - gcol)
        st_ref[h] = s_prev * jnp.exp(g_last) + _g_rec_tn(kg, v_new)
        on = o * lax.rsqrt(jnp.mean(o * o, axis=-1, keepdims=True) + NORM_EPS) * nw_ref[...]
        o_ref[:, h * HEAD_DIM:(h + 1) * HEAD_DIM] = on[:tv]

    @pl.when(ci == nci - 1)
    def _():
        s_out_ref[0] = st_ref[...]


def _gdn(qkv_b, small, conv0, s0, conv_w, alog_row, dtb_row, gdn_norm_w, nbatch, t, c):
    tv = min(c, t)
    nci = t // tv
    halo = 8
    return pl.pallas_call(
        functools.partial(_gdn_kernel, c=c, tv=tv, halo=halo),
        grid=(nbatch, nci),
        in_specs=[
            pl.BlockSpec((tv, CONV_DIM), lambda b, i: (b * nci + i, 0)),
            pl.BlockSpec((tv, LANES), lambda b, i: (b * nci + i, 0)),
            pl.BlockSpec((1, CONV_W - 1, CONV_DIM), lambda b, i: (b, 0, 0)),
            pl.BlockSpec((1, N_HEADS_B, HEAD_DIM, HEAD_DIM), lambda b, i: (b, 0, 0, 0)),
            pl.BlockSpec((CONV_W, CONV_DIM), lambda b, i: (0, 0)),
            pl.BlockSpec((1, LANES), lambda b, i: (0, 0)),
            pl.BlockSpec((1, LANES), lambda b, i: (0, 0)),
            pl.BlockSpec((1, HEAD_DIM), lambda b, i: (0, 0)),
        ],
        out_specs=[
            pl.BlockSpec((tv, D_GDN), lambda b, i: (b * nci + i, 0)),
            pl.BlockSpec((1, N_HEADS_B, HEAD_DIM, HEAD_DIM), lambda b, i: (b, 0, 0, 0)),
        ],
        out_shape=[jax.ShapeDtypeStruct((nbatch * t, D_GDN), f32),
                   jax.ShapeDtypeStruct((nbatch, N_HEADS_B, HEAD_DIM, HEAD_DIM), f32)],
        scratch_shapes=[pltpu.VMEM((halo + c, CONV_DIM), f32),
                        pltpu.VMEM((N_HEADS_B, HEAD_DIM, HEAD_DIM), f32)],
        compiler_params=_cparams(("parallel", "arbitrary")),
        name="gdn",
    )(qkv_b, small, conv0, s0, conv_w, alog_row, dtb_row, gdn_norm_w.reshape(1, HEAD_DIM))


def _merge_kernel(x_ref, oc_ref, os_ref, ow_ref, small_ref, za_ref, zb_ref, ob_ref, w_ref, fw_ref, y_ref, mix_ref):
    gates = _sigmoid(small_ref[...])
    for h in range(N_HEADS_A):
        cs = slice(h * HEAD_DIM, (h + 1) * HEAD_DIM)
        o_a = (gates[:, G_OFF + h:G_OFF + h + 1] * oc_ref[:, cs]
               + gates[:, G_OFF + N_HEADS_A + h:G_OFF + N_HEADS_A + h + 1] * os_ref[:, cs]
               + gates[:, G_OFF + 2 * N_HEADS_A + h:G_OFF + 2 * N_HEADS_A + h + 1] * ow_ref[:, cs])
        mix_ref[:, cs] = (o_a * _silu(za_ref[:, cs])).astype(bf16)
    mix_ref[:, D_ATT:] = (ob_ref[...] * _silu(zb_ref[...])).astype(bf16)
    hres = x_ref[...] + _dot(mix_ref[...], w_ref[...])
    ms = jnp.mean(hres * hres, axis=-1, keepdims=True)
    y_ref[...] = hres * lax.rsqrt(ms + NORM_EPS) * fw_ref[...]


def _merge_out(x2d, o_cmp, o_slc, o_swa, small, qz, o_b, w_out_bf, final_norm_w, tm):
    m = x2d.shape[0]
    row = lambda i: (i, 0)
    return pl.pallas_call(
        _merge_kernel,
        grid=(m // tm,),
        in_specs=[
            pl.BlockSpec((tm, D_MODEL), row),
            pl.BlockSpec((tm, D_ATT), row),
            pl.BlockSpec((tm, D_ATT), row),
            pl.BlockSpec((tm, D_ATT), row),
            pl.BlockSpec((tm, LANES), row),
            pl.BlockSpec((tm, D_ATT), lambda i: (i, 1)),
            pl.BlockSpec((tm, D_GDN), lambda i: (i, 2)),
            pl.BlockSpec((tm, D_GDN), row),
            pl.BlockSpec((D_ATT + D_GDN, D_MODEL), lambda i: (0, 0)),
            pl.BlockSpec((1, D_MODEL), lambda i: (0, 0)),
        ],
        out_specs=pl.BlockSpec((tm, D_MODEL), row),
        out_shape=jax.ShapeDtypeStruct((m, D_MODEL), f32),
        scratch_shapes=[pltpu.VMEM((tm, D_ATT + D_GDN), bf16)],
        compiler_params=_cparams(("parallel",)),
        name="merge_out",
    )(x2d, o_cmp, o_slc, o_swa, small, qz, qz, o_b, w_out_bf, final_norm_w.reshape(1, D_MODEL))


def _page_copy(cache_ref, buf_ref, sem_ref, pt_ref, b, j, slot, i, pps):
    page = pt_ref[b, j * pps + i]
    return pltpu.make_async_copy(
        cache_ref.at[pl.ds(page * ROWS_PER_PAGE4, ROWS_PER_PAGE4), :],
        buf_ref.at[slot, pl.ds(i * ROWS_PER_PAGE4, ROWS_PER_PAGE4), :],
        sem_ref.at[slot])


def _paged_fetch(cache_ref, buf_ref, sem_ref, pt_ref, pps, nj_pages):
    b = pl.program_id(0)
    j = pl.program_id(1)
    nb = pl.num_programs(0)
    nj = pl.num_programs(1)
    step = b * nj_pages + jnp.minimum(j, nj_pages - 1)
    slot = step % 2

    def start(bb, jj, sl):
        for i in range(pps):
            _page_copy(cache_ref, buf_ref, sem_ref, pt_ref, bb, jj, sl, i, pps).start()

    @pl.when((b == 0) & (j == 0))
    def _():
        start(0, 0, 0)

    @pl.when(j < nj_pages)
    def _():
        last_j = j == nj_pages - 1
        nb_ = jnp.where(last_j, b + 1, b)
        nj_ = jnp.where(last_j, 0, j + 1)

        @pl.when(nb_ < nb)
        def _():
            start(nb_, nj_, 1 - slot)

        for i in range(pps):
            _page_copy(cache_ref, buf_ref, sem_ref, pt_ref, b, j, slot, i, pps).wait()

    return slot


def _compress_paged_kernel(pt_ref, cache_ref, pe_ref, w_ref, o_ref, buf_ref, sem_ref, *, pps):
    slot = _paged_fetch(cache_ref, buf_ref, sem_ref, pt_ref, pps, pl.num_programs(1))
    nbk = pps * (PAGE_SIZE // CMP_BLOCK)
    stride = CMP_BLOCK * N_CH

    def body(l, accs):
        new = []
        for c in range(2):
            pe_row = pe_ref[c, pl.ds(l, 1), :]
            rows = [buf_ref[slot, pl.ds(l * N_CH + c * N_KV_A + h, nbk, stride=stride), :] + pe_row
                    for h in range(N_KV_A)]
            lhs = jnp.concatenate(rows, axis=0).astype(bf16)
            new.append(accs[c] + _dot(lhs, w_ref[l, c]))
        return tuple(new)

    z = jnp.zeros((N_KV_A * nbk, HEAD_DIM), f32)
    accs = lax.fori_loop(0, CMP_BLOCK, body, (z, z))
    for c in range(2):
        for h in range(N_KV_A):
            o_ref[0, c * N_KV_A + h] = accs[c][h * nbk:(h + 1) * nbk]


def _compress_paged(page_table, cache4, pe2d, w_cmp_bf, pps):
    nbatch, n_pages = page_table.shape
    nj = n_pages // pps
    nbk = pps * (PAGE_SIZE // CMP_BLOCK)
    gs = pltpu.PrefetchScalarGridSpec(
        num_scalar_prefetch=1,
        grid=(nbatch, nj),
        in_specs=[
            pl.BlockSpec(memory_space=pl.ANY),
            pl.BlockSpec((2, CMP_BLOCK, HEAD_DIM), lambda b, j, pt: (0, 0, 0)),
            pl.BlockSpec((CMP_BLOCK, 2, HEAD_DIM, HEAD_DIM), lambda b, j, pt: (0, 0, 0, 0)),
        ],
        out_specs=pl.BlockSpec((1, N_CH, nbk, HEAD_DIM), lambda b, j, pt: (b, 0, j, 0)),
        scratch_shapes=[pltpu.VMEM((2, pps * ROWS_PER_PAGE4, HEAD_DIM), f32), pltpu.SemaphoreType.DMA((2,))],
    )
    return pl.pallas_call(
        functools.partial(_compress_paged_kernel, pps=pps),
        grid_spec=gs,
        out_shape=jax.ShapeDtypeStruct((nbatch, N_CH, nj * nbk, HEAD_DIM), f32),
        compiler_params=_cparams(("arbitrary", "arbitrary")),
        name="compress_paged",
    )(page_table, cache4, pe2d, w_cmp_bf)


def _slc_paged_kernel(pt_ref, slopes_ref, cache_ref, q_ref, tail_ref, sel_ref, e_ref, o_ref,
                      buf_ref, sem_ref, m_ref, l_ref, acc_ref, *, pps, t, sub, pos0):
    j = pl.program_id(1)
    nj_pages = pl.num_programs(1) - 1
    slot = _paged_fetch(cache_ref, buf_ref, sem_ref, pt_ref, pps, nj_pages)
    rows = GQA * t

    @pl.when(j == 0)
    def _():
        _flash_init(m_ref, l_ref, acc_ref)

    tpos = lax.broadcasted_iota(jnp.int32, (t, 1), 0)
    qpos = pos0 + jnp.concatenate([tpos] * GQA, axis=0)

    def attend(kvh, k, v, maskf, kpos0, nk):
        q = q_ref[:, kvh * GQA * HEAD_DIM:(kvh + 1) * GQA * HEAD_DIM]
        qs = jnp.concatenate([q[:, g * HEAD_DIM:(g + 1) * HEAD_DIM] for g in range(GQA)], axis=0).astype(bf16)
        slope = jnp.concatenate([jnp.full((t, 1), slopes_ref[kvh, g], f32) for g in range(GQA)], axis=0)
        kpos = kpos0 + lax.broadcasted_iota(jnp.int32, (1, nk), 1)
        d = qpos - kpos
        ok = (jnp.concatenate([maskf] * GQA, axis=0) > 0.5) & (d >= 0)
        _flash_step(qs, k, v, slope * d.astype(f32), ok, m_ref, l_ref, acc_ref, pl.ds(kvh * rows, rows))

    @pl.when(j < nj_pages)
    def _():
        for kvh in range(N_KV_A):
            selb = sel_ref[0, kvh].astype(bf16)

            def sub_body(si, carry, kvh=kvh, selb=selb):
                base = si * (sub * N_CH)
                k = buf_ref[slot, pl.ds(base + kvh, sub, stride=N_CH), :].astype(bf16)
                v = buf_ref[slot, pl.ds(base + N_KV_A + kvh, sub, stride=N_CH), :].astype(bf16)
                maskf = _dot(selb, e_ref[:, pl.ds(pl.multiple_of(si * sub, sub), sub)])
                attend(kvh, k, v, maskf, (j * pps * PAGE_SIZE) + si * sub, sub)
                return carry

            lax.fori_loop(0, pps * PAGE_SIZE // sub, sub_body, 0)

    @pl.when(j == nj_pages)
    def _():
        for kvh in range(N_KV_A):
            k = tail_ref[0, :, kvh * HEAD_DIM:(kvh + 1) * HEAD_DIM].astype(bf16)
            v = tail_ref[0, :, (N_KV_A + kvh) * HEAD_DIM:(N_KV_A + kvh + 1) * HEAD_DIM].astype(bf16)
            maskf = _dot(sel_ref[0, kvh].astype(bf16), e_ref[:, 0:CMP_BLOCK])
            attend(kvh, k, v, maskf, nj_pages * pps * PAGE_SIZE, CMP_BLOCK)
        for kvh in range(N_KV_A):
            for g in range(GQA):
                r = pl.ds(kvh * rows + g * t, t)
                o_ref[:, (kvh * GQA + g) * HEAD_DIM:(kvh * GQA + g + 1) * HEAD_DIM] = acc_ref[r, :] / l_ref[r, :]


def _slc_paged(page_table, slopes, cache4, qz, tail, sel, expand, t, pps, pos0):
    nbatch, n_pages = page_table.shape
    njp = n_pages // pps
    bps = pps * (PAGE_SIZE // CMP_BLOCK)
    sub = min(512, pps * PAGE_SIZE)
    gs = pltpu.PrefetchScalarGridSpec(
        num_scalar_prefetch=1,
        grid=(nbatch, njp + 1),
        in_specs=[
            pl.BlockSpec(memory_space=pltpu.SMEM),
            pl.BlockSpec(memory_space=pl.ANY),
            pl.BlockSpec((t, D_ATT), lambda b, j, pt: (b, 0)),
            pl.BlockSpec((1, CMP_BLOCK, KV_BRANCH), lambda b, j, pt: (b, 0, 0)),
            pl.BlockSpec((1, N_KV_A, t, bps), lambda b, j, pt: (b, 0, 0, j)),
            pl.BlockSpec((bps, pps * PAGE_SIZE), lambda b, j, pt: (0, 0)),
        ],
        out_specs=pl.BlockSpec((t, D_ATT), lambda b, j, pt: (b, 0)),
        scratch_shapes=[pltpu.VMEM((2, pps * ROWS_PER_PAGE4, HEAD_DIM), f32), pltpu.SemaphoreType.DMA((2,)),
                        pltpu.VMEM((N_KV_A * GQA * t, 1), f32), pltpu.VMEM((N_KV_A * GQA * t, 1), f32),
                        pltpu.VMEM((N_KV_A * GQA * t, HEAD_DIM), f32)],
    )
    return pl.pallas_call(
        functools.partial(_slc_paged_kernel, pps=pps, t=t, sub=sub, pos0=pos0),
        grid_spec=gs,
        out_shape=jax.ShapeDtypeStruct((nbatch * t, D_ATT), f32),
        compiler_params=_cparams(("arbitrary", "arbitrary")),
        name="slc_paged",
    )(page_table, slopes, cache4, qz, tail, sel, expand)


def _swa_sample_kernel(slopes_ref, q_ref, win_ref, tail_ref, o_ref, *, t, wbuf, pos0):
    tpos = lax.broadcasted_iota(jnp.int32, (t, 1), 0)
    qpos = pos0 + jnp.concatenate([tpos] * GQA, axis=0)
    nk = wbuf + CMP_BLOCK
    kpos = pos0 - wbuf + lax.broadcasted_iota(jnp.int32, (1, nk), 1)
    d = qpos - kpos
    ok = (d >= 0) & (d < WINDOW)
    df = d.astype(f32)
    for kvh in range(N_KV_A):
        k = jnp.concatenate([win_ref[pl.ds(kvh, wbuf, stride=N_CH), :],
                             tail_ref[0, :, kvh * HEAD_DIM:(kvh + 1) * HEAD_DIM]], axis=0).astype(bf16)
        v = jnp.concatenate([win_ref[pl.ds(N_KV_A + kvh, wbuf, stride=N_CH), :],
                             tail_ref[0, :, (N_KV_A + kvh) * HEAD_DIM:(N_KV_A + kvh + 1) * HEAD_DIM]],
                            axis=0).astype(bf16)
        q = q_ref[:, kvh * GQA * HEAD_DIM:(kvh + 1) * GQA * HEAD_DIM]
        qs = jnp.concatenate([q[:, g * HEAD_DIM:(g + 1) * HEAD_DIM] for g in range(GQA)], axis=0).astype(bf16)
        slope = jnp.concatenate([jnp.full((t, 1), slopes_ref[kvh, g], f32) for g in range(GQA)], axis=0)
        s = _dot_nt(qs, k) * SCALE - slope * df
        s = jnp.where(ok, s, NEG_INF)
        e = jnp.exp(s - jnp.max(s, axis=-1, keepdims=True))
        p = jnp.where(ok, e / jnp.sum(e, axis=-1, keepdims=True), 0.0)
        o = _dot(p.astype(bf16), v)
        for g in range(GQA):
            o_ref[:, (kvh * GQA + g) * HEAD_DIM:(kvh * GQA + g + 1) * HEAD_DIM] = o[g * t:(g + 1) * t]


def _swa_sample(slopes, qz, win4, tail, nbatch, t, wbuf, pos0):
    return pl.pallas_call(
        functools.partial(_swa_sample_kernel, t=t, wbuf=wbuf, pos0=pos0),
        grid=(nbatch,),
        in_specs=[
            pl.BlockSpec(memory_space=pltpu.SMEM),
            pl.BlockSpec((t, D_ATT), lambda b: (b, 0)),
            pl.BlockSpec((wbuf * N_CH, HEAD_DIM), lambda b: (b, 0)),
            pl.BlockSpec((1, CMP_BLOCK, KV_BRANCH), lambda b: (b, 0, 0)),
        ],
        out_specs=pl.BlockSpec((t, D_ATT), lambda b: (b, 0)),
        out_shape=jax.ShapeDtypeStruct((nbatch * t, D_ATT), f32),
        compiler_params=_cparams(("parallel",)),
        name="swa_sample",
    )(slopes, qz, win4, tail)


def _round_up(a, b):
    return -(-a // b) * b


def _pad_blocks(kcv, ncp):
    return jnp.pad(kcv, ((0, 0), (0, 0), (0, ncp - kcv.shape[2]), (0, 0)))


def _expand_matrix(nblocks, nkeys):
    return (jnp.arange(nkeys)[None, :] // CMP_BLOCK == jnp.arange(nblocks)[:, None]).astype(bf16)


def _kv6(rows2d, nbatch, t):
    return rows2d.reshape(nbatch, t, 2, N_KV_A, HEAD_DIM)


def _layer_prompt(x, prm):
    nbatch, t, _ = x.shape
    m = nbatch * t
    x2d = x.reshape(m, D_MODEL)
    tm = min(512, m)
    qz, qkvb, kv_cmp, kv_slc, kv_swa, small = _in_proj(x2d, prm["norm_w"], prm["w_main"], prm["w_small"], tm)
    nc = t // CMP_BLOCK
    kcv = _compress(kv_cmp, nbatch, prm["pe2d"], prm["w_cmp"], min(64, nc))
    kcv = _pad_blocks(kcv, _round_up(nc, LANES))
    tq = min(256, t)
    o_cmp, sel = _cmp_attend(prm["slopes"], qz, kcv, nbatch, t, tq, 0, min(N_SELECT, nc))
    tk = min(512, t)
    o_slc = _slc_prompt(prm["slopes"], qz, kv_slc, sel, _expand_matrix(sel.shape[-1], t), nbatch, t, tq, tk)
    o_swa = _swa_prompt(prm["slopes"], qz, kv_swa, nbatch, t)
    conv0 = jnp.zeros((nbatch, CONV_W - 1, CONV_DIM), f32)
    s0 = jnp.zeros((nbatch, N_HEADS_B, HEAD_DIM, HEAD_DIM), f32)
    o_b, s_new = _gdn(qkvb, small, conv0, s0, prm["conv_w"], prm["alog_row"], prm["dtb_row"],
                      prm["gdn_norm_w"], nbatch, t, 128)
    y = _merge_out(x2d, o_cmp, o_slc, o_swa, small, qz, o_b, prm["w_out"], prm["final_norm_w"], min(256, m))
    w = min(WINDOW, t)
    conv_new = qkvb.reshape(nbatch, t, CONV_DIM)[:, t - (CONV_W - 1):]
    return (y.reshape(nbatch, t, D_MODEL), _kv6(kv_cmp, nbatch, t), _kv6(kv_slc, nbatch, t),
            _kv6(kv_swa, nbatch, t)[:, t - w:], conv_new, s_new)


def _layer_sample(x, cache_cmp, cache_slc, cache_swa, conv_buf, s0, page_table, prm):
    nbatch, t, _ = x.shape
    m = nbatch * t
    n_pages = page_table.shape[1]
    past_len = n_pages * PAGE_SIZE
    wbuf = cache_swa.shape[1]
    x2d = x.reshape(m, D_MODEL)
    qz, qkvb, kv_cmp, kv_slc, kv_swa, small = _in_proj(x2d, prm["norm_w"], prm["w_main"], prm["w_small"], m)
    pad_tail = lambda r: jnp.pad(r.reshape(nbatch, t, KV_BRANCH), ((0, 0), (0, CMP_BLOCK - t), (0, 0)))
    pps = min(32, n_pages)
    kc_past = _compress_paged(page_table, cache_cmp.reshape(-1, HEAD_DIM), prm["pe2d"], prm["w_cmp"], pps)
    kc_tail = _compress(pad_tail(kv_cmp).reshape(nbatch * CMP_BLOCK, KV_BRANCH), 1, prm["pe2d"], prm["w_cmp"], nbatch)
    kc_tail = jnp.swapaxes(kc_tail[0], 0, 1)[:, :, None, :]
    n_pb = past_len // CMP_BLOCK
    pps_s = min(64, n_pages)
    bps = pps_s * (PAGE_SIZE // CMP_BLOCK)
    ncp = _round_up(n_pb + 1, max(LANES, bps))
    kcv = _pad_blocks(jnp.concatenate([kc_past, kc_tail], axis=2), ncp)
    o_cmp, sel = _cmp_attend(prm["slopes"], qz, kcv, nbatch, t, t, past_len, min(N_SELECT, n_pb + 1))
    o_slc = _slc_paged(page_table, prm["slopes"], cache_slc.reshape(-1, HEAD_DIM), qz, pad_tail(kv_slc), sel,
                       _expand_matrix(bps, pps_s * PAGE_SIZE), t, pps_s, past_len)
    o_swa = _swa_sample(prm["slopes"], qz, cache_swa.reshape(-1, HEAD_DIM), pad_tail(kv_swa), nbatch, t, wbuf, past_len)
    o_b, s_new = _gdn(qkvb, small, conv_buf, s0, prm["conv_w"], prm["alog_row"], prm["dtb_row"],
                      prm["gdn_norm_w"], nbatch, t, 128)
    y = _merge_out(x2d, o_cmp, o_slc, o_swa, small, qz, o_b, prm["w_out"], prm["final_norm_w"], min(256, m))
    kv_win = jnp.concatenate([cache_swa, _kv6(kv_swa, nbatch, t)], axis=1)[:, t:]
    conv_new = jnp.concatenate([conv_buf, qkvb.reshape(nbatch, t, CONV_DIM)], axis=1)[:, t:]
    return (y.reshape(nbatch, t, D_MODEL), _kv6(kv_cmp, nbatch, t), _kv6(kv_slc, nbatch, t), kv_win, conv_new, s_new)


def kernel(x_prompt, x_sample, cache_cmp, cache_slc, cache_swa, state_conv, state_gdn, page_table,
           norm_w, w_in, pe_cmp, w_cmp, conv_w, a_log, dt_bias, gdn_norm_w, w_out, final_norm_w):
    depth = norm_w.shape[0]
    assert depth == 1, "the final norm is fused into the single layer's output projection"
    l = 0
    w_main, w_small = _prep_w_in(w_in[l])
    head = jnp.arange(1, N_HEADS_A + 1, dtype=f32)
    lane_row = lambda vals: jnp.zeros((1, LANES), f32).at[0, A_OFF:A_OFF + N_HEADS_B].set(vals.astype(f32))
    prm = {
        "norm_w": norm_w[l], "w_main": w_main, "w_small": w_small,
        "pe2d": jnp.swapaxes(pe_cmp[l], 0, 1), "w_cmp": w_cmp[l].astype(bf16),
        "conv_w": conv_w[l], "alog_row": lane_row(a_log[l]), "dtb_row": lane_row(dt_bias[l]),
        "gdn_norm_w": gdn_norm_w[l], "w_out": w_out[l].astype(bf16), "final_norm_w": final_norm_w,
        "slopes": jnp.exp2(-8.0 * head / N_HEADS_A).reshape(N_KV_A, GQA),
    }
    yp, p_cmp, p_slc, p_swa, p_conv, p_gdn = _layer_prompt(x_prompt, prm)
    ys, s_cmp, s_slc, s_swa, s_conv, s_gdn = _layer_sample(
        x_sample, cache_cmp[l], cache_slc[l], cache_swa[l], state_conv[l], state_gdn[l], page_table, prm)
    st = lambda a: a[None]
    return (yp, ys, st(p_cmp), st(p_slc), st(p_swa), st(p_conv), st(p_gdn),
            st(s_cmp), st(s_slc), st(s_swa), st(s_conv), st(s_gdn))
```

```python
import functools
import math

import jax
import jax.numpy as jnp
from jax import lax
from jax.experimental import pallas as pl
from jax.experimental.pallas import tpu as pltpu

f32 = jnp.float32
bf16 = jnp.bfloat16

D_MODEL = 2048
HEAD_DIM = 128
N_HEADS_A = 8
N_KV_A = 2
GQA = 4
D_ATT = N_HEADS_A * HEAD_DIM
CMP_BLOCK = 64
N_SELECT = 16
WINDOW = 512
N_HEADS_B = 8
D_GDN = N_HEADS_B * HEAD_DIM
CONV_W = 4
CONV_DIM = 3 * D_GDN
PAGE_SIZE = 128
NORM_EPS = 1e-6
NEG_INF = -1e30
FORCE_SCORE = 1e4
KV_BRANCH = 2 * N_KV_A * HEAD_DIM
N_CH = 2 * N_KV_A
ROWS_PER_PAGE4 = PAGE_SIZE * N_CH
SCALE = HEAD_DIM ** -0.5
G_OFF, A_OFF, B_OFF = 0, 3 * N_HEADS_A, 3 * N_HEADS_A + N_HEADS_B
LANES = 128
VMEM_LIMIT = 56 * 1024 * 1024

IN_TN = 512
N_QZ_TILES = 3 * D_ATT // IN_TN
N_B_TILES = CONV_DIM // IN_TN
N_IN_TILES = N_QZ_TILES + N_B_TILES + 3


def _cparams(sem):
    return pltpu.CompilerParams(dimension_semantics=sem, vmem_limit_bytes=VMEM_LIMIT)


def _dot(a, b):
    return jnp.dot(a, b, preferred_element_type=f32)


def _dot_nt(a, b):
    return lax.dot_general(a, b, (((1,), (1,)), ((), ())), preferred_element_type=f32)


def _dot_tn(a, b):
    return lax.dot_general(a, b, (((0,), (0,)), ((), ())), preferred_element_type=f32)


def _in_proj_kernel(x_ref, nw_ref, w_ref, ws_ref, qz_ref, b_ref, cmp_ref, slc_ref, swa_ref, small_ref, xn_ref):
    n = pl.program_id(1)

    @pl.when(n == 0)
    def _():
        x = x_ref[...]
        ms = jnp.mean(x * x, axis=-1, keepdims=True)
        xn = (x * lax.rsqrt(ms + NORM_EPS) * nw_ref[...]).astype(bf16)
        xn_ref[...] = xn
        small_ref[...] = _dot(xn, ws_ref[...])

    acc = _dot(xn_ref[...], w_ref[...])

    @pl.when(n < N_QZ_TILES)
    def _():
        qz_ref[...] = acc

    @pl.when((n >= N_QZ_TILES) & (n < N_QZ_TILES + N_B_TILES))
    def _():
        b_ref[...] = acc

    @pl.when(n == N_QZ_TILES + N_B_TILES)
    def _():
        cmp_ref[...] = acc

    @pl.when(n == N_QZ_TILES + N_B_TILES + 1)
    def _():
        slc_ref[...] = acc

    @pl.when(n == N_QZ_TILES + N_B_TILES + 2)
    def _():
        swa_ref[...] = acc


def _in_proj(x2d, norm_w, w_main, w_small, tm):
    m = x2d.shape[0]
    S = jax.ShapeDtypeStruct
    nb0 = N_QZ_TILES
    return pl.pallas_call(
        _in_proj_kernel,
        grid=(m // tm, N_IN_TILES),
        in_specs=[
            pl.BlockSpec((tm, D_MODEL), lambda i, n: (i, 0)),
            pl.BlockSpec((1, D_MODEL), lambda i, n: (0, 0)),
            pl.BlockSpec((D_MODEL, IN_TN), lambda i, n: (0, n)),
            pl.BlockSpec((D_MODEL, LANES), lambda i, n: (0, 0)),
        ],
        out_specs=[
            pl.BlockSpec((tm, IN_TN), lambda i, n: (i, jnp.minimum(n, nb0 - 1))),
            pl.BlockSpec((tm, IN_TN), lambda i, n: (i, jnp.clip(n - nb0, 0, N_B_TILES - 1))),
            pl.BlockSpec((tm, KV_BRANCH), lambda i, n: (i, 0)),
            pl.BlockSpec((tm, KV_BRANCH), lambda i, n: (i, 0)),
            pl.BlockSpec((tm, KV_BRANCH), lambda i, n: (i, 0)),
            pl.BlockSpec((tm, LANES), lambda i, n: (i, 0)),
        ],
        out_shape=[S((m, 3 * D_ATT), f32), S((m, CONV_DIM), f32), S((m, KV_BRANCH), f32),
                   S((m, KV_BRANCH), f32), S((m, KV_BRANCH), f32), S((m, LANES), f32)],
        scratch_shapes=[pltpu.VMEM((tm, D_MODEL), bf16)],
        compiler_params=_cparams(("parallel", "arbitrary")),
        name="in_proj",
    )(x2d, norm_w.reshape(1, D_MODEL), w_main, w_small)


def _prep_w_in(w_in):
    o = [0]
    for s in (D_ATT, 3 * KV_BRANCH, 3 * N_HEADS_A, D_ATT, CONV_DIM, N_HEADS_B, N_HEADS_B, D_GDN):
        o.append(o[-1] + s)
    q_a, kv_a, g_a, z_a, qkv_b, a_b, b_b, z_b = (w_in[:, o[i]:o[i + 1]] for i in range(8))
    w_main = jnp.concatenate([q_a, z_a, z_b, qkv_b, kv_a], axis=1).astype(bf16)
    pad = jnp.zeros((w_in.shape[0], LANES - (3 * N_HEADS_A + 2 * N_HEADS_B)), w_in.dtype)
    w_small = jnp.concatenate([g_a, a_b, b_b, pad], axis=1).astype(bf16)
    return w_main, w_small


CMP_UNROLL = 4


def _compress_accumulate(load_rows, pe_ref, w_ref, nbk):
    def body(i, accs):
        new = []
        for c in range(2):
            halves = []
            for dl in range(2):
                l = 2 * i + dl
                pe_row = pe_ref[c, pl.ds(l, 1), :]
                halves.append(jnp.concatenate(
                    [load_rows(l, c * N_KV_A + h) + pe_row for h in range(N_KV_A)], axis=0))
            lhs = jnp.concatenate(halves, axis=1).astype(bf16)
            new.append(accs[c] + _dot(lhs, w_ref[c, i]))
        return tuple(new)

    z = jnp.zeros((N_KV_A * nbk, HEAD_DIM), f32)
    return lax.fori_loop(0, CMP_BLOCK // 2, body, (z, z), unroll=CMP_UNROLL)


def _compress_kernel(x0, x1, x2, x3, pe_ref, w_ref, o_ref, *, nbk):
    xs = (x0, x1, x2, x3)
    accs = _compress_accumulate(lambda l, ch: xs[ch][pl.ds(l, nbk, stride=CMP_BLOCK), :], pe_ref, w_ref, nbk)
    for c in range(2):
        for h in range(N_KV_A):
            o_ref[0, c * N_KV_A + h] = accs[c][h * nbk:(h + 1) * nbk]


def _compress(rows2d, nbatch, pe2d, w_cmp_bf, nbk):
    nblk = rows2d.shape[0] // (nbatch * CMP_BLOCK)
    nj = nblk // nbk
    rows = nbk * CMP_BLOCK
    xspec = [pl.BlockSpec((rows, HEAD_DIM), functools.partial(lambda b, j, ch: (b * nj + j, ch), ch=ch))
             for ch in range(N_CH)]
    return pl.pallas_call(
        functools.partial(_compress_kernel, nbk=nbk),
        grid=(nbatch, nj),
        in_specs=xspec + [
            pl.BlockSpec((2, CMP_BLOCK, HEAD_DIM), lambda b, j: (0, 0, 0)),
            pl.BlockSpec((2, CMP_BLOCK // 2, 2 * HEAD_DIM, HEAD_DIM), lambda b, j: (0, 0, 0, 0)),
        ],
        out_specs=pl.BlockSpec((1, N_CH, nbk, HEAD_DIM), lambda b, j: (b, 0, j, 0)),
        out_shape=jax.ShapeDtypeStruct((nbatch, N_CH, nblk, HEAD_DIM), f32),
        compiler_params=_cparams(("parallel", "parallel")),
        name="compress",
    )(rows2d, rows2d, rows2d, rows2d, pe2d, w_cmp_bf)


def _cmp_kernel(slopes_ref, q_ref, kc_ref, vc_ref, o_ref, sel_ref, *, tq, tqp, ncp, pos0, nsel):
    kvh = pl.program_id(1)
    qt = pl.program_id(2)
    q = q_ref[...]
    if tqp > tq:
        q = jnp.concatenate([q, jnp.zeros((tqp - tq, q.shape[1]), f32)], axis=0)
    kc = kc_ref[0, 0].astype(bf16)
    vc = vc_ref[0, 0].astype(bf16)
    qpos = pos0 + qt * tq + lax.broadcasted_iota(jnp.int32, (1, tqp), 1)
    jblk = lax.broadcasted_iota(jnp.int32, (ncp, 1), 0)
    d = qpos - ((jblk + 1) * CMP_BLOCK - 1)
    ok = d >= 0
    df = d.astype(f32)
    imp = jnp.zeros((ncp, tqp), f32)
    for g in range(GQA):
        qg = q[:, g * HEAD_DIM:(g + 1) * HEAD_DIM].astype(bf16)
        s = _dot_nt(kc, qg) * SCALE
        s = jnp.where(ok, s - slopes_ref[kvh, g] * df, NEG_INF)
        mx = jnp.max(s, axis=0, keepdims=True)
        e = jnp.exp(s - mx)
        p = jnp.where(ok, e / jnp.sum(e, axis=0, keepdims=True), 0.0)
        imp = imp + p
        og = _dot_tn(p.astype(bf16), vc)
        o_ref[:, g * HEAD_DIM:(g + 1) * HEAD_DIM] = og[:tq]
    cur = qpos // CMP_BLOCK
    forced = (jblk == cur) | (jblk == 0)
    score = jnp.where(forced, FORCE_SCORE, jnp.where(jblk <= cur, imp, -1.0))
    for cg in range(tqp // LANES):
        sc = score[:, cg * LANES:(cg + 1) * LANES]
        work = sc
        picked = jnp.zeros((ncp, LANES), f32)
        for _ in range(nsel):
            mx = jnp.max(work, axis=0, keepdims=True)
            first = jnp.min(jnp.where(work == mx, jblk, ncp), axis=0, keepdims=True)
            hit = jblk == first
            picked = jnp.where(hit, 1.0, picked)
            work = jnp.where(hit, -2.0, work)
        sel = jnp.where(sc >= 0.0, picked, 0.0)
        lo = cg * LANES
        hi = min(tq, lo + LANES)
        sel_ref[0, 0, lo:hi, :] = sel.T[:hi - lo]


def _cmp_attend(slopes, qz, kcv, nbatch, t, tq, pos0, nsel):
    ncp = kcv.shape[2]
    tqp = max(tq, LANES)
    nqt = t // tq
    gw = GQA * HEAD_DIM
    return pl.pallas_call(
        functools.partial(_cmp_kernel, tq=tq, tqp=tqp, ncp=ncp, pos0=pos0, nsel=nsel),
        grid=(nbatch, N_KV_A, nqt),
        in_specs=[
            pl.BlockSpec(memory_space=pltpu.SMEM),
            pl.BlockSpec((tq, gw), lambda b, h, i: (b * nqt + i, h)),
            pl.BlockSpec((1, 1, ncp, HEAD_DIM), lambda b, h, i: (b, h, 0, 0)),
            pl.BlockSpec((1, 1, ncp, HEAD_DIM), lambda b, h, i: (b, N_KV_A + h, 0, 0)),
        ],
        out_specs=[
            pl.BlockSpec((tq, gw), lambda b, h, i: (b * nqt + i, h)),
            pl.BlockSpec((1, 1, tq, ncp), lambda b, h, i: (b, h, i, 0)),
        ],
        out_shape=[jax.ShapeDtypeStruct((nbatch * t, D_ATT), f32),
                   jax.ShapeDtypeStruct((nbatch, N_KV_A, t, ncp), f32)],
        compiler_params=_cparams(("parallel", "parallel", "parallel")),
        name="cmp_attend",
    )(slopes, qz, kcv, kcv)


def _lane_repeat(x, n):
    return x if n == 1 else jnp.concatenate([x] * n, axis=1)


def _flash_init(m_ref, l_ref, acc_ref):
    m_ref[...] = jnp.full(m_ref.shape, NEG_INF, f32)
    l_ref[...] = jnp.zeros(l_ref.shape, f32)
    acc_ref[...] = jnp.zeros(acc_ref.shape, f32)


LOG2E = 1.4426950408889634
MASK_DIST = 1e33
SLC_ROWS = 128


def _slc_kernel(slopes_ref, q_ref, k_ref, v_ref, sel_ref, e_ref, o_ref,
                qb_ref, dm_ref, s_ref, mn_ref, m_ref, l_ref, acc_ref, *, tq, tk):
    kvh = pl.program_id(1)
    qt = pl.program_id(2)
    _flash_init(m_ref, l_ref, acc_ref)
    for g in range(GQA):
        qb_ref[g * tq:(g + 1) * tq, :] = q_ref[:, g * HEAD_DIM:(g + 1) * HEAD_DIM].astype(bf16)
    selb = sel_ref[0, 0].astype(bf16)
    qpos = qt * tq + lax.broadcasted_iota(jnp.int32, (tq, 1), 0)
    kofs = lax.broadcasted_iota(jnp.int32, (1, tk), 1)
    units = [(g, r) for g in range(GQA) for r in range(tq // SLC_ROWS)]
    nrep = tk // LANES

    def kt_body(kt, carry):
        ks = pl.multiple_of(kt * tk, tk)
        k = k_ref[pl.ds(ks, tk), :].astype(bf16)
        v = v_ref[pl.ds(ks, tk), :].astype(bf16)
        maskf = _dot(selb, e_ref[:, pl.ds(ks, tk)])
        d = qpos - (ks + kofs)
        dm_ref[...] = jnp.where((maskf > 0.5) & (d >= 0), d.astype(f32), MASK_DIST)
        for g, r in units:
            rows = pl.ds(g * tq + r * SLC_ROWS, SLC_ROWS)
            t2 = (_dot_nt(qb_ref[rows, :], k) * (SCALE * LOG2E)
                  - (slopes_ref[kvh, g] * LOG2E) * dm_ref[r * SLC_ROWS:(r + 1) * SLC_ROWS, :])
            s_ref[rows, :] = t2
            mx = jnp.max(t2, axis=-1, keepdims=True)
            mn_ref[rows, :] = jnp.maximum(m_ref[rows, :], jnp.broadcast_to(mx, (SLC_ROWS, LANES)))
        for g, r in units:
            rows = pl.ds(g * tq + r * SLC_ROWS, SLC_ROWS)
            m_new = mn_ref[rows, :]
            p = jnp.exp2(s_ref[rows, :] - _lane_repeat(m_new, nrep))
            alpha = jnp.exp2(m_ref[rows, :] - m_new)
            psum = p[:, 0:LANES]
            for c in range(1, nrep):
                psum = psum + p[:, c * LANES:(c + 1) * LANES]
            l_ref[rows, :] = alpha * l_ref[rows, :] + psum
            acc_ref[rows, :] = alpha * acc_ref[rows, :] + _dot(p.astype(bf16), v)
            m_ref[rows, :] = m_new
        return carry

    lax.fori_loop(0, (qt * tq + tq - 1) // tk + 1, kt_body, 0)
    for g in range(GQA):
        rows = pl.ds(g * tq, tq)
        o_ref[:, g * HEAD_DIM:(g + 1) * HEAD_DIM] = (
            acc_ref[rows, :] / jnp.sum(l_ref[rows, :], axis=-1, keepdims=True))


def _slc_prompt(slopes, qz, kv_slc, sel, expand, nbatch, t, tq, tk):
    nqt = t // tq
    gw = GQA * HEAD_DIM
    ncp = sel.shape[-1]
    return pl.pallas_call(
        functools.partial(_slc_kernel, tq=tq, tk=tk),
        grid=(nbatch, N_KV_A, nqt),
        in_specs=[
            pl.BlockSpec(memory_space=pltpu.SMEM),
            pl.BlockSpec((tq, gw), lambda b, h, i: (b * nqt + i, h)),
            pl.BlockSpec((t, HEAD_DIM), lambda b, h, i: (b, h)),
            pl.BlockSpec((t, HEAD_DIM), lambda b, h, i: (b, N_KV_A + h)),
            pl.BlockSpec((1, 1, tq, ncp), lambda b, h, i: (b, h, i, 0)),
            pl.BlockSpec((ncp, t), lambda b, h, i: (0, 0)),
        ],
        out_specs=pl.BlockSpec((tq, gw), lambda b, h, i: (b * nqt + i, h)),
        out_shape=jax.ShapeDtypeStruct((nbatch * t, D_ATT), f32),
        scratch_shapes=[pltpu.VMEM((GQA * tq, HEAD_DIM), bf16), pltpu.VMEM((tq, tk), f32),
                        pltpu.VMEM((GQA * tq, tk), f32), pltpu.VMEM((GQA * tq, LANES), f32),
                        pltpu.VMEM((GQA * tq, LANES), f32), pltpu.VMEM((GQA * tq, LANES), f32),
                        pltpu.VMEM((GQA * tq, HEAD_DIM), f32)],
        compiler_params=_cparams(("parallel", "parallel", "parallel")),
        name="slc_prompt",
    )(slopes, qz, kv_slc, kv_slc, sel, expand)


def _swa_kernel(slopes_ref, q_ref, kp_ref, vp_ref, kc_ref, vc_ref, o_ref, *, tq):
    kvh = pl.program_id(1)
    qt = pl.program_id(2)
    qpos = qt * tq + lax.broadcasted_iota(jnp.int32, (tq, 1), 0)
    kpos = (qt - 1) * tq + lax.broadcasted_iota(jnp.int32, (1, 2 * tq), 1)
    d = qpos - kpos
    ok = (d >= 0) & (d < WINDOW) & (kpos >= 0)
    df = d.astype(f32)
    k = jnp.concatenate([kp_ref[...], kc_ref[...]], axis=0).astype(bf16)
    v = jnp.concatenate([vp_ref[...], vc_ref[...]], axis=0).astype(bf16)
    for g in range(GQA):
        qg = q_ref[:, g * HEAD_DIM:(g + 1) * HEAD_DIM].astype(bf16)
        s = _dot_nt(qg, k) * SCALE - slopes_ref[kvh, g] * df
        s = jnp.where(ok, s, NEG_INF)
        e = jnp.exp(s - jnp.max(s, axis=-1, keepdims=True))
        p = e / jnp.sum(e, axis=-1, keepdims=True)
        o_ref[:, g * HEAD_DIM:(g + 1) * HEAD_DIM] = _dot(p.astype(bf16), v)


def _swa_prompt(slopes, qz, kv_swa, nbatch, t):
    tq = WINDOW
    nqt = t // tq
    gw = GQA * HEAD_DIM
    prev = lambda b, h, i, c: (b * nqt + jnp.maximum(i - 1, 0), c * N_KV_A + h)
    curr = lambda b, h, i, c: (b * nqt + i, c * N_KV_A + h)
    return pl.pallas_call(
        functools.partial(_swa_kernel, tq=tq),
        grid=(nbatch, N_KV_A, nqt),
        in_specs=[
            pl.BlockSpec(memory_space=pltpu.SMEM),
            pl.BlockSpec((tq, gw), lambda b, h, i: (b * nqt + i, h)),
            pl.BlockSpec((tq, HEAD_DIM), functools.partial(prev, c=0)),
            pl.BlockSpec((tq, HEAD_DIM), functools.partial(prev, c=1)),
            pl.BlockSpec((tq, HEAD_DIM), functools.partial(curr, c=0)),
            pl.BlockSpec((tq, HEAD_DIM), functools.partial(curr, c=1)),
        ],
        out_specs=pl.BlockSpec((tq, gw), lambda b, h, i: (b * nqt + i, h)),
        out_shape=jax.ShapeDtypeStruct((nbatch * t, D_ATT), f32),
        compiler_params=_cparams(("parallel", "parallel", "parallel")),
        name="swa_prompt",
    )(slopes, qz, kv_swa, kv_swa, kv_swa, kv_swa)


def _softplus(x):
    return jnp.maximum(x, 0.0) + jnp.log1p(jnp.exp(-jnp.abs(x)))


def _sigmoid(x):
    return 1.0 / (1.0 + jnp.exp(-x))


def _silu(x):
    return x * _sigmoid(x)


def _dot_hi(a, b):
    return jnp.dot(a, b, precision=lax.Precision.HIGHEST, preferred_element_type=f32)


INV_BASE = 16


def _bdot(a, b):
    return _dot(a.astype(bf16), b.astype(bf16))


def _gdn_kernel(x_ref, small_ref, conv0_ref, s0_ref, cw_ref, alog_ref, dtb_ref, nw_ref,
                o_ref, s_out_ref, xp_ref, st_ref, gam_ref, n_ref, t_ref, pw_ref, uw_ref,
                k_ref, kb_ref, q_ref, qg_ref, kg_ref, qk_ref, vn_ref, rhs_ref, *, c, tv, halo):
    ci = pl.program_id(1)
    nci = pl.num_programs(1)

    @pl.when(ci == 0)
    def _():
        xp_ref[...] = jnp.zeros(xp_ref.shape, f32)
        xp_ref[halo - (CONV_W - 1):halo, :] = conv0_ref[0]
        st_ref[...] = s0_ref[0]

    xp_ref[halo:halo + tv, :] = x_ref[...]
    y = xp_ref[pl.ds(halo - (CONV_W - 1), c), :] * cw_ref[0:1, :]
    for i in range(1, CONV_W):
        y = y + xp_ref[pl.ds(halo - (CONV_W - 1) + i, c), :] * cw_ref[i:i + 1, :]
    y = _silu(y)
    xp_ref[halo - (CONV_W - 1):halo, :] = xp_ref[halo + tv - (CONV_W - 1):halo + tv, :]

    ri = lax.broadcasted_iota(jnp.int32, (c, 1), 0)
    rowi = lax.broadcasted_iota(jnp.int32, (c, c), 0)
    coli = lax.broadcasted_iota(jnp.int32, (c, c), 1)
    lower = rowi >= coli
    strict = rowi > coli
    tril = lower.astype(f32)
    eye = (rowi == coli).astype(f32)
    same_blk = {}
    s = INV_BASE
    while s <= c:
        same_blk[s] = (rowi // s) == (coli // s)
        s *= 2

    sm = small_ref[...]
    if tv < c:
        sm = jnp.concatenate([sm, jnp.zeros((c - tv, LANES), f32)], axis=0)
    g_all = -jnp.exp(alog_ref[...]) * _softplus(sm + dtb_ref[...])
    beta_all = _sigmoid(sm)
    if tv < c:
        live = ri < tv
        g_all = jnp.where(live, g_all, 0.0)
        beta_all = jnp.where(live, beta_all, 0.0)
    gc_all = _dot_hi(tril, g_all)
    gc_rows = gc_all.T
    heads = range(N_HEADS_B)
    decay = []

    for h in heads:
        yq = y[:, h * HEAD_DIM:(h + 1) * HEAD_DIM]
        yk = y[:, D_GDN + h * HEAD_DIM:D_GDN + (h + 1) * HEAD_DIM]
        v = y[:, 2 * D_GDN + h * HEAD_DIM:2 * D_GDN + (h + 1) * HEAD_DIM]
        q = yq * lax.rsqrt(jnp.sum(yq * yq, axis=-1, keepdims=True) + NORM_EPS) * SCALE
        k = yk * lax.rsqrt(jnp.sum(yk * yk, axis=-1, keepdims=True) + NORM_EPS)
        if tv < c:
            q = jnp.where(live, q, 0.0)
            k = jnp.where(live, k, 0.0)
            v = jnp.where(live, v, 0.0)
        beta = beta_all[:, B_OFF + h:B_OFF + h + 1]
        gcol = gc_all[:, A_OFF + h:A_OFF + h + 1]
        grow = gc_rows[A_OFF + h:A_OFF + h + 1, :]
        gam_ref[h] = jnp.where(lower, jnp.exp(jnp.where(lower, gcol - grow, 0.0)), 0.0)
        eg = jnp.exp(gcol)
        g_last = gc_all[c - 1:c, A_OFF + h:A_OFF + h + 1]
        decay.append(jnp.exp(g_last))
        kb = k * beta
        k_ref[h] = k.astype(bf16)
        kb_ref[h] = kb.astype(bf16)
        q_ref[h] = q.astype(bf16)
        qg_ref[h] = (q * eg).astype(bf16)
        kg_ref[h] = (k * jnp.exp(g_last - gcol)).astype(bf16)
        rhs_ref[h, :, 0:HEAD_DIM] = (v * beta).astype(bf16)
        rhs_ref[h, :, HEAD_DIM:2 * HEAD_DIM] = (kb * eg).astype(bf16)

    for h in heads:
        gam = gam_ref[h]
        nmat = jnp.where(strict, _dot_nt(kb_ref[h], k_ref[h]) * gam, 0.0)
        n_ref[h] = nmat
        nd = jnp.where(same_blk[INV_BASE], nmat, 0.0)
        t_ref[h] = eye - nd
        pw_ref[h] = nd.astype(bf16)
        qk_ref[h] = (_dot_nt(q_ref[h], k_ref[h]) * gam).astype(bf16)

    for _ in range(int(math.log2(INV_BASE)) - 1):
        for h in heads:
            pw = pw_ref[h]
            pw_ref[h] = _dot(pw, pw).astype(bf16)
        for h in heads:
            t = t_ref[h]
            t_ref[h] = t + _dot(t.astype(bf16), pw_ref[h])

    s = INV_BASE
    while s < c:
        pair = same_blk[2 * s] & jnp.logical_not(same_blk[s])
        for h in heads:
            off = jnp.where(pair, n_ref[h], 0.0)
            uw_ref[h, :, 0:HEAD_DIM] = _bdot(t_ref[h], off)
        for h in heads:
            t = t_ref[h]
            t_ref[h] = t - _bdot(uw_ref[h, :, 0:HEAD_DIM], t)
        s *= 2

    for h in heads:
        uw_ref[h] = _dot(t_ref[h].astype(bf16), rhs_ref[h])

    for h in heads:
        sb = st_ref[h].astype(bf16)
        v_new = uw_ref[h, :, 0:HEAD_DIM] - _dot(uw_ref[h, :, HEAD_DIM:2 * HEAD_DIM].astype(bf16), sb)
        vn_ref[h] = v_new.astype(bf16)
        uw_ref[h, :, 0:HEAD_DIM] = _dot(qg_ref[h], sb)
    for h in heads:
        vn = vn_ref[h]
        o = uw_ref[h, :, 0:HEAD_DIM] + _dot(qk_ref[h], vn)
        st_ref[h] = st_ref[h] * decay[h] + _dot_tn(kg_ref[h], vn)
        on = o * lax.rsqrt(jnp.mean(o * o, axis=-1, keepdims=True) + NORM_EPS) * nw_ref[...]
        o_ref[:, h * HEAD_DIM:(h + 1) * HEAD_DIM] = on[:tv]

    @pl.when(ci == nci - 1)
    def _():
        s_out_ref[0] = st_ref[...]


def _gdn(qkv_b, small, conv0, s0, conv_w, alog_row, dtb_row, gdn_norm_w, nbatch, t, c):
    tv = min(c, t)
    nci = t // tv
    halo = 8
    return pl.pallas_call(
        functools.partial(_gdn_kernel, c=c, tv=tv, halo=halo),
        grid=(nbatch, nci),
        in_specs=[
            pl.BlockSpec((tv, CONV_DIM), lambda b, i: (b * nci + i, 0)),
            pl.BlockSpec((tv, LANES), lambda b, i: (b * nci + i, 0)),
            pl.BlockSpec((1, CONV_W - 1, CONV_DIM), lambda b, i: (b, 0, 0)),
            pl.BlockSpec((1, N_HEADS_B, HEAD_DIM, HEAD_DIM), lambda b, i: (b, 0, 0, 0)),
            pl.BlockSpec((CONV_W, CONV_DIM), lambda b, i: (0, 0)),
            pl.BlockSpec((1, LANES), lambda b, i: (0, 0)),
            pl.BlockSpec((1, LANES), lambda b, i: (0, 0)),
            pl.BlockSpec((1, HEAD_DIM), lambda b, i: (0, 0)),
        ],
        out_specs=[
            pl.BlockSpec((tv, D_GDN), lambda b, i: (b * nci + i, 0)),
            pl.BlockSpec((1, N_HEADS_B, HEAD_DIM, HEAD_DIM), lambda b, i: (b, 0, 0, 0)),
        ],
        out_shape=[jax.ShapeDtypeStruct((nbatch * t, D_GDN), f32),
                   jax.ShapeDtypeStruct((nbatch, N_HEADS_B, HEAD_DIM, HEAD_DIM), f32)],
        scratch_shapes=[pltpu.VMEM((halo + c, CONV_DIM), f32),
                        pltpu.VMEM((N_HEADS_B, HEAD_DIM, HEAD_DIM), f32)]
        + [pltpu.VMEM((N_HEADS_B, c, c), f32)] * 3
        + [pltpu.VMEM((N_HEADS_B, c, c), bf16)]
        + [pltpu.VMEM((N_HEADS_B, c, 2 * HEAD_DIM), f32)]
        + [pltpu.VMEM((N_HEADS_B, c, HEAD_DIM), bf16)] * 5
        + [pltpu.VMEM((N_HEADS_B, c, c), bf16)]
        + [pltpu.VMEM((N_HEADS_B, c, HEAD_DIM), bf16)]
        + [pltpu.VMEM((N_HEADS_B, c, 2 * HEAD_DIM), bf16)],
        compiler_params=_cparams(("parallel", "arbitrary")),
        name="gdn",
    )(qkv_b, small, conv0, s0, conv_w, alog_row, dtb_row, gdn_norm_w.reshape(1, HEAD_DIM))


def _merge_kernel(x_ref, oc_ref, os_ref, ow_ref, small_ref, za_ref, zb_ref, ob_ref, w_ref, fw_ref, y_ref, mix_ref):
    gates = _sigmoid(small_ref[...])
    for h in range(N_HEADS_A):
        cs = slice(h * HEAD_DIM, (h + 1) * HEAD_DIM)
        o_a = (gates[:, G_OFF + h:G_OFF + h + 1] * oc_ref[:, cs]
               + gates[:, G_OFF + N_HEADS_A + h:G_OFF + N_HEADS_A + h + 1] * os_ref[:, cs]
               + gates[:, G_OFF + 2 * N_HEADS_A + h:G_OFF + 2 * N_HEADS_A + h + 1] * ow_ref[:, cs])
        mix_ref[:, cs] = (o_a * _silu(za_ref[:, cs])).astype(bf16)
    mix_ref[:, D_ATT:] = (ob_ref[...] * _silu(zb_ref[...])).astype(bf16)
    hres = x_ref[...] + _dot(mix_ref[...], w_ref[...])
    ms = jnp.mean(hres * hres, axis=-1, keepdims=True)
    y_ref[...] = hres * lax.rsqrt(ms + NORM_EPS) * fw_ref[...]


def _merge_out(x2d, o_cmp, o_slc, o_swa, small, qz, o_b, w_out_bf, final_norm_w, tm):
    m = x2d.shape[0]
    row = lambda i: (i, 0)
    return pl.pallas_call(
        _merge_kernel,
        grid=(m // tm,),
        in_specs=[
            pl.BlockSpec((tm, D_MODEL), row),
            pl.BlockSpec((tm, D_ATT), row),
            pl.BlockSpec((tm, D_ATT), row),
            pl.BlockSpec((tm, D_ATT), row),
            pl.BlockSpec((tm, LANES), row),
            pl.BlockSpec((tm, D_ATT), lambda i: (i, 1)),
            pl.BlockSpec((tm, D_GDN), lambda i: (i, 2)),
            pl.BlockSpec((tm, D_GDN), row),
            pl.BlockSpec((D_ATT + D_GDN, D_MODEL), lambda i: (0, 0)),
            pl.BlockSpec((1, D_MODEL), lambda i: (0, 0)),
        ],
        out_specs=pl.BlockSpec((tm, D_MODEL), row),
        out_shape=jax.ShapeDtypeStruct((m, D_MODEL), f32),
        scratch_shapes=[pltpu.VMEM((tm, D_ATT + D_GDN), bf16)],
        compiler_params=_cparams(("parallel",)),
        name="merge_out",
    )(x2d, o_cmp, o_slc, o_swa, small, qz, qz, o_b, w_out_bf, final_norm_w.reshape(1, D_MODEL))


def _page_copy(cache_ref, buf_ref, sem_ref, pt_ref, b, j, slot, i, pps):
    page = pt_ref[b, j * pps + i]
    return pltpu.make_async_copy(
        cache_ref.at[pl.ds(page * ROWS_PER_PAGE4, ROWS_PER_PAGE4), :],
        buf_ref.at[slot, pl.ds(i * ROWS_PER_PAGE4, ROWS_PER_PAGE4), :],
        sem_ref.at[slot])


def _paged_fetch(cache_ref, buf_ref, sem_ref, pt_ref, pps, nj_pages):
    b = pl.program_id(0)
    j = pl.program_id(1)
    nb = pl.num_programs(0)
    nj = pl.num_programs(1)
    step = b * nj_pages + jnp.minimum(j, nj_pages - 1)
    slot = step % 2

    def start(bb, jj, sl):
        for i in range(pps):
            _page_copy(cache_ref, buf_ref, sem_ref, pt_ref, bb, jj, sl, i, pps).start()

    @pl.when((b == 0) & (j == 0))
    def _():
        start(0, 0, 0)

    @pl.when(j < nj_pages)
    def _():
        last_j = j == nj_pages - 1
        nb_ = jnp.where(last_j, b + 1, b)
        nj_ = jnp.where(last_j, 0, j + 1)

        @pl.when(nb_ < nb)
        def _():
            start(nb_, nj_, 1 - slot)

        for i in range(pps):
            _page_copy(cache_ref, buf_ref, sem_ref, pt_ref, b, j, slot, i, pps).wait()

    return slot


def _compress_paged_kernel(pt_ref, cache_ref, pe_ref, w_ref, o_ref, buf_ref, sem_ref, *, pps):
    slot = _paged_fetch(cache_ref, buf_ref, sem_ref, pt_ref, pps, pl.num_programs(1))
    nbk = pps * (PAGE_SIZE // CMP_BLOCK)
    stride = CMP_BLOCK * N_CH
    accs = _compress_accumulate(
        lambda l, ch: buf_ref[slot, pl.ds(l * N_CH + ch, nbk, stride=stride), :], pe_ref, w_ref, nbk)
    for c in range(2):
        for h in range(N_KV_A):
            o_ref[0, c * N_KV_A + h] = accs[c][h * nbk:(h + 1) * nbk]


def _compress_paged(page_table, cache4, pe2d, w_cmp_bf, pps):
    nbatch, n_pages = page_table.shape
    nj = n_pages // pps
    nbk = pps * (PAGE_SIZE // CMP_BLOCK)
    gs = pltpu.PrefetchScalarGridSpec(
        num_scalar_prefetch=1,
        grid=(nbatch, nj),
        in_specs=[
            pl.BlockSpec(memory_space=pl.ANY),
            pl.BlockSpec((2, CMP_BLOCK, HEAD_DIM), lambda b, j, pt: (0, 0, 0)),
            pl.BlockSpec((2, CMP_BLOCK // 2, 2 * HEAD_DIM, HEAD_DIM), lambda b, j, pt: (0, 0, 0, 0)),
        ],
        out_specs=pl.BlockSpec((1, N_CH, nbk, HEAD_DIM), lambda b, j, pt: (b, 0, j, 0)),
        scratch_shapes=[pltpu.VMEM((2, pps * ROWS_PER_PAGE4, HEAD_DIM), f32), pltpu.SemaphoreType.DMA((2,))],
    )
    return pl.pallas_call(
        functools.partial(_compress_paged_kernel, pps=pps),
        grid_spec=gs,
        out_shape=jax.ShapeDtypeStruct((nbatch, N_CH, nj * nbk, HEAD_DIM), f32),
        compiler_params=_cparams(("arbitrary", "arbitrary")),
        name="compress_paged",
    )(page_table, cache4, pe2d, w_cmp_bf)


def _two_phase_attend(qs, slope2, qpos, kpos0, load_k, load_v, load_mask, nchunks, chunk,
                      s_ref, rows, m_ref, l_ref, acc_ref):
    nrow = qs.shape[0]
    nrep = chunk // LANES
    kofs = lax.broadcasted_iota(jnp.int32, (1, chunk), 1)
    unroll = math.gcd(nchunks, 4)

    def cols(ci):
        return pl.ds(pl.multiple_of(ci * chunk, chunk), chunk)

    def phase1(ci, mrun):
        d = qpos - (kpos0 + ci * chunk + kofs)
        dmask = jnp.where((load_mask(ci) > 0.5) & (d >= 0), d.astype(f32), MASK_DIST)
        t2 = _dot_nt(qs, load_k(ci)) * (SCALE * LOG2E) - slope2 * dmask
        s_ref[rows, cols(ci)] = t2
        for c in range(nrep):
            mrun = jnp.maximum(mrun, t2[:, c * LANES:(c + 1) * LANES])
        return mrun

    mrun = lax.fori_loop(0, nchunks, phase1, jnp.full((nrow, LANES), NEG_INF, f32), unroll=unroll)
    m_prev = m_ref[rows, :]
    m_new = jnp.maximum(m_prev, jnp.broadcast_to(jnp.max(mrun, axis=-1, keepdims=True), (nrow, LANES)))

    def phase2(ci, carry):
        lp, acc = carry
        p = jnp.exp2(s_ref[rows, cols(ci)] - _lane_repeat(m_new, nrep))
        for c in range(nrep):
            lp = lp + p[:, c * LANES:(c + 1) * LANES]
        return lp, acc + _dot(p.astype(bf16), load_v(ci))

    z = jnp.zeros((nrow, LANES), f32)
    lp, acc = lax.fori_loop(0, nchunks, phase2, (z, z), unroll=unroll)
    alpha = jnp.exp2(m_prev - m_new)
    l_ref[rows, :] = alpha * l_ref[rows, :] + lp
    acc_ref[rows, :] = alpha * acc_ref[rows, :] + acc
    m_ref[rows, :] = m_new


def _slc_paged_kernel(pt_ref, slopes_ref, cache_ref, q_ref, tail_ref, sel_ref, e_ref, o_ref,
                      buf_ref, sem_ref, s_ref, m_ref, l_ref, acc_ref, *, pps, t, sub, pos0):
    j = pl.program_id(1)
    nj_pages = pl.num_programs(1) - 1
    slot = _paged_fetch(cache_ref, buf_ref, sem_ref, pt_ref, pps, nj_pages)
    nrow = GQA * t

    @pl.when(j == 0)
    def _():
        _flash_init(m_ref, l_ref, acc_ref)

    tpos = lax.broadcasted_iota(jnp.int32, (t, 1), 0)
    qpos = pos0 + jnp.concatenate([tpos] * GQA, axis=0)

    def attend(kvh, load_k, load_v, nchunks, chunk, kpos0):
        q = q_ref[:, kvh * GQA * HEAD_DIM:(kvh + 1) * GQA * HEAD_DIM]
        qs = jnp.concatenate([q[:, g * HEAD_DIM:(g + 1) * HEAD_DIM] for g in range(GQA)], axis=0).astype(bf16)
        slope2 = jnp.concatenate([jnp.full((t, 1), slopes_ref[kvh, g] * LOG2E, f32) for g in range(GQA)], axis=0)
        selb = sel_ref[0, kvh].astype(bf16)

        def load_mask(ci):
            maskf = _dot(selb, e_ref[:, pl.ds(pl.multiple_of(ci * chunk, chunk), chunk)])
            return jnp.concatenate([maskf] * GQA, axis=0)

        _two_phase_attend(qs, slope2, qpos, kpos0, load_k, load_v, load_mask, nchunks, chunk,
                          s_ref, pl.ds(kvh * nrow, nrow), m_ref, l_ref, acc_ref)

    @pl.when(j < nj_pages)
    def _():
        for kvh in range(N_KV_A):
            def load(ci, ch):
                return buf_ref[slot, pl.ds(ci * (sub * N_CH) + ch, sub, stride=N_CH), :].astype(bf16)

            attend(kvh, functools.partial(load, ch=kvh), functools.partial(load, ch=N_KV_A + kvh),
                   pps * PAGE_SIZE // sub, sub, j * pps * PAGE_SIZE)

    @pl.when(j == nj_pages)
    def _():
        ntail = tail_ref.shape[1]
        for kvh in range(N_KV_A):
            k = tail_ref[0, :, kvh * HEAD_DIM:(kvh + 1) * HEAD_DIM].astype(bf16)
            v = tail_ref[0, :, (N_KV_A + kvh) * HEAD_DIM:(N_KV_A + kvh + 1) * HEAD_DIM].astype(bf16)
            attend(kvh, lambda ci, k=k: k, lambda ci, v=v: v, 1, ntail, nj_pages * pps * PAGE_SIZE)
        for kvh in range(N_KV_A):
            for g in range(GQA):
                r = pl.ds(kvh * nrow + g * t, t)
                o_ref[:, (kvh * GQA + g) * HEAD_DIM:(kvh * GQA + g + 1) * HEAD_DIM] = (
                    acc_ref[r, :] / jnp.sum(l_ref[r, :], axis=-1, keepdims=True))


def _slc_paged(page_table, slopes, cache4, qz, tail, sel, expand, t, pps, pos0):
    nbatch, n_pages = page_table.shape
    njp = n_pages // pps
    bps = pps * (PAGE_SIZE // CMP_BLOCK)
    sub = min(512, pps * PAGE_SIZE)
    gs = pltpu.PrefetchScalarGridSpec(
        num_scalar_prefetch=1,
        grid=(nbatch, njp + 1),
        in_specs=[
            pl.BlockSpec(memory_space=pltpu.SMEM),
            pl.BlockSpec(memory_space=pl.ANY),
            pl.BlockSpec((t, D_ATT), lambda b, j, pt: (b, 0)),
            pl.BlockSpec((1, LANES, KV_BRANCH), lambda b, j, pt: (b, 0, 0)),
            pl.BlockSpec((1, N_KV_A, t, bps), lambda b, j, pt: (b, 0, 0, j)),
            pl.BlockSpec((bps, pps * PAGE_SIZE), lambda b, j, pt: (0, 0)),
        ],
        out_specs=pl.BlockSpec((t, D_ATT), lambda b, j, pt: (b, 0)),
        scratch_shapes=[pltpu.VMEM((2, pps * ROWS_PER_PAGE4, HEAD_DIM), f32), pltpu.SemaphoreType.DMA((2,)),
                        pltpu.VMEM((N_KV_A * GQA * t, pps * PAGE_SIZE), f32),
                        pltpu.VMEM((N_KV_A * GQA * t, LANES), f32), pltpu.VMEM((N_KV_A * GQA * t, LANES), f32),
                        pltpu.VMEM((N_KV_A * GQA * t, HEAD_DIM), f32)],
    )
    return pl.pallas_call(
        functools.partial(_slc_paged_kernel, pps=pps, t=t, sub=sub, pos0=pos0),
        grid_spec=gs,
        out_shape=jax.ShapeDtypeStruct((nbatch * t, D_ATT), f32),
        compiler_params=_cparams(("arbitrary", "arbitrary")),
        name="slc_paged",
    )(page_table, slopes, cache4, qz, tail, sel, expand)


def _swa_sample_kernel(slopes_ref, q_ref, win_ref, tail_ref, o_ref, *, t, wbuf, pos0):
    tpos = lax.broadcasted_iota(jnp.int32, (t, 1), 0)
    qpos = pos0 + jnp.concatenate([tpos] * GQA, axis=0)
    nk = wbuf + CMP_BLOCK
    kpos = pos0 - wbuf + lax.broadcasted_iota(jnp.int32, (1, nk), 1)
    d = qpos - kpos
    ok = (d >= 0) & (d < WINDOW)
    df = d.astype(f32)
    for kvh in range(N_KV_A):
        k = jnp.concatenate([win_ref[pl.ds(kvh, wbuf, stride=N_CH), :],
                             tail_ref[0, :, kvh * HEAD_DIM:(kvh + 1) * HEAD_DIM]], axis=0).astype(bf16)
        v = jnp.concatenate([win_ref[pl.ds(N_KV_A + kvh, wbuf, stride=N_CH), :],
                             tail_ref[0, :, (N_KV_A + kvh) * HEAD_DIM:(N_KV_A + kvh + 1) * HEAD_DIM]],
                            axis=0).astype(bf16)
        q = q_ref[:, kvh * GQA * HEAD_DIM:(kvh + 1) * GQA * HEAD_DIM]
        qs = jnp.concatenate([q[:, g * HEAD_DIM:(g + 1) * HEAD_DIM] for g in range(GQA)], axis=0).astype(bf16)
        slope = jnp.concatenate([jnp.full((t, 1), slopes_ref[kvh, g], f32) for g in range(GQA)], axis=0)
        s = _dot_nt(qs, k) * SCALE - slope * df
        s = jnp.where(ok, s, NEG_INF)
        e = jnp.exp(s - jnp.max(s, axis=-1, keepdims=True))
        p = jnp.where(ok, e / jnp.sum(e, axis=-1, keepdims=True), 0.0)
        o = _dot(p.astype(bf16), v)
        for g in range(GQA):
            o_ref[:, (kvh * GQA + g) * HEAD_DIM:(kvh * GQA + g + 1) * HEAD_DIM] = o[g * t:(g + 1) * t]


def _swa_sample(slopes, qz, win4, tail, nbatch, t, wbuf, pos0):
    return pl.pallas_call(
        functools.partial(_swa_sample_kernel, t=t, wbuf=wbuf, pos0=pos0),
        grid=(nbatch,),
        in_specs=[
            pl.BlockSpec(memory_space=pltpu.SMEM),
            pl.BlockSpec((t, D_ATT), lambda b: (b, 0)),
            pl.BlockSpec((wbuf * N_CH, HEAD_DIM), lambda b: (b, 0)),
            pl.BlockSpec((1, CMP_BLOCK, KV_BRANCH), lambda b: (b, 0, 0)),
        ],
        out_specs=pl.BlockSpec((t, D_ATT), lambda b: (b, 0)),
        out_shape=jax.ShapeDtypeStruct((nbatch * t, D_ATT), f32),
        compiler_params=_cparams(("parallel",)),
        name="swa_sample",
    )(slopes, qz, win4, tail)


def _round_up(a, b):
    return -(-a // b) * b


def _pad_blocks(kcv, ncp):
    return jnp.pad(kcv, ((0, 0), (0, 0), (0, ncp - kcv.shape[2]), (0, 0)))


def _expand_matrix(nblocks, nkeys):
    return (jnp.arange(nkeys)[None, :] // CMP_BLOCK == jnp.arange(nblocks)[:, None]).astype(bf16)


def _kv6(rows2d, nbatch, t):
    return rows2d.reshape(nbatch, t, 2, N_KV_A, HEAD_DIM)


def _layer_prompt(x, prm):
    nbatch, t, _ = x.shape
    m = nbatch * t
    x2d = x.reshape(m, D_MODEL)
    tm = min(1024, m)
    qz, qkvb, kv_cmp, kv_slc, kv_swa, small = _in_proj(x2d, prm["norm_w"], prm["w_main"], prm["w_small"], tm)
    nc = t // CMP_BLOCK
    kcv = _compress(kv_cmp, nbatch, prm["pe2d"], prm["w_cmp"], min(64, nc))
    kcv = _pad_blocks(kcv, _round_up(nc, LANES))
    tq = min(256, t)
    o_cmp, sel = _cmp_attend(prm["slopes"], qz, kcv, nbatch, t, tq, 0, min(N_SELECT, nc))
    tk = min(256, t)
    o_slc = _slc_prompt(prm["slopes"], qz, kv_slc, sel, _expand_matrix(sel.shape[-1], t), nbatch, t, tq, tk)
    o_swa = _swa_prompt(prm["slopes"], qz, kv_swa, nbatch, t)
    conv0 = jnp.zeros((nbatch, CONV_W - 1, CONV_DIM), f32)
    s0 = jnp.zeros((nbatch, N_HEADS_B, HEAD_DIM, HEAD_DIM), f32)
    o_b, s_new = _gdn(qkvb, small, conv0, s0, prm["conv_w"], prm["alog_row"], prm["dtb_row"],
                      prm["gdn_norm_w"], nbatch, t, 128)
    y = _merge_out(x2d, o_cmp, o_slc, o_swa, small, qz, o_b, prm["w_out"], prm["final_norm_w"], min(256, m))
    w = min(WINDOW, t)
    conv_new = qkvb.reshape(nbatch, t, CONV_DIM)[:, t - (CONV_W - 1):]
    return (y.reshape(nbatch, t, D_MODEL), _kv6(kv_cmp, nbatch, t), _kv6(kv_slc, nbatch, t),
            _kv6(kv_swa, nbatch, t)[:, t - w:], conv_new, s_new)


def _layer_sample(x, cache_cmp, cache_slc, cache_swa, conv_buf, s0, page_table, prm):
    nbatch, t, _ = x.shape
    m = nbatch * t
    n_pages = page_table.shape[1]
    past_len = n_pages * PAGE_SIZE
    wbuf = cache_swa.shape[1]
    x2d = x.reshape(m, D_MODEL)
    qz, qkvb, kv_cmp, kv_slc, kv_swa, small = _in_proj(x2d, prm["norm_w"], prm["w_main"], prm["w_small"], m)
    pad_tail = lambda r: jnp.pad(r.reshape(nbatch, t, KV_BRANCH), ((0, 0), (0, CMP_BLOCK - t), (0, 0)))
    pps = min(32, n_pages)
    kc_past = _compress_paged(page_table, cache_cmp.reshape(-1, HEAD_DIM), prm["pe2d"], prm["w_cmp"], pps)
    kc_tail = _compress(pad_tail(kv_cmp).reshape(nbatch * CMP_BLOCK, KV_BRANCH), 1, prm["pe2d"], prm["w_cmp"], nbatch)
    kc_tail = jnp.swapaxes(kc_tail[0], 0, 1)[:, :, None, :]
    n_pb = past_len // CMP_BLOCK
    pps_s = min(64, n_pages)
    bps = pps_s * (PAGE_SIZE // CMP_BLOCK)
    ncp = _round_up(n_pb + 1, max(LANES, bps))
    kcv = _pad_blocks(jnp.concatenate([kc_past, kc_tail], axis=2), ncp)
    o_cmp, sel = _cmp_attend(prm["slopes"], qz, kcv, nbatch, t, t, past_len, min(N_SELECT, n_pb + 1))
    o_slc = _slc_paged(page_table, prm["slopes"], cache_slc.reshape(-1, HEAD_DIM), qz,
                       jnp.pad(kv_slc.reshape(nbatch, t, KV_BRANCH), ((0, 0), (0, LANES - t), (0, 0))), sel,
                       _expand_matrix(bps, pps_s * PAGE_SIZE), t, pps_s, past_len)
    o_swa = _swa_sample(prm["slopes"], qz, cache_swa.reshape(-1, HEAD_DIM), pad_tail(kv_swa), nbatch, t, wbuf, past_len)
    o_b, s_new = _gdn(qkvb, small, conv_buf, s0, prm["conv_w"], prm["alog_row"], prm["dtb_row"],
                      prm["gdn_norm_w"], nbatch, t, 128)
    y = _merge_out(x2d, o_cmp, o_slc, o_swa, small, qz, o_b, prm["w_out"], prm["final_norm_w"], min(256, m))
    kv_win = jnp.concatenate([cache_swa, _kv6(kv_swa, nbatch, t)], axis=1)[:, t:]
    conv_new = jnp.concatenate([conv_buf, qkvb.reshape(nbatch, t, CONV_DIM)], axis=1)[:, t:]
    return (y.reshape(nbatch, t, D_MODEL), _kv6(kv_cmp, nbatch, t), _kv6(kv_slc, nbatch, t), kv_win, conv_new, s_new)


def kernel(x_prompt, x_sample, cache_cmp, cache_slc, cache_swa, state_conv, state_gdn, page_table,
           norm_w, w_in, pe_cmp, w_cmp, conv_w, a_log, dt_bias, gdn_norm_w, w_out, final_norm_w):
    depth = norm_w.shape[0]
    assert depth == 1, "the final norm is fused into the single layer's output projection"
    l = 0
    w_main, w_small = _prep_w_in(w_in[l])
    head = jnp.arange(1, N_HEADS_A + 1, dtype=f32)
    lane_row = lambda vals: jnp.zeros((1, LANES), f32).at[0, A_OFF:A_OFF + N_HEADS_B].set(vals.astype(f32))
    prm = {
        "norm_w": norm_w[l], "w_main": w_main, "w_small": w_small,
        "pe2d": jnp.swapaxes(pe_cmp[l], 0, 1),
        "w_cmp": jnp.swapaxes(w_cmp[l], 0, 1).reshape(2, CMP_BLOCK // 2, 2 * HEAD_DIM, HEAD_DIM).astype(bf16),
        "conv_w": conv_w[l], "alog_row": lane_row(a_log[l]), "dtb_row": lane_row(dt_bias[l]),
        "gdn_norm_w": gdn_norm_w[l], "w_out": w_out[l].astype(bf16), "final_norm_w": final_norm_w,
        "slopes": jnp.exp2(-8.0 * head / N_HEADS_A).reshape(N_KV_A, GQA),
    }
    yp, p_cmp, p_slc, p_swa, p_conv, p_gdn = _layer_prompt(x_prompt, prm)
    ys, s_cmp, s_slc, s_swa, s_conv, s_gdn = _layer_sample(
        x_sample, cache_cmp[l], cache_slc[l], cache_swa[l], state_conv[l], state_gdn[l], page_table, prm)
    st = lambda a: a[None]
    return (yp, ys, st(p_cmp), st(p_slc), st(p_swa), st(p_conv), st(p_gdn),
            st(s_cmp), st(s_slc), st(s_swa), st(s_conv), st(s_gdn))
```

```python
import functools
import math

import jax
import jax.numpy as jnp
from jax import lax
from jax.experimental import pallas as pl
from jax.experimental.pallas import tpu as pltpu

f32 = jnp.float32
bf16 = jnp.bfloat16

D_MODEL = 2048
HEAD_DIM = 128
N_HEADS_A = 8
N_KV_A = 2
GQA = 4
D_ATT = N_HEADS_A * HEAD_DIM
CMP_BLOCK = 64
N_SELECT = 16
WINDOW = 512
N_HEADS_B = 8
D_GDN = N_HEADS_B * HEAD_DIM
CONV_W = 4
CONV_DIM = 3 * D_GDN
PAGE_SIZE = 128
NORM_EPS = 1e-6
NEG_INF = -1e30
FORCE_SCORE = 1e4
KV_BRANCH = 2 * N_KV_A * HEAD_DIM
N_CH = 2 * N_KV_A
ROWS_PER_PAGE4 = PAGE_SIZE * N_CH
SCALE = HEAD_DIM ** -0.5
G_OFF, A_OFF, B_OFF = 0, 3 * N_HEADS_A, 3 * N_HEADS_A + N_HEADS_B
LANES = 128
VMEM_LIMIT = 56 * 1024 * 1024

IN_TN = 512
N_QZ_TILES = 3 * D_ATT // IN_TN
N_B_TILES = CONV_DIM // IN_TN
N_IN_TILES = N_QZ_TILES + N_B_TILES + 3


def _cparams(sem):
    return pltpu.CompilerParams(dimension_semantics=sem, vmem_limit_bytes=VMEM_LIMIT)


def _dot(a, b):
    return jnp.dot(a, b, preferred_element_type=f32)


def _dot_nt(a, b):
    return lax.dot_general(a, b, (((1,), (1,)), ((), ())), preferred_element_type=f32)


def _dot_tn(a, b):
    return lax.dot_general(a, b, (((0,), (0,)), ((), ())), preferred_element_type=f32)


def _in_proj_kernel(x_ref, nw_ref, w_ref, ws_ref, qz_ref, b_ref, cmp_ref, slc_ref, swa_ref, small_ref, xn_ref):
    n = pl.program_id(1)

    @pl.when(n == 0)
    def _():
        x = x_ref[...]
        ms = jnp.mean(x * x, axis=-1, keepdims=True)
        xn = (x * lax.rsqrt(ms + NORM_EPS) * nw_ref[...]).astype(bf16)
        xn_ref[...] = xn
        small_ref[...] = _dot(xn, ws_ref[...])

    acc = _dot(xn_ref[...], w_ref[...])

    @pl.when(n < N_QZ_TILES)
    def _():
        qz_ref[...] = acc

    @pl.when((n >= N_QZ_TILES) & (n < N_QZ_TILES + N_B_TILES))
    def _():
        b_ref[...] = acc

    @pl.when(n == N_QZ_TILES + N_B_TILES)
    def _():
        cmp_ref[...] = acc

    @pl.when(n == N_QZ_TILES + N_B_TILES + 1)
    def _():
        slc_ref[...] = acc

    @pl.when(n == N_QZ_TILES + N_B_TILES + 2)
    def _():
        swa_ref[...] = acc


def _in_proj(x2d, norm_w, w_main, w_small, tm):
    m = x2d.shape[0]
    S = jax.ShapeDtypeStruct
    nb0 = N_QZ_TILES
    return pl.pallas_call(
        _in_proj_kernel,
        grid=(m // tm, N_IN_TILES),
        in_specs=[
            pl.BlockSpec((tm, D_MODEL), lambda i, n: (i, 0)),
            pl.BlockSpec((1, D_MODEL), lambda i, n: (0, 0)),
            pl.BlockSpec((D_MODEL, IN_TN), lambda i, n: (0, n)),
            pl.BlockSpec((D_MODEL, LANES), lambda i, n: (0, 0)),
        ],
        out_specs=[
            pl.BlockSpec((tm, IN_TN), lambda i, n: (i, jnp.minimum(n, nb0 - 1))),
            pl.BlockSpec((tm, IN_TN), lambda i, n: (i, jnp.clip(n - nb0, 0, N_B_TILES - 1))),
            pl.BlockSpec((tm, KV_BRANCH), lambda i, n: (i, 0)),
            pl.BlockSpec((tm, KV_BRANCH), lambda i, n: (i, 0)),
            pl.BlockSpec((tm, KV_BRANCH), lambda i, n: (i, 0)),
            pl.BlockSpec((tm, LANES), lambda i, n: (i, 0)),
        ],
        out_shape=[S((m, 3 * D_ATT), f32), S((m, CONV_DIM), f32), S((m, KV_BRANCH), f32),
                   S((m, KV_BRANCH), f32), S((m, KV_BRANCH), f32), S((m, LANES), f32)],
        scratch_shapes=[pltpu.VMEM((tm, D_MODEL), bf16)],
        compiler_params=_cparams(("parallel", "arbitrary")),
        name="in_proj",
    )(x2d, norm_w.reshape(1, D_MODEL), w_main, w_small)


def _prep_w_in(w_in):
    o = [0]
    for s in (D_ATT, 3 * KV_BRANCH, 3 * N_HEADS_A, D_ATT, CONV_DIM, N_HEADS_B, N_HEADS_B, D_GDN):
        o.append(o[-1] + s)
    q_a, kv_a, g_a, z_a, qkv_b, a_b, b_b, z_b = (w_in[:, o[i]:o[i + 1]] for i in range(8))
    w_main = jnp.concatenate([q_a, z_a, z_b, qkv_b, kv_a], axis=1).astype(bf16)
    pad = jnp.zeros((w_in.shape[0], LANES - (3 * N_HEADS_A + 2 * N_HEADS_B)), w_in.dtype)
    w_small = jnp.concatenate([g_a, a_b, b_b, pad], axis=1).astype(bf16)
    return w_main, w_small


CMP_UNROLL = 4


def _compress_accumulate(load_rows, pe_ref, w_ref, nbk):
    def body(i, accs):
        new = []
        for c in range(2):
            halves = []
            for dl in range(2):
                l = 2 * i + dl
                pe_row = pe_ref[c, pl.ds(l, 1), :]
                halves.append(jnp.concatenate(
                    [load_rows(l, c * N_KV_A + h) + pe_row for h in range(N_KV_A)], axis=0))
            lhs = jnp.concatenate(halves, axis=1).astype(bf16)
            new.append(accs[c] + _dot(lhs, w_ref[c, i]))
        return tuple(new)

    z = jnp.zeros((N_KV_A * nbk, HEAD_DIM), f32)
    return lax.fori_loop(0, CMP_BLOCK // 2, body, (z, z), unroll=CMP_UNROLL)


def _compress_kernel(x0, x1, x2, x3, pe_ref, w_ref, o_ref, *, nbk):
    xs = (x0, x1, x2, x3)
    accs = _compress_accumulate(lambda l, ch: xs[ch][pl.ds(l, nbk, stride=CMP_BLOCK), :], pe_ref, w_ref, nbk)
    for c in range(2):
        for h in range(N_KV_A):
            o_ref[0, c * N_KV_A + h] = accs[c][h * nbk:(h + 1) * nbk]


def _compress(rows2d, nbatch, pe2d, w_cmp_bf, nbk):
    nblk = rows2d.shape[0] // (nbatch * CMP_BLOCK)
    nj = nblk // nbk
    rows = nbk * CMP_BLOCK
    xspec = [pl.BlockSpec((rows, HEAD_DIM), functools.partial(lambda b, j, ch: (b * nj + j, ch), ch=ch))
             for ch in range(N_CH)]
    return pl.pallas_call(
        functools.partial(_compress_kernel, nbk=nbk),
        grid=(nbatch, nj),
        in_specs=xspec + [
            pl.BlockSpec((2, CMP_BLOCK, HEAD_DIM), lambda b, j: (0, 0, 0)),
            pl.BlockSpec((2, CMP_BLOCK // 2, 2 * HEAD_DIM, HEAD_DIM), lambda b, j: (0, 0, 0, 0)),
        ],
        out_specs=pl.BlockSpec((1, N_CH, nbk, HEAD_DIM), lambda b, j: (b, 0, j, 0)),
        out_shape=jax.ShapeDtypeStruct((nbatch, N_CH, nblk, HEAD_DIM), f32),
        compiler_params=_cparams(("parallel", "parallel")),
        name="compress",
    )(rows2d, rows2d, rows2d, rows2d, pe2d, w_cmp_bf)


def _cmp_kernel(slopes_ref, q_ref, kc_ref, vc_ref, o_ref, sel_ref, *, tq, tqp, ncp, pos0, nsel):
    kvh = pl.program_id(1)
    qt = pl.program_id(2)
    q = q_ref[...]
    if tqp > tq:
        q = jnp.concatenate([q, jnp.zeros((tqp - tq, q.shape[1]), f32)], axis=0)
    kc = kc_ref[0, 0].astype(bf16)
    vc = vc_ref[0, 0].astype(bf16)
    qpos = pos0 + qt * tq + lax.broadcasted_iota(jnp.int32, (1, tqp), 1)
    jblk = lax.broadcasted_iota(jnp.int32, (ncp, 1), 0)
    d = qpos - ((jblk + 1) * CMP_BLOCK - 1)
    ok = d >= 0
    df = d.astype(f32)
    imp = jnp.zeros((ncp, tqp), f32)
    for g in range(GQA):
        qg = q[:, g * HEAD_DIM:(g + 1) * HEAD_DIM].astype(bf16)
        s = _dot_nt(kc, qg) * SCALE
        s = jnp.where(ok, s - slopes_ref[kvh, g] * df, NEG_INF)
        mx = jnp.max(s, axis=0, keepdims=True)
        e = jnp.exp(s - mx)
        p = jnp.where(ok, e / jnp.sum(e, axis=0, keepdims=True), 0.0)
        imp = imp + p
        og = _dot_tn(p.astype(bf16), vc)
        o_ref[:, g * HEAD_DIM:(g + 1) * HEAD_DIM] = og[:tq]
    cur = qpos // CMP_BLOCK
    forced = (jblk == cur) | (jblk == 0)
    score = jnp.where(forced, FORCE_SCORE, jnp.where(jblk <= cur, imp, -1.0))
    for cg in range(tqp // LANES):
        sc = score[:, cg * LANES:(cg + 1) * LANES]
        work = sc
        picked = jnp.zeros((ncp, LANES), f32)
        for _ in range(nsel):
            mx = jnp.max(work, axis=0, keepdims=True)
            first = jnp.min(jnp.where(work == mx, jblk, ncp), axis=0, keepdims=True)
            hit = jblk == first
            picked = jnp.where(hit, 1.0, picked)
            work = jnp.where(hit, -2.0, work)
        sel = jnp.where(sc >= 0.0, picked, 0.0)
        lo = cg * LANES
        hi = min(tq, lo + LANES)
        sel_ref[0, 0, lo:hi, :] = sel.T[:hi - lo]


def _cmp_attend(slopes, qz, kcv, nbatch, t, tq, pos0, nsel):
    ncp = kcv.shape[2]
    tqp = max(tq, LANES)
    nqt = t // tq
    gw = GQA * HEAD_DIM
    return pl.pallas_call(
        functools.partial(_cmp_kernel, tq=tq, tqp=tqp, ncp=ncp, pos0=pos0, nsel=nsel),
        grid=(nbatch, N_KV_A, nqt),
        in_specs=[
            pl.BlockSpec(memory_space=pltpu.SMEM),
            pl.BlockSpec((tq, gw), lambda b, h, i: (b * nqt + i, h)),
            pl.BlockSpec((1, 1, ncp, HEAD_DIM), lambda b, h, i: (b, h, 0, 0)),
            pl.BlockSpec((1, 1, ncp, HEAD_DIM), lambda b, h, i: (b, N_KV_A + h, 0, 0)),
        ],
        out_specs=[
            pl.BlockSpec((tq, gw), lambda b, h, i: (b * nqt + i, h)),
            pl.BlockSpec((1, 1, tq, ncp), lambda b, h, i: (b, h, i, 0)),
        ],
        out_shape=[jax.ShapeDtypeStruct((nbatch * t, D_ATT), f32),
                   jax.ShapeDtypeStruct((nbatch, N_KV_A, t, ncp), f32)],
        compiler_params=_cparams(("parallel", "parallel", "parallel")),
        name="cmp_attend",
    )(slopes, qz, kcv, kcv)


def _lane_repeat(x, n):
    return x if n == 1 else jnp.concatenate([x] * n, axis=1)


def _flash_init(m_ref, l_ref, acc_ref):
    m_ref[...] = jnp.full(m_ref.shape, NEG_INF, f32)
    l_ref[...] = jnp.zeros(l_ref.shape, f32)
    acc_ref[...] = jnp.zeros(acc_ref.shape, f32)


LOG2E = 1.4426950408889634
MASK_DIST = 1e33
SLC_ROWS = 128


def _attn_prompt_body(kvh, qt, slopes_ref, q_ref, k_ref, v_ref, o_ref, qb_ref, dm_ref, s_ref, mn_ref,
                      m_ref, l_ref, acc_ref, *, tq, tk, kt_lo, kt_hi, tile_dist, active):
    _flash_init(m_ref, l_ref, acc_ref)
    for g in range(GQA):
        qb_ref[g * tq:(g + 1) * tq, :] = q_ref[:, g * HEAD_DIM:(g + 1) * HEAD_DIM].astype(bf16)
    nrep = tk // LANES
    ngrp = tq // SLC_ROWS

    def group_update(groups, k, v):
        units = [(g, r) for r in groups for g in range(GQA)]
        for g, r in units:
            rows = pl.ds(g * tq + r * SLC_ROWS, SLC_ROWS)
            t2 = (_dot_nt(qb_ref[rows, :], k) * (SCALE * LOG2E)
                  - (slopes_ref[kvh, g] * LOG2E) * dm_ref[r * SLC_ROWS:(r + 1) * SLC_ROWS, :])
            s_ref[rows, :] = t2
            mx = jnp.max(t2, axis=-1, keepdims=True)
            mn_ref[rows, :] = jnp.maximum(m_ref[rows, :], jnp.broadcast_to(mx, (SLC_ROWS, LANES)))
        for g, r in units:
            rows = pl.ds(g * tq + r * SLC_ROWS, SLC_ROWS)
            m_new = mn_ref[rows, :]
            p = jnp.exp2(s_ref[rows, :] - _lane_repeat(m_new, nrep))
            alpha = jnp.exp2(m_ref[rows, :] - m_new)
            psum = p[:, 0:LANES]
            for c in range(1, nrep):
                psum = psum + p[:, c * LANES:(c + 1) * LANES]
            l_ref[rows, :] = alpha * l_ref[rows, :] + psum
            acc_ref[rows, :] = alpha * acc_ref[rows, :] + _dot(p.astype(bf16), v)
            m_ref[rows, :] = m_new

    def tile_update(kt, acts):
        ks = pl.multiple_of(kt * tk, tk)
        k = k_ref[pl.ds(ks, tk), :].astype(bf16)
        v = v_ref[pl.ds(ks, tk), :].astype(bf16)
        dm_ref[...] = tile_dist(ks)
        if acts is None:
            group_update(tuple(range(ngrp)), k, v)
        elif ngrp == 2:
            pl.when(acts[0] & acts[1])(functools.partial(group_update, (0, 1), k, v))
            pl.when(acts[0] & jnp.logical_not(acts[1]))(functools.partial(group_update, (0,), k, v))
            pl.when(acts[1] & jnp.logical_not(acts[0]))(functools.partial(group_update, (1,), k, v))
        else:
            for r in range(ngrp):
                pl.when(acts[r])(functools.partial(group_update, (r,), k, v))

    def kt_body(kt, carry):
        if active is None:
            tile_update(kt, None)
        else:
            acts = [active(kt, r) for r in range(ngrp)]
            any_act = acts[0]
            for a in acts[1:]:
                any_act = any_act | a
            pl.when(any_act)(functools.partial(tile_update, kt, acts))
        return carry

    lax.fori_loop(kt_lo, kt_hi, kt_body, 0)
    for g in range(GQA):
        rows = pl.ds(g * tq, tq)
        o_ref[:, g * HEAD_DIM:(g + 1) * HEAD_DIM] = (
            acc_ref[rows, :] / jnp.sum(l_ref[rows, :], axis=-1, keepdims=True))


def _slc_kernel(flags_ref, slopes_ref, q_ref, k_ref, v_ref, sel_ref, e_ref, o_ref, *scratch, tq, tk, t):
    b = pl.program_id(0)
    kvh = pl.program_id(1)
    qt = pl.program_id(2)
    selb = sel_ref[0, 0].astype(bf16)
    qpos = qt * tq + lax.broadcasted_iota(jnp.int32, (tq, 1), 0)
    kofs = lax.broadcasted_iota(jnp.int32, (1, tk), 1)
    ngrp = tq // SLC_ROWS
    nkt = t // tk

    def tile_dist(ks):
        maskf = _dot(selb, e_ref[:, pl.ds(ks, tk)])
        d = qpos - (ks + kofs)
        return jnp.where((maskf > 0.5) & (d >= 0), d.astype(f32), MASK_DIST)

    def active(kt, r):
        row_group = (b * N_KV_A + kvh) * (t // SLC_ROWS) + qt * ngrp + r
        return flags_ref[row_group * nkt + kt] > 0

    _attn_prompt_body(kvh, qt, slopes_ref, q_ref, k_ref, v_ref, o_ref, *scratch, tq=tq, tk=tk,
                      kt_lo=0, kt_hi=(qt * tq + tq - 1) // tk + 1, tile_dist=tile_dist, active=active)


def _attn_scratch(tq, tk):
    return [pltpu.VMEM((GQA * tq, HEAD_DIM), bf16), pltpu.VMEM((tq, tk), f32),
            pltpu.VMEM((GQA * tq, tk), f32), pltpu.VMEM((GQA * tq, LANES), f32),
            pltpu.VMEM((GQA * tq, LANES), f32), pltpu.VMEM((GQA * tq, LANES), f32),
            pltpu.VMEM((GQA * tq, HEAD_DIM), f32)]


def _slc_prompt(slopes, qz, kv_slc, sel, expand, nbatch, t, tq, tk):
    nqt = t // tq
    gw = GQA * HEAD_DIM
    ncp = sel.shape[-1]
    bpt = tk // CMP_BLOCK
    flags = sel.reshape(nbatch, N_KV_A, t // SLC_ROWS, SLC_ROWS, ncp // bpt, bpt).max(axis=(3, 5))
    flags = (flags[..., :t // tk] > 0.5).astype(jnp.int32).reshape(-1)
    gs = pltpu.PrefetchScalarGridSpec(
        num_scalar_prefetch=1,
        grid=(nbatch, N_KV_A, nqt),
        in_specs=[
            pl.BlockSpec(memory_space=pltpu.SMEM),
            pl.BlockSpec((tq, gw), lambda b, h, i, fl: (b * nqt + i, h)),
            pl.BlockSpec((t, HEAD_DIM), lambda b, h, i, fl: (b, h)),
            pl.BlockSpec((t, HEAD_DIM), lambda b, h, i, fl: (b, N_KV_A + h)),
            pl.BlockSpec((1, 1, tq, ncp), lambda b, h, i, fl: (b, h, i, 0)),
            pl.BlockSpec((ncp, t), lambda b, h, i, fl: (0, 0)),
        ],
        out_specs=pl.BlockSpec((tq, gw), lambda b, h, i, fl: (b * nqt + i, h)),
        scratch_shapes=_attn_scratch(tq, tk),
    )
    return pl.pallas_call(
        functools.partial(_slc_kernel, tq=tq, tk=tk, t=t),
        grid_spec=gs,
        out_shape=jax.ShapeDtypeStruct((nbatch * t, D_ATT), f32),
        compiler_params=_cparams(("parallel", "parallel", "parallel")),
        name="slc_prompt",
    )(flags, slopes, qz, kv_slc, kv_slc, sel, expand)


def _swa_kernel(slopes_ref, q_ref, k_ref, v_ref, o_ref, *scratch, tq, tk):
    kvh = pl.program_id(1)
    qt = pl.program_id(2)
    qpos = qt * tq + lax.broadcasted_iota(jnp.int32, (tq, 1), 0)
    kofs = lax.broadcasted_iota(jnp.int32, (1, tk), 1)

    def tile_dist(ks):
        d = qpos - (ks + kofs)
        return jnp.where((d >= 0) & (d < WINDOW), d.astype(f32), MASK_DIST)

    _attn_prompt_body(kvh, qt, slopes_ref, q_ref, k_ref, v_ref, o_ref, *scratch, tq=tq, tk=tk,
                      kt_lo=jnp.maximum(qt * tq - (WINDOW - 1), 0) // tk, kt_hi=(qt * tq + tq - 1) // tk + 1,
                      tile_dist=tile_dist, active=None)


def _swa_prompt(slopes, qz, kv_swa, nbatch, t, tq, tk):
    nqt = t // tq
    gw = GQA * HEAD_DIM
    return pl.pallas_call(
        functools.partial(_swa_kernel, tq=tq, tk=tk),
        grid=(nbatch, N_KV_A, nqt),
        in_specs=[
            pl.BlockSpec(memory_space=pltpu.SMEM),
            pl.BlockSpec((tq, gw), lambda b, h, i: (b * nqt + i, h)),
            pl.BlockSpec((t, HEAD_DIM), lambda b, h, i: (b, h)),
            pl.BlockSpec((t, HEAD_DIM), lambda b, h, i: (b, N_KV_A + h)),
        ],
        out_specs=pl.BlockSpec((tq, gw), lambda b, h, i: (b * nqt + i, h)),
        out_shape=jax.ShapeDtypeStruct((nbatch * t, D_ATT), f32),
        scratch_shapes=_attn_scratch(tq, tk),
        compiler_params=_cparams(("parallel", "parallel", "parallel")),
        name="swa_prompt",
    )(slopes, qz, kv_swa, kv_swa)


def _softplus(x):
    return jnp.maximum(x, 0.0) + jnp.log1p(jnp.exp(-jnp.abs(x)))


def _sigmoid(x):
    return 1.0 / (1.0 + jnp.exp(-x))


def _silu(x):
    return x * _sigmoid(x)


def _dot_hi(a, b):
    return jnp.dot(a, b, precision=lax.Precision.HIGHEST, preferred_element_type=f32)


INV_BASE = 16


def _bdot(a, b):
    return _dot(a.astype(bf16), b.astype(bf16))


def _gdn_kernel(x_ref, small_ref, conv0_ref, s0_ref, cw_ref, alog_ref, dtb_ref, nw_ref,
                o_ref, s_out_ref, xp_ref, st_ref, gam_ref, n_ref, t_ref, pw_ref, uw_ref,
                k_ref, kb_ref, q_ref, qg_ref, kg_ref, qk_ref, vn_ref, rhs_ref, *, c, tv, halo):
    ci = pl.program_id(1)
    nci = pl.num_programs(1)

    @pl.when(ci == 0)
    def _():
        xp_ref[...] = jnp.zeros(xp_ref.shape, f32)
        xp_ref[halo - (CONV_W - 1):halo, :] = conv0_ref[0]
        st_ref[...] = s0_ref[0]

    xp_ref[halo:halo + tv, :] = x_ref[...]
    y = xp_ref[pl.ds(halo - (CONV_W - 1), c), :] * cw_ref[0:1, :]
    for i in range(1, CONV_W):
        y = y + xp_ref[pl.ds(halo - (CONV_W - 1) + i, c), :] * cw_ref[i:i + 1, :]
    y = _silu(y)
    xp_ref[halo - (CONV_W - 1):halo, :] = xp_ref[halo + tv - (CONV_W - 1):halo + tv, :]

    ri = lax.broadcasted_iota(jnp.int32, (c, 1), 0)
    rowi = lax.broadcasted_iota(jnp.int32, (c, c), 0)
    coli = lax.broadcasted_iota(jnp.int32, (c, c), 1)
    lower = rowi >= coli
    strict = rowi > coli
    tril = lower.astype(f32)
    eye = (rowi == coli).astype(f32)
    same_blk = {}
    s = INV_BASE
    while s <= c:
        same_blk[s] = (rowi // s) == (coli // s)
        s *= 2

    sm = small_ref[...]
    if tv < c:
        sm = jnp.concatenate([sm, jnp.zeros((c - tv, LANES), f32)], axis=0)
    g_all = -jnp.exp(alog_ref[...]) * _softplus(sm + dtb_ref[...])
    beta_all = _sigmoid(sm)
    if tv < c:
        live = ri < tv
        g_all = jnp.where(live, g_all, 0.0)
        beta_all = jnp.where(live, beta_all, 0.0)
    gc_all = _dot_hi(tril, g_all)
    gc_rows = gc_all.T
    heads = range(N_HEADS_B)
    decay = []

    for h in heads:
        yq = y[:, h * HEAD_DIM:(h + 1) * HEAD_DIM]
        yk = y[:, D_GDN + h * HEAD_DIM:D_GDN + (h + 1) * HEAD_DIM]
        v = y[:, 2 * D_GDN + h * HEAD_DIM:2 * D_GDN + (h + 1) * HEAD_DIM]
        q = yq * lax.rsqrt(jnp.sum(yq * yq, axis=-1, keepdims=True) + NORM_EPS) * SCALE
        k = yk * lax.rsqrt(jnp.sum(yk * yk, axis=-1, keepdims=True) + NORM_EPS)
        if tv < c:
            q = jnp.where(live, q, 0.0)
            k = jnp.where(live, k, 0.0)
            v = jnp.where(live, v, 0.0)
        beta = beta_all[:, B_OFF + h:B_OFF + h + 1]
        gcol = gc_all[:, A_OFF + h:A_OFF + h + 1]
        grow = gc_rows[A_OFF + h:A_OFF + h + 1, :]
        gam_ref[h] = jnp.where(lower, jnp.exp(jnp.where(lower, gcol - grow, 0.0)), 0.0)
        eg = jnp.exp(gcol)
        g_last = gc_all[c - 1:c, A_OFF + h:A_OFF + h + 1]
        decay.append(jnp.exp(g_last))
        kb = k * beta
        k_ref[h] = k.astype(bf16)
        kb_ref[h] = kb.astype(bf16)
        q_ref[h] = q.astype(bf16)
        qg_ref[h] = (q * eg).astype(bf16)
        kg_ref[h] = (k * jnp.exp(g_last - gcol)).astype(bf16)
        rhs_ref[h, :, 0:HEAD_DIM] = (v * beta).astype(bf16)
        rhs_ref[h, :, HEAD_DIM:2 * HEAD_DIM] = (kb * eg).astype(bf16)

    for h in heads:
        gam = gam_ref[h]
        nmat = jnp.where(strict, _dot_nt(kb_ref[h], k_ref[h]) * gam, 0.0)
        n_ref[h] = nmat
        nd = jnp.where(same_blk[INV_BASE], nmat, 0.0)
        t_ref[h] = eye - nd
        pw_ref[h] = nd.astype(bf16)
        qk_ref[h] = (_dot_nt(q_ref[h], k_ref[h]) * gam).astype(bf16)

    for _ in range(int(math.log2(INV_BASE)) - 1):
        for h in heads:
            pw = pw_ref[h]
            pw_ref[h] = _dot(pw, pw).astype(bf16)
        for h in heads:
            t = t_ref[h]
            t_ref[h] = t + _dot(t.astype(bf16), pw_ref[h])

    s = INV_BASE
    while s < c:
        pair = same_blk[2 * s] & jnp.logical_not(same_blk[s])
        for h in heads:
            off = jnp.where(pair, n_ref[h], 0.0)
            uw_ref[h, :, 0:HEAD_DIM] = _bdot(t_ref[h], off)
        for h in heads:
            t = t_ref[h]
            t_ref[h] = t - _bdot(uw_ref[h, :, 0:HEAD_DIM], t)
        s *= 2

    for h in heads:
        uw_ref[h] = _dot(t_ref[h].astype(bf16), rhs_ref[h])

    for h in heads:
        sb = st_ref[h].astype(bf16)
        v_new = uw_ref[h, :, 0:HEAD_DIM] - _dot(uw_ref[h, :, HEAD_DIM:2 * HEAD_DIM].astype(bf16), sb)
        vn_ref[h] = v_new.astype(bf16)
        uw_ref[h, :, 0:HEAD_DIM] = _dot(qg_ref[h], sb)
    for h in heads:
        vn = vn_ref[h]
        o = uw_ref[h, :, 0:HEAD_DIM] + _dot(qk_ref[h], vn)
        st_ref[h] = st_ref[h] * decay[h] + _dot_tn(kg_ref[h], vn)
        on = o * lax.rsqrt(jnp.mean(o * o, axis=-1, keepdims=True) + NORM_EPS) * nw_ref[...]
        o_ref[:, h * HEAD_DIM:(h + 1) * HEAD_DIM] = on[:tv]

    @pl.when(ci == nci - 1)
    def _():
        s_out_ref[0] = st_ref[...]


def _gdn(qkv_b, small, conv0, s0, conv_w, alog_row, dtb_row, gdn_norm_w, nbatch, t, c):
    tv = min(c, t)
    nci = t // tv
    halo = 8
    return pl.pallas_call(
        functools.partial(_gdn_kernel, c=c, tv=tv, halo=halo),
        grid=(nbatch, nci),
        in_specs=[
            pl.BlockSpec((tv, CONV_DIM), lambda b, i: (b * nci + i, 0)),
            pl.BlockSpec((tv, LANES), lambda b, i: (b * nci + i, 0)),
            pl.BlockSpec((1, CONV_W - 1, CONV_DIM), lambda b, i: (b, 0, 0)),
            pl.BlockSpec((1, N_HEADS_B, HEAD_DIM, HEAD_DIM), lambda b, i: (b, 0, 0, 0)),
            pl.BlockSpec((CONV_W, CONV_DIM), lambda b, i: (0, 0)),
            pl.BlockSpec((1, LANES), lambda b, i: (0, 0)),
            pl.BlockSpec((1, LANES), lambda b, i: (0, 0)),
            pl.BlockSpec((1, HEAD_DIM), lambda b, i: (0, 0)),
        ],
        out_specs=[
            pl.BlockSpec((tv, D_GDN), lambda b, i: (b * nci + i, 0)),
            pl.BlockSpec((1, N_HEADS_B, HEAD_DIM, HEAD_DIM), lambda b, i: (b, 0, 0, 0)),
        ],
        out_shape=[jax.ShapeDtypeStruct((nbatch * t, D_GDN), f32),
                   jax.ShapeDtypeStruct((nbatch, N_HEADS_B, HEAD_DIM, HEAD_DIM), f32)],
        scratch_shapes=[pltpu.VMEM((halo + c, CONV_DIM), f32),
                        pltpu.VMEM((N_HEADS_B, HEAD_DIM, HEAD_DIM), f32)]
        + [pltpu.VMEM((N_HEADS_B, c, c), f32)] * 3
        + [pltpu.VMEM((N_HEADS_B, c, c), bf16)]
        + [pltpu.VMEM((N_HEADS_B, c, 2 * HEAD_DIM), f32)]
        + [pltpu.VMEM((N_HEADS_B, c, HEAD_DIM), bf16)] * 5
        + [pltpu.VMEM((N_HEADS_B, c, c), bf16)]
        + [pltpu.VMEM((N_HEADS_B, c, HEAD_DIM), bf16)]
        + [pltpu.VMEM((N_HEADS_B, c, 2 * HEAD_DIM), bf16)],
        compiler_params=_cparams(("parallel", "arbitrary")),
        name="gdn",
    )(qkv_b, small, conv0, s0, conv_w, alog_row, dtb_row, gdn_norm_w.reshape(1, HEAD_DIM))


def _merge_kernel(x_ref, oc_ref, os_ref, ow_ref, small_ref, za_ref, zb_ref, ob_ref, w_ref, fw_ref, y_ref, mix_ref):
    gates = _sigmoid(small_ref[...])
    for h in range(N_HEADS_A):
        cs = slice(h * HEAD_DIM, (h + 1) * HEAD_DIM)
        o_a = (gates[:, G_OFF + h:G_OFF + h + 1] * oc_ref[:, cs]
               + gates[:, G_OFF + N_HEADS_A + h:G_OFF + N_HEADS_A + h + 1] * os_ref[:, cs]
               + gates[:, G_OFF + 2 * N_HEADS_A + h:G_OFF + 2 * N_HEADS_A + h + 1] * ow_ref[:, cs])
        mix_ref[:, cs] = (o_a * _silu(za_ref[:, cs])).astype(bf16)
    mix_ref[:, D_ATT:] = (ob_ref[...] * _silu(zb_ref[...])).astype(bf16)
    hres = x_ref[...] + _dot(mix_ref[...], w_ref[...])
    ms = jnp.mean(hres * hres, axis=-1, keepdims=True)
    y_ref[...] = hres * lax.rsqrt(ms + NORM_EPS) * fw_ref[...]


def _merge_out(x2d, o_cmp, o_slc, o_swa, small, qz, o_b, w_out_bf, final_norm_w, tm):
    m = x2d.shape[0]
    row = lambda i: (i, 0)
    return pl.pallas_call(
        _merge_kernel,
        grid=(m // tm,),
        in_specs=[
            pl.BlockSpec((tm, D_MODEL), row),
            pl.BlockSpec((tm, D_ATT), row),
            pl.BlockSpec((tm, D_ATT), row),
            pl.BlockSpec((tm, D_ATT), row),
            pl.BlockSpec((tm, LANES), row),
            pl.BlockSpec((tm, D_ATT), lambda i: (i, 1)),
            pl.BlockSpec((tm, D_GDN), lambda i: (i, 2)),
            pl.BlockSpec((tm, D_GDN), row),
            pl.BlockSpec((D_ATT + D_GDN, D_MODEL), lambda i: (0, 0), pipeline_mode=pl.Buffered(1)),
            pl.BlockSpec((1, D_MODEL), lambda i: (0, 0)),
        ],
        out_specs=pl.BlockSpec((tm, D_MODEL), row),
        out_shape=jax.ShapeDtypeStruct((m, D_MODEL), f32),
        scratch_shapes=[pltpu.VMEM((tm, D_ATT + D_GDN), bf16)],
        compiler_params=_cparams(("parallel",)),
        name="merge_out",
    )(x2d, o_cmp, o_slc, o_swa, small, qz, qz, o_b, w_out_bf, final_norm_w.reshape(1, D_MODEL))


def _page_copy(cache_ref, buf_ref, sem_ref, pt_ref, b, j, slot, i, pps):
    page = pt_ref[b, j * pps + i]
    return pltpu.make_async_copy(
        cache_ref.at[pl.ds(page * ROWS_PER_PAGE4, ROWS_PER_PAGE4), :],
        buf_ref.at[slot, pl.ds(i * ROWS_PER_PAGE4, ROWS_PER_PAGE4), :],
        sem_ref.at[slot])


def _paged_fetch(cache_ref, buf_ref, sem_ref, pt_ref, pps, nj_pages, need=None):
    b = pl.program_id(0)
    j = pl.program_id(1)
    nb = pl.num_programs(0)
    nj = pl.num_programs(1)
    step = b * nj_pages + jnp.minimum(j, nj_pages - 1)
    slot = step % 2

    def guarded(bb, jj, i, fn):
        if need is None:
            fn()
        else:
            pl.when(need(bb, jj, i))(fn)

    def start(bb, jj, sl):
        for i in range(pps):
            guarded(bb, jj, i, lambda i=i: _page_copy(cache_ref, buf_ref, sem_ref, pt_ref, bb, jj, sl, i, pps).start())

    @pl.when((b == 0) & (j == 0))
    def _():
        start(0, 0, 0)

    @pl.when(j < nj_pages)
    def _():
        last_j = j == nj_pages - 1
        nb_ = jnp.where(last_j, b + 1, b)
        nj_ = jnp.where(last_j, 0, j + 1)

        @pl.when(nb_ < nb)
        def _():
            start(nb_, nj_, 1 - slot)

        for i in range(pps):
            guarded(b, j, i, lambda i=i: _page_copy(cache_ref, buf_ref, sem_ref, pt_ref, b, j, slot, i, pps).wait())

    return slot


def _compress_paged_kernel(pt_ref, cache_ref, pe_ref, w_ref, o_ref, buf_ref, sem_ref, *, pps):
    slot = _paged_fetch(cache_ref, buf_ref, sem_ref, pt_ref, pps, pl.num_programs(1))
    nbk = pps * (PAGE_SIZE // CMP_BLOCK)
    stride = CMP_BLOCK * N_CH
    accs = _compress_accumulate(
        lambda l, ch: buf_ref[slot, pl.ds(l * N_CH + ch, nbk, stride=stride), :], pe_ref, w_ref, nbk)
    for c in range(2):
        for h in range(N_KV_A):
            o_ref[0, c * N_KV_A + h] = accs[c][h * nbk:(h + 1) * nbk]


def _compress_paged(page_table, cache4, pe2d, w_cmp_bf, pps):
    nbatch, n_pages = page_table.shape
    nj = n_pages // pps
    nbk = pps * (PAGE_SIZE // CMP_BLOCK)
    gs = pltpu.PrefetchScalarGridSpec(
        num_scalar_prefetch=1,
        grid=(nbatch, nj),
        in_specs=[
            pl.BlockSpec(memory_space=pl.ANY),
            pl.BlockSpec((2, CMP_BLOCK, HEAD_DIM), lambda b, j, pt: (0, 0, 0)),
            pl.BlockSpec((2, CMP_BLOCK // 2, 2 * HEAD_DIM, HEAD_DIM), lambda b, j, pt: (0, 0, 0, 0)),
        ],
        out_specs=pl.BlockSpec((1, N_CH, nbk, HEAD_DIM), lambda b, j, pt: (b, 0, j, 0)),
        scratch_shapes=[pltpu.VMEM((2, pps * ROWS_PER_PAGE4, HEAD_DIM), f32), pltpu.SemaphoreType.DMA((2,))],
    )
    return pl.pallas_call(
        functools.partial(_compress_paged_kernel, pps=pps),
        grid_spec=gs,
        out_shape=jax.ShapeDtypeStruct((nbatch, N_CH, nj * nbk, HEAD_DIM), f32),
        compiler_params=_cparams(("arbitrary", "arbitrary")),
        name="compress_paged",
    )(page_table, cache4, pe2d, w_cmp_bf)


def _two_phase_attend(qs, slope2, qpos, kpos0, load_k, load_v, load_mask, nchunks, chunk,
                      s_ref, rows, m_ref, l_ref, acc_ref, active=None):
    nrow = qs.shape[0]
    nrep = chunk // LANES
    kofs = lax.broadcasted_iota(jnp.int32, (1, chunk), 1)
    unroll = math.gcd(nchunks, 4)

    def cols(ci):
        return pl.ds(pl.multiple_of(ci * chunk, chunk), chunk)

    def skippable(ci, fn, carry):
        if active is None:
            return fn(carry)
        return lax.cond(active(ci), fn, lambda c: c, carry)

    def phase1(ci, mrun):
        def update(mrun):
            d = qpos - (kpos0 + ci * chunk + kofs)
            dmask = jnp.where((load_mask(ci) > 0.5) & (d >= 0), d.astype(f32), MASK_DIST)
            t2 = _dot_nt(qs, load_k(ci)) * (SCALE * LOG2E) - slope2 * dmask
            s_ref[rows, cols(ci)] = t2
            for c in range(nrep):
                mrun = jnp.maximum(mrun, t2[:, c * LANES:(c + 1) * LANES])
            return mrun

        return skippable(ci, update, mrun)

    mrun = lax.fori_loop(0, nchunks, phase1, jnp.full((nrow, LANES), NEG_INF, f32), unroll=unroll)
    m_prev = m_ref[rows, :]
    m_new = jnp.maximum(m_prev, jnp.broadcast_to(jnp.max(mrun, axis=-1, keepdims=True), (nrow, LANES)))

    def phase2(ci, carry):
        def update(carry):
            lp, acc = carry
            p = jnp.exp2(s_ref[rows, cols(ci)] - _lane_repeat(m_new, nrep))
            for c in range(nrep):
                lp = lp + p[:, c * LANES:(c + 1) * LANES]
            return lp, acc + _dot(p.astype(bf16), load_v(ci))

        return skippable(ci, update, carry)

    z = jnp.zeros((nrow, LANES), f32)
    lp, acc = lax.fori_loop(0, nchunks, phase2, (z, z), unroll=unroll)
    alpha = jnp.exp2(m_prev - m_new)
    l_ref[rows, :] = alpha * l_ref[rows, :] + lp
    acc_ref[rows, :] = alpha * acc_ref[rows, :] + acc
    m_ref[rows, :] = m_new


def _slc_paged_kernel(pt_ref, chunk_any_ref, chunk_kvh_ref, slopes_ref, cache_ref, q_ref, tail_ref, sel_ref,
                      e_ref, o_ref, buf_ref, sem_ref, s_ref, m_ref, l_ref, acc_ref, *, pps, t, sub, pos0):
    b = pl.program_id(0)
    j = pl.program_id(1)
    nj_pages = pl.num_programs(1) - 1
    ppc = sub // PAGE_SIZE
    cps = pps // ppc
    slot = _paged_fetch(cache_ref, buf_ref, sem_ref, pt_ref, pps, nj_pages,
                        need=lambda bb, jj, i: chunk_any_ref[bb, jj * cps + i // ppc] > 0)
    nrow = GQA * t

    @pl.when(j == 0)
    def _():
        _flash_init(m_ref, l_ref, acc_ref)

    tpos = lax.broadcasted_iota(jnp.int32, (t, 1), 0)
    qpos = pos0 + jnp.concatenate([tpos] * GQA, axis=0)

    def attend(kvh, load_k, load_v, nchunks, chunk, kpos0, active=None):
        q = q_ref[:, kvh * GQA * HEAD_DIM:(kvh + 1) * GQA * HEAD_DIM]
        qs = jnp.concatenate([q[:, g * HEAD_DIM:(g + 1) * HEAD_DIM] for g in range(GQA)], axis=0).astype(bf16)
        slope2 = jnp.concatenate([jnp.full((t, 1), slopes_ref[kvh, g] * LOG2E, f32) for g in range(GQA)], axis=0)
        selb = sel_ref[0, kvh].astype(bf16)

        def load_mask(ci):
            maskf = _dot(selb, e_ref[:, pl.ds(pl.multiple_of(ci * chunk, chunk), chunk)])
            return jnp.concatenate([maskf] * GQA, axis=0)

        _two_phase_attend(qs, slope2, qpos, kpos0, load_k, load_v, load_mask, nchunks, chunk,
                          s_ref, pl.ds(kvh * nrow, nrow), m_ref, l_ref, acc_ref, active)

    @pl.when(j < nj_pages)
    def _():
        for kvh in range(N_KV_A):
            def load(ci, ch):
                return buf_ref[slot, pl.ds(ci * (sub * N_CH) + ch, sub, stride=N_CH), :].astype(bf16)

            first = ((b * N_KV_A + kvh) * nj_pages + j) * cps
            attend(kvh, functools.partial(load, ch=kvh), functools.partial(load, ch=N_KV_A + kvh),
                   cps, sub, j * pps * PAGE_SIZE, lambda ci, first=first: chunk_kvh_ref[first + ci] > 0)

    @pl.when(j == nj_pages)
    def _():
        ntail = tail_ref.shape[1]
        for kvh in range(N_KV_A):
            k = tail_ref[0, :, kvh * HEAD_DIM:(kvh + 1) * HEAD_DIM].astype(bf16)
            v = tail_ref[0, :, (N_KV_A + kvh) * HEAD_DIM:(N_KV_A + kvh + 1) * HEAD_DIM].astype(bf16)
            attend(kvh, lambda ci, k=k: k, lambda ci, v=v: v, 1, ntail, nj_pages * pps * PAGE_SIZE)
        for kvh in range(N_KV_A):
            for g in range(GQA):
                r = pl.ds(kvh * nrow + g * t, t)
                o_ref[:, (kvh * GQA + g) * HEAD_DIM:(kvh * GQA + g + 1) * HEAD_DIM] = (
                    acc_ref[r, :] / jnp.sum(l_ref[r, :], axis=-1, keepdims=True))


def _slc_paged(page_table, slopes, cache4, qz, tail, sel, expand, t, pps, pos0):
    nbatch, n_pages = page_table.shape
    njp = n_pages // pps
    bps = pps * (PAGE_SIZE // CMP_BLOCK)
    sub = min(512, pps * PAGE_SIZE)
    bpc = sub // CMP_BLOCK
    n_pb = n_pages * (PAGE_SIZE // CMP_BLOCK)
    chunk_sel = sel[..., :n_pb].reshape(nbatch, N_KV_A, t, n_pb // bpc, bpc).max(axis=(2, 4)) > 0.5
    chunk_kvh = chunk_sel.astype(jnp.int32).reshape(-1)
    chunk_any = chunk_sel.any(axis=1).astype(jnp.int32)
    gs = pltpu.PrefetchScalarGridSpec(
        num_scalar_prefetch=3,
        grid=(nbatch, njp + 1),
        in_specs=[
            pl.BlockSpec(memory_space=pltpu.SMEM),
            pl.BlockSpec(memory_space=pl.ANY),
            pl.BlockSpec((t, D_ATT), lambda b, j, *_: (b, 0)),
            pl.BlockSpec((1, LANES, KV_BRANCH), lambda b, j, *_: (b, 0, 0)),
            pl.BlockSpec((1, N_KV_A, t, bps), lambda b, j, *_: (b, 0, 0, j)),
            pl.BlockSpec((bps, pps * PAGE_SIZE), lambda b, j, *_: (0, 0)),
        ],
        out_specs=pl.BlockSpec((t, D_ATT), lambda b, j, *_: (b, 0)),
        scratch_shapes=[pltpu.VMEM((2, pps * ROWS_PER_PAGE4, HEAD_DIM), f32), pltpu.SemaphoreType.DMA((2,)),
                        pltpu.VMEM((N_KV_A * GQA * t, pps * PAGE_SIZE), f32),
                        pltpu.VMEM((N_KV_A * GQA * t, LANES), f32), pltpu.VMEM((N_KV_A * GQA * t, LANES), f32),
                        pltpu.VMEM((N_KV_A * GQA * t, HEAD_DIM), f32)],
    )
    return pl.pallas_call(
        functools.partial(_slc_paged_kernel, pps=pps, t=t, sub=sub, pos0=pos0),
        grid_spec=gs,
        out_shape=jax.ShapeDtypeStruct((nbatch * t, D_ATT), f32),
        compiler_params=_cparams(("arbitrary", "arbitrary")),
        name="slc_paged",
    )(page_table, chunk_any, chunk_kvh, slopes, cache4, qz, tail, sel, expand)


def _swa_sample_kernel(slopes_ref, q_ref, win_ref, tail_ref, o_ref, *, t, wbuf, pos0):
    tpos = lax.broadcasted_iota(jnp.int32, (t, 1), 0)
    qpos = pos0 + jnp.concatenate([tpos] * GQA, axis=0)
    nk = wbuf + CMP_BLOCK
    kpos = pos0 - wbuf + lax.broadcasted_iota(jnp.int32, (1, nk), 1)
    d = qpos - kpos
    ok = (d >= 0) & (d < WINDOW)
    df = d.astype(f32)
    for kvh in range(N_KV_A):
        k = jnp.concatenate([win_ref[pl.ds(kvh, wbuf, stride=N_CH), :],
                             tail_ref[0, :, kvh * HEAD_DIM:(kvh + 1) * HEAD_DIM]], axis=0).astype(bf16)
        v = jnp.concatenate([win_ref[pl.ds(N_KV_A + kvh, wbuf, stride=N_CH), :],
                             tail_ref[0, :, (N_KV_A + kvh) * HEAD_DIM:(N_KV_A + kvh + 1) * HEAD_DIM]],
                            axis=0).astype(bf16)
        q = q_ref[:, kvh * GQA * HEAD_DIM:(kvh + 1) * GQA * HEAD_DIM]
        qs = jnp.concatenate([q[:, g * HEAD_DIM:(g + 1) * HEAD_DIM] for g in range(GQA)], axis=0).astype(bf16)
        slope = jnp.concatenate([jnp.full((t, 1), slopes_ref[kvh, g], f32) for g in range(GQA)], axis=0)
        s = _dot_nt(qs, k) * SCALE - slope * df
        s = jnp.where(ok, s, NEG_INF)
        e = jnp.exp(s - jnp.max(s, axis=-1, keepdims=True))
        p = jnp.where(ok, e / jnp.sum(e, axis=-1, keepdims=True), 0.0)
        o = _dot(p.astype(bf16), v)
        for g in range(GQA):
            o_ref[:, (kvh * GQA + g) * HEAD_DIM:(kvh * GQA + g + 1) * HEAD_DIM] = o[g * t:(g + 1) * t]


def _swa_sample(slopes, qz, win4, tail, nbatch, t, wbuf, pos0):
    return pl.pallas_call(
        functools.partial(_swa_sample_kernel, t=t, wbuf=wbuf, pos0=pos0),
        grid=(nbatch,),
        in_specs=[
            pl.BlockSpec(memory_space=pltpu.SMEM),
            pl.BlockSpec((t, D_ATT), lambda b: (b, 0)),
            pl.BlockSpec((wbuf * N_CH, HEAD_DIM), lambda b: (b, 0)),
            pl.BlockSpec((1, CMP_BLOCK, KV_BRANCH), lambda b: (b, 0, 0)),
        ],
        out_specs=pl.BlockSpec((t, D_ATT), lambda b: (b, 0)),
        out_shape=jax.ShapeDtypeStruct((nbatch * t, D_ATT), f32),
        compiler_params=_cparams(("parallel",)),
        name="swa_sample",
    )(slopes, qz, win4, tail)


def _round_up(a, b):
    return -(-a // b) * b


def _pad_blocks(kcv, ncp):
    return jnp.pad(kcv, ((0, 0), (0, 0), (0, ncp - kcv.shape[2]), (0, 0)))


def _expand_matrix(nblocks, nkeys):
    return (jnp.arange(nkeys)[None, :] // CMP_BLOCK == jnp.arange(nblocks)[:, None]).astype(bf16)


def _kv6(rows2d, nbatch, t):
    return rows2d.reshape(nbatch, t, 2, N_KV_A, HEAD_DIM)


def _layer_prompt(x, prm):
    nbatch, t, _ = x.shape
    m = nbatch * t
    x2d = x.reshape(m, D_MODEL)
    tm = min(1024, m)
    qz, qkvb, kv_cmp, kv_slc, kv_swa, small = _in_proj(x2d, prm["norm_w"], prm["w_main"], prm["w_small"], tm)
    nc = t // CMP_BLOCK
    kcv = _compress(kv_cmp, nbatch, prm["pe2d"], prm["w_cmp"], min(64, nc))
    kcv = _pad_blocks(kcv, _round_up(nc, LANES))
    tq = min(256, t)
    o_cmp, sel = _cmp_attend(prm["slopes"], qz, kcv, nbatch, t, tq, 0, min(N_SELECT, nc))
    tk = min(256, t)
    o_slc = _slc_prompt(prm["slopes"], qz, kv_slc, sel, _expand_matrix(sel.shape[-1], t), nbatch, t, tq, tk)
    o_swa = _swa_prompt(prm["slopes"], qz, kv_swa, nbatch, t, tq, tk)
    conv0 = jnp.zeros((nbatch, CONV_W - 1, CONV_DIM), f32)
    s0 = jnp.zeros((nbatch, N_HEADS_B, HEAD_DIM, HEAD_DIM), f32)
    o_b, s_new = _gdn(qkvb, small, conv0, s0, prm["conv_w"], prm["alog_row"], prm["dtb_row"],
                      prm["gdn_norm_w"], nbatch, t, 128)
    y = _merge_out(x2d, o_cmp, o_slc, o_swa, small, qz, o_b, prm["w_out"], prm["final_norm_w"], min(512, m))
    w = min(WINDOW, t)
    conv_new = qkvb.reshape(nbatch, t, CONV_DIM)[:, t - (CONV_W - 1):]
    return (y.reshape(nbatch, t, D_MODEL), _kv6(kv_cmp, nbatch, t), _kv6(kv_slc, nbatch, t),
            _kv6(kv_swa, nbatch, t)[:, t - w:], conv_new, s_new)


def _layer_sample(x, cache_cmp, cache_slc, cache_swa, conv_buf, s0, page_table, prm):
    nbatch, t, _ = x.shape
    m = nbatch * t
    n_pages = page_table.shape[1]
    past_len = n_pages * PAGE_SIZE
    wbuf = cache_swa.shape[1]
    x2d = x.reshape(m, D_MODEL)
    qz, qkvb, kv_cmp, kv_slc, kv_swa, small = _in_proj(x2d, prm["norm_w"], prm["w_main"], prm["w_small"], m)
    pad_tail = lambda r: jnp.pad(r.reshape(nbatch, t, KV_BRANCH), ((0, 0), (0, CMP_BLOCK - t), (0, 0)))
    pps = min(32, n_pages)
    kc_past = _compress_paged(page_table, cache_cmp.reshape(-1, HEAD_DIM), prm["pe2d"], prm["w_cmp"], pps)
    kc_tail = _compress(pad_tail(kv_cmp).reshape(nbatch * CMP_BLOCK, KV_BRANCH), 1, prm["pe2d"], prm["w_cmp"], nbatch)
    kc_tail = jnp.swapaxes(kc_tail[0], 0, 1)[:, :, None, :]
    n_pb = past_len // CMP_BLOCK
    pps_s = min(64, n_pages)
    bps = pps_s * (PAGE_SIZE // CMP_BLOCK)
    ncp = _round_up(n_pb + 1, max(LANES, bps))
    kcv = _pad_blocks(jnp.concatenate([kc_past, kc_tail], axis=2), ncp)
    o_cmp, sel = _cmp_attend(prm["slopes"], qz, kcv, nbatch, t, t, past_len, min(N_SELECT, n_pb + 1))
    o_slc = _slc_paged(page_table, prm["slopes"], cache_slc.reshape(-1, HEAD_DIM), qz,
                       jnp.pad(kv_slc.reshape(nbatch, t, KV_BRANCH), ((0, 0), (0, LANES - t), (0, 0))), sel,
                       _expand_matrix(bps, pps_s * PAGE_SIZE), t, pps_s, past_len)
    o_swa = _swa_sample(prm["slopes"], qz, cache_swa.reshape(-1, HEAD_DIM), pad_tail(kv_swa), nbatch, t, wbuf, past_len)
    o_b, s_new = _gdn(qkvb, small, conv_buf, s0, prm["conv_w"], prm["alog_row"], prm["dtb_row"],
                      prm["gdn_norm_w"], nbatch, t, 128)
    y = _merge_out(x2d, o_cmp, o_slc, o_swa, small, qz, o_b, prm["w_out"], prm["final_norm_w"], min(512, m))
    kv_win = jnp.concatenate([cache_swa, _kv6(kv_swa, nbatch, t)], axis=1)[:, t:]
    conv_new = jnp.concatenate([conv_buf, qkvb.reshape(nbatch, t, CONV_DIM)], axis=1)[:, t:]
    return (y.reshape(nbatch, t, D_MODEL), _kv6(kv_cmp, nbatch, t), _kv6(kv_slc, nbatch, t), kv_win, conv_new, s_new)


def kernel(x_prompt, x_sample, cache_cmp, cache_slc, cache_swa, state_conv, state_gdn, page_table,
           norm_w, w_in, pe_cmp, w_cmp, conv_w, a_log, dt_bias, gdn_norm_w, w_out, final_norm_w):
    depth = norm_w.shape[0]
    assert depth == 1, "the final norm is fused into the single layer's output projection"
    l = 0
    w_main, w_small = _prep_w_in(w_in[l])
    head = jnp.arange(1, N_HEADS_A + 1, dtype=f32)
    lane_row = lambda vals: jnp.zeros((1, LANES), f32).at[0, A_OFF:A_OFF + N_HEADS_B].set(vals.astype(f32))
    prm = {
        "norm_w": norm_w[l], "w_main": w_main, "w_small": w_small,
        "pe2d": jnp.swapaxes(pe_cmp[l], 0, 1),
        "w_cmp": jnp.swapaxes(w_cmp[l], 0, 1).reshape(2, CMP_BLOCK // 2, 2 * HEAD_DIM, HEAD_DIM).astype(bf16),
        "conv_w": conv_w[l], "alog_row": lane_row(a_log[l]), "dtb_row": lane_row(dt_bias[l]),
        "gdn_norm_w": gdn_norm_w[l], "w_out": w_out[l].astype(bf16), "final_norm_w": final_norm_w,
        "slopes": jnp.exp2(-8.0 * head / N_HEADS_A).reshape(N_KV_A, GQA),
    }
    yp, p_cmp, p_slc, p_swa, p_conv, p_gdn = _layer_prompt(x_prompt, prm)
    ys, s_cmp, s_slc, s_swa, s_conv, s_gdn = _layer_sample(
        x_sample, cache_cmp[l], cache_slc[l], cache_swa[l], state_conv[l], state_gdn[l], page_table, prm)
    st = lambda a: a[None]
    return (yp, ys, st(p_cmp), st(p_slc), st(p_swa), st(p_conv), st(p_gdn),
            st(s_cmp), st(s_slc), st(s_swa), st(s_conv), st(s_gdn))
```

```python
import functools
import math

import jax
import jax.numpy as jnp
from jax import lax
from jax.experimental import pallas as pl
from jax.experimental.pallas import tpu as pltpu

f32 = jnp.float32
bf16 = jnp.bfloat16

D_MODEL = 2048
HEAD_DIM = 128
N_HEADS_A = 8
N_KV_A = 2
GQA = 4
D_ATT = N_HEADS_A * HEAD_DIM
CMP_BLOCK = 64
N_SELECT = 16
WINDOW = 512
N_HEADS_B = 8
D_GDN = N_HEADS_B * HEAD_DIM
CONV_W = 4
CONV_DIM = 3 * D_GDN
PAGE_SIZE = 128
NORM_EPS = 1e-6
NEG_INF = -1e30
FORCE_SCORE = 1e4
KV_BRANCH = 2 * N_KV_A * HEAD_DIM
N_CH = 2 * N_KV_A
ROWS_PER_PAGE4 = PAGE_SIZE * N_CH
SCALE = HEAD_DIM ** -0.5
G_OFF, A_OFF, B_OFF = 0, 3 * N_HEADS_A, 3 * N_HEADS_A + N_HEADS_B
LANES = 128
VMEM_LIMIT = 56 * 1024 * 1024

IN_TN = 512
N_QZ_TILES = 3 * D_ATT // IN_TN
N_B_TILES = CONV_DIM // IN_TN
N_IN_TILES = N_QZ_TILES + N_B_TILES + 3


def _cparams(sem):
    return pltpu.CompilerParams(dimension_semantics=sem, vmem_limit_bytes=VMEM_LIMIT)


def _dot(a, b):
    return jnp.dot(a, b, preferred_element_type=f32)


def _dot_nt(a, b):
    return lax.dot_general(a, b, (((1,), (1,)), ((), ())), preferred_element_type=f32)


def _dot_tn(a, b):
    return lax.dot_general(a, b, (((0,), (0,)), ((), ())), preferred_element_type=f32)


def _in_proj_kernel(x_ref, nw_ref, w_ref, ws_ref, qz_ref, b_ref, cmp_ref, slc_ref, swa_ref, small_ref, xn_ref):
    n = pl.program_id(1)

    @pl.when(n == 0)
    def _():
        x = x_ref[...]
        ms = jnp.mean(x * x, axis=-1, keepdims=True)
        xn = (x * lax.rsqrt(ms + NORM_EPS) * nw_ref[...]).astype(bf16)
        xn_ref[...] = xn
        small_ref[...] = _dot(xn, ws_ref[...])

    acc = _dot(xn_ref[...], w_ref[...])

    @pl.when(n < N_QZ_TILES)
    def _():
        qz_ref[...] = acc

    @pl.when((n >= N_QZ_TILES) & (n < N_QZ_TILES + N_B_TILES))
    def _():
        b_ref[...] = acc

    @pl.when(n == N_QZ_TILES + N_B_TILES)
    def _():
        cmp_ref[...] = acc

    @pl.when(n == N_QZ_TILES + N_B_TILES + 1)
    def _():
        slc_ref[...] = acc

    @pl.when(n == N_QZ_TILES + N_B_TILES + 2)
    def _():
        swa_ref[...] = acc


def _in_proj(x2d, norm_w, w_main, w_small, tm):
    m = x2d.shape[0]
    S = jax.ShapeDtypeStruct
    nb0 = N_QZ_TILES
    return pl.pallas_call(
        _in_proj_kernel,
        grid=(m // tm, N_IN_TILES),
        in_specs=[
            pl.BlockSpec((tm, D_MODEL), lambda i, n: (i, 0)),
            pl.BlockSpec((1, D_MODEL), lambda i, n: (0, 0)),
            pl.BlockSpec((D_MODEL, IN_TN), lambda i, n: (0, n)),
            pl.BlockSpec((D_MODEL, LANES), lambda i, n: (0, 0)),
        ],
        out_specs=[
            pl.BlockSpec((tm, IN_TN), lambda i, n: (i, jnp.minimum(n, nb0 - 1))),
            pl.BlockSpec((tm, IN_TN), lambda i, n: (i, jnp.clip(n - nb0, 0, N_B_TILES - 1))),
            pl.BlockSpec((tm, KV_BRANCH), lambda i, n: (i, 0)),
            pl.BlockSpec((tm, KV_BRANCH), lambda i, n: (i, 0)),
            pl.BlockSpec((tm, KV_BRANCH), lambda i, n: (i, 0)),
            pl.BlockSpec((tm, LANES), lambda i, n: (i, 0)),
        ],
        out_shape=[S((m, 3 * D_ATT), f32), S((m, CONV_DIM), f32), S((m, KV_BRANCH), f32),
                   S((m, KV_BRANCH), f32), S((m, KV_BRANCH), f32), S((m, LANES), f32)],
        scratch_shapes=[pltpu.VMEM((tm, D_MODEL), bf16)],
        compiler_params=_cparams(("parallel", "arbitrary")),
        name="in_proj",
    )(x2d, norm_w.reshape(1, D_MODEL), w_main, w_small)


def _prep_w_in(w_in):
    o = [0]
    for s in (D_ATT, 3 * KV_BRANCH, 3 * N_HEADS_A, D_ATT, CONV_DIM, N_HEADS_B, N_HEADS_B, D_GDN):
        o.append(o[-1] + s)
    q_a, kv_a, g_a, z_a, qkv_b, a_b, b_b, z_b = (w_in[:, o[i]:o[i + 1]] for i in range(8))
    w_main = jnp.concatenate([q_a, z_a, z_b, qkv_b, kv_a], axis=1).astype(bf16)
    pad = jnp.zeros((w_in.shape[0], LANES - (3 * N_HEADS_A + 2 * N_HEADS_B)), w_in.dtype)
    w_small = jnp.concatenate([g_a, a_b, b_b, pad], axis=1).astype(bf16)
    return w_main, w_small


CMP_UNROLL = 4


def _compress_accumulate(load_rows, pe_ref, w_ref, nbk):
    def body(i, accs):
        new = []
        for c in range(2):
            halves = []
            for dl in range(2):
                l = 2 * i + dl
                pe_row = pe_ref[c, pl.ds(l, 1), :]
                halves.append(jnp.concatenate(
                    [load_rows(l, c * N_KV_A + h) + pe_row for h in range(N_KV_A)], axis=0))
            lhs = jnp.concatenate(halves, axis=1).astype(bf16)
            new.append(accs[c] + _dot(lhs, w_ref[c, i]))
        return tuple(new)

    z = jnp.zeros((N_KV_A * nbk, HEAD_DIM), f32)
    return lax.fori_loop(0, CMP_BLOCK // 2, body, (z, z), unroll=CMP_UNROLL)


def _compress_kernel(x0, x1, x2, x3, pe_ref, w_ref, o_ref, *, nbk):
    xs = (x0, x1, x2, x3)
    accs = _compress_accumulate(lambda l, ch: xs[ch][pl.ds(l, nbk, stride=CMP_BLOCK), :], pe_ref, w_ref, nbk)
    for c in range(2):
        for h in range(N_KV_A):
            o_ref[0, c * N_KV_A + h] = accs[c][h * nbk:(h + 1) * nbk]


def _compress(rows2d, nbatch, pe2d, w_cmp_bf, nbk):
    nblk = rows2d.shape[0] // (nbatch * CMP_BLOCK)
    nj = nblk // nbk
    rows = nbk * CMP_BLOCK
    xspec = [pl.BlockSpec((rows, HEAD_DIM), functools.partial(lambda b, j, ch: (b * nj + j, ch), ch=ch))
             for ch in range(N_CH)]
    return pl.pallas_call(
        functools.partial(_compress_kernel, nbk=nbk),
        grid=(nbatch, nj),
        in_specs=xspec + [
            pl.BlockSpec((2, CMP_BLOCK, HEAD_DIM), lambda b, j: (0, 0, 0)),
            pl.BlockSpec((2, CMP_BLOCK // 2, 2 * HEAD_DIM, HEAD_DIM), lambda b, j: (0, 0, 0, 0)),
        ],
        out_specs=pl.BlockSpec((1, N_CH, nbk, HEAD_DIM), lambda b, j: (b, 0, j, 0)),
        out_shape=jax.ShapeDtypeStruct((nbatch, N_CH, nblk, HEAD_DIM), f32),
        compiler_params=_cparams(("parallel", "parallel")),
        name="compress",
    )(rows2d, rows2d, rows2d, rows2d, pe2d, w_cmp_bf)


def _cmp_kernel(slopes_ref, q_ref, kc_ref, vc_ref, o_ref, sel_ref, *, tq, tqp, ncp, pos0, nsel):
    kvh = pl.program_id(1)
    qt = pl.program_id(2)
    q = q_ref[...]
    if tqp > tq:
        q = jnp.concatenate([q, jnp.zeros((tqp - tq, q.shape[1]), f32)], axis=0)
    kc = kc_ref[0, 0].astype(bf16)
    vc = vc_ref[0, 0].astype(bf16)
    qpos = pos0 + qt * tq + lax.broadcasted_iota(jnp.int32, (1, tqp), 1)
    jblk = lax.broadcasted_iota(jnp.int32, (ncp, 1), 0)
    d = qpos - ((jblk + 1) * CMP_BLOCK - 1)
    ok = d >= 0
    df = d.astype(f32)
    imp = jnp.zeros((ncp, tqp), f32)
    for g in range(GQA):
        qg = q[:, g * HEAD_DIM:(g + 1) * HEAD_DIM].astype(bf16)
        s = _dot_nt(kc, qg) * SCALE
        s = jnp.where(ok, s - slopes_ref[kvh, g] * df, NEG_INF)
        mx = jnp.max(s, axis=0, keepdims=True)
        e = jnp.exp(s - mx)
        p = jnp.where(ok, e / jnp.sum(e, axis=0, keepdims=True), 0.0)
        imp = imp + p
        og = _dot_tn(p.astype(bf16), vc)
        o_ref[:, g * HEAD_DIM:(g + 1) * HEAD_DIM] = og[:tq]
    cur = qpos // CMP_BLOCK
    forced = (jblk == cur) | (jblk == 0)
    score = jnp.where(forced, FORCE_SCORE, jnp.where(jblk <= cur, imp, -1.0))
    for cg in range(tqp // LANES):
        sc = score[:, cg * LANES:(cg + 1) * LANES]
        work = sc
        picked = jnp.zeros((ncp, LANES), f32)
        for _ in range(nsel):
            mx = jnp.max(work, axis=0, keepdims=True)
            first = jnp.min(jnp.where(work == mx, jblk, ncp), axis=0, keepdims=True)
            hit = jblk == first
            picked = jnp.where(hit, 1.0, picked)
            work = jnp.where(hit, -2.0, work)
        sel = jnp.where(sc >= 0.0, picked, 0.0)
        lo = cg * LANES
        hi = min(tq, lo + LANES)
        sel_ref[0, 0, lo:hi, :] = sel.T[:hi - lo]


def _cmp_attend(slopes, qz, kcv, nbatch, t, tq, pos0, nsel):
    ncp = kcv.shape[2]
    tqp = max(tq, LANES)
    nqt = t // tq
    gw = GQA * HEAD_DIM
    return pl.pallas_call(
        functools.partial(_cmp_kernel, tq=tq, tqp=tqp, ncp=ncp, pos0=pos0, nsel=nsel),
        grid=(nbatch, N_KV_A, nqt),
        in_specs=[
            pl.BlockSpec(memory_space=pltpu.SMEM),
            pl.BlockSpec((tq, gw), lambda b, h, i: (b * nqt + i, h)),
            pl.BlockSpec((1, 1, ncp, HEAD_DIM), lambda b, h, i: (b, h, 0, 0)),
            pl.BlockSpec((1, 1, ncp, HEAD_DIM), lambda b, h, i: (b, N_KV_A + h, 0, 0)),
        ],
        out_specs=[
            pl.BlockSpec((tq, gw), lambda b, h, i: (b * nqt + i, h)),
            pl.BlockSpec((1, 1, tq, ncp), lambda b, h, i: (b, h, i, 0)),
        ],
        out_shape=[jax.ShapeDtypeStruct((nbatch * t, D_ATT), f32),
                   jax.ShapeDtypeStruct((nbatch, N_KV_A, t, ncp), f32)],
        compiler_params=_cparams(("parallel", "parallel", "parallel")),
        name="cmp_attend",
    )(slopes, qz, kcv, kcv)


def _lane_repeat(x, n):
    return x if n == 1 else jnp.concatenate([x] * n, axis=1)


def _flash_init(m_ref, l_ref, acc_ref):
    m_ref[...] = jnp.full(m_ref.shape, NEG_INF, f32)
    l_ref[...] = jnp.zeros(l_ref.shape, f32)
    acc_ref[...] = jnp.zeros(acc_ref.shape, f32)


LOG2E = 1.4426950408889634
MASK_DIST = 1e33
SLC_ROWS = 128


def _attn_prompt_body(kvh, qt, slopes_ref, q_ref, k_ref, v_ref, o_ref, qb_ref, dm_ref, s_ref, mn_ref,
                      m_ref, l_ref, acc_ref, *, tq, tk, kt_lo, kt_hi, tile_dist, active):
    _flash_init(m_ref, l_ref, acc_ref)
    for g in range(GQA):
        qb_ref[g * tq:(g + 1) * tq, :] = q_ref[:, g * HEAD_DIM:(g + 1) * HEAD_DIM].astype(bf16)
    nrep = tk // LANES
    ngrp = tq // SLC_ROWS

    def group_update(groups, k, v):
        units = [(g, r) for r in groups for g in range(GQA)]
        for g, r in units:
            rows = pl.ds(g * tq + r * SLC_ROWS, SLC_ROWS)
            t2 = (_dot_nt(qb_ref[rows, :], k) * (SCALE * LOG2E)
                  - (slopes_ref[kvh, g] * LOG2E) * dm_ref[r * SLC_ROWS:(r + 1) * SLC_ROWS, :])
            s_ref[rows, :] = t2
            mx = jnp.max(t2, axis=-1, keepdims=True)
            mn_ref[rows, :] = jnp.maximum(m_ref[rows, :], jnp.broadcast_to(mx, (SLC_ROWS, LANES)))
        for g, r in units:
            rows = pl.ds(g * tq + r * SLC_ROWS, SLC_ROWS)
            m_new = mn_ref[rows, :]
            p = jnp.exp2(s_ref[rows, :] - _lane_repeat(m_new, nrep))
            alpha = jnp.exp2(m_ref[rows, :] - m_new)
            psum = p[:, 0:LANES]
            for c in range(1, nrep):
                psum = psum + p[:, c * LANES:(c + 1) * LANES]
            l_ref[rows, :] = alpha * l_ref[rows, :] + psum
            acc_ref[rows, :] = alpha * acc_ref[rows, :] + _dot(p.astype(bf16), v)
            m_ref[rows, :] = m_new

    def tile_update(kt, acts):
        ks = pl.multiple_of(kt * tk, tk)
        k = k_ref[pl.ds(ks, tk), :].astype(bf16)
        v = v_ref[pl.ds(ks, tk), :].astype(bf16)
        dm_ref[...] = tile_dist(ks)
        if acts is None:
            group_update(tuple(range(ngrp)), k, v)
        elif ngrp == 2:
            pl.when(acts[0] & acts[1])(functools.partial(group_update, (0, 1), k, v))
            pl.when(acts[0] & jnp.logical_not(acts[1]))(functools.partial(group_update, (0,), k, v))
            pl.when(acts[1] & jnp.logical_not(acts[0]))(functools.partial(group_update, (1,), k, v))
        else:
            for r in range(ngrp):
                pl.when(acts[r])(functools.partial(group_update, (r,), k, v))

    def kt_body(kt, carry):
        if active is None:
            tile_update(kt, None)
        else:
            acts = [active(kt, r) for r in range(ngrp)]
            any_act = acts[0]
            for a in acts[1:]:
                any_act = any_act | a
            pl.when(any_act)(functools.partial(tile_update, kt, acts))
        return carry

    lax.fori_loop(kt_lo, kt_hi, kt_body, 0)
    for g in range(GQA):
        rows = pl.ds(g * tq, tq)
        o_ref[:, g * HEAD_DIM:(g + 1) * HEAD_DIM] = (
            acc_ref[rows, :] / jnp.sum(l_ref[rows, :], axis=-1, keepdims=True))


def _slc_kernel(flags_ref, slopes_ref, q_ref, k_ref, v_ref, sel_ref, e_ref, o_ref, *scratch, tq, tk, t):
    b = pl.program_id(0)
    kvh = pl.program_id(1)
    qt = pl.program_id(2)
    selb = sel_ref[0, 0].astype(bf16)
    qpos = qt * tq + lax.broadcasted_iota(jnp.int32, (tq, 1), 0)
    kofs = lax.broadcasted_iota(jnp.int32, (1, tk), 1)
    ngrp = tq // SLC_ROWS
    nkt = t // tk

    def tile_dist(ks):
        maskf = _dot(selb, e_ref[:, pl.ds(ks, tk)])
        d = qpos - (ks + kofs)
        return jnp.where((maskf > 0.5) & (d >= 0), d.astype(f32), MASK_DIST)

    def active(kt, r):
        row_group = (b * N_KV_A + kvh) * (t // SLC_ROWS) + qt * ngrp + r
        return flags_ref[row_group * nkt + kt] > 0

    _attn_prompt_body(kvh, qt, slopes_ref, q_ref, k_ref, v_ref, o_ref, *scratch, tq=tq, tk=tk,
                      kt_lo=0, kt_hi=(qt * tq + tq - 1) // tk + 1, tile_dist=tile_dist, active=active)


def _attn_scratch(tq, tk):
    return [pltpu.VMEM((GQA * tq, HEAD_DIM), bf16), pltpu.VMEM((tq, tk), f32),
            pltpu.VMEM((GQA * tq, tk), f32), pltpu.VMEM((GQA * tq, LANES), f32),
            pltpu.VMEM((GQA * tq, LANES), f32), pltpu.VMEM((GQA * tq, LANES), f32),
            pltpu.VMEM((GQA * tq, HEAD_DIM), f32)]


def _slc_prompt(slopes, qz, kv_slc, sel, expand, nbatch, t, tq, tk):
    nqt = t // tq
    gw = GQA * HEAD_DIM
    ncp = sel.shape[-1]
    bpt = tk // CMP_BLOCK
    flags = sel.reshape(nbatch, N_KV_A, t // SLC_ROWS, SLC_ROWS, ncp // bpt, bpt).max(axis=(3, 5))
    flags = (flags[..., :t // tk] > 0.5).astype(jnp.int32).reshape(-1)
    gs = pltpu.PrefetchScalarGridSpec(
        num_scalar_prefetch=1,
        grid=(nbatch, N_KV_A, nqt),
        in_specs=[
            pl.BlockSpec(memory_space=pltpu.SMEM),
            pl.BlockSpec((tq, gw), lambda b, h, i, fl: (b * nqt + i, h)),
            pl.BlockSpec((t, HEAD_DIM), lambda b, h, i, fl: (b, h)),
            pl.BlockSpec((t, HEAD_DIM), lambda b, h, i, fl: (b, N_KV_A + h)),
            pl.BlockSpec((1, 1, tq, ncp), lambda b, h, i, fl: (b, h, i, 0)),
            pl.BlockSpec((ncp, t), lambda b, h, i, fl: (0, 0)),
        ],
        out_specs=pl.BlockSpec((tq, gw), lambda b, h, i, fl: (b * nqt + i, h)),
        scratch_shapes=_attn_scratch(tq, tk),
    )
    return pl.pallas_call(
        functools.partial(_slc_kernel, tq=tq, tk=tk, t=t),
        grid_spec=gs,
        out_shape=jax.ShapeDtypeStruct((nbatch * t, D_ATT), f32),
        compiler_params=_cparams(("parallel", "parallel", "parallel")),
        name="slc_prompt",
    )(flags, slopes, qz, kv_slc, kv_slc, sel, expand)


def _swa_kernel(slopes_ref, q_ref, k_ref, v_ref, o_ref, *scratch, tq, tk):
    kvh = pl.program_id(1)
    qt = pl.program_id(2)
    qpos = qt * tq + lax.broadcasted_iota(jnp.int32, (tq, 1), 0)
    kofs = lax.broadcasted_iota(jnp.int32, (1, tk), 1)

    def tile_dist(ks):
        d = qpos - (ks + kofs)
        return jnp.where((d >= 0) & (d < WINDOW), d.astype(f32), MASK_DIST)

    _attn_prompt_body(kvh, qt, slopes_ref, q_ref, k_ref, v_ref, o_ref, *scratch, tq=tq, tk=tk,
                      kt_lo=jnp.maximum(qt * tq - (WINDOW - 1), 0) // tk, kt_hi=(qt * tq + tq - 1) // tk + 1,
                      tile_dist=tile_dist, active=None)


def _swa_prompt(slopes, qz, kv_swa, nbatch, t, tq, tk):
    nqt = t // tq
    gw = GQA * HEAD_DIM
    return pl.pallas_call(
        functools.partial(_swa_kernel, tq=tq, tk=tk),
        grid=(nbatch, N_KV_A, nqt),
        in_specs=[
            pl.BlockSpec(memory_space=pltpu.SMEM),
            pl.BlockSpec((tq, gw), lambda b, h, i: (b * nqt + i, h)),
            pl.BlockSpec((t, HEAD_DIM), lambda b, h, i: (b, h)),
            pl.BlockSpec((t, HEAD_DIM), lambda b, h, i: (b, N_KV_A + h)),
        ],
        out_specs=pl.BlockSpec((tq, gw), lambda b, h, i: (b * nqt + i, h)),
        out_shape=jax.ShapeDtypeStruct((nbatch * t, D_ATT), f32),
        scratch_shapes=_attn_scratch(tq, tk),
        compiler_params=_cparams(("parallel", "parallel", "parallel")),
        name="swa_prompt",
    )(slopes, qz, kv_swa, kv_swa)


def _softplus(x):
    return jnp.maximum(x, 0.0) + jnp.log1p(jnp.exp(-jnp.abs(x)))


def _sigmoid(x):
    return 1.0 / (1.0 + jnp.exp(-x))


def _silu(x):
    return x * _sigmoid(x)


def _dot_hi(a, b):
    return jnp.dot(a, b, precision=lax.Precision.HIGHEST, preferred_element_type=f32)


INV_BASE = 16


def _bdot(a, b):
    return _dot(a.astype(bf16), b.astype(bf16))


def _gdn_kernel(x_ref, small_ref, conv0_ref, s0_ref, cw_ref, alog_ref, dtb_ref, nw_ref,
                o_ref, s_out_ref, xp_ref, st_ref, gam_ref, n_ref, t_ref, pw_ref, uw_ref,
                k_ref, kb_ref, q_ref, qg_ref, kg_ref, qk_ref, vn_ref, rhs_ref, *, c, tv, halo):
    ci = pl.program_id(1)
    nci = pl.num_programs(1)

    @pl.when(ci == 0)
    def _():
        xp_ref[...] = jnp.zeros(xp_ref.shape, f32)
        xp_ref[halo - (CONV_W - 1):halo, :] = conv0_ref[0]
        st_ref[...] = s0_ref[0]

    xp_ref[halo:halo + tv, :] = x_ref[...]
    y = xp_ref[pl.ds(halo, c), :] * cw_ref[CONV_W - 1:CONV_W, :]
    for i in range(CONV_W - 2, -1, -1):
        y = y + xp_ref[pl.ds(halo - (CONV_W - 1) + i, c), :] * cw_ref[i:i + 1, :]
    y = _silu(y)
    xp_ref[halo - (CONV_W - 1):halo, :] = xp_ref[halo + tv - (CONV_W - 1):halo + tv, :]

    ri = lax.broadcasted_iota(jnp.int32, (c, 1), 0)
    rowi = lax.broadcasted_iota(jnp.int32, (c, c), 0)
    coli = lax.broadcasted_iota(jnp.int32, (c, c), 1)
    lower = rowi >= coli
    strict = rowi > coli
    tril = lower.astype(f32)
    eye = (rowi == coli).astype(f32)
    same_blk = {}
    s = INV_BASE
    while s <= c:
        same_blk[s] = (rowi // s) == (coli // s)
        s *= 2

    sm = small_ref[...]
    if tv < c:
        sm = jnp.concatenate([sm, jnp.zeros((c - tv, LANES), f32)], axis=0)
    g_all = -jnp.exp(alog_ref[...]) * _softplus(sm + dtb_ref[...])
    beta_all = _sigmoid(sm)
    if tv < c:
        live = ri < tv
        g_all = jnp.where(live, g_all, 0.0)
        beta_all = jnp.where(live, beta_all, 0.0)
    gc_all = _dot_hi(tril, g_all)
    gc_rows = gc_all.T
    heads = range(N_HEADS_B)
    decay = []

    for h in heads:
        yq = y[:, h * HEAD_DIM:(h + 1) * HEAD_DIM]
        yk = y[:, D_GDN + h * HEAD_DIM:D_GDN + (h + 1) * HEAD_DIM]
        v = y[:, 2 * D_GDN + h * HEAD_DIM:2 * D_GDN + (h + 1) * HEAD_DIM]
        q = yq * lax.rsqrt(jnp.sum(yq * yq, axis=-1, keepdims=True) + NORM_EPS) * SCALE
        k = yk * lax.rsqrt(jnp.sum(yk * yk, axis=-1, keepdims=True) + NORM_EPS)
        if tv < c:
            q = jnp.where(live, q, 0.0)
            k = jnp.where(live, k, 0.0)
            v = jnp.where(live, v, 0.0)
        beta = beta_all[:, B_OFF + h:B_OFF + h + 1]
        gcol = gc_all[:, A_OFF + h:A_OFF + h + 1]
        grow = gc_rows[A_OFF + h:A_OFF + h + 1, :]
        gam_ref[h] = jnp.where(lower, jnp.exp(jnp.where(lower, gcol - grow, 0.0)), 0.0)
        eg = jnp.exp(gcol)
        g_last = gc_all[c - 1:c, A_OFF + h:A_OFF + h + 1]
        decay.append(jnp.exp(g_last))
        kb = k * beta
        k_ref[h] = k.astype(bf16)
        kb_ref[h] = kb.astype(bf16)
        q_ref[h] = q.astype(bf16)
        qg_ref[h] = (q * eg).astype(bf16)
        kg_ref[h] = (k * jnp.exp(g_last - gcol)).astype(bf16)
        rhs_ref[h, :, 0:HEAD_DIM] = (v * beta).astype(bf16)
        rhs_ref[h, :, HEAD_DIM:2 * HEAD_DIM] = (kb * eg).astype(bf16)

    for h in heads:
        gam = gam_ref[h]
        nmat = jnp.where(strict, _dot_nt(kb_ref[h], k_ref[h]) * gam, 0.0)
        n_ref[h] = nmat
        nd = jnp.where(same_blk[INV_BASE], nmat, 0.0)
        t_ref[h] = eye - nd
        pw_ref[h] = nd.astype(bf16)
        qk_ref[h] = (_dot_nt(q_ref[h], k_ref[h]) * gam).astype(bf16)

    for _ in range(int(math.log2(INV_BASE)) - 1):
        for h in heads:
            pw = pw_ref[h]
            pw_ref[h] = _dot(pw, pw).astype(bf16)
        for h in heads:
            t = t_ref[h]
            t_ref[h] = t + _dot(t.astype(bf16), pw_ref[h])

    s = INV_BASE
    while s < c:
        pair = same_blk[2 * s] & jnp.logical_not(same_blk[s])
        for h in heads:
            off = jnp.where(pair, n_ref[h], 0.0)
            uw_ref[h, :, 0:HEAD_DIM] = _bdot(t_ref[h], off)
        for h in heads:
            t = t_ref[h]
            t_ref[h] = t - _bdot(uw_ref[h, :, 0:HEAD_DIM], t)
        s *= 2

    for h in heads:
        uw_ref[h] = _dot(t_ref[h].astype(bf16), rhs_ref[h])

    for h in heads:
        sb = st_ref[h].astype(bf16)
        v_new = uw_ref[h, :, 0:HEAD_DIM] - _dot(uw_ref[h, :, HEAD_DIM:2 * HEAD_DIM].astype(bf16), sb)
        vn_ref[h] = v_new.astype(bf16)
        uw_ref[h, :, 0:HEAD_DIM] = _dot(qg_ref[h], sb)
    for h in heads:
        vn = vn_ref[h]
        o = uw_ref[h, :, 0:HEAD_DIM] + _dot(qk_ref[h], vn)
        st_ref[h] = st_ref[h] * decay[h] + _dot_tn(kg_ref[h], vn)
        on = o * lax.rsqrt(jnp.mean(o * o, axis=-1, keepdims=True) + NORM_EPS) * nw_ref[...]
        o_ref[:, h * HEAD_DIM:(h + 1) * HEAD_DIM] = on[:tv]

    @pl.when(ci == nci - 1)
    def _():
        s_out_ref[0] = st_ref[...]


def _gdn(qkv_b, small, conv0, s0, conv_w, alog_row, dtb_row, gdn_norm_w, nbatch, t, c):
    tv = min(c, t)
    nci = t // tv
    halo = 8
    return pl.pallas_call(
        functools.partial(_gdn_kernel, c=c, tv=tv, halo=halo),
        grid=(nbatch, nci),
        in_specs=[
            pl.BlockSpec((tv, CONV_DIM), lambda b, i: (b * nci + i, 0)),
            pl.BlockSpec((tv, LANES), lambda b, i: (b * nci + i, 0)),
            pl.BlockSpec((1, CONV_W - 1, CONV_DIM), lambda b, i: (b, 0, 0)),
            pl.BlockSpec((1, N_HEADS_B, HEAD_DIM, HEAD_DIM), lambda b, i: (b, 0, 0, 0)),
            pl.BlockSpec((CONV_W, CONV_DIM), lambda b, i: (0, 0)),
            pl.BlockSpec((1, LANES), lambda b, i: (0, 0)),
            pl.BlockSpec((1, LANES), lambda b, i: (0, 0)),
            pl.BlockSpec((1, HEAD_DIM), lambda b, i: (0, 0)),
        ],
        out_specs=[
            pl.BlockSpec((tv, D_GDN), lambda b, i: (b * nci + i, 0)),
            pl.BlockSpec((1, N_HEADS_B, HEAD_DIM, HEAD_DIM), lambda b, i: (b, 0, 0, 0)),
        ],
        out_shape=[jax.ShapeDtypeStruct((nbatch * t, D_GDN), f32),
                   jax.ShapeDtypeStruct((nbatch, N_HEADS_B, HEAD_DIM, HEAD_DIM), f32)],
        scratch_shapes=[pltpu.VMEM((halo + c, CONV_DIM), f32),
                        pltpu.VMEM((N_HEADS_B, HEAD_DIM, HEAD_DIM), f32)]
        + [pltpu.VMEM((N_HEADS_B, c, c), f32)] * 3
        + [pltpu.VMEM((N_HEADS_B, c, c), bf16)]
        + [pltpu.VMEM((N_HEADS_B, c, 2 * HEAD_DIM), f32)]
        + [pltpu.VMEM((N_HEADS_B, c, HEAD_DIM), bf16)] * 5
        + [pltpu.VMEM((N_HEADS_B, c, c), bf16)]
        + [pltpu.VMEM((N_HEADS_B, c, HEAD_DIM), bf16)]
        + [pltpu.VMEM((N_HEADS_B, c, 2 * HEAD_DIM), bf16)],
        compiler_params=_cparams(("parallel", "arbitrary")),
        name="gdn",
    )(qkv_b, small, conv0, s0, conv_w, alog_row, dtb_row, gdn_norm_w.reshape(1, HEAD_DIM))


def _merge_kernel(x_ref, oc_ref, os_ref, ow_ref, small_ref, za_ref, zb_ref, ob_ref, w_ref, fw_ref, y_ref, mix_ref):
    gates = _sigmoid(small_ref[...])
    for h in range(N_HEADS_A):
        cs = slice(h * HEAD_DIM, (h + 1) * HEAD_DIM)
        o_a = (gates[:, G_OFF + h:G_OFF + h + 1] * oc_ref[:, cs]
               + gates[:, G_OFF + N_HEADS_A + h:G_OFF + N_HEADS_A + h + 1] * os_ref[:, cs]
               + gates[:, G_OFF + 2 * N_HEADS_A + h:G_OFF + 2 * N_HEADS_A + h + 1] * ow_ref[:, cs])
        mix_ref[:, cs] = (o_a * _silu(za_ref[:, cs])).astype(bf16)
    mix_ref[:, D_ATT:] = (ob_ref[...] * _silu(zb_ref[...])).astype(bf16)
    hres = x_ref[...] + _dot(mix_ref[...], w_ref[...])
    ms = jnp.mean(hres * hres, axis=-1, keepdims=True)
    y_ref[...] = hres * lax.rsqrt(ms + NORM_EPS) * fw_ref[...]


def _merge_out(x2d, o_cmp, o_slc, o_swa, small, qz, o_b, w_out_bf, final_norm_w, tm):
    m = x2d.shape[0]
    row = lambda i: (i, 0)
    return pl.pallas_call(
        _merge_kernel,
        grid=(m // tm,),
        in_specs=[
            pl.BlockSpec((tm, D_MODEL), row),
            pl.BlockSpec((tm, D_ATT), row),
            pl.BlockSpec((tm, D_ATT), row),
            pl.BlockSpec((tm, D_ATT), row),
            pl.BlockSpec((tm, LANES), row),
            pl.BlockSpec((tm, D_ATT), lambda i: (i, 1)),
            pl.BlockSpec((tm, D_GDN), lambda i: (i, 2)),
            pl.BlockSpec((tm, D_GDN), row),
            pl.BlockSpec((D_ATT + D_GDN, D_MODEL), lambda i: (0, 0), pipeline_mode=pl.Buffered(1)),
            pl.BlockSpec((1, D_MODEL), lambda i: (0, 0)),
        ],
        out_specs=pl.BlockSpec((tm, D_MODEL), row),
        out_shape=jax.ShapeDtypeStruct((m, D_MODEL), f32),
        scratch_shapes=[pltpu.VMEM((tm, D_ATT + D_GDN), bf16)],
        compiler_params=_cparams(("parallel",)),
        name="merge_out",
    )(x2d, o_cmp, o_slc, o_swa, small, qz, qz, o_b, w_out_bf, final_norm_w.reshape(1, D_MODEL))


BLOCK_ROWS4 = CMP_BLOCK * N_CH


def _page_copies(cache_ref, buf_ref, sem_ref, pt_ref, b, j, slot, i, pps, by_row):
    page = pt_ref[b, j * pps + i]
    if not by_row:
        return [pltpu.make_async_copy(
            cache_ref.at[pl.ds(page * ROWS_PER_PAGE4, ROWS_PER_PAGE4), :],
            buf_ref.at[slot, pl.ds(i * ROWS_PER_PAGE4, ROWS_PER_PAGE4), :],
            sem_ref.at[slot])]
    bpp = PAGE_SIZE // CMP_BLOCK
    return [pltpu.make_async_copy(
        cache_ref.at[pl.ds(page * ROWS_PER_PAGE4 + n * BLOCK_ROWS4, BLOCK_ROWS4), :],
        buf_ref.at[slot, :, i * bpp + n, :],
        sem_ref.at[slot]) for n in range(bpp)]


def _paged_fetch(cache_ref, buf_ref, sem_ref, pt_ref, pps, nj_pages, need=None, by_row=False):
    b = pl.program_id(0)
    j = pl.program_id(1)
    nb = pl.num_programs(0)
    nj = pl.num_programs(1)
    step = b * nj_pages + jnp.minimum(j, nj_pages - 1)
    slot = step % 2

    def guarded(bb, jj, i, fn):
        if need is None:
            fn()
        else:
            pl.when(need(bb, jj, i))(fn)

    def run(what, bb, jj, sl, i):
        for cp in _page_copies(cache_ref, buf_ref, sem_ref, pt_ref, bb, jj, sl, i, pps, by_row):
            getattr(cp, what)()

    def start(bb, jj, sl):
        for i in range(pps):
            guarded(bb, jj, i, functools.partial(run, "start", bb, jj, sl, i))

    @pl.when((b == 0) & (j == 0))
    def _():
        start(0, 0, 0)

    @pl.when(j < nj_pages)
    def _():
        last_j = j == nj_pages - 1
        nb_ = jnp.where(last_j, b + 1, b)
        nj_ = jnp.where(last_j, 0, j + 1)

        @pl.when(nb_ < nb)
        def _():
            start(nb_, nj_, 1 - slot)

        for i in range(pps):
            guarded(b, j, i, functools.partial(run, "wait", b, j, slot, i))

    return slot


def _compress_paged_kernel(pt_ref, cache_ref, pe_ref, w_ref, o_ref, buf_ref, sem_ref, *, pps):
    slot = _paged_fetch(cache_ref, buf_ref, sem_ref, pt_ref, pps, pl.num_programs(1), by_row=True)
    nbk = pps * (PAGE_SIZE // CMP_BLOCK)
    accs = _compress_accumulate(lambda l, ch: buf_ref[slot, l * N_CH + ch], pe_ref, w_ref, nbk)
    for c in range(2):
        for h in range(N_KV_A):
            o_ref[0, c * N_KV_A + h] = accs[c][h * nbk:(h + 1) * nbk]


def _compress_paged(page_table, cache4, pe2d, w_cmp_bf, pps):
    nbatch, n_pages = page_table.shape
    nj = n_pages // pps
    nbk = pps * (PAGE_SIZE // CMP_BLOCK)
    gs = pltpu.PrefetchScalarGridSpec(
        num_scalar_prefetch=1,
        grid=(nbatch, nj),
        in_specs=[
            pl.BlockSpec(memory_space=pl.ANY),
            pl.BlockSpec((2, CMP_BLOCK, HEAD_DIM), lambda b, j, pt: (0, 0, 0)),
            pl.BlockSpec((2, CMP_BLOCK // 2, 2 * HEAD_DIM, HEAD_DIM), lambda b, j, pt: (0, 0, 0, 0)),
        ],
        out_specs=pl.BlockSpec((1, N_CH, nbk, HEAD_DIM), lambda b, j, pt: (b, 0, j, 0)),
        scratch_shapes=[pltpu.VMEM((2, BLOCK_ROWS4, nbk, HEAD_DIM), f32), pltpu.SemaphoreType.DMA((2,))],
    )
    return pl.pallas_call(
        functools.partial(_compress_paged_kernel, pps=pps),
        grid_spec=gs,
        out_shape=jax.ShapeDtypeStruct((nbatch, N_CH, nj * nbk, HEAD_DIM), f32),
        compiler_params=_cparams(("arbitrary", "arbitrary")),
        name="compress_paged",
    )(page_table, cache4, pe2d, w_cmp_bf)


def _two_phase_attend(qs, slope2, qpos, kpos0, load_k, load_v, load_mask, nchunks, chunk,
                      s_ref, rows, m_ref, l_ref, acc_ref, active=None):
    nrow = qs.shape[0]
    nrep = chunk // LANES
    kofs = lax.broadcasted_iota(jnp.int32, (1, chunk), 1)
    unroll = math.gcd(nchunks, 4)

    def cols(ci):
        return pl.ds(pl.multiple_of(ci * chunk, chunk), chunk)

    def skippable(ci, fn, carry):
        if active is None:
            return fn(carry)
        return lax.cond(active(ci), fn, lambda c: c, carry)

    def phase1(ci, mrun):
        def update(mrun):
            d = qpos - (kpos0 + ci * chunk + kofs)
            dmask = jnp.where((load_mask(ci) > 0.5) & (d >= 0), d.astype(f32), MASK_DIST)
            t2 = _dot_nt(qs, load_k(ci)) * (SCALE * LOG2E) - slope2 * dmask
            s_ref[rows, cols(ci)] = t2
            for c in range(nrep):
                mrun = jnp.maximum(mrun, t2[:, c * LANES:(c + 1) * LANES])
            return mrun

        return skippable(ci, update, mrun)

    mrun = lax.fori_loop(0, nchunks, phase1, jnp.full((nrow, LANES), NEG_INF, f32), unroll=unroll)
    m_prev = m_ref[rows, :]
    m_new = jnp.maximum(m_prev, jnp.broadcast_to(jnp.max(mrun, axis=-1, keepdims=True), (nrow, LANES)))

    def phase2(ci, carry):
        def update(carry):
            lp, acc = carry
            p = jnp.exp2(s_ref[rows, cols(ci)] - _lane_repeat(m_new, nrep))
            for c in range(nrep):
                lp = lp + p[:, c * LANES:(c + 1) * LANES]
            return lp, acc + _dot(p.astype(bf16), load_v(ci))

        return skippable(ci, update, carry)

    z = jnp.zeros((nrow, LANES), f32)
    lp, acc = lax.fori_loop(0, nchunks, phase2, (z, z), unroll=unroll)
    alpha = jnp.exp2(m_prev - m_new)
    l_ref[rows, :] = alpha * l_ref[rows, :] + lp
    acc_ref[rows, :] = alpha * acc_ref[rows, :] + acc
    m_ref[rows, :] = m_new


def _slc_paged_kernel(pt_ref, chunk_any_ref, chunk_kvh_ref, slopes_ref, cache_ref, q_ref, tail_ref, sel_ref,
                      e_ref, o_ref, buf_ref, sem_ref, s_ref, m_ref, l_ref, acc_ref, *, pps, t, sub, pos0):
    b = pl.program_id(0)
    j = pl.program_id(1)
    nj_pages = pl.num_programs(1) - 1
    ppc = sub // PAGE_SIZE
    cps = pps // ppc
    slot = _paged_fetch(cache_ref, buf_ref, sem_ref, pt_ref, pps, nj_pages,
                        need=lambda bb, jj, i: chunk_any_ref[bb, jj * cps + i // ppc] > 0)
    nrow = GQA * t

    @pl.when(j == 0)
    def _():
        _flash_init(m_ref, l_ref, acc_ref)

    tpos = lax.broadcasted_iota(jnp.int32, (t, 1), 0)
    qpos = pos0 + jnp.concatenate([tpos] * GQA, axis=0)

    def attend(kvh, load_k, load_v, nchunks, chunk, kpos0, active=None):
        q = q_ref[:, kvh * GQA * HEAD_DIM:(kvh + 1) * GQA * HEAD_DIM]
        qs = jnp.concatenate([q[:, g * HEAD_DIM:(g + 1) * HEAD_DIM] for g in range(GQA)], axis=0).astype(bf16)
        slope2 = jnp.concatenate([jnp.full((t, 1), slopes_ref[kvh, g] * LOG2E, f32) for g in range(GQA)], axis=0)
        selb = sel_ref[0, kvh].astype(bf16)

        def load_mask(ci):
            maskf = _dot(selb, e_ref[:, pl.ds(pl.multiple_of(ci * chunk, chunk), chunk)])
            return jnp.concatenate([maskf] * GQA, axis=0)

        _two_phase_attend(qs, slope2, qpos, kpos0, load_k, load_v, load_mask, nchunks, chunk,
                          s_ref, pl.ds(kvh * nrow, nrow), m_ref, l_ref, acc_ref, active)

    @pl.when(j < nj_pages)
    def _():
        for kvh in range(N_KV_A):
            def load(ci, ch):
                return buf_ref[slot, pl.ds(ci * (sub * N_CH) + ch, sub, stride=N_CH), :].astype(bf16)

            first = ((b * N_KV_A + kvh) * nj_pages + j) * cps
            attend(kvh, functools.partial(load, ch=kvh), functools.partial(load, ch=N_KV_A + kvh),
                   cps, sub, j * pps * PAGE_SIZE, lambda ci, first=first: chunk_kvh_ref[first + ci] > 0)

    @pl.when(j == nj_pages)
    def _():
        ntail = tail_ref.shape[1]
        for kvh in range(N_KV_A):
            k = tail_ref[0, :, kvh * HEAD_DIM:(kvh + 1) * HEAD_DIM].astype(bf16)
            v = tail_ref[0, :, (N_KV_A + kvh) * HEAD_DIM:(N_KV_A + kvh + 1) * HEAD_DIM].astype(bf16)
            attend(kvh, lambda ci, k=k: k, lambda ci, v=v: v, 1, ntail, nj_pages * pps * PAGE_SIZE)
        for kvh in range(N_KV_A):
            for g in range(GQA):
                r = pl.ds(kvh * nrow + g * t, t)
                o_ref[:, (kvh * GQA + g) * HEAD_DIM:(kvh * GQA + g + 1) * HEAD_DIM] = (
                    acc_ref[r, :] / jnp.sum(l_ref[r, :], axis=-1, keepdims=True))


def _slc_paged(page_table, slopes, cache4, qz, tail, sel, expand, t, pps, pos0):
    nbatch, n_pages = page_table.shape
    njp = n_pages // pps
    bps = pps * (PAGE_SIZE // CMP_BLOCK)
    sub = min(512, pps * PAGE_SIZE)
    bpc = sub // CMP_BLOCK
    n_pb = n_pages * (PAGE_SIZE // CMP_BLOCK)
    chunk_sel = sel[..., :n_pb].reshape(nbatch, N_KV_A, t, n_pb // bpc, bpc).max(axis=(2, 4)) > 0.5
    chunk_kvh = chunk_sel.astype(jnp.int32).reshape(-1)
    chunk_any = chunk_sel.any(axis=1).astype(jnp.int32)
    gs = pltpu.PrefetchScalarGridSpec(
        num_scalar_prefetch=3,
        grid=(nbatch, njp + 1),
        in_specs=[
            pl.BlockSpec(memory_space=pltpu.SMEM),
            pl.BlockSpec(memory_space=pl.ANY),
            pl.BlockSpec((t, D_ATT), lambda b, j, *_: (b, 0)),
            pl.BlockSpec((1, LANES, KV_BRANCH), lambda b, j, *_: (b, 0, 0)),
            pl.BlockSpec((1, N_KV_A, t, bps), lambda b, j, *_: (b, 0, 0, j)),
            pl.BlockSpec((bps, pps * PAGE_SIZE), lambda b, j, *_: (0, 0)),
        ],
        out_specs=pl.BlockSpec((t, D_ATT), lambda b, j, *_: (b, 0)),
        scratch_shapes=[pltpu.VMEM((2, pps * ROWS_PER_PAGE4, HEAD_DIM), f32), pltpu.SemaphoreType.DMA((2,)),
                        pltpu.VMEM((N_KV_A * GQA * t, pps * PAGE_SIZE), f32),
                        pltpu.VMEM((N_KV_A * GQA * t, LANES), f32), pltpu.VMEM((N_KV_A * GQA * t, LANES), f32),
                        pltpu.VMEM((N_KV_A * GQA * t, HEAD_DIM), f32)],
    )
    return pl.pallas_call(
        functools.partial(_slc_paged_kernel, pps=pps, t=t, sub=sub, pos0=pos0),
        grid_spec=gs,
        out_shape=jax.ShapeDtypeStruct((nbatch * t, D_ATT), f32),
        compiler_params=_cparams(("arbitrary", "arbitrary")),
        name="slc_paged",
    )(page_table, chunk_any, chunk_kvh, slopes, cache4, qz, tail, sel, expand)


def _swa_sample_kernel(slopes_ref, q_ref, win_ref, tail_ref, o_ref, *, t, wbuf, pos0):
    tpos = lax.broadcasted_iota(jnp.int32, (t, 1), 0)
    qpos = pos0 + jnp.concatenate([tpos] * GQA, axis=0)
    nk = wbuf + CMP_BLOCK
    kpos = pos0 - wbuf + lax.broadcasted_iota(jnp.int32, (1, nk), 1)
    d = qpos - kpos
    ok = (d >= 0) & (d < WINDOW)
    df = d.astype(f32)
    for kvh in range(N_KV_A):
        k = jnp.concatenate([win_ref[pl.ds(kvh, wbuf, stride=N_CH), :],
                             tail_ref[0, :, kvh * HEAD_DIM:(kvh + 1) * HEAD_DIM]], axis=0).astype(bf16)
        v = jnp.concatenate([win_ref[pl.ds(N_KV_A + kvh, wbuf, stride=N_CH), :],
                             tail_ref[0, :, (N_KV_A + kvh) * HEAD_DIM:(N_KV_A + kvh + 1) * HEAD_DIM]],
                            axis=0).astype(bf16)
        q = q_ref[:, kvh * GQA * HEAD_DIM:(kvh + 1) * GQA * HEAD_DIM]
        qs = jnp.concatenate([q[:, g * HEAD_DIM:(g + 1) * HEAD_DIM] for g in range(GQA)], axis=0).astype(bf16)
        slope = jnp.concatenate([jnp.full((t, 1), slopes_ref[kvh, g], f32) for g in range(GQA)], axis=0)
        s = _dot_nt(qs, k) * SCALE - slope * df
        s = jnp.where(ok, s, NEG_INF)
        e = jnp.exp(s - jnp.max(s, axis=-1, keepdims=True))
        p = jnp.where(ok, e / jnp.sum(e, axis=-1, keepdims=True), 0.0)
        o = _dot(p.astype(bf16), v)
        for g in range(GQA):
            o_ref[:, (kvh * GQA + g) * HEAD_DIM:(kvh * GQA + g + 1) * HEAD_DIM] = o[g * t:(g + 1) * t]


def _swa_sample(slopes, qz, win4, tail, nbatch, t, wbuf, pos0):
    return pl.pallas_call(
        functools.partial(_swa_sample_kernel, t=t, wbuf=wbuf, pos0=pos0),
        grid=(nbatch,),
        in_specs=[
            pl.BlockSpec(memory_space=pltpu.SMEM),
            pl.BlockSpec((t, D_ATT), lambda b: (b, 0)),
            pl.BlockSpec((wbuf * N_CH, HEAD_DIM), lambda b: (b, 0)),
            pl.BlockSpec((1, CMP_BLOCK, KV_BRANCH), lambda b: (b, 0, 0)),
        ],
        out_specs=pl.BlockSpec((t, D_ATT), lambda b: (b, 0)),
        out_shape=jax.ShapeDtypeStruct((nbatch * t, D_ATT), f32),
        compiler_params=_cparams(("parallel",)),
        name="swa_sample",
    )(slopes, qz, win4, tail)


def _round_up(a, b):
    return -(-a // b) * b


def _pad_blocks(kcv, ncp):
    return jnp.pad(kcv, ((0, 0), (0, 0), (0, ncp - kcv.shape[2]), (0, 0)))


def _expand_matrix(nblocks, nkeys):
    return (jnp.arange(nkeys)[None, :] // CMP_BLOCK == jnp.arange(nblocks)[:, None]).astype(bf16)


def _kv6(rows2d, nbatch, t):
    return rows2d.reshape(nbatch, t, 2, N_KV_A, HEAD_DIM)


def _layer_prompt(x, prm):
    nbatch, t, _ = x.shape
    m = nbatch * t
    x2d = x.reshape(m, D_MODEL)
    tm = min(1024, m)
    qz, qkvb, kv_cmp, kv_slc, kv_swa, small = _in_proj(x2d, prm["norm_w"], prm["w_main"], prm["w_small"], tm)
    nc = t // CMP_BLOCK
    kcv = _compress(kv_cmp, nbatch, prm["pe2d"], prm["w_cmp"], min(64, nc))
    kcv = _pad_blocks(kcv, _round_up(nc, LANES))
    tq = min(256, t)
    o_cmp, sel = _cmp_attend(prm["slopes"], qz, kcv, nbatch, t, tq, 0, min(N_SELECT, nc))
    tk = min(256, t)
    o_slc = _slc_prompt(prm["slopes"], qz, kv_slc, sel, _expand_matrix(sel.shape[-1], t), nbatch, t, tq, tk)
    o_swa = _swa_prompt(prm["slopes"], qz, kv_swa, nbatch, t, tq, tk)
    conv0 = jnp.zeros((nbatch, CONV_W - 1, CONV_DIM), f32)
    s0 = jnp.zeros((nbatch, N_HEADS_B, HEAD_DIM, HEAD_DIM), f32)
    o_b, s_new = _gdn(qkvb, small, conv0, s0, prm["conv_w"], prm["alog_row"], prm["dtb_row"],
                      prm["gdn_norm_w"], nbatch, t, 128)
    y = _merge_out(x2d, o_cmp, o_slc, o_swa, small, qz, o_b, prm["w_out"], prm["final_norm_w"], min(512, m))
    w = min(WINDOW, t)
    conv_new = qkvb.reshape(nbatch, t, CONV_DIM)[:, t - (CONV_W - 1):]
    return (y.reshape(nbatch, t, D_MODEL), _kv6(kv_cmp, nbatch, t), _kv6(kv_slc, nbatch, t),
            _kv6(kv_swa, nbatch, t)[:, t - w:], conv_new, s_new)


def _layer_sample(x, cache_cmp, cache_slc, cache_swa, conv_buf, s0, page_table, prm):
    nbatch, t, _ = x.shape
    m = nbatch * t
    n_pages = page_table.shape[1]
    past_len = n_pages * PAGE_SIZE
    wbuf = cache_swa.shape[1]
    x2d = x.reshape(m, D_MODEL)
    qz, qkvb, kv_cmp, kv_slc, kv_swa, small = _in_proj(x2d, prm["norm_w"], prm["w_main"], prm["w_small"], m)
    pad_tail = lambda r: jnp.pad(r.reshape(nbatch, t, KV_BRANCH), ((0, 0), (0, CMP_BLOCK - t), (0, 0)))
    pps = min(32, n_pages)
    kc_past = _compress_paged(page_table, cache_cmp.reshape(-1, HEAD_DIM), prm["pe2d"], prm["w_cmp"], pps)
    kc_tail = _compress(pad_tail(kv_cmp).reshape(nbatch * CMP_BLOCK, KV_BRANCH), 1, prm["pe2d"], prm["w_cmp"], nbatch)
    kc_tail = jnp.swapaxes(kc_tail[0], 0, 1)[:, :, None, :]
    n_pb = past_len // CMP_BLOCK
    pps_s = min(64, n_pages)
    bps = pps_s * (PAGE_SIZE // CMP_BLOCK)
    ncp = _round_up(n_pb + 1, max(LANES, bps))
    kcv = _pad_blocks(jnp.concatenate([kc_past, kc_tail], axis=2), ncp)
    o_cmp, sel = _cmp_attend(prm["slopes"], qz, kcv, nbatch, t, t, past_len, min(N_SELECT, n_pb + 1))
    o_slc = _slc_paged(page_table, prm["slopes"], cache_slc.reshape(-1, HEAD_DIM), qz,
                       jnp.pad(kv_slc.reshape(nbatch, t, KV_BRANCH), ((0, 0), (0, LANES - t), (0, 0))), sel,
                       _expand_matrix(bps, pps_s * PAGE_SIZE), t, pps_s, past_len)
    o_swa = _swa_sample(prm["slopes"], qz, cache_swa.reshape(-1, HEAD_DIM), pad_tail(kv_swa), nbatch, t, wbuf, past_len)
    o_b, s_new = _gdn(qkvb, small, conv_buf, s0, prm["conv_w"], prm["alog_row"], prm["dtb_row"],
                      prm["gdn_norm_w"], nbatch, t, 128)
    y = _merge_out(x2d, o_cmp, o_slc, o_swa, small, qz, o_b, prm["w_out"], prm["final_norm_w"], min(512, m))
    kv_win = jnp.concatenate([cache_swa, _kv6(kv_swa, nbatch, t)], axis=1)[:, t:]
    conv_new = jnp.concatenate([conv_buf, qkvb.reshape(nbatch, t, CONV_DIM)], axis=1)[:, t:]
    return (y.reshape(nbatch, t, D_MODEL), _kv6(kv_cmp, nbatch, t), _kv6(kv_slc, nbatch, t), kv_win, conv_new, s_new)


def kernel(x_prompt, x_sample, cache_cmp, cache_slc, cache_swa, state_conv, state_gdn, page_table,
           norm_w, w_in, pe_cmp, w_cmp, conv_w, a_log, dt_bias, gdn_norm_w, w_out, final_norm_w):
    depth = norm_w.shape[0]
    assert depth == 1, "the final norm is fused into the single layer's output projection"
    l = 0
    w_main, w_small = _prep_w_in(w_in[l])
    head = jnp.arange(1, N_HEADS_A + 1, dtype=f32)
    lane_row = lambda vals: jnp.zeros((1, LANES), f32).at[0, A_OFF:A_OFF + N_HEADS_B].set(vals.astype(f32))
    prm = {
        "norm_w": norm_w[l], "w_main": w_main, "w_small": w_small,
        "pe2d": jnp.swapaxes(pe_cmp[l], 0, 1),
        "w_cmp": jnp.swapaxes(w_cmp[l], 0, 1).reshape(2, CMP_BLOCK // 2, 2 * HEAD_DIM, HEAD_DIM).astype(bf16),
        "conv_w": conv_w[l], "alog_row": lane_row(a_log[l]), "dtb_row": lane_row(dt_bias[l]),
        "gdn_norm_w": gdn_norm_w[l], "w_out": w_out[l].astype(bf16), "final_norm_w": final_norm_w,
        "slopes": jnp.exp2(-8.0 * head / N_HEADS_A).reshape(N_KV_A, GQA),
    }
    yp, p_cmp, p_slc, p_swa, p_conv, p_gdn = _layer_prompt(x_prompt, prm)
    ys, s_cmp, s_slc, s_swa, s_conv, s_gdn = _layer_sample(
        x_sample, cache_cmp[l], cache_slc[l], cache_swa[l], state_conv[l], state_gdn[l], page_table, prm)
    st = lambda a: a[None]
    return (yp, ys, st(p_cmp), st(p_slc), st(p_swa), st(p_conv), st(p_gdn),
            st(s_cmp), st(s_slc), st(s_swa), st(s_conv), st(s_gdn))
```

```python
import functools
import math

import jax
import jax.numpy as jnp
from jax import lax
from jax.experimental import pallas as pl
from jax.experimental.pallas import tpu as pltpu

f32 = jnp.float32
bf16 = jnp.bfloat16

D_MODEL = 2048
HEAD_DIM = 128
N_HEADS_A = 8
N_KV_A = 2
GQA = 4
D_ATT = N_HEADS_A * HEAD_DIM
CMP_BLOCK = 64
N_SELECT = 16
WINDOW = 512
N_HEADS_B = 8
D_GDN = N_HEADS_B * HEAD_DIM
CONV_W = 4
CONV_DIM = 3 * D_GDN
PAGE_SIZE = 128
NORM_EPS = 1e-6
NEG_INF = -1e30
FORCE_SCORE = 1e4
KV_BRANCH = 2 * N_KV_A * HEAD_DIM
N_CH = 2 * N_KV_A
ROWS_PER_PAGE4 = PAGE_SIZE * N_CH
SCALE = HEAD_DIM ** -0.5
G_OFF, A_OFF, B_OFF = 0, 3 * N_HEADS_A, 3 * N_HEADS_A + N_HEADS_B
LANES = 128
VMEM_LIMIT = 56 * 1024 * 1024
IN_PROJ_VMEM_LIMIT = 60 * 1024 * 1024

IN_TN = 512
N_QZ_TILES = 3 * D_ATT // IN_TN
N_B_TILES = CONV_DIM // IN_TN
N_IN_TILES = N_QZ_TILES + N_B_TILES + 3


def _cparams(sem):
    return pltpu.CompilerParams(dimension_semantics=sem, vmem_limit_bytes=VMEM_LIMIT)


def _dot(a, b):
    return jnp.dot(a, b, preferred_element_type=f32)


def _dot_nt(a, b):
    return lax.dot_general(a, b, (((1,), (1,)), ((), ())), preferred_element_type=f32)


def _dot_tn(a, b):
    return lax.dot_general(a, b, (((0,), (0,)), ((), ())), preferred_element_type=f32)


def _store_row4(o4_ref, acc):
    for ch in range(N_CH):
        o4_ref[pl.ds(ch, acc.shape[0], stride=N_CH), :] = acc[:, ch * HEAD_DIM:(ch + 1) * HEAD_DIM]


def _in_proj_kernel(x_ref, nw_ref, w_ref, ws_ref, qz_ref, b_ref, slc_ref, swa_ref, cmp4_ref, slc4_ref, swa4_ref,
                    small_ref, xn_ref):
    n = pl.program_id(1)

    @pl.when(n == 0)
    def _():
        x = x_ref[...]
        ms = jnp.mean(x * x, axis=-1, keepdims=True)
        xn = (x * lax.rsqrt(ms + NORM_EPS) * nw_ref[...]).astype(bf16)
        xn_ref[...] = xn
        small_ref[...] = _dot(xn, ws_ref[...])

    acc = _dot(xn_ref[...], w_ref[...])

    @pl.when(n < N_QZ_TILES)
    def _():
        qz_ref[...] = acc

    @pl.when((n >= N_QZ_TILES) & (n < N_QZ_TILES + N_B_TILES))
    def _():
        b_ref[...] = acc

    @pl.when(n == N_QZ_TILES + N_B_TILES)
    def _():
        _store_row4(cmp4_ref, acc)

    @pl.when(n == N_QZ_TILES + N_B_TILES + 1)
    def _():
        slc_ref[...] = acc
        _store_row4(slc4_ref, acc)

    @pl.when(n == N_QZ_TILES + N_B_TILES + 2)
    def _():
        swa_ref[...] = acc
        _store_row4(swa4_ref, acc)


def _in_proj(x2d, norm_w, w_main, w_small, tm):
    m = x2d.shape[0]
    S = jax.ShapeDtypeStruct
    nb0 = N_QZ_TILES
    return pl.pallas_call(
        _in_proj_kernel,
        grid=(m // tm, N_IN_TILES),
        in_specs=[
            pl.BlockSpec((tm, D_MODEL), lambda i, n: (i, 0)),
            pl.BlockSpec((1, D_MODEL), lambda i, n: (0, 0)),
            pl.BlockSpec((D_MODEL, IN_TN), lambda i, n: (0, n)),
            pl.BlockSpec((D_MODEL, LANES), lambda i, n: (0, 0)),
        ],
        out_specs=[
            pl.BlockSpec((tm, IN_TN), lambda i, n: (i, jnp.minimum(n, nb0 - 1))),
            pl.BlockSpec((tm, IN_TN), lambda i, n: (i, jnp.clip(n - nb0, 0, N_B_TILES - 1))),
            pl.BlockSpec((tm, KV_BRANCH), lambda i, n: (i, 0)),
            pl.BlockSpec((tm, KV_BRANCH), lambda i, n: (i, 0)),
            pl.BlockSpec((tm * N_CH, HEAD_DIM), lambda i, n: (i, 0)),
            pl.BlockSpec((tm * N_CH, HEAD_DIM), lambda i, n: (i, 0)),
            pl.BlockSpec((tm * N_CH, HEAD_DIM), lambda i, n: (i, 0)),
            pl.BlockSpec((tm, LANES), lambda i, n: (i, 0)),
        ],
        out_shape=[S((m, 3 * D_ATT), f32), S((m, CONV_DIM), f32), S((m, KV_BRANCH), f32), S((m, KV_BRANCH), f32),
                   S((m * N_CH, HEAD_DIM), f32), S((m * N_CH, HEAD_DIM), f32), S((m * N_CH, HEAD_DIM), f32),
                   S((m, LANES), f32)],
        scratch_shapes=[pltpu.VMEM((tm, D_MODEL), bf16)],
        compiler_params=pltpu.CompilerParams(dimension_semantics=("parallel", "arbitrary"),
                                             vmem_limit_bytes=IN_PROJ_VMEM_LIMIT),
        name="in_proj",
    )(x2d, norm_w.reshape(1, D_MODEL), w_main, w_small)


def _prep_w_in(w_in):
    o = [0]
    for s in (D_ATT, 3 * KV_BRANCH, 3 * N_HEADS_A, D_ATT, CONV_DIM, N_HEADS_B, N_HEADS_B, D_GDN):
        o.append(o[-1] + s)
    q_a, kv_a, g_a, z_a, qkv_b, a_b, b_b, z_b = (w_in[:, o[i]:o[i + 1]] for i in range(8))
    w_main = jnp.concatenate([q_a, z_a, z_b, qkv_b, kv_a], axis=1).astype(bf16)
    pad = jnp.zeros((w_in.shape[0], LANES - (3 * N_HEADS_A + 2 * N_HEADS_B)), w_in.dtype)
    w_small = jnp.concatenate([g_a, a_b, b_b, pad], axis=1).astype(bf16)
    return w_main, w_small


CMP_UNROLL = 4


def _compress_accumulate(load_rows, pe_ref, w_ref, nbk):
    def body(i, accs):
        new = []
        for c in range(2):
            halves = []
            for dl in range(2):
                l = 2 * i + dl
                pe_row = pe_ref[c, pl.ds(l, 1), :]
                halves.append(jnp.concatenate(
                    [load_rows(l, c * N_KV_A + h) + pe_row for h in range(N_KV_A)], axis=0))
            lhs = jnp.concatenate(halves, axis=1).astype(bf16)
            new.append(accs[c] + _dot(lhs, w_ref[c, i]))
        return tuple(new)

    z = jnp.zeros((N_KV_A * nbk, HEAD_DIM), f32)
    return lax.fori_loop(0, CMP_BLOCK // 2, body, (z, z), unroll=CMP_UNROLL)


def _compress_kernel(x_ref, pe_ref, w_ref, o_ref, *, nbk):
    accs = _compress_accumulate(
        lambda l, ch: x_ref[pl.ds(l * N_CH + ch, nbk, stride=CMP_BLOCK * N_CH), :], pe_ref, w_ref, nbk)
    for c in range(2):
        for h in range(N_KV_A):
            o_ref[0, c * N_KV_A + h] = accs[c][h * nbk:(h + 1) * nbk]


def _compress(rows4, nbatch, pe2d, w_cmp_bf, nbk):
    nblk = rows4.shape[0] // (nbatch * CMP_BLOCK * N_CH)
    nj = nblk // nbk
    return pl.pallas_call(
        functools.partial(_compress_kernel, nbk=nbk),
        grid=(nbatch, nj),
        in_specs=[
            pl.BlockSpec((nbk * CMP_BLOCK * N_CH, HEAD_DIM), lambda b, j: (b * nj + j, 0)),
            pl.BlockSpec((2, CMP_BLOCK, HEAD_DIM), lambda b, j: (0, 0, 0)),
            pl.BlockSpec((2, CMP_BLOCK // 2, 2 * HEAD_DIM, HEAD_DIM), lambda b, j: (0, 0, 0, 0)),
        ],
        out_specs=pl.BlockSpec((1, N_CH, nbk, HEAD_DIM), lambda b, j: (b, 0, j, 0)),
        out_shape=jax.ShapeDtypeStruct((nbatch, N_CH, nblk, HEAD_DIM), f32),
        compiler_params=_cparams(("parallel", "parallel")),
        name="compress",
    )(rows4, pe2d, w_cmp_bf)


def _cmp_kernel(slopes_ref, q_ref, kc_ref, vc_ref, o_ref, sel_ref, *, tq, tqp, ncp, pos0, nsel):
    kvh = pl.program_id(1)
    qt = pl.program_id(2)
    q = q_ref[...]
    if tqp > tq:
        q = jnp.concatenate([q, jnp.zeros((tqp - tq, q.shape[1]), f32)], axis=0)
    kc = kc_ref[0, 0].astype(bf16)
    vc = vc_ref[0, 0].astype(bf16)
    qpos = pos0 + qt * tq + lax.broadcasted_iota(jnp.int32, (1, tqp), 1)
    jblk = lax.broadcasted_iota(jnp.int32, (ncp, 1), 0)
    d = qpos - ((jblk + 1) * CMP_BLOCK - 1)
    ok = d >= 0
    df = d.astype(f32)
    imp = jnp.zeros((ncp, tqp), f32)
    for g in range(GQA):
        qg = q[:, g * HEAD_DIM:(g + 1) * HEAD_DIM].astype(bf16)
        s = _dot_nt(kc, qg) * SCALE
        s = jnp.where(ok, s - slopes_ref[kvh, g] * df, NEG_INF)
        mx = jnp.max(s, axis=0, keepdims=True)
        e = jnp.exp(s - mx)
        p = jnp.where(ok, e / jnp.sum(e, axis=0, keepdims=True), 0.0)
        imp = imp + p
        og = _dot_tn(p.astype(bf16), vc)
        o_ref[:, g * HEAD_DIM:(g + 1) * HEAD_DIM] = og[:tq]
    cur = qpos // CMP_BLOCK
    forced = (jblk == cur) | (jblk == 0)
    score = jnp.where(forced, FORCE_SCORE, jnp.where(jblk <= cur, imp, -1.0))
    for cg in range(tqp // LANES):
        sc = score[:, cg * LANES:(cg + 1) * LANES]
        work = sc
        picked = jnp.zeros((ncp, LANES), f32)
        for _ in range(nsel):
            mx = jnp.max(work, axis=0, keepdims=True)
            first = jnp.min(jnp.where(work == mx, jblk, ncp), axis=0, keepdims=True)
            hit = jblk == first
            picked = jnp.where(hit, 1.0, picked)
            work = jnp.where(hit, -2.0, work)
        sel = jnp.where(sc >= 0.0, picked, 0.0)
        lo = cg * LANES
        hi = min(tq, lo + LANES)
        sel_ref[0, 0, lo:hi, :] = sel.T[:hi - lo]


def _cmp_attend(slopes, qz, kcv, nbatch, t, tq, pos0, nsel):
    ncp = kcv.shape[2]
    tqp = max(tq, LANES)
    nqt = t // tq
    gw = GQA * HEAD_DIM
    return pl.pallas_call(
        functools.partial(_cmp_kernel, tq=tq, tqp=tqp, ncp=ncp, pos0=pos0, nsel=nsel),
        grid=(nbatch, N_KV_A, nqt),
        in_specs=[
            pl.BlockSpec(memory_space=pltpu.SMEM),
            pl.BlockSpec((tq, gw), lambda b, h, i: (b * nqt + i, h)),
            pl.BlockSpec((1, 1, ncp, HEAD_DIM), lambda b, h, i: (b, h, 0, 0)),
            pl.BlockSpec((1, 1, ncp, HEAD_DIM), lambda b, h, i: (b, N_KV_A + h, 0, 0)),
        ],
        out_specs=[
            pl.BlockSpec((tq, gw), lambda b, h, i: (b * nqt + i, h)),
            pl.BlockSpec((1, 1, tq, ncp), lambda b, h, i: (b, h, i, 0)),
        ],
        out_shape=[jax.ShapeDtypeStruct((nbatch * t, D_ATT), f32),
                   jax.ShapeDtypeStruct((nbatch, N_KV_A, t, ncp), f32)],
        compiler_params=_cparams(("parallel", "parallel", "parallel")),
        name="cmp_attend",
    )(slopes, qz, kcv, kcv)


def _lane_repeat(x, n):
    return x if n == 1 else jnp.concatenate([x] * n, axis=1)


def _flash_init(m_ref, l_ref, acc_ref):
    m_ref[...] = jnp.full(m_ref.shape, NEG_INF, f32)
    l_ref[...] = jnp.zeros(l_ref.shape, f32)
    acc_ref[...] = jnp.zeros(acc_ref.shape, f32)


LOG2E = 1.4426950408889634
MASK_DIST = 1e33
SLC_ROWS = 128


def _attn_prompt_body(kvh, qt, slopes_ref, q_ref, k_ref, v_ref, o_ref, qb_ref, dm_ref, s_ref, mn_ref,
                      m_ref, l_ref, acc_ref, *, tq, tk, kt_lo, kt_hi, tile_dist, active):
    _flash_init(m_ref, l_ref, acc_ref)
    for g in range(GQA):
        qb_ref[g * tq:(g + 1) * tq, :] = q_ref[:, g * HEAD_DIM:(g + 1) * HEAD_DIM].astype(bf16)
    nrep = tk // LANES
    ngrp = tq // SLC_ROWS

    def group_update(groups, k, v):
        units = [(g, r) for r in groups for g in range(GQA)]
        for g, r in units:
            rows = pl.ds(g * tq + r * SLC_ROWS, SLC_ROWS)
            t2 = (_dot_nt(qb_ref[rows, :], k) * (SCALE * LOG2E)
                  - (slopes_ref[kvh, g] * LOG2E) * dm_ref[r * SLC_ROWS:(r + 1) * SLC_ROWS, :])
            s_ref[rows, :] = t2
            mx = jnp.max(t2, axis=-1, keepdims=True)
            mn_ref[rows, :] = jnp.maximum(m_ref[rows, :], jnp.broadcast_to(mx, (SLC_ROWS, LANES)))
        for g, r in units:
            rows = pl.ds(g * tq + r * SLC_ROWS, SLC_ROWS)
            m_new = mn_ref[rows, :]
            p = jnp.exp2(s_ref[rows, :] - _lane_repeat(m_new, nrep))
            alpha = jnp.exp2(m_ref[rows, :] - m_new)
            psum = p[:, 0:LANES]
            for c in range(1, nrep):
                psum = psum + p[:, c * LANES:(c + 1) * LANES]
            l_ref[rows, :] = alpha * l_ref[rows, :] + psum
            acc_ref[rows, :] = alpha * acc_ref[rows, :] + _dot(p.astype(bf16), v)
            m_ref[rows, :] = m_new

    def tile_update(kt, acts):
        ks = pl.multiple_of(kt * tk, tk)
        k = k_ref[pl.ds(ks, tk), :].astype(bf16)
        v = v_ref[pl.ds(ks, tk), :].astype(bf16)
        dm_ref[...] = tile_dist(ks)
        if acts is None:
            group_update(tuple(range(ngrp)), k, v)
        elif ngrp == 2:
            pl.when(acts[0] & acts[1])(functools.partial(group_update, (0, 1), k, v))
            pl.when(acts[0] & jnp.logical_not(acts[1]))(functools.partial(group_update, (0,), k, v))
            pl.when(acts[1] & jnp.logical_not(acts[0]))(functools.partial(group_update, (1,), k, v))
        else:
            for r in range(ngrp):
                pl.when(acts[r])(functools.partial(group_update, (r,), k, v))

    def kt_body(kt, carry):
        if active is None:
            tile_update(kt, None)
        else:
            acts = [active(kt, r) for r in range(ngrp)]
            any_act = acts[0]
            for a in acts[1:]:
                any_act = any_act | a
            pl.when(any_act)(functools.partial(tile_update, kt, acts))
        return carry

    lax.fori_loop(kt_lo, kt_hi, kt_body, 0)
    for g in range(GQA):
        rows = pl.ds(g * tq, tq)
        o_ref[:, g * HEAD_DIM:(g + 1) * HEAD_DIM] = (
            acc_ref[rows, :] / jnp.sum(l_ref[rows, :], axis=-1, keepdims=True))


def _slc_kernel(flags_ref, slopes_ref, q_ref, k_ref, v_ref, sel_ref, e_ref, o_ref, *scratch, tq, tk, t):
    b = pl.program_id(0)
    kvh = pl.program_id(1)
    qt = pl.program_id(2)
    selb = sel_ref[0, 0].astype(bf16)
    qpos = qt * tq + lax.broadcasted_iota(jnp.int32, (tq, 1), 0)
    kofs = lax.broadcasted_iota(jnp.int32, (1, tk), 1)
    ngrp = tq // SLC_ROWS
    nkt = t // tk

    def tile_dist(ks):
        maskf = _dot(selb, e_ref[:, pl.ds(ks, tk)])
        d = qpos - (ks + kofs)
        return jnp.where((maskf > 0.5) & (d >= 0), d.astype(f32), MASK_DIST)

    def active(kt, r):
        row_group = (b * N_KV_A + kvh) * (t // SLC_ROWS) + qt * ngrp + r
        return flags_ref[row_group * nkt + kt] > 0

    _attn_prompt_body(kvh, qt, slopes_ref, q_ref, k_ref, v_ref, o_ref, *scratch, tq=tq, tk=tk,
                      kt_lo=0, kt_hi=(qt * tq + tq - 1) // tk + 1, tile_dist=tile_dist, active=active)


def _attn_scratch(tq, tk):
    return [pltpu.VMEM((GQA * tq, HEAD_DIM), bf16), pltpu.VMEM((tq, tk), f32),
            pltpu.VMEM((GQA * tq, tk), f32), pltpu.VMEM((GQA * tq, LANES), f32),
            pltpu.VMEM((GQA * tq, LANES), f32), pltpu.VMEM((GQA * tq, LANES), f32),
            pltpu.VMEM((GQA * tq, HEAD_DIM), f32)]


def _slc_prompt(slopes, qz, kv_slc, sel, expand, nbatch, t, tq, tk):
    nqt = t // tq
    gw = GQA * HEAD_DIM
    ncp = sel.shape[-1]
    bpt = tk // CMP_BLOCK
    flags = sel.reshape(nbatch, N_KV_A, t // SLC_ROWS, SLC_ROWS, ncp // bpt, bpt).max(axis=(3, 5))
    flags = (flags[..., :t // tk] > 0.5).astype(jnp.int32).reshape(-1)
    gs = pltpu.PrefetchScalarGridSpec(
        num_scalar_prefetch=1,
        grid=(nbatch, N_KV_A, nqt),
        in_specs=[
            pl.BlockSpec(memory_space=pltpu.SMEM),
            pl.BlockSpec((tq, gw), lambda b, h, i, fl: (b * nqt + i, h)),
            pl.BlockSpec((t, HEAD_DIM), lambda b, h, i, fl: (b, h)),
            pl.BlockSpec((t, HEAD_DIM), lambda b, h, i, fl: (b, N_KV_A + h)),
            pl.BlockSpec((1, 1, tq, ncp), lambda b, h, i, fl: (b, h, i, 0)),
            pl.BlockSpec((ncp, t), lambda b, h, i, fl: (0, 0)),
        ],
        out_specs=pl.BlockSpec((tq, gw), lambda b, h, i, fl: (b * nqt + i, h)),
        scratch_shapes=_attn_scratch(tq, tk),
    )
    return pl.pallas_call(
        functools.partial(_slc_kernel, tq=tq, tk=tk, t=t),
        grid_spec=gs,
        out_shape=jax.ShapeDtypeStruct((nbatch * t, D_ATT), f32),
        compiler_params=_cparams(("parallel", "parallel", "parallel")),
        name="slc_prompt",
    )(flags, slopes, qz, kv_slc, kv_slc, sel, expand)


def _swa_kernel(slopes_ref, q_ref, k_ref, v_ref, o_ref, *scratch, tq, tk):
    kvh = pl.program_id(1)
    qt = pl.program_id(2)
    qpos = qt * tq + lax.broadcasted_iota(jnp.int32, (tq, 1), 0)
    kofs = lax.broadcasted_iota(jnp.int32, (1, tk), 1)

    def tile_dist(ks):
        d = qpos - (ks + kofs)
        return jnp.where((d >= 0) & (d < WINDOW), d.astype(f32), MASK_DIST)

    _attn_prompt_body(kvh, qt, slopes_ref, q_ref, k_ref, v_ref, o_ref, *scratch, tq=tq, tk=tk,
                      kt_lo=jnp.maximum(qt * tq - (WINDOW - 1), 0) // tk, kt_hi=(qt * tq + tq - 1) // tk + 1,
                      tile_dist=tile_dist, active=None)


def _swa_prompt(slopes, qz, kv_swa, nbatch, t, tq, tk):
    nqt = t // tq
    gw = GQA * HEAD_DIM
    return pl.pallas_call(
        functools.partial(_swa_kernel, tq=tq, tk=tk),
        grid=(nbatch, N_KV_A, nqt),
        in_specs=[
            pl.BlockSpec(memory_space=pltpu.SMEM),
            pl.BlockSpec((tq, gw), lambda b, h, i: (b * nqt + i, h)),
            pl.BlockSpec((t, HEAD_DIM), lambda b, h, i: (b, h)),
            pl.BlockSpec((t, HEAD_DIM), lambda b, h, i: (b, N_KV_A + h)),
        ],
        out_specs=pl.BlockSpec((tq, gw), lambda b, h, i: (b * nqt + i, h)),
        out_shape=jax.ShapeDtypeStruct((nbatch * t, D_ATT), f32),
        scratch_shapes=_attn_scratch(tq, tk),
        compiler_params=_cparams(("parallel", "parallel", "parallel")),
        name="swa_prompt",
    )(slopes, qz, kv_swa, kv_swa)


def _softplus(x):
    return jnp.maximum(x, 0.0) + jnp.log1p(jnp.exp(-jnp.abs(x)))


def _sigmoid(x):
    return 1.0 / (1.0 + jnp.exp(-x))


def _silu(x):
    return x * _sigmoid(x)


def _dot_hi(a, b):
    return jnp.dot(a, b, precision=lax.Precision.HIGHEST, preferred_element_type=f32)


INV_BASE = 16


def _bdot(a, b):
    return _dot(a.astype(bf16), b.astype(bf16))


def _gdn_kernel(x_ref, small_ref, conv0_ref, s0_ref, cw_ref, alog_ref, dtb_ref, nw_ref,
                o_ref, s_out_ref, xp_ref, st_ref, gam_ref, n_ref, t_ref, pw_ref, uw_ref,
                k_ref, kb_ref, q_ref, qg_ref, kg_ref, qk_ref, vn_ref, rhs_ref, *, c, tv, halo):
    ci = pl.program_id(1)
    nci = pl.num_programs(1)

    @pl.when(ci == 0)
    def _():
        xp_ref[...] = jnp.zeros(xp_ref.shape, f32)
        xp_ref[halo - (CONV_W - 1):halo, :] = conv0_ref[0]
        st_ref[...] = s0_ref[0]

    xp_ref[halo:halo + tv, :] = x_ref[...]
    y = xp_ref[pl.ds(halo, c), :] * cw_ref[CONV_W - 1:CONV_W, :]
    for i in range(CONV_W - 2, -1, -1):
        y = y + xp_ref[pl.ds(halo - (CONV_W - 1) + i, c), :] * cw_ref[i:i + 1, :]
    y = _silu(y)
    xp_ref[halo - (CONV_W - 1):halo, :] = xp_ref[halo + tv - (CONV_W - 1):halo + tv, :]

    ri = lax.broadcasted_iota(jnp.int32, (c, 1), 0)
    rowi = lax.broadcasted_iota(jnp.int32, (c, c), 0)
    coli = lax.broadcasted_iota(jnp.int32, (c, c), 1)
    lower = rowi >= coli
    strict = rowi > coli
    tril = lower.astype(f32)
    eye = (rowi == coli).astype(f32)
    same_blk = {}
    s = INV_BASE
    while s <= c:
        same_blk[s] = (rowi // s) == (coli // s)
        s *= 2

    sm = small_ref[...]
    if tv < c:
        sm = jnp.concatenate([sm, jnp.zeros((c - tv, LANES), f32)], axis=0)
    g_all = -jnp.exp(alog_ref[...]) * _softplus(sm + dtb_ref[...])
    beta_all = _sigmoid(sm)
    if tv < c:
        live = ri < tv
        g_all = jnp.where(live, g_all, 0.0)
        beta_all = jnp.where(live, beta_all, 0.0)
    gc_all = _dot_hi(tril, g_all)
    gc_rows = gc_all.T
    heads = range(N_HEADS_B)
    decay = []

    for h in heads:
        yq = y[:, h * HEAD_DIM:(h + 1) * HEAD_DIM]
        yk = y[:, D_GDN + h * HEAD_DIM:D_GDN + (h + 1) * HEAD_DIM]
        v = y[:, 2 * D_GDN + h * HEAD_DIM:2 * D_GDN + (h + 1) * HEAD_DIM]
        q = yq * lax.rsqrt(jnp.sum(yq * yq, axis=-1, keepdims=True) + NORM_EPS) * SCALE
        k = yk * lax.rsqrt(jnp.sum(yk * yk, axis=-1, keepdims=True) + NORM_EPS)
        if tv < c:
            q = jnp.where(live, q, 0.0)
            k = jnp.where(live, k, 0.0)
            v = jnp.where(live, v, 0.0)
        beta = beta_all[:, B_OFF + h:B_OFF + h + 1]
        gcol = gc_all[:, A_OFF + h:A_OFF + h + 1]
        grow = gc_rows[A_OFF + h:A_OFF + h + 1, :]
        gam_ref[h] = jnp.where(lower, jnp.exp(jnp.where(lower, gcol - grow, 0.0)), 0.0)
        eg = jnp.exp(gcol)
        g_last = gc_all[c - 1:c, A_OFF + h:A_OFF + h + 1]
        decay.append(jnp.exp(g_last))
        kb = k * beta
        k_ref[h] = k.astype(bf16)
        kb_ref[h] = kb.astype(bf16)
        q_ref[h] = q.astype(bf16)
        qg_ref[h] = (q * eg).astype(bf16)
        kg_ref[h] = (k * jnp.exp(g_last - gcol)).astype(bf16)
        rhs_ref[h, :, 0:HEAD_DIM] = (v * beta).astype(bf16)
        rhs_ref[h, :, HEAD_DIM:2 * HEAD_DIM] = (kb * eg).astype(bf16)

    for h in heads:
        gam = gam_ref[h]
        nmat = jnp.where(strict, _dot_nt(kb_ref[h], k_ref[h]) * gam, 0.0)
        n_ref[h] = nmat
        nd = jnp.where(same_blk[INV_BASE], nmat, 0.0)
        t_ref[h] = eye - nd
        pw_ref[h] = nd.astype(bf16)
        qk_ref[h] = (_dot_nt(q_ref[h], k_ref[h]) * gam).astype(bf16)

    for _ in range(int(math.log2(INV_BASE)) - 1):
        for h in heads:
            pw = pw_ref[h]
            pw_ref[h] = _dot(pw, pw).astype(bf16)
        for h in heads:
            t = t_ref[h]
            t_ref[h] = t + _dot(t.astype(bf16), pw_ref[h])

    s = INV_BASE
    while s < c:
        pair = same_blk[2 * s] & jnp.logical_not(same_blk[s])
        for h in heads:
            off = jnp.where(pair, n_ref[h], 0.0)
            uw_ref[h, :, 0:HEAD_DIM] = _bdot(t_ref[h], off)
        for h in heads:
            t = t_ref[h]
            t_ref[h] = t - _bdot(uw_ref[h, :, 0:HEAD_DIM], t)
        s *= 2

    for h in heads:
        uw_ref[h] = _dot(t_ref[h].astype(bf16), rhs_ref[h])

    for h in heads:
        sb = st_ref[h].astype(bf16)
        v_new = uw_ref[h, :, 0:HEAD_DIM] - _dot(uw_ref[h, :, HEAD_DIM:2 * HEAD_DIM].astype(bf16), sb)
        vn_ref[h] = v_new.astype(bf16)
        uw_ref[h, :, 0:HEAD_DIM] = _dot(qg_ref[h], sb)
    for h in heads:
        vn = vn_ref[h]
        o = uw_ref[h, :, 0:HEAD_DIM] + _dot(qk_ref[h], vn)
        st_ref[h] = st_ref[h] * decay[h] + _dot_tn(kg_ref[h], vn)
        on = o * lax.rsqrt(jnp.mean(o * o, axis=-1, keepdims=True) + NORM_EPS) * nw_ref[...]
        o_ref[:, h * HEAD_DIM:(h + 1) * HEAD_DIM] = on[:tv]

    @pl.when(ci == nci - 1)
    def _():
        s_out_ref[0] = st_ref[...]


def _gdn(qkv_b, small, conv0, s0, conv_w, alog_row, dtb_row, gdn_norm_w, nbatch, t, c):
    tv = min(c, t)
    nci = t // tv
    halo = 8
    return pl.pallas_call(
        functools.partial(_gdn_kernel, c=c, tv=tv, halo=halo),
        grid=(nbatch, nci),
        in_specs=[
            pl.BlockSpec((tv, CONV_DIM), lambda b, i: (b * nci + i, 0)),
            pl.BlockSpec((tv, LANES), lambda b, i: (b * nci + i, 0)),
            pl.BlockSpec((1, CONV_W - 1, CONV_DIM), lambda b, i: (b, 0, 0)),
            pl.BlockSpec((1, N_HEADS_B, HEAD_DIM, HEAD_DIM), lambda b, i: (b, 0, 0, 0)),
            pl.BlockSpec((CONV_W, CONV_DIM), lambda b, i: (0, 0)),
            pl.BlockSpec((1, LANES), lambda b, i: (0, 0)),
            pl.BlockSpec((1, LANES), lambda b, i: (0, 0)),
            pl.BlockSpec((1, HEAD_DIM), lambda b, i: (0, 0)),
        ],
        out_specs=[
            pl.BlockSpec((tv, D_GDN), lambda b, i: (b * nci + i, 0)),
            pl.BlockSpec((1, N_HEADS_B, HEAD_DIM, HEAD_DIM), lambda b, i: (b, 0, 0, 0)),
        ],
        out_shape=[jax.ShapeDtypeStruct((nbatch * t, D_GDN), f32),
                   jax.ShapeDtypeStruct((nbatch, N_HEADS_B, HEAD_DIM, HEAD_DIM), f32)],
        scratch_shapes=[pltpu.VMEM((halo + c, CONV_DIM), f32),
                        pltpu.VMEM((N_HEADS_B, HEAD_DIM, HEAD_DIM), f32)]
        + [pltpu.VMEM((N_HEADS_B, c, c), f32)] * 3
        + [pltpu.VMEM((N_HEADS_B, c, c), bf16)]
        + [pltpu.VMEM((N_HEADS_B, c, 2 * HEAD_DIM), f32)]
        + [pltpu.VMEM((N_HEADS_B, c, HEAD_DIM), bf16)] * 5
        + [pltpu.VMEM((N_HEADS_B, c, c), bf16)]
        + [pltpu.VMEM((N_HEADS_B, c, HEAD_DIM), bf16)]
        + [pltpu.VMEM((N_HEADS_B, c, 2 * HEAD_DIM), bf16)],
        compiler_params=_cparams(("parallel", "arbitrary")),
        name="gdn",
    )(qkv_b, small, conv0, s0, conv_w, alog_row, dtb_row, gdn_norm_w.reshape(1, HEAD_DIM))


def _merge_kernel(x_ref, oc_ref, os_ref, ow_ref, small_ref, za_ref, zb_ref, ob_ref, w_ref, fw_ref, y_ref, mix_ref):
    gates = _sigmoid(small_ref[...])
    for h in range(N_HEADS_A):
        cs = slice(h * HEAD_DIM, (h + 1) * HEAD_DIM)
        o_a = (gates[:, G_OFF + h:G_OFF + h + 1] * oc_ref[:, cs]
               + gates[:, G_OFF + N_HEADS_A + h:G_OFF + N_HEADS_A + h + 1] * os_ref[:, cs]
               + gates[:, G_OFF + 2 * N_HEADS_A + h:G_OFF + 2 * N_HEADS_A + h + 1] * ow_ref[:, cs])
        mix_ref[:, cs] = (o_a * _silu(za_ref[:, cs])).astype(bf16)
    mix_ref[:, D_ATT:] = (ob_ref[...] * _silu(zb_ref[...])).astype(bf16)
    hres = x_ref[...] + _dot(mix_ref[...], w_ref[...])
    ms = jnp.mean(hres * hres, axis=-1, keepdims=True)
    y_ref[...] = hres * lax.rsqrt(ms + NORM_EPS) * fw_ref[...]


def _merge_out(x2d, o_cmp, o_slc, o_swa, small, qz, o_b, w_out_bf, final_norm_w, tm):
    m = x2d.shape[0]
    row = lambda i: (i, 0)
    return pl.pallas_call(
        _merge_kernel,
        grid=(m // tm,),
        in_specs=[
            pl.BlockSpec((tm, D_MODEL), row),
            pl.BlockSpec((tm, D_ATT), row),
            pl.BlockSpec((tm, D_ATT), row),
            pl.BlockSpec((tm, D_ATT), row),
            pl.BlockSpec((tm, LANES), row),
            pl.BlockSpec((tm, D_ATT), lambda i: (i, 1)),
            pl.BlockSpec((tm, D_GDN), lambda i: (i, 2)),
            pl.BlockSpec((tm, D_GDN), row),
            pl.BlockSpec((D_ATT + D_GDN, D_MODEL), lambda i: (0, 0), pipeline_mode=pl.Buffered(1)),
            pl.BlockSpec((1, D_MODEL), lambda i: (0, 0)),
        ],
        out_specs=pl.BlockSpec((tm, D_MODEL), row),
        out_shape=jax.ShapeDtypeStruct((m, D_MODEL), f32),
        scratch_shapes=[pltpu.VMEM((tm, D_ATT + D_GDN), bf16)],
        compiler_params=_cparams(("parallel",)),
        name="merge_out",
    )(x2d, o_cmp, o_slc, o_swa, small, qz, qz, o_b, w_out_bf, final_norm_w.reshape(1, D_MODEL))


BLOCK_ROWS4 = CMP_BLOCK * N_CH


def _page_copies(cache_ref, buf_ref, sem_ref, pt_ref, b, j, slot, i, pps, by_row):
    page = pt_ref[b, j * pps + i]
    if not by_row:
        return [pltpu.make_async_copy(
            cache_ref.at[pl.ds(page * ROWS_PER_PAGE4, ROWS_PER_PAGE4), :],
            buf_ref.at[slot, pl.ds(i * ROWS_PER_PAGE4, ROWS_PER_PAGE4), :],
            sem_ref.at[slot])]
    bpp = PAGE_SIZE // CMP_BLOCK
    return [pltpu.make_async_copy(
        cache_ref.at[pl.ds(page * ROWS_PER_PAGE4 + n * BLOCK_ROWS4, BLOCK_ROWS4), :],
        buf_ref.at[slot, :, i * bpp + n, :],
        sem_ref.at[slot]) for n in range(bpp)]


def _paged_fetch(cache_ref, buf_ref, sem_ref, pt_ref, pps, nj_pages, need=None, by_row=False):
    b = pl.program_id(0)
    j = pl.program_id(1)
    nb = pl.num_programs(0)
    nj = pl.num_programs(1)
    step = b * nj_pages + jnp.minimum(j, nj_pages - 1)
    slot = step % 2

    def guarded(bb, jj, i, fn):
        if need is None:
            fn()
        else:
            pl.when(need(bb, jj, i))(fn)

    def run(what, bb, jj, sl, i):
        for cp in _page_copies(cache_ref, buf_ref, sem_ref, pt_ref, bb, jj, sl, i, pps, by_row):
            getattr(cp, what)()

    def start(bb, jj, sl):
        for i in range(pps):
            guarded(bb, jj, i, functools.partial(run, "start", bb, jj, sl, i))

    @pl.when((b == 0) & (j == 0))
    def _():
        start(0, 0, 0)

    @pl.when(j < nj_pages)
    def _():
        last_j = j == nj_pages - 1
        nb_ = jnp.where(last_j, b + 1, b)
        nj_ = jnp.where(last_j, 0, j + 1)

        @pl.when(nb_ < nb)
        def _():
            start(nb_, nj_, 1 - slot)

        for i in range(pps):
            guarded(b, j, i, functools.partial(run, "wait", b, j, slot, i))

    return slot


def _compress_paged_kernel(pt_ref, cache_ref, pe_ref, w_ref, o_ref, buf_ref, sem_ref, *, pps):
    slot = _paged_fetch(cache_ref, buf_ref, sem_ref, pt_ref, pps, pl.num_programs(1), by_row=True)
    nbk = pps * (PAGE_SIZE // CMP_BLOCK)
    accs = _compress_accumulate(lambda l, ch: buf_ref[slot, l * N_CH + ch], pe_ref, w_ref, nbk)
    for c in range(2):
        for h in range(N_KV_A):
            o_ref[0, c * N_KV_A + h] = accs[c][h * nbk:(h + 1) * nbk]


def _compress_paged(page_table, cache4, pe2d, w_cmp_bf, pps):
    nbatch, n_pages = page_table.shape
    nj = n_pages // pps
    nbk = pps * (PAGE_SIZE // CMP_BLOCK)
    gs = pltpu.PrefetchScalarGridSpec(
        num_scalar_prefetch=1,
        grid=(nbatch, nj),
        in_specs=[
            pl.BlockSpec(memory_space=pl.ANY),
            pl.BlockSpec((2, CMP_BLOCK, HEAD_DIM), lambda b, j, pt: (0, 0, 0)),
            pl.BlockSpec((2, CMP_BLOCK // 2, 2 * HEAD_DIM, HEAD_DIM), lambda b, j, pt: (0, 0, 0, 0)),
        ],
        out_specs=pl.BlockSpec((1, N_CH, nbk, HEAD_DIM), lambda b, j, pt: (b, 0, j, 0)),
        scratch_shapes=[pltpu.VMEM((2, BLOCK_ROWS4, nbk, HEAD_DIM), f32), pltpu.SemaphoreType.DMA((2,))],
    )
    return pl.pallas_call(
        functools.partial(_compress_paged_kernel, pps=pps),
        grid_spec=gs,
        out_shape=jax.ShapeDtypeStruct((nbatch, N_CH, nj * nbk, HEAD_DIM), f32),
        compiler_params=_cparams(("arbitrary", "arbitrary")),
        name="compress_paged",
    )(page_table, cache4, pe2d, w_cmp_bf)


def _two_phase_attend(qs, slope2, qpos, kpos0, load_k, load_v, load_mask, nchunks, chunk,
                      s_ref, rows, m_ref, l_ref, acc_ref, active=None):
    nrow = qs.shape[0]
    nrep = chunk // LANES
    kofs = lax.broadcasted_iota(jnp.int32, (1, chunk), 1)
    unroll = math.gcd(nchunks, 4)

    def cols(ci):
        return pl.ds(pl.multiple_of(ci * chunk, chunk), chunk)

    def skippable(ci, fn, carry):
        if active is None:
            return fn(carry)
        return lax.cond(active(ci), fn, lambda c: c, carry)

    def phase1(ci, mrun):
        def update(mrun):
            d = qpos - (kpos0 + ci * chunk + kofs)
            dmask = jnp.where((load_mask(ci) > 0.5) & (d >= 0), d.astype(f32), MASK_DIST)
            t2 = _dot_nt(qs, load_k(ci)) * (SCALE * LOG2E) - slope2 * dmask
            s_ref[rows, cols(ci)] = t2
            for c in range(nrep):
                mrun = jnp.maximum(mrun, t2[:, c * LANES:(c + 1) * LANES])
            return mrun

        return skippable(ci, update, mrun)

    mrun = lax.fori_loop(0, nchunks, phase1, jnp.full((nrow, LANES), NEG_INF, f32), unroll=unroll)
    m_prev = m_ref[rows, :]
    m_new = jnp.maximum(m_prev, jnp.broadcast_to(jnp.max(mrun, axis=-1, keepdims=True), (nrow, LANES)))

    def phase2(ci, carry):
        def update(carry):
            lp, acc = carry
            p = jnp.exp2(s_ref[rows, cols(ci)] - _lane_repeat(m_new, nrep))
            for c in range(nrep):
                lp = lp + p[:, c * LANES:(c + 1) * LANES]
            return lp, acc + _dot(p.astype(bf16), load_v(ci))

        return skippable(ci, update, carry)

    z = jnp.zeros((nrow, LANES), f32)
    lp, acc = lax.fori_loop(0, nchunks, phase2, (z, z), unroll=unroll)
    alpha = jnp.exp2(m_prev - m_new)
    l_ref[rows, :] = alpha * l_ref[rows, :] + lp
    acc_ref[rows, :] = alpha * acc_ref[rows, :] + acc
    m_ref[rows, :] = m_new


def _slc_paged_kernel(pt_ref, chunk_any_ref, chunk_kvh_ref, slopes_ref, cache_ref, q_ref, tail_ref, sel_ref,
                      e_ref, o_ref, buf_ref, sem_ref, s_ref, m_ref, l_ref, acc_ref, *, pps, t, sub, pos0):
    b = pl.program_id(0)
    j = pl.program_id(1)
    nj_pages = pl.num_programs(1) - 1
    ppc = sub // PAGE_SIZE
    cps = pps // ppc
    slot = _paged_fetch(cache_ref, buf_ref, sem_ref, pt_ref, pps, nj_pages,
                        need=lambda bb, jj, i: chunk_any_ref[bb, jj * cps + i // ppc] > 0)
    nrow = GQA * t

    @pl.when(j == 0)
    def _():
        _flash_init(m_ref, l_ref, acc_ref)

    tpos = lax.broadcasted_iota(jnp.int32, (t, 1), 0)
    qpos = pos0 + jnp.concatenate([tpos] * GQA, axis=0)

    def attend(kvh, load_k, load_v, nchunks, chunk, kpos0, active=None):
        q = q_ref[:, kvh * GQA * HEAD_DIM:(kvh + 1) * GQA * HEAD_DIM]
        qs = jnp.concatenate([q[:, g * HEAD_DIM:(g + 1) * HEAD_DIM] for g in range(GQA)], axis=0).astype(bf16)
        slope2 = jnp.concatenate([jnp.full((t, 1), slopes_ref[kvh, g] * LOG2E, f32) for g in range(GQA)], axis=0)
        selb = sel_ref[0, kvh].astype(bf16)

        def load_mask(ci):
            maskf = _dot(selb, e_ref[:, pl.ds(pl.multiple_of(ci * chunk, chunk), chunk)])
            return jnp.concatenate([maskf] * GQA, axis=0)

        _two_phase_attend(qs, slope2, qpos, kpos0, load_k, load_v, load_mask, nchunks, chunk,
                          s_ref, pl.ds(kvh * nrow, nrow), m_ref, l_ref, acc_ref, active)

    @pl.when(j < nj_pages)
    def _():
        for kvh in range(N_KV_A):
            def load(ci, ch):
                return buf_ref[slot, pl.ds(ci * (sub * N_CH) + ch, sub, stride=N_CH), :].astype(bf16)

            first = ((b * N_KV_A + kvh) * nj_pages + j) * cps
            attend(kvh, functools.partial(load, ch=kvh), functools.partial(load, ch=N_KV_A + kvh),
                   cps, sub, j * pps * PAGE_SIZE, lambda ci, first=first: chunk_kvh_ref[first + ci] > 0)

    @pl.when(j == nj_pages)
    def _():
        ntail = tail_ref.shape[1]
        for kvh in range(N_KV_A):
            k = tail_ref[0, :, kvh * HEAD_DIM:(kvh + 1) * HEAD_DIM].astype(bf16)
            v = tail_ref[0, :, (N_KV_A + kvh) * HEAD_DIM:(N_KV_A + kvh + 1) * HEAD_DIM].astype(bf16)
            attend(kvh, lambda ci, k=k: k, lambda ci, v=v: v, 1, ntail, nj_pages * pps * PAGE_SIZE)
        for kvh in range(N_KV_A):
            for g in range(GQA):
                r = pl.ds(kvh * nrow + g * t, t)
                o_ref[:, (kvh * GQA + g) * HEAD_DIM:(kvh * GQA + g + 1) * HEAD_DIM] = (
                    acc_ref[r, :] / jnp.sum(l_ref[r, :], axis=-1, keepdims=True))


def _slc_paged(page_table, slopes, cache4, qz, tail, sel, expand, t, pps, pos0):
    nbatch, n_pages = page_table.shape
    njp = n_pages // pps
    bps = pps * (PAGE_SIZE // CMP_BLOCK)
    sub = min(512, pps * PAGE_SIZE)
    bpc = sub // CMP_BLOCK
    n_pb = n_pages * (PAGE_SIZE // CMP_BLOCK)
    chunk_sel = sel[..., :n_pb].reshape(nbatch, N_KV_A, t, n_pb // bpc, bpc).max(axis=(2, 4)) > 0.5
    chunk_kvh = chunk_sel.astype(jnp.int32).reshape(-1)
    chunk_any = chunk_sel.any(axis=1).astype(jnp.int32)
    gs = pltpu.PrefetchScalarGridSpec(
        num_scalar_prefetch=3,
        grid=(nbatch, njp + 1),
        in_specs=[
            pl.BlockSpec(memory_space=pltpu.SMEM),
            pl.BlockSpec(memory_space=pl.ANY),
            pl.BlockSpec((t, D_ATT), lambda b, j, *_: (b, 0)),
            pl.BlockSpec((1, LANES, KV_BRANCH), lambda b, j, *_: (b, 0, 0)),
            pl.BlockSpec((1, N_KV_A, t, bps), lambda b, j, *_: (b, 0, 0, j)),
            pl.BlockSpec((bps, pps * PAGE_SIZE), lambda b, j, *_: (0, 0)),
        ],
        out_specs=pl.BlockSpec((t, D_ATT), lambda b, j, *_: (b, 0)),
        scratch_shapes=[pltpu.VMEM((2, pps * ROWS_PER_PAGE4, HEAD_DIM), f32), pltpu.SemaphoreType.DMA((2,)),
                        pltpu.VMEM((N_KV_A * GQA * t, pps * PAGE_SIZE), f32),
                        pltpu.VMEM((N_KV_A * GQA * t, LANES), f32), pltpu.VMEM((N_KV_A * GQA * t, LANES), f32),
                        pltpu.VMEM((N_KV_A * GQA * t, HEAD_DIM), f32)],
    )
    return pl.pallas_call(
        functools.partial(_slc_paged_kernel, pps=pps, t=t, sub=sub, pos0=pos0),
        grid_spec=gs,
        out_shape=jax.ShapeDtypeStruct((nbatch * t, D_ATT), f32),
        compiler_params=_cparams(("arbitrary", "arbitrary")),
        name="slc_paged",
    )(page_table, chunk_any, chunk_kvh, slopes, cache4, qz, tail, sel, expand)


def _swa_sample_kernel(slopes_ref, q_ref, win_ref, tail_ref, o_ref, *, t, wbuf, pos0):
    tpos = lax.broadcasted_iota(jnp.int32, (t, 1), 0)
    qpos = pos0 + jnp.concatenate([tpos] * GQA, axis=0)
    nk = wbuf + CMP_BLOCK
    kpos = pos0 - wbuf + lax.broadcasted_iota(jnp.int32, (1, nk), 1)
    d = qpos - kpos
    ok = (d >= 0) & (d < WINDOW)
    df = d.astype(f32)
    for kvh in range(N_KV_A):
        k = jnp.concatenate([win_ref[pl.ds(kvh, wbuf, stride=N_CH), :],
                             tail_ref[0, :, kvh * HEAD_DIM:(kvh + 1) * HEAD_DIM]], axis=0).astype(bf16)
        v = jnp.concatenate([win_ref[pl.ds(N_KV_A + kvh, wbuf, stride=N_CH), :],
                             tail_ref[0, :, (N_KV_A + kvh) * HEAD_DIM:(N_KV_A + kvh + 1) * HEAD_DIM]],
                            axis=0).astype(bf16)
        q = q_ref[:, kvh * GQA * HEAD_DIM:(kvh + 1) * GQA * HEAD_DIM]
        qs = jnp.concatenate([q[:, g * HEAD_DIM:(g + 1) * HEAD_DIM] for g in range(GQA)], axis=0).astype(bf16)
        slope = jnp.concatenate([jnp.full((t, 1), slopes_ref[kvh, g], f32) for g in range(GQA)], axis=0)
        s = _dot_nt(qs, k) * SCALE - slope * df
        s = jnp.where(ok, s, NEG_INF)
        e = jnp.exp(s - jnp.max(s, axis=-1, keepdims=True))
        p = jnp.where(ok, e / jnp.sum(e, axis=-1, keepdims=True), 0.0)
        o = _dot(p.astype(bf16), v)
        for g in range(GQA):
            o_ref[:, (kvh * GQA + g) * HEAD_DIM:(kvh * GQA + g + 1) * HEAD_DIM] = o[g * t:(g + 1) * t]


def _swa_sample(slopes, qz, win4, tail, nbatch, t, wbuf, pos0):
    return pl.pallas_call(
        functools.partial(_swa_sample_kernel, t=t, wbuf=wbuf, pos0=pos0),
        grid=(nbatch,),
        in_specs=[
            pl.BlockSpec(memory_space=pltpu.SMEM),
            pl.BlockSpec((t, D_ATT), lambda b: (b, 0)),
            pl.BlockSpec((wbuf * N_CH, HEAD_DIM), lambda b: (b, 0)),
            pl.BlockSpec((1, CMP_BLOCK, KV_BRANCH), lambda b: (b, 0, 0)),
        ],
        out_specs=pl.BlockSpec((t, D_ATT), lambda b: (b, 0)),
        out_shape=jax.ShapeDtypeStruct((nbatch * t, D_ATT), f32),
        compiler_params=_cparams(("parallel",)),
        name="swa_sample",
    )(slopes, qz, win4, tail)


def _round_up(a, b):
    return -(-a // b) * b


def _pad_blocks(kcv, ncp):
    return jnp.pad(kcv, ((0, 0), (0, 0), (0, ncp - kcv.shape[2]), (0, 0)))


def _expand_matrix(nblocks, nkeys):
    return (jnp.arange(nkeys)[None, :] // CMP_BLOCK == jnp.arange(nblocks)[:, None]).astype(bf16)


def _kv6(rows2d, nbatch, t):
    return rows2d.reshape(nbatch, t, 2, N_KV_A, HEAD_DIM)


def _layer_prompt(x, prm):
    nbatch, t, _ = x.shape
    m = nbatch * t
    x2d = x.reshape(m, D_MODEL)
    tm = min(1024, m)
    qz, qkvb, kv_slc, kv_swa, cmp4, slc4, swa4, small = _in_proj(
        x2d, prm["norm_w"], prm["w_main"], prm["w_small"], tm)
    nc = t // CMP_BLOCK
    kcv = _compress(cmp4, nbatch, prm["pe2d"], prm["w_cmp"], min(64, nc))
    kcv = _pad_blocks(kcv, _round_up(nc, LANES))
    tq = min(256, t)
    o_cmp, sel = _cmp_attend(prm["slopes"], qz, kcv, nbatch, t, tq, 0, min(N_SELECT, nc))
    tk = min(256, t)
    o_slc = _slc_prompt(prm["slopes"], qz, kv_slc, sel, _expand_matrix(sel.shape[-1], t), nbatch, t, tq, tk)
    o_swa = _swa_prompt(prm["slopes"], qz, kv_swa, nbatch, t, tq, tk)
    conv0 = jnp.zeros((nbatch, CONV_W - 1, CONV_DIM), f32)
    s0 = jnp.zeros((nbatch, N_HEADS_B, HEAD_DIM, HEAD_DIM), f32)
    o_b, s_new = _gdn(qkvb, small, conv0, s0, prm["conv_w"], prm["alog_row"], prm["dtb_row"],
                      prm["gdn_norm_w"], nbatch, t, 128)
    y = _merge_out(x2d, o_cmp, o_slc, o_swa, small, qz, o_b, prm["w_out"], prm["final_norm_w"], min(512, m))
    w = min(WINDOW, t)
    conv_new = qkvb.reshape(nbatch, t, CONV_DIM)[:, t - (CONV_W - 1):]
    return (y.reshape(nbatch, t, D_MODEL), _kv6(cmp4, nbatch, t), _kv6(slc4, nbatch, t),
            _kv6(swa4, nbatch, t)[:, t - w:], conv_new, s_new)


def _layer_sample(x, cache_cmp, cache_slc, cache_swa, conv_buf, s0, page_table, prm):
    nbatch, t, _ = x.shape
    m = nbatch * t
    n_pages = page_table.shape[1]
    past_len = n_pages * PAGE_SIZE
    wbuf = cache_swa.shape[1]
    x2d = x.reshape(m, D_MODEL)
    qz, qkvb, kv_slc, kv_swa, cmp4, slc4, swa4, small = _in_proj(
        x2d, prm["norm_w"], prm["w_main"], prm["w_small"], m)
    pad_tail = lambda r: jnp.pad(r.reshape(nbatch, t, KV_BRANCH), ((0, 0), (0, CMP_BLOCK - t), (0, 0)))
    pps = min(32, n_pages)
    kc_past = _compress_paged(page_table, cache_cmp.reshape(-1, HEAD_DIM), prm["pe2d"], prm["w_cmp"], pps)
    kc_tail = _compress(pad_tail(cmp4).reshape(nbatch * CMP_BLOCK * N_CH, HEAD_DIM), 1, prm["pe2d"], prm["w_cmp"], nbatch)
    kc_tail = jnp.swapaxes(kc_tail[0], 0, 1)[:, :, None, :]
    n_pb = past_len // CMP_BLOCK
    pps_s = min(64, n_pages)
    bps = pps_s * (PAGE_SIZE // CMP_BLOCK)
    ncp = _round_up(n_pb + 1, max(LANES, bps))
    kcv = _pad_blocks(jnp.concatenate([kc_past, kc_tail], axis=2), ncp)
    o_cmp, sel = _cmp_attend(prm["slopes"], qz, kcv, nbatch, t, t, past_len, min(N_SELECT, n_pb + 1))
    o_slc = _slc_paged(page_table, prm["slopes"], cache_slc.reshape(-1, HEAD_DIM), qz,
                       jnp.pad(kv_slc.reshape(nbatch, t, KV_BRANCH), ((0, 0), (0, LANES - t), (0, 0))), sel,
                       _expand_matrix(bps, pps_s * PAGE_SIZE), t, pps_s, past_len)
    o_swa = _swa_sample(prm["slopes"], qz, cache_swa.reshape(-1, HEAD_DIM), pad_tail(kv_swa), nbatch, t, wbuf, past_len)
    o_b, s_new = _gdn(qkvb, small, conv_buf, s0, prm["conv_w"], prm["alog_row"], prm["dtb_row"],
                      prm["gdn_norm_w"], nbatch, t, 128)
    y = _merge_out(x2d, o_cmp, o_slc, o_swa, small, qz, o_b, prm["w_out"], prm["final_norm_w"], min(512, m))
    kv_win = jnp.concatenate([cache_swa, _kv6(swa4, nbatch, t)], axis=1)[:, t:]
    conv_new = jnp.concatenate([conv_buf, qkvb.reshape(nbatch, t, CONV_DIM)], axis=1)[:, t:]
    return (y.reshape(nbatch, t, D_MODEL), _kv6(cmp4, nbatch, t), _kv6(slc4, nbatch, t), kv_win, conv_new, s_new)


def kernel(x_prompt, x_sample, cache_cmp, cache_slc, cache_swa, state_conv, state_gdn, page_table,
           norm_w, w_in, pe_cmp, w_cmp, conv_w, a_log, dt_bias, gdn_norm_w, w_out, final_norm_w):
    depth = norm_w.shape[0]
    assert depth == 1, "the final norm is fused into the single layer's output projection"
    l = 0
    w_main, w_small = _prep_w_in(w_in[l])
    head = jnp.arange(1, N_HEADS_A + 1, dtype=f32)
    lane_row = lambda vals: jnp.zeros((1, LANES), f32).at[0, A_OFF:A_OFF + N_HEADS_B].set(vals.astype(f32))
    prm = {
        "norm_w": norm_w[l], "w_main": w_main, "w_small": w_small,
        "pe2d": jnp.swapaxes(pe_cmp[l], 0, 1),
        "w_cmp": jnp.swapaxes(w_cmp[l], 0, 1).reshape(2, CMP_BLOCK // 2, 2 * HEAD_DIM, HEAD_DIM).astype(bf16),
        "conv_w": conv_w[l], "alog_row": lane_row(a_log[l]), "dtb_row": lane_row(dt_bias[l]),
        "gdn_norm_w": gdn_norm_w[l], "w_out": w_out[l].astype(bf16), "final_norm_w": final_norm_w,
        "slopes": jnp.exp2(-8.0 * head / N_HEADS_A).reshape(N_KV_A, GQA),
    }
    yp, p_cmp, p_slc, p_swa, p_conv, p_gdn = _layer_prompt(x_prompt, prm)
    ys, s_cmp, s_slc, s_swa, s_conv, s_gdn = _layer_sample(
        x_sample, cache_cmp[l], cache_slc[l], cache_swa[l], state_conv[l], state_gdn[l], page_table, prm)
    st = lambda a: a[None]
    return (yp, ys, st(p_cmp), st(p_slc), st(p_swa), st(p_conv), st(p_gdn),
            st(s_cmp), st(s_slc), st(s_swa), st(s_conv), st(s_gdn))
```

```python
import functools
import math

import jax
import jax.numpy as jnp
from jax import lax
from jax.experimental import pallas as pl
from jax.experimental.pallas import tpu as pltpu

f32 = jnp.float32
bf16 = jnp.bfloat16

D_MODEL = 2048
HEAD_DIM = 128
N_HEADS_A = 8
N_KV_A = 2
GQA = 4
D_ATT = N_HEADS_A * HEAD_DIM
CMP_BLOCK = 64
N_SELECT = 16
WINDOW = 512
N_HEADS_B = 8
D_GDN = N_HEADS_B * HEAD_DIM
CONV_W = 4
CONV_DIM = 3 * D_GDN
PAGE_SIZE = 128
NORM_EPS = 1e-6
NEG_INF = -1e30
FORCE_SCORE = 1e4
KV_BRANCH = 2 * N_KV_A * HEAD_DIM
N_CH = 2 * N_KV_A
ROWS_PER_PAGE4 = PAGE_SIZE * N_CH
SCALE = HEAD_DIM ** -0.5
G_OFF, A_OFF, B_OFF = 0, 3 * N_HEADS_A, 3 * N_HEADS_A + N_HEADS_B
LANES = 128
VMEM_LIMIT = 56 * 1024 * 1024
IN_PROJ_VMEM_LIMIT = 60 * 1024 * 1024

IN_TN = 512
N_QZ_TILES = 3 * D_ATT // IN_TN
N_B_TILES = CONV_DIM // IN_TN
N_IN_TILES = N_QZ_TILES + N_B_TILES + 3


def _cparams(sem):
    return pltpu.CompilerParams(dimension_semantics=sem, vmem_limit_bytes=VMEM_LIMIT)


def _dot(a, b):
    return jnp.dot(a, b, preferred_element_type=f32)


def _dot_nt(a, b):
    return lax.dot_general(a, b, (((1,), (1,)), ((), ())), preferred_element_type=f32)


def _dot_tn(a, b):
    return lax.dot_general(a, b, (((0,), (0,)), ((), ())), preferred_element_type=f32)


def _store_row4(o4_ref, acc):
    for ch in range(N_CH):
        o4_ref[pl.ds(ch, acc.shape[0], stride=N_CH), :] = acc[:, ch * HEAD_DIM:(ch + 1) * HEAD_DIM]


def _in_proj_kernel(x_ref, nw_ref, w_ref, ws_ref, qz_ref, b_ref, slc_ref, swa_ref, cmp4_ref, slc4_ref, swa4_ref,
                    small_ref, xn_ref):
    n = pl.program_id(1)

    @pl.when(n == 0)
    def _():
        x = x_ref[...]
        ms = jnp.mean(x * x, axis=-1, keepdims=True)
        xn = (x * lax.rsqrt(ms + NORM_EPS) * nw_ref[...]).astype(bf16)
        xn_ref[...] = xn
        small_ref[...] = _dot(xn, ws_ref[...])

    acc = _dot(xn_ref[...], w_ref[...])

    @pl.when(n < N_QZ_TILES)
    def _():
        qz_ref[...] = acc

    @pl.when((n >= N_QZ_TILES) & (n < N_QZ_TILES + N_B_TILES))
    def _():
        b_ref[...] = acc

    @pl.when(n == N_QZ_TILES + N_B_TILES)
    def _():
        _store_row4(cmp4_ref, acc)

    @pl.when(n == N_QZ_TILES + N_B_TILES + 1)
    def _():
        slc_ref[...] = acc.astype(bf16)
        _store_row4(slc4_ref, acc)

    @pl.when(n == N_QZ_TILES + N_B_TILES + 2)
    def _():
        swa_ref[...] = acc.astype(bf16)
        _store_row4(swa4_ref, acc)


def _in_proj(x2d, norm_w, w_main, w_small, tm):
    m = x2d.shape[0]
    S = jax.ShapeDtypeStruct
    nb0 = N_QZ_TILES
    return pl.pallas_call(
        _in_proj_kernel,
        grid=(m // tm, N_IN_TILES),
        in_specs=[
            pl.BlockSpec((tm, D_MODEL), lambda i, n: (i, 0)),
            pl.BlockSpec((1, D_MODEL), lambda i, n: (0, 0)),
            pl.BlockSpec((D_MODEL, IN_TN), lambda i, n: (0, n)),
            pl.BlockSpec((D_MODEL, LANES), lambda i, n: (0, 0)),
        ],
        out_specs=[
            pl.BlockSpec((tm, IN_TN), lambda i, n: (i, jnp.minimum(n, nb0 - 1))),
            pl.BlockSpec((tm, IN_TN), lambda i, n: (i, jnp.clip(n - nb0, 0, N_B_TILES - 1))),
            pl.BlockSpec((tm, KV_BRANCH), lambda i, n: (i, 0)),
            pl.BlockSpec((tm, KV_BRANCH), lambda i, n: (i, 0)),
            pl.BlockSpec((tm * N_CH, HEAD_DIM), lambda i, n: (i, 0)),
            pl.BlockSpec((tm * N_CH, HEAD_DIM), lambda i, n: (i, 0)),
            pl.BlockSpec((tm * N_CH, HEAD_DIM), lambda i, n: (i, 0)),
            pl.BlockSpec((tm, LANES), lambda i, n: (i, 0)),
        ],
        out_shape=[S((m, 3 * D_ATT), f32), S((m, CONV_DIM), f32), S((m, KV_BRANCH), bf16), S((m, KV_BRANCH), bf16),
                   S((m * N_CH, HEAD_DIM), f32), S((m * N_CH, HEAD_DIM), f32), S((m * N_CH, HEAD_DIM), f32),
                   S((m, LANES), f32)],
        scratch_shapes=[pltpu.VMEM((tm, D_MODEL), bf16)],
        compiler_params=pltpu.CompilerParams(dimension_semantics=("parallel", "arbitrary"),
                                             vmem_limit_bytes=IN_PROJ_VMEM_LIMIT),
        name="in_proj",
    )(x2d, norm_w.reshape(1, D_MODEL), w_main, w_small)


def _prep_w_in(w_in):
    o = [0]
    for s in (D_ATT, 3 * KV_BRANCH, 3 * N_HEADS_A, D_ATT, CONV_DIM, N_HEADS_B, N_HEADS_B, D_GDN):
        o.append(o[-1] + s)
    q_a, kv_a, g_a, z_a, qkv_b, a_b, b_b, z_b = (w_in[:, o[i]:o[i + 1]] for i in range(8))
    w_main = jnp.concatenate([q_a, z_a, z_b, qkv_b, kv_a], axis=1).astype(bf16)
    pad = jnp.zeros((w_in.shape[0], LANES - (3 * N_HEADS_A + 2 * N_HEADS_B)), w_in.dtype)
    w_small = jnp.concatenate([g_a, a_b, b_b, pad], axis=1).astype(bf16)
    return w_main, w_small


CMP_UNROLL = 4


def _compress_accumulate(load_rows, pe_ref, w_ref, nbk):
    def body(i, accs):
        new = []
        for c in range(2):
            halves = []
            for dl in range(2):
                l = 2 * i + dl
                pe_row = pe_ref[c, pl.ds(l, 1), :]
                halves.append(jnp.concatenate(
                    [load_rows(l, c * N_KV_A + h) + pe_row for h in range(N_KV_A)], axis=0))
            lhs = jnp.concatenate(halves, axis=1).astype(bf16)
            new.append(accs[c] + _dot(lhs, w_ref[c, i]))
        return tuple(new)

    z = jnp.zeros((N_KV_A * nbk, HEAD_DIM), f32)
    return lax.fori_loop(0, CMP_BLOCK // 2, body, (z, z), unroll=CMP_UNROLL)


def _compress_kernel(x_ref, pe_ref, w_ref, o_ref, *, nbk):
    accs = _compress_accumulate(
        lambda l, ch: x_ref[pl.ds(l * N_CH + ch, nbk, stride=CMP_BLOCK * N_CH), :], pe_ref, w_ref, nbk)
    for c in range(2):
        for h in range(N_KV_A):
            o_ref[0, c * N_KV_A + h] = accs[c][h * nbk:(h + 1) * nbk]


def _compress(rows4, nbatch, pe2d, w_cmp_bf, nbk):
    nblk = rows4.shape[0] // (nbatch * CMP_BLOCK * N_CH)
    nj = nblk // nbk
    return pl.pallas_call(
        functools.partial(_compress_kernel, nbk=nbk),
        grid=(nbatch, nj),
        in_specs=[
            pl.BlockSpec((nbk * CMP_BLOCK * N_CH, HEAD_DIM), lambda b, j: (b * nj + j, 0)),
            pl.BlockSpec((2, CMP_BLOCK, HEAD_DIM), lambda b, j: (0, 0, 0)),
            pl.BlockSpec((2, CMP_BLOCK // 2, 2 * HEAD_DIM, HEAD_DIM), lambda b, j: (0, 0, 0, 0)),
        ],
        out_specs=pl.BlockSpec((1, N_CH, nbk, HEAD_DIM), lambda b, j: (b, 0, j, 0)),
        out_shape=jax.ShapeDtypeStruct((nbatch, N_CH, nblk, HEAD_DIM), f32),
        compiler_params=_cparams(("parallel", "parallel")),
        name="compress",
    )(rows4, pe2d, w_cmp_bf)


def _cmp_kernel(slopes_ref, q_ref, kc_ref, vc_ref, o_ref, sel_ref, *, tq, tqp, ncp, pos0, nsel):
    kvh = pl.program_id(1)
    qt = pl.program_id(2)
    q = q_ref[...]
    if tqp > tq:
        q = jnp.concatenate([q, jnp.zeros((tqp - tq, q.shape[1]), f32)], axis=0)
    kc = kc_ref[0, 0].astype(bf16)
    vc = vc_ref[0, 0].astype(bf16)
    qpos = pos0 + qt * tq + lax.broadcasted_iota(jnp.int32, (1, tqp), 1)
    jblk = lax.broadcasted_iota(jnp.int32, (ncp, 1), 0)
    d = qpos - ((jblk + 1) * CMP_BLOCK - 1)
    ok = d >= 0
    df = d.astype(f32)
    imp = jnp.zeros((ncp, tqp), f32)
    for g in range(GQA):
        qg = q[:, g * HEAD_DIM:(g + 1) * HEAD_DIM].astype(bf16)
        s = _dot_nt(kc, qg) * SCALE
        s = jnp.where(ok, s - slopes_ref[kvh, g] * df, NEG_INF)
        mx = jnp.max(s, axis=0, keepdims=True)
        e = jnp.exp(s - mx)
        p = jnp.where(ok, e / jnp.sum(e, axis=0, keepdims=True), 0.0)
        imp = imp + p
        og = _dot_tn(p.astype(bf16), vc)
        o_ref[:, g * HEAD_DIM:(g + 1) * HEAD_DIM] = og[:tq]
    cur = qpos // CMP_BLOCK
    forced = (jblk == cur) | (jblk == 0)
    score = jnp.where(forced, FORCE_SCORE, jnp.where(jblk <= cur, imp, -1.0))
    for cg in range(tqp // LANES):
        sc = score[:, cg * LANES:(cg + 1) * LANES]
        work = sc
        picked = jnp.zeros((ncp, LANES), f32)
        for _ in range(nsel):
            mx = jnp.max(work, axis=0, keepdims=True)
            first = jnp.min(jnp.where(work == mx, jblk, ncp), axis=0, keepdims=True)
            hit = jblk == first
            picked = jnp.where(hit, 1.0, picked)
            work = jnp.where(hit, -2.0, work)
        sel = jnp.where(sc >= 0.0, picked, 0.0)
        lo = cg * LANES
        hi = min(tq, lo + LANES)
        sel_ref[0, 0, lo:hi, :] = sel.T[:hi - lo]


def _cmp_attend(slopes, qz, kcv, nbatch, t, tq, pos0, nsel):
    ncp = kcv.shape[2]
    tqp = max(tq, LANES)
    nqt = t // tq
    gw = GQA * HEAD_DIM
    return pl.pallas_call(
        functools.partial(_cmp_kernel, tq=tq, tqp=tqp, ncp=ncp, pos0=pos0, nsel=nsel),
        grid=(nbatch, N_KV_A, nqt),
        in_specs=[
            pl.BlockSpec(memory_space=pltpu.SMEM),
            pl.BlockSpec((tq, gw), lambda b, h, i: (b * nqt + i, h)),
            pl.BlockSpec((1, 1, ncp, HEAD_DIM), lambda b, h, i: (b, h, 0, 0)),
            pl.BlockSpec((1, 1, ncp, HEAD_DIM), lambda b, h, i: (b, N_KV_A + h, 0, 0)),
        ],
        out_specs=[
            pl.BlockSpec((tq, gw), lambda b, h, i: (b * nqt + i, h)),
            pl.BlockSpec((1, 1, tq, ncp), lambda b, h, i: (b, h, i, 0)),
        ],
        out_shape=[jax.ShapeDtypeStruct((nbatch * t, D_ATT), f32),
                   jax.ShapeDtypeStruct((nbatch, N_KV_A, t, ncp), f32)],
        compiler_params=_cparams(("parallel", "parallel", "parallel")),
        name="cmp_attend",
    )(slopes, qz, kcv, kcv)


def _lane_repeat(x, n):
    return x if n == 1 else jnp.concatenate([x] * n, axis=1)


def _flash_init(m_ref, l_ref, acc_ref):
    m_ref[...] = jnp.full(m_ref.shape, NEG_INF, f32)
    l_ref[...] = jnp.zeros(l_ref.shape, f32)
    acc_ref[...] = jnp.zeros(acc_ref.shape, f32)


LOG2E = 1.4426950408889634
MASK_DIST = 1e33
SLC_ROWS = 128


def _attn_prompt_body(kvh, qt, slopes_ref, q_ref, k_ref, v_ref, o_ref, qb_ref, dm_ref, s_ref, mn_ref,
                      m_ref, l_ref, acc_ref, *, tq, tk, kt_lo, kt_hi, tile_dist, active):
    _flash_init(m_ref, l_ref, acc_ref)
    for g in range(GQA):
        qb_ref[g * tq:(g + 1) * tq, :] = (q_ref[:, g * HEAD_DIM:(g + 1) * HEAD_DIM] * (SCALE * LOG2E)).astype(bf16)
    nrep = tk // LANES
    ngrp = tq // SLC_ROWS

    def group_update(groups, k, v):
        units = [(g, r) for r in groups for g in range(GQA)]
        for g, r in units:
            rows = pl.ds(g * tq + r * SLC_ROWS, SLC_ROWS)
            t2 = (_dot_nt(qb_ref[rows, :], k)
                  - (slopes_ref[kvh, g] * LOG2E) * dm_ref[r * SLC_ROWS:(r + 1) * SLC_ROWS, :])
            s_ref[rows, :] = t2
            mx = jnp.max(t2, axis=-1, keepdims=True)
            mn_ref[rows, :] = jnp.maximum(m_ref[rows, :], jnp.broadcast_to(mx, (SLC_ROWS, LANES)))
        for g, r in units:
            rows = pl.ds(g * tq + r * SLC_ROWS, SLC_ROWS)
            m_new = mn_ref[rows, :]
            p = jnp.exp2(s_ref[rows, :] - _lane_repeat(m_new, nrep))
            alpha = jnp.exp2(m_ref[rows, :] - m_new)
            psum = p[:, 0:LANES]
            for c in range(1, nrep):
                psum = psum + p[:, c * LANES:(c + 1) * LANES]
            l_ref[rows, :] = alpha * l_ref[rows, :] + psum
            acc_ref[rows, :] = alpha * acc_ref[rows, :] + _dot(p.astype(bf16), v)
            m_ref[rows, :] = m_new

    def tile_update(kt, acts):
        ks = pl.multiple_of(kt * tk, tk)
        k = k_ref[pl.ds(ks, tk), :]
        v = v_ref[pl.ds(ks, tk), :]
        dm_ref[...] = tile_dist(ks)
        if acts is None:
            group_update(tuple(range(ngrp)), k, v)
        elif ngrp == 2:
            pl.when(acts[0] & acts[1])(functools.partial(group_update, (0, 1), k, v))
            pl.when(acts[0] & jnp.logical_not(acts[1]))(functools.partial(group_update, (0,), k, v))
            pl.when(acts[1] & jnp.logical_not(acts[0]))(functools.partial(group_update, (1,), k, v))
        else:
            for r in range(ngrp):
                pl.when(acts[r])(functools.partial(group_update, (r,), k, v))

    def kt_body(kt, carry):
        if active is None:
            tile_update(kt, None)
        else:
            acts = [active(kt, r) for r in range(ngrp)]
            any_act = acts[0]
            for a in acts[1:]:
                any_act = any_act | a
            pl.when(any_act)(functools.partial(tile_update, kt, acts))
        return carry

    lax.fori_loop(kt_lo, kt_hi, kt_body, 0)
    for g in range(GQA):
        rows = pl.ds(g * tq, tq)
        o_ref[:, g * HEAD_DIM:(g + 1) * HEAD_DIM] = (
            acc_ref[rows, :] / jnp.sum(l_ref[rows, :], axis=-1, keepdims=True))


def _slc_kernel(flags_ref, slopes_ref, q_ref, k_ref, v_ref, sel_ref, e_ref, o_ref, *scratch, tq, tk, t):
    b = pl.program_id(0)
    kvh = pl.program_id(1)
    qt = pl.program_id(2)
    selb = sel_ref[0, 0].astype(bf16)
    qpos = qt * tq + lax.broadcasted_iota(jnp.int32, (tq, 1), 0)
    kofs = lax.broadcasted_iota(jnp.int32, (1, tk), 1)
    ngrp = tq // SLC_ROWS
    nkt = t // tk

    def tile_dist(ks):
        maskf = _dot(selb, e_ref[:, pl.ds(ks, tk)])
        d = qpos - (ks + kofs)
        return jnp.where((maskf > 0.5) & (d >= 0), d.astype(f32), MASK_DIST)

    def active(kt, r):
        row_group = (b * N_KV_A + kvh) * (t // SLC_ROWS) + qt * ngrp + r
        return flags_ref[row_group * nkt + kt] > 0

    _attn_prompt_body(kvh, qt, slopes_ref, q_ref, k_ref, v_ref, o_ref, *scratch, tq=tq, tk=tk,
                      kt_lo=0, kt_hi=(qt * tq + tq - 1) // tk + 1, tile_dist=tile_dist, active=active)


def _attn_scratch(tq, tk):
    return [pltpu.VMEM((GQA * tq, HEAD_DIM), bf16), pltpu.VMEM((tq, tk), f32),
            pltpu.VMEM((GQA * tq, tk), f32), pltpu.VMEM((GQA * tq, LANES), f32),
            pltpu.VMEM((GQA * tq, LANES), f32), pltpu.VMEM((GQA * tq, LANES), f32),
            pltpu.VMEM((GQA * tq, HEAD_DIM), f32)]


def _slc_prompt(slopes, qz, kv_slc, sel, expand, nbatch, t, tq, tk):
    nqt = t // tq
    gw = GQA * HEAD_DIM
    ncp = sel.shape[-1]
    bpt = tk // CMP_BLOCK
    flags = sel.reshape(nbatch, N_KV_A, t // SLC_ROWS, SLC_ROWS, ncp // bpt, bpt).max(axis=(3, 5))
    flags = (flags[..., :t // tk] > 0.5).astype(jnp.int32).reshape(-1)
    gs = pltpu.PrefetchScalarGridSpec(
        num_scalar_prefetch=1,
        grid=(nbatch, N_KV_A, nqt),
        in_specs=[
            pl.BlockSpec(memory_space=pltpu.SMEM),
            pl.BlockSpec((tq, gw), lambda b, h, i, fl: (b * nqt + i, h)),
            pl.BlockSpec((t, HEAD_DIM), lambda b, h, i, fl: (b, h)),
            pl.BlockSpec((t, HEAD_DIM), lambda b, h, i, fl: (b, N_KV_A + h)),
            pl.BlockSpec((1, 1, tq, ncp), lambda b, h, i, fl: (b, h, i, 0)),
            pl.BlockSpec((ncp, t), lambda b, h, i, fl: (0, 0)),
        ],
        out_specs=pl.BlockSpec((tq, gw), lambda b, h, i, fl: (b * nqt + i, h)),
        scratch_shapes=_attn_scratch(tq, tk),
    )
    return pl.pallas_call(
        functools.partial(_slc_kernel, tq=tq, tk=tk, t=t),
        grid_spec=gs,
        out_shape=jax.ShapeDtypeStruct((nbatch * t, D_ATT), f32),
        compiler_params=_cparams(("parallel", "parallel", "parallel")),
        name="slc_prompt",
    )(flags, slopes, qz, kv_slc, kv_slc, sel, expand)


def _swa_kernel(slopes_ref, q_ref, k_ref, v_ref, o_ref, *scratch, tq, tk):
    kvh = pl.program_id(1)
    qt = pl.program_id(2)
    qpos = qt * tq + lax.broadcasted_iota(jnp.int32, (tq, 1), 0)
    kofs = lax.broadcasted_iota(jnp.int32, (1, tk), 1)

    def tile_dist(ks):
        d = qpos - (ks + kofs)
        return jnp.where((d >= 0) & (d < WINDOW), d.astype(f32), MASK_DIST)

    _attn_prompt_body(kvh, qt, slopes_ref, q_ref, k_ref, v_ref, o_ref, *scratch, tq=tq, tk=tk,
                      kt_lo=jnp.maximum(qt * tq - (WINDOW - 1), 0) // tk, kt_hi=(qt * tq + tq - 1) // tk + 1,
                      tile_dist=tile_dist, active=None)


def _swa_prompt(slopes, qz, kv_swa, nbatch, t, tq, tk):
    nqt = t // tq
    gw = GQA * HEAD_DIM
    return pl.pallas_call(
        functools.partial(_swa_kernel, tq=tq, tk=tk),
        grid=(nbatch, N_KV_A, nqt),
        in_specs=[
            pl.BlockSpec(memory_space=pltpu.SMEM),
            pl.BlockSpec((tq, gw), lambda b, h, i: (b * nqt + i, h)),
            pl.BlockSpec((t, HEAD_DIM), lambda b, h, i: (b, h)),
            pl.BlockSpec((t, HEAD_DIM), lambda b, h, i: (b, N_KV_A + h)),
        ],
        out_specs=pl.BlockSpec((tq, gw), lambda b, h, i: (b * nqt + i, h)),
        out_shape=jax.ShapeDtypeStruct((nbatch * t, D_ATT), f32),
        scratch_shapes=_attn_scratch(tq, tk),
        compiler_params=_cparams(("parallel", "parallel", "parallel")),
        name="swa_prompt",
    )(slopes, qz, kv_swa, kv_swa)


def _softplus(x):
    return jnp.maximum(x, 0.0) + jnp.log1p(jnp.exp(-jnp.abs(x)))


def _sigmoid(x):
    return 1.0 / (1.0 + jnp.exp(-x))


def _silu(x):
    return x * _sigmoid(x)


def _dot_hi(a, b):
    return jnp.dot(a, b, precision=lax.Precision.HIGHEST, preferred_element_type=f32)


INV_BASE = 16


def _bdot(a, b):
    return _dot(a.astype(bf16), b.astype(bf16))


def _gdn_kernel(x_ref, small_ref, conv0_ref, s0_ref, cw_ref, alog_ref, dtb_ref, nw_ref,
                o_ref, s_out_ref, xp_ref, st_ref, gam_ref, n_ref, t_ref, pw_ref, uw_ref,
                k_ref, kb_ref, q_ref, qg_ref, kg_ref, qk_ref, vn_ref, rhs_ref, *, c, tv, halo):
    ci = pl.program_id(1)
    nci = pl.num_programs(1)

    @pl.when(ci == 0)
    def _():
        xp_ref[...] = jnp.zeros(xp_ref.shape, f32)
        xp_ref[halo - (CONV_W - 1):halo, :] = conv0_ref[0]
        st_ref[...] = s0_ref[0]

    xp_ref[halo:halo + tv, :] = x_ref[...]
    y = xp_ref[pl.ds(halo, c), :] * cw_ref[CONV_W - 1:CONV_W, :]
    for i in range(CONV_W - 2, -1, -1):
        y = y + xp_ref[pl.ds(halo - (CONV_W - 1) + i, c), :] * cw_ref[i:i + 1, :]
    y = _silu(y)
    xp_ref[halo - (CONV_W - 1):halo, :] = xp_ref[halo + tv - (CONV_W - 1):halo + tv, :]

    ri = lax.broadcasted_iota(jnp.int32, (c, 1), 0)
    rowi = lax.broadcasted_iota(jnp.int32, (c, c), 0)
    coli = lax.broadcasted_iota(jnp.int32, (c, c), 1)
    lower = rowi >= coli
    strict = rowi > coli
    tril = lower.astype(f32)
    eye = (rowi == coli).astype(f32)
    same_blk = {}
    s = INV_BASE
    while s <= c:
        same_blk[s] = (rowi // s) == (coli // s)
        s *= 2

    sm = small_ref[...]
    if tv < c:
        sm = jnp.concatenate([sm, jnp.zeros((c - tv, LANES), f32)], axis=0)
    g_all = -jnp.exp(alog_ref[...]) * _softplus(sm + dtb_ref[...])
    beta_all = _sigmoid(sm)
    if tv < c:
        live = ri < tv
        g_all = jnp.where(live, g_all, 0.0)
        beta_all = jnp.where(live, beta_all, 0.0)
    gc_all = _dot_hi(tril, g_all)
    gc_rows = gc_all.T
    heads = range(N_HEADS_B)
    decay = []

    for h in heads:
        yq = y[:, h * HEAD_DIM:(h + 1) * HEAD_DIM]
        yk = y[:, D_GDN + h * HEAD_DIM:D_GDN + (h + 1) * HEAD_DIM]
        v = y[:, 2 * D_GDN + h * HEAD_DIM:2 * D_GDN + (h + 1) * HEAD_DIM]
        q = yq * lax.rsqrt(jnp.sum(yq * yq, axis=-1, keepdims=True) + NORM_EPS) * SCALE
        k = yk * lax.rsqrt(jnp.sum(yk * yk, axis=-1, keepdims=True) + NORM_EPS)
        if tv < c:
            q = jnp.where(live, q, 0.0)
            k = jnp.where(live, k, 0.0)
            v = jnp.where(live, v, 0.0)
        beta = beta_all[:, B_OFF + h:B_OFF + h + 1]
        gcol = gc_all[:, A_OFF + h:A_OFF + h + 1]
        grow = gc_rows[A_OFF + h:A_OFF + h + 1, :]
        gam_ref[h] = jnp.where(lower, jnp.exp(jnp.where(lower, gcol - grow, 0.0)), 0.0)
        eg = jnp.exp(gcol)
        g_last = gc_all[c - 1:c, A_OFF + h:A_OFF + h + 1]
        decay.append(jnp.exp(g_last))
        kb = k * beta
        k_ref[h] = k.astype(bf16)
        kb_ref[h] = kb.astype(bf16)
        q_ref[h] = q.astype(bf16)
        qg_ref[h] = (q * eg).astype(bf16)
        kg_ref[h] = (k * jnp.exp(g_last - gcol)).astype(bf16)
        rhs_ref[h, :, 0:HEAD_DIM] = (v * beta).astype(bf16)
        rhs_ref[h, :, HEAD_DIM:2 * HEAD_DIM] = (kb * eg).astype(bf16)

    for h in heads:
        gam = gam_ref[h]
        nmat = jnp.where(strict, _dot_nt(kb_ref[h], k_ref[h]) * gam, 0.0)
        n_ref[h] = nmat
        nd = jnp.where(same_blk[INV_BASE], nmat, 0.0)
        t_ref[h] = eye - nd
        pw_ref[h] = nd.astype(bf16)
        qk_ref[h] = (_dot_nt(q_ref[h], k_ref[h]) * gam).astype(bf16)

    for _ in range(int(math.log2(INV_BASE)) - 1):
        for h in heads:
            pw = pw_ref[h]
            pw_ref[h] = _dot(pw, pw).astype(bf16)
        for h in heads:
            t = t_ref[h]
            t_ref[h] = t + _dot(t.astype(bf16), pw_ref[h])

    s = INV_BASE
    while s < c:
        pair = same_blk[2 * s] & jnp.logical_not(same_blk[s])
        for h in heads:
            off = jnp.where(pair, n_ref[h], 0.0)
            uw_ref[h, :, 0:HEAD_DIM] = _bdot(t_ref[h], off)
        for h in heads:
            t = t_ref[h]
            t_ref[h] = t - _bdot(uw_ref[h, :, 0:HEAD_DIM], t)
        s *= 2

    for h in heads:
        uw_ref[h] = _dot(t_ref[h].astype(bf16), rhs_ref[h])

    for h in heads:
        sb = st_ref[h].astype(bf16)
        v_new = uw_ref[h, :, 0:HEAD_DIM] - _dot(uw_ref[h, :, HEAD_DIM:2 * HEAD_DIM].astype(bf16), sb)
        vn_ref[h] = v_new.astype(bf16)
        uw_ref[h, :, 0:HEAD_DIM] = _dot(qg_ref[h], sb)
    for h in heads:
        vn = vn_ref[h]
        o = uw_ref[h, :, 0:HEAD_DIM] + _dot(qk_ref[h], vn)
        st_ref[h] = st_ref[h] * decay[h] + _dot_tn(kg_ref[h], vn)
        on = o * lax.rsqrt(jnp.mean(o * o, axis=-1, keepdims=True) + NORM_EPS) * nw_ref[...]
        o_ref[:, h * HEAD_DIM:(h + 1) * HEAD_DIM] = on[:tv]

    @pl.when(ci == nci - 1)
    def _():
        s_out_ref[0] = st_ref[...]


def _gdn(qkv_b, small, conv0, s0, conv_w, alog_row, dtb_row, gdn_norm_w, nbatch, t, c):
    tv = min(c, t)
    nci = t // tv
    halo = 8
    return pl.pallas_call(
        functools.partial(_gdn_kernel, c=c, tv=tv, halo=halo),
        grid=(nbatch, nci),
        in_specs=[
            pl.BlockSpec((tv, CONV_DIM), lambda b, i: (b * nci + i, 0)),
            pl.BlockSpec((tv, LANES), lambda b, i: (b * nci + i, 0)),
            pl.BlockSpec((1, CONV_W - 1, CONV_DIM), lambda b, i: (b, 0, 0)),
            pl.BlockSpec((1, N_HEADS_B, HEAD_DIM, HEAD_DIM), lambda b, i: (b, 0, 0, 0)),
            pl.BlockSpec((CONV_W, CONV_DIM), lambda b, i: (0, 0)),
            pl.BlockSpec((1, LANES), lambda b, i: (0, 0)),
            pl.BlockSpec((1, LANES), lambda b, i: (0, 0)),
            pl.BlockSpec((1, HEAD_DIM), lambda b, i: (0, 0)),
        ],
        out_specs=[
            pl.BlockSpec((tv, D_GDN), lambda b, i: (b * nci + i, 0)),
            pl.BlockSpec((1, N_HEADS_B, HEAD_DIM, HEAD_DIM), lambda b, i: (b, 0, 0, 0)),
        ],
        out_shape=[jax.ShapeDtypeStruct((nbatch * t, D_GDN), f32),
                   jax.ShapeDtypeStruct((nbatch, N_HEADS_B, HEAD_DIM, HEAD_DIM), f32)],
        scratch_shapes=[pltpu.VMEM((halo + c, CONV_DIM), f32),
                        pltpu.VMEM((N_HEADS_B, HEAD_DIM, HEAD_DIM), f32)]
        + [pltpu.VMEM((N_HEADS_B, c, c), f32)] * 3
        + [pltpu.VMEM((N_HEADS_B, c, c), bf16)]
        + [pltpu.VMEM((N_HEADS_B, c, 2 * HEAD_DIM), f32)]
        + [pltpu.VMEM((N_HEADS_B, c, HEAD_DIM), bf16)] * 5
        + [pltpu.VMEM((N_HEADS_B, c, c), bf16)]
        + [pltpu.VMEM((N_HEADS_B, c, HEAD_DIM), bf16)]
        + [pltpu.VMEM((N_HEADS_B, c, 2 * HEAD_DIM), bf16)],
        compiler_params=_cparams(("parallel", "arbitrary")),
        name="gdn",
    )(qkv_b, small, conv0, s0, conv_w, alog_row, dtb_row, gdn_norm_w.reshape(1, HEAD_DIM))


def _merge_kernel(x_ref, oc_ref, os_ref, ow_ref, small_ref, za_ref, zb_ref, ob_ref, w_ref, fw_ref, y_ref, mix_ref):
    gates = _sigmoid(small_ref[...])
    for h in range(N_HEADS_A):
        cs = slice(h * HEAD_DIM, (h + 1) * HEAD_DIM)
        o_a = (gates[:, G_OFF + h:G_OFF + h + 1] * oc_ref[:, cs]
               + gates[:, G_OFF + N_HEADS_A + h:G_OFF + N_HEADS_A + h + 1] * os_ref[:, cs]
               + gates[:, G_OFF + 2 * N_HEADS_A + h:G_OFF + 2 * N_HEADS_A + h + 1] * ow_ref[:, cs])
        mix_ref[:, cs] = (o_a * _silu(za_ref[:, cs])).astype(bf16)
    mix_ref[:, D_ATT:] = (ob_ref[...] * _silu(zb_ref[...])).astype(bf16)
    hres = x_ref[...] + _dot(mix_ref[...], w_ref[...])
    ms = jnp.mean(hres * hres, axis=-1, keepdims=True)
    y_ref[...] = hres * lax.rsqrt(ms + NORM_EPS) * fw_ref[...]


def _merge_out(x2d, o_cmp, o_slc, o_swa, small, qz, o_b, w_out_bf, final_norm_w, tm):
    m = x2d.shape[0]
    row = lambda i: (i, 0)
    return pl.pallas_call(
        _merge_kernel,
        grid=(m // tm,),
        in_specs=[
            pl.BlockSpec((tm, D_MODEL), row),
            pl.BlockSpec((tm, D_ATT), row),
            pl.BlockSpec((tm, D_ATT), row),
            pl.BlockSpec((tm, D_ATT), row),
            pl.BlockSpec((tm, LANES), row),
            pl.BlockSpec((tm, D_ATT), lambda i: (i, 1)),
            pl.BlockSpec((tm, D_GDN), lambda i: (i, 2)),
            pl.BlockSpec((tm, D_GDN), row),
            pl.BlockSpec((D_ATT + D_GDN, D_MODEL), lambda i: (0, 0), pipeline_mode=pl.Buffered(1)),
            pl.BlockSpec((1, D_MODEL), lambda i: (0, 0)),
        ],
        out_specs=pl.BlockSpec((tm, D_MODEL), row),
        out_shape=jax.ShapeDtypeStruct((m, D_MODEL), f32),
        scratch_shapes=[pltpu.VMEM((tm, D_ATT + D_GDN), bf16)],
        compiler_params=_cparams(("parallel",)),
        name="merge_out",
    )(x2d, o_cmp, o_slc, o_swa, small, qz, qz, o_b, w_out_bf, final_norm_w.reshape(1, D_MODEL))


BLOCK_ROWS4 = CMP_BLOCK * N_CH


def _page_copies(cache_ref, buf_ref, sem_ref, pt_ref, b, j, slot, i, pps, by_row):
    page = pt_ref[b, j * pps + i]
    if not by_row:
        return [pltpu.make_async_copy(
            cache_ref.at[pl.ds(page * ROWS_PER_PAGE4, ROWS_PER_PAGE4), :],
            buf_ref.at[slot, pl.ds(i * ROWS_PER_PAGE4, ROWS_PER_PAGE4), :],
            sem_ref.at[slot])]
    bpp = PAGE_SIZE // CMP_BLOCK
    return [pltpu.make_async_copy(
        cache_ref.at[pl.ds(page * ROWS_PER_PAGE4 + n * BLOCK_ROWS4, BLOCK_ROWS4), :],
        buf_ref.at[slot, :, i * bpp + n, :],
        sem_ref.at[slot]) for n in range(bpp)]


def _paged_fetch(cache_ref, buf_ref, sem_ref, pt_ref, pps, nj_pages, need=None, by_row=False):
    b = pl.program_id(0)
    j = pl.program_id(1)
    nb = pl.num_programs(0)
    nj = pl.num_programs(1)
    step = b * nj_pages + jnp.minimum(j, nj_pages - 1)
    slot = step % 2

    def guarded(bb, jj, i, fn):
        if need is None:
            fn()
        else:
            pl.when(need(bb, jj, i))(fn)

    def run(what, bb, jj, sl, i):
        for cp in _page_copies(cache_ref, buf_ref, sem_ref, pt_ref, bb, jj, sl, i, pps, by_row):
            getattr(cp, what)()

    def start(bb, jj, sl):
        for i in range(pps):
            guarded(bb, jj, i, functools.partial(run, "start", bb, jj, sl, i))

    @pl.when((b == 0) & (j == 0))
    def _():
        start(0, 0, 0)

    @pl.when(j < nj_pages)
    def _():
        last_j = j == nj_pages - 1
        nb_ = jnp.where(last_j, b + 1, b)
        nj_ = jnp.where(last_j, 0, j + 1)

        @pl.when(nb_ < nb)
        def _():
            start(nb_, nj_, 1 - slot)

        for i in range(pps):
            guarded(b, j, i, functools.partial(run, "wait", b, j, slot, i))

    return slot


def _compress_paged_kernel(pt_ref, cache_ref, pe_ref, w_ref, o_ref, buf_ref, sem_ref, *, pps):
    slot = _paged_fetch(cache_ref, buf_ref, sem_ref, pt_ref, pps, pl.num_programs(1), by_row=True)
    nbk = pps * (PAGE_SIZE // CMP_BLOCK)
    accs = _compress_accumulate(lambda l, ch: buf_ref[slot, l * N_CH + ch], pe_ref, w_ref, nbk)
    for c in range(2):
        for h in range(N_KV_A):
            o_ref[0, c * N_KV_A + h] = accs[c][h * nbk:(h + 1) * nbk]


def _compress_paged(page_table, cache4, pe2d, w_cmp_bf, pps):
    nbatch, n_pages = page_table.shape
    nj = n_pages // pps
    nbk = pps * (PAGE_SIZE // CMP_BLOCK)
    gs = pltpu.PrefetchScalarGridSpec(
        num_scalar_prefetch=1,
        grid=(nbatch, nj),
        in_specs=[
            pl.BlockSpec(memory_space=pl.ANY),
            pl.BlockSpec((2, CMP_BLOCK, HEAD_DIM), lambda b, j, pt: (0, 0, 0)),
            pl.BlockSpec((2, CMP_BLOCK // 2, 2 * HEAD_DIM, HEAD_DIM), lambda b, j, pt: (0, 0, 0, 0)),
        ],
        out_specs=pl.BlockSpec((1, N_CH, nbk, HEAD_DIM), lambda b, j, pt: (b, 0, j, 0)),
        scratch_shapes=[pltpu.VMEM((2, BLOCK_ROWS4, nbk, HEAD_DIM), f32), pltpu.SemaphoreType.DMA((2,))],
    )
    return pl.pallas_call(
        functools.partial(_compress_paged_kernel, pps=pps),
        grid_spec=gs,
        out_shape=jax.ShapeDtypeStruct((nbatch, N_CH, nj * nbk, HEAD_DIM), f32),
        compiler_params=_cparams(("arbitrary", "arbitrary")),
        name="compress_paged",
    )(page_table, cache4, pe2d, w_cmp_bf)


def _two_phase_attend(qs, slope2, qpos, kpos0, load_k, load_v, load_mask, nchunks, chunk,
                      s_ref, rows, m_ref, l_ref, acc_ref, active=None):
    nrow = qs.shape[0]
    nrep = chunk // LANES
    kofs = lax.broadcasted_iota(jnp.int32, (1, chunk), 1)
    unroll = math.gcd(nchunks, 4)

    def cols(ci):
        return pl.ds(pl.multiple_of(ci * chunk, chunk), chunk)

    def skippable(ci, fn, carry):
        if active is None:
            return fn(carry)
        return lax.cond(active(ci), fn, lambda c: c, carry)

    def phase1(ci, mrun):
        def update(mrun):
            d = qpos - (kpos0 + ci * chunk + kofs)
            dmask = jnp.where((load_mask(ci) > 0.5) & (d >= 0), d.astype(f32), MASK_DIST)
            t2 = _dot_nt(qs, load_k(ci)) * (SCALE * LOG2E) - slope2 * dmask
            s_ref[rows, cols(ci)] = t2
            for c in range(nrep):
                mrun = jnp.maximum(mrun, t2[:, c * LANES:(c + 1) * LANES])
            return mrun

        return skippable(ci, update, mrun)

    mrun = lax.fori_loop(0, nchunks, phase1, jnp.full((nrow, LANES), NEG_INF, f32), unroll=unroll)
    m_prev = m_ref[rows, :]
    m_new = jnp.maximum(m_prev, jnp.broadcast_to(jnp.max(mrun, axis=-1, keepdims=True), (nrow, LANES)))

    def phase2(ci, carry):
        def update(carry):
            lp, acc = carry
            p = jnp.exp2(s_ref[rows, cols(ci)] - _lane_repeat(m_new, nrep))
            for c in range(nrep):
                lp = lp + p[:, c * LANES:(c + 1) * LANES]
            return lp, acc + _dot(p.astype(bf16), load_v(ci))

        return skippable(ci, update, carry)

    z = jnp.zeros((nrow, LANES), f32)
    lp, acc = lax.fori_loop(0, nchunks, phase2, (z, z), unroll=unroll)
    alpha = jnp.exp2(m_prev - m_new)
    l_ref[rows, :] = alpha * l_ref[rows, :] + lp
    acc_ref[rows, :] = alpha * acc_ref[rows, :] + acc
    m_ref[rows, :] = m_new


def _slc_paged_kernel(pt_ref, chunk_any_ref, chunk_kvh_ref, slopes_ref, cache_ref, q_ref, tail_ref, sel_ref,
                      e_ref, o_ref, buf_ref, sem_ref, s_ref, m_ref, l_ref, acc_ref, *, pps, t, sub, pos0):
    b = pl.program_id(0)
    j = pl.program_id(1)
    nj_pages = pl.num_programs(1) - 1
    ppc = sub // PAGE_SIZE
    cps = pps // ppc
    slot = _paged_fetch(cache_ref, buf_ref, sem_ref, pt_ref, pps, nj_pages,
                        need=lambda bb, jj, i: chunk_any_ref[bb, jj * cps + i // ppc] > 0)
    nrow = GQA * t

    @pl.when(j == 0)
    def _():
        _flash_init(m_ref, l_ref, acc_ref)

    tpos = lax.broadcasted_iota(jnp.int32, (t, 1), 0)
    qpos = pos0 + jnp.concatenate([tpos] * GQA, axis=0)

    def attend(kvh, load_k, load_v, nchunks, chunk, kpos0, active=None):
        q = q_ref[:, kvh * GQA * HEAD_DIM:(kvh + 1) * GQA * HEAD_DIM]
        qs = jnp.concatenate([q[:, g * HEAD_DIM:(g + 1) * HEAD_DIM] for g in range(GQA)], axis=0).astype(bf16)
        slope2 = jnp.concatenate([jnp.full((t, 1), slopes_ref[kvh, g] * LOG2E, f32) for g in range(GQA)], axis=0)
        selb = sel_ref[0, kvh].astype(bf16)

        def load_mask(ci):
            maskf = _dot(selb, e_ref[:, pl.ds(pl.multiple_of(ci * chunk, chunk), chunk)])
            return jnp.concatenate([maskf] * GQA, axis=0)

        _two_phase_attend(qs, slope2, qpos, kpos0, load_k, load_v, load_mask, nchunks, chunk,
                          s_ref, pl.ds(kvh * nrow, nrow), m_ref, l_ref, acc_ref, active)

    @pl.when(j < nj_pages)
    def _():
        for kvh in range(N_KV_A):
            def load(ci, ch):
                return buf_ref[slot, pl.ds(ci * (sub * N_CH) + ch, sub, stride=N_CH), :].astype(bf16)

            first = ((b * N_KV_A + kvh) * nj_pages + j) * cps
            attend(kvh, functools.partial(load, ch=kvh), functools.partial(load, ch=N_KV_A + kvh),
                   cps, sub, j * pps * PAGE_SIZE, lambda ci, first=first: chunk_kvh_ref[first + ci] > 0)

    @pl.when(j == nj_pages)
    def _():
        ntail = tail_ref.shape[1]
        for kvh in range(N_KV_A):
            k = tail_ref[0, :, kvh * HEAD_DIM:(kvh + 1) * HEAD_DIM].astype(bf16)
            v = tail_ref[0, :, (N_KV_A + kvh) * HEAD_DIM:(N_KV_A + kvh + 1) * HEAD_DIM].astype(bf16)
            attend(kvh, lambda ci, k=k: k, lambda ci, v=v: v, 1, ntail, nj_pages * pps * PAGE_SIZE)
        for kvh in range(N_KV_A):
            for g in range(GQA):
                r = pl.ds(kvh * nrow + g * t, t)
                o_ref[:, (kvh * GQA + g) * HEAD_DIM:(kvh * GQA + g + 1) * HEAD_DIM] = (
                    acc_ref[r, :] / jnp.sum(l_ref[r, :], axis=-1, keepdims=True))


def _slc_paged(page_table, slopes, cache4, qz, tail, sel, expand, t, pps, pos0):
    nbatch, n_pages = page_table.shape
    njp = n_pages // pps
    bps = pps * (PAGE_SIZE // CMP_BLOCK)
    sub = min(512, pps * PAGE_SIZE)
    bpc = sub // CMP_BLOCK
    n_pb = n_pages * (PAGE_SIZE // CMP_BLOCK)
    chunk_sel = sel[..., :n_pb].reshape(nbatch, N_KV_A, t, n_pb // bpc, bpc).max(axis=(2, 4)) > 0.5
    chunk_kvh = chunk_sel.astype(jnp.int32).reshape(-1)
    chunk_any = chunk_sel.any(axis=1).astype(jnp.int32)
    gs = pltpu.PrefetchScalarGridSpec(
        num_scalar_prefetch=3,
        grid=(nbatch, njp + 1),
        in_specs=[
            pl.BlockSpec(memory_space=pltpu.SMEM),
            pl.BlockSpec(memory_space=pl.ANY),
            pl.BlockSpec((t, D_ATT), lambda b, j, *_: (b, 0)),
            pl.BlockSpec((1, LANES, KV_BRANCH), lambda b, j, *_: (b, 0, 0)),
            pl.BlockSpec((1, N_KV_A, t, bps), lambda b, j, *_: (b, 0, 0, j)),
            pl.BlockSpec((bps, pps * PAGE_SIZE), lambda b, j, *_: (0, 0)),
        ],
        out_specs=pl.BlockSpec((t, D_ATT), lambda b, j, *_: (b, 0)),
        scratch_shapes=[pltpu.VMEM((2, pps * ROWS_PER_PAGE4, HEAD_DIM), f32), pltpu.SemaphoreType.DMA((2,)),
                        pltpu.VMEM((N_KV_A * GQA * t, pps * PAGE_SIZE), f32),
                        pltpu.VMEM((N_KV_A * GQA * t, LANES), f32), pltpu.VMEM((N_KV_A * GQA * t, LANES), f32),
                        pltpu.VMEM((N_KV_A * GQA * t, HEAD_DIM), f32)],
    )
    return pl.pallas_call(
        functools.partial(_slc_paged_kernel, pps=pps, t=t, sub=sub, pos0=pos0),
        grid_spec=gs,
        out_shape=jax.ShapeDtypeStruct((nbatch * t, D_ATT), f32),
        compiler_params=_cparams(("arbitrary", "arbitrary")),
        name="slc_paged",
    )(page_table, chunk_any, chunk_kvh, slopes, cache4, qz, tail, sel, expand)


def _swa_sample_kernel(slopes_ref, q_ref, win_ref, tail_ref, o_ref, *, t, wbuf, pos0):
    tpos = lax.broadcasted_iota(jnp.int32, (t, 1), 0)
    qpos = pos0 + jnp.concatenate([tpos] * GQA, axis=0)
    nk = wbuf + CMP_BLOCK
    kpos = pos0 - wbuf + lax.broadcasted_iota(jnp.int32, (1, nk), 1)
    d = qpos - kpos
    ok = (d >= 0) & (d < WINDOW)
    df = d.astype(f32)
    for kvh in range(N_KV_A):
        k = jnp.concatenate([win_ref[pl.ds(kvh, wbuf, stride=N_CH), :].astype(bf16),
                             tail_ref[0, :, kvh * HEAD_DIM:(kvh + 1) * HEAD_DIM]], axis=0)
        v = jnp.concatenate([win_ref[pl.ds(N_KV_A + kvh, wbuf, stride=N_CH), :].astype(bf16),
                             tail_ref[0, :, (N_KV_A + kvh) * HEAD_DIM:(N_KV_A + kvh + 1) * HEAD_DIM]], axis=0)
        q = q_ref[:, kvh * GQA * HEAD_DIM:(kvh + 1) * GQA * HEAD_DIM]
        qs = jnp.concatenate([q[:, g * HEAD_DIM:(g + 1) * HEAD_DIM] for g in range(GQA)], axis=0).astype(bf16)
        slope = jnp.concatenate([jnp.full((t, 1), slopes_ref[kvh, g], f32) for g in range(GQA)], axis=0)
        s = _dot_nt(qs, k) * SCALE - slope * df
        s = jnp.where(ok, s, NEG_INF)
        e = jnp.exp(s - jnp.max(s, axis=-1, keepdims=True))
        p = jnp.where(ok, e / jnp.sum(e, axis=-1, keepdims=True), 0.0)
        o = _dot(p.astype(bf16), v)
        for g in range(GQA):
            o_ref[:, (kvh * GQA + g) * HEAD_DIM:(kvh * GQA + g + 1) * HEAD_DIM] = o[g * t:(g + 1) * t]


def _swa_sample(slopes, qz, win4, tail, nbatch, t, wbuf, pos0):
    return pl.pallas_call(
        functools.partial(_swa_sample_kernel, t=t, wbuf=wbuf, pos0=pos0),
        grid=(nbatch,),
        in_specs=[
            pl.BlockSpec(memory_space=pltpu.SMEM),
            pl.BlockSpec((t, D_ATT), lambda b: (b, 0)),
            pl.BlockSpec((wbuf * N_CH, HEAD_DIM), lambda b: (b, 0)),
            pl.BlockSpec((1, CMP_BLOCK, KV_BRANCH), lambda b: (b, 0, 0)),
        ],
        out_specs=pl.BlockSpec((t, D_ATT), lambda b: (b, 0)),
        out_shape=jax.ShapeDtypeStruct((nbatch * t, D_ATT), f32),
        compiler_params=_cparams(("parallel",)),
        name="swa_sample",
    )(slopes, qz, win4, tail)


def _round_up(a, b):
    return -(-a // b) * b


def _pad_blocks(kcv, ncp):
    return jnp.pad(kcv, ((0, 0), (0, 0), (0, ncp - kcv.shape[2]), (0, 0)))


def _expand_matrix(nblocks, nkeys):
    return (jnp.arange(nkeys)[None, :] // CMP_BLOCK == jnp.arange(nblocks)[:, None]).astype(bf16)


def _kv6(rows2d, nbatch, t):
    return rows2d.reshape(nbatch, t, 2, N_KV_A, HEAD_DIM)


def _layer_prompt(x, prm):
    nbatch, t, _ = x.shape
    m = nbatch * t
    x2d = x.reshape(m, D_MODEL)
    tm = min(1024, m)
    qz, qkvb, kv_slc, kv_swa, cmp4, slc4, swa4, small = _in_proj(
        x2d, prm["norm_w"], prm["w_main"], prm["w_small"], tm)
    nc = t // CMP_BLOCK
    kcv = _compress(cmp4, nbatch, prm["pe2d"], prm["w_cmp"], min(64, nc))
    kcv = _pad_blocks(kcv, _round_up(nc, LANES))
    tq = min(256, t)
    o_cmp, sel = _cmp_attend(prm["slopes"], qz, kcv, nbatch, t, tq, 0, min(N_SELECT, nc))
    tk = min(256, t)
    o_slc = _slc_prompt(prm["slopes"], qz, kv_slc, sel, _expand_matrix(sel.shape[-1], t), nbatch, t, tq, tk)
    o_swa = _swa_prompt(prm["slopes"], qz, kv_swa, nbatch, t, tq, tk)
    conv0 = jnp.zeros((nbatch, CONV_W - 1, CONV_DIM), f32)
    s0 = jnp.zeros((nbatch, N_HEADS_B, HEAD_DIM, HEAD_DIM), f32)
    o_b, s_new = _gdn(qkvb, small, conv0, s0, prm["conv_w"], prm["alog_row"], prm["dtb_row"],
                      prm["gdn_norm_w"], nbatch, t, 128)
    y = _merge_out(x2d, o_cmp, o_slc, o_swa, small, qz, o_b, prm["w_out"], prm["final_norm_w"], min(512, m))
    w = min(WINDOW, t)
    conv_new = qkvb.reshape(nbatch, t, CONV_DIM)[:, t - (CONV_W - 1):]
    return (y.reshape(nbatch, t, D_MODEL), _kv6(cmp4, nbatch, t), _kv6(slc4, nbatch, t),
            _kv6(swa4, nbatch, t)[:, t - w:], conv_new, s_new)


def _layer_sample(x, cache_cmp, cache_slc, cache_swa, conv_buf, s0, page_table, prm):
    nbatch, t, _ = x.shape
    m = nbatch * t
    n_pages = page_table.shape[1]
    past_len = n_pages * PAGE_SIZE
    wbuf = cache_swa.shape[1]
    x2d = x.reshape(m, D_MODEL)
    qz, qkvb, kv_slc, kv_swa, cmp4, slc4, swa4, small = _in_proj(
        x2d, prm["norm_w"], prm["w_main"], prm["w_small"], m)
    pad_tail = lambda r: jnp.pad(r.reshape(nbatch, t, KV_BRANCH), ((0, 0), (0, CMP_BLOCK - t), (0, 0)))
    pps = min(32, n_pages)
    kc_past = _compress_paged(page_table, cache_cmp.reshape(-1, HEAD_DIM), prm["pe2d"], prm["w_cmp"], pps)
    kc_tail = _compress(pad_tail(cmp4).reshape(nbatch * CMP_BLOCK * N_CH, HEAD_DIM), 1, prm["pe2d"], prm["w_cmp"], nbatch)
    kc_tail = jnp.swapaxes(kc_tail[0], 0, 1)[:, :, None, :]
    n_pb = past_len // CMP_BLOCK
    pps_s = min(64, n_pages)
    bps = pps_s * (PAGE_SIZE // CMP_BLOCK)
    ncp = _round_up(n_pb + 1, max(LANES, bps))
    kcv = _pad_blocks(jnp.concatenate([kc_past, kc_tail], axis=2), ncp)
    o_cmp, sel = _cmp_attend(prm["slopes"], qz, kcv, nbatch, t, t, past_len, min(N_SELECT, n_pb + 1))
    o_slc = _slc_paged(page_table, prm["slopes"], cache_slc.reshape(-1, HEAD_DIM), qz,
                       jnp.pad(kv_slc.reshape(nbatch, t, KV_BRANCH), ((0, 0), (0, LANES - t), (0, 0))), sel,
                       _expand_matrix(bps, pps_s * PAGE_SIZE), t, pps_s, past_len)
    o_swa = _swa_sample(prm["slopes"], qz, cache_swa.reshape(-1, HEAD_DIM), pad_tail(kv_swa), nbatch, t, wbuf, past_len)
    o_b, s_new = _gdn(qkvb, small, conv_buf, s0, prm["conv_w"], prm["alog_row"], prm["dtb_row"],
                      prm["gdn_norm_w"], nbatch, t, 128)
    y = _merge_out(x2d, o_cmp, o_slc, o_swa, small, qz, o_b, prm["w_out"], prm["final_norm_w"], min(512, m))
    kv_win = jnp.concatenate([cache_swa, _kv6(swa4, nbatch, t)], axis=1)[:, t:]
    conv_new = jnp.concatenate([conv_buf, qkvb.reshape(nbatch, t, CONV_DIM)], axis=1)[:, t:]
    return (y.reshape(nbatch, t, D_MODEL), _kv6(cmp4, nbatch, t), _kv6(slc4, nbatch, t), kv_win, conv_new, s_new)


def kernel(x_prompt, x_sample, cache_cmp, cache_slc, cache_swa, state_conv, state_gdn, page_table,
           norm_w, w_in, pe_cmp, w_cmp, conv_w, a_log, dt_bias, gdn_norm_w, w_out, final_norm_w):
    depth = norm_w.shape[0]
    assert depth == 1, "the final norm is fused into the single layer's output projection"
    l = 0
    w_main, w_small = _prep_w_in(w_in[l])
    head = jnp.arange(1, N_HEADS_A + 1, dtype=f32)
    lane_row = lambda vals: jnp.zeros((1, LANES), f32).at[0, A_OFF:A_OFF + N_HEADS_B].set(vals.astype(f32))
    prm = {
        "norm_w": norm_w[l], "w_main": w_main, "w_small": w_small,
        "pe2d": jnp.swapaxes(pe_cmp[l], 0, 1),
        "w_cmp": jnp.swapaxes(w_cmp[l], 0, 1).reshape(2, CMP_BLOCK // 2, 2 * HEAD_DIM, HEAD_DIM).astype(bf16),
        "conv_w": conv_w[l], "alog_row": lane_row(a_log[l]), "dtb_row": lane_row(dt_bias[l]),
        "gdn_norm_w": gdn_norm_w[l], "w_out": w_out[l].astype(bf16), "final_norm_w": final_norm_w,
        "slopes": jnp.exp2(-8.0 * head / N_HEADS_A).reshape(N_KV_A, GQA),
    }
    yp, p_cmp, p_slc, p_swa, p_conv, p_gdn = _layer_prompt(x_prompt, prm)
    ys, s_cmp, s_slc, s_swa, s_conv, s_gdn = _layer_sample(
        x_sample, cache_cmp[l], cache_slc[l], cache_swa[l], state_conv[l], state_gdn[l], page_table, prm)
    st = lambda a: a[None]
    return (yp, ys, st(p_cmp), st(p_slc), st(p_swa), st(p_conv), st(p_gdn),
            st(s_cmp), st(s_slc), st(s_swa), st(s_conv), st(s_gdn))
```

```python
import functools
import math

import jax
import jax.numpy as jnp
from jax import lax
from jax.experimental import pallas as pl
from jax.experimental.pallas import tpu as pltpu

f32 = jnp.float32
bf16 = jnp.bfloat16

D_MODEL = 2048
HEAD_DIM = 128
N_HEADS_A = 8
N_KV_A = 2
GQA = 4
D_ATT = N_HEADS_A * HEAD_DIM
CMP_BLOCK = 64
N_SELECT = 16
WINDOW = 512
N_HEADS_B = 8
D_GDN = N_HEADS_B * HEAD_DIM
CONV_W = 4
CONV_DIM = 3 * D_GDN
PAGE_SIZE = 128
NORM_EPS = 1e-6
NEG_INF = -1e30
FORCE_SCORE = 1e4
KV_BRANCH = 2 * N_KV_A * HEAD_DIM
N_CH = 2 * N_KV_A
ROWS_PER_PAGE4 = PAGE_SIZE * N_CH
SCALE = HEAD_DIM ** -0.5
G_OFF, A_OFF, B_OFF = 0, 3 * N_HEADS_A, 3 * N_HEADS_A + N_HEADS_B
LANES = 128
VMEM_LIMIT = 56 * 1024 * 1024
IN_PROJ_VMEM_LIMIT = 60 * 1024 * 1024

IN_TN = 512
N_QZ_TILES = 3 * D_ATT // IN_TN
N_B_TILES = CONV_DIM // IN_TN
N_IN_TILES = N_QZ_TILES + N_B_TILES + 3


def _cparams(sem):
    return pltpu.CompilerParams(dimension_semantics=sem, vmem_limit_bytes=VMEM_LIMIT)


def _dot(a, b):
    return jnp.dot(a, b, preferred_element_type=f32)


def _dot_nt(a, b):
    return lax.dot_general(a, b, (((1,), (1,)), ((), ())), preferred_element_type=f32)


def _dot_tn(a, b):
    return lax.dot_general(a, b, (((0,), (0,)), ((), ())), preferred_element_type=f32)


def _store_row4(o4_ref, acc):
    for ch in range(N_CH):
        o4_ref[pl.ds(ch, acc.shape[0], stride=N_CH), :] = acc[:, ch * HEAD_DIM:(ch + 1) * HEAD_DIM]


def _in_proj_kernel(x_ref, nw_ref, w_ref, ws_ref, qz_ref, b_ref, slc_ref, swa_ref, cmp4_ref, slc4_ref, swa4_ref,
                    small_ref, xn_ref):
    n = pl.program_id(1)

    @pl.when(n == 0)
    def _():
        x = x_ref[...]
        ms = jnp.mean(x * x, axis=-1, keepdims=True)
        xn = (x * lax.rsqrt(ms + NORM_EPS) * nw_ref[...]).astype(bf16)
        xn_ref[...] = xn
        small_ref[...] = _dot(xn, ws_ref[...])

    acc = _dot(xn_ref[...], w_ref[...])

    @pl.when(n < N_QZ_TILES)
    def _():
        qz_ref[...] = acc

    @pl.when((n >= N_QZ_TILES) & (n < N_QZ_TILES + N_B_TILES))
    def _():
        b_ref[...] = acc

    @pl.when(n == N_QZ_TILES + N_B_TILES)
    def _():
        _store_row4(cmp4_ref, acc)

    @pl.when(n == N_QZ_TILES + N_B_TILES + 1)
    def _():
        slc_ref[...] = acc.astype(bf16)
        _store_row4(slc4_ref, acc)

    @pl.when(n == N_QZ_TILES + N_B_TILES + 2)
    def _():
        swa_ref[...] = acc.astype(bf16)
        _store_row4(swa4_ref, acc)


def _in_proj(x2d, norm_w, w_main, w_small, tm):
    m = x2d.shape[0]
    S = jax.ShapeDtypeStruct
    nb0 = N_QZ_TILES
    return pl.pallas_call(
        _in_proj_kernel,
        grid=(m // tm, N_IN_TILES),
        in_specs=[
            pl.BlockSpec((tm, D_MODEL), lambda i, n: (i, 0)),
            pl.BlockSpec((1, D_MODEL), lambda i, n: (0, 0)),
            pl.BlockSpec((D_MODEL, IN_TN), lambda i, n: (0, n)),
            pl.BlockSpec((D_MODEL, LANES), lambda i, n: (0, 0)),
        ],
        out_specs=[
            pl.BlockSpec((tm, IN_TN), lambda i, n: (i, jnp.minimum(n, nb0 - 1))),
            pl.BlockSpec((tm, IN_TN), lambda i, n: (i, jnp.clip(n - nb0, 0, N_B_TILES - 1))),
            pl.BlockSpec((tm, KV_BRANCH), lambda i, n: (i, 0)),
            pl.BlockSpec((tm, KV_BRANCH), lambda i, n: (i, 0)),
            pl.BlockSpec((tm * N_CH, HEAD_DIM), lambda i, n: (i, 0)),
            pl.BlockSpec((tm * N_CH, HEAD_DIM), lambda i, n: (i, 0)),
            pl.BlockSpec((tm * N_CH, HEAD_DIM), lambda i, n: (i, 0)),
            pl.BlockSpec((tm, LANES), lambda i, n: (i, 0)),
        ],
        out_shape=[S((m, 3 * D_ATT), f32), S((m, CONV_DIM), f32), S((m, KV_BRANCH), bf16), S((m, KV_BRANCH), bf16),
                   S((m * N_CH, HEAD_DIM), f32), S((m * N_CH, HEAD_DIM), f32), S((m * N_CH, HEAD_DIM), f32),
                   S((m, LANES), f32)],
        scratch_shapes=[pltpu.VMEM((tm, D_MODEL), bf16)],
        compiler_params=pltpu.CompilerParams(dimension_semantics=("parallel", "arbitrary"),
                                             vmem_limit_bytes=IN_PROJ_VMEM_LIMIT),
        name="in_proj",
    )(x2d, norm_w.reshape(1, D_MODEL), w_main, w_small)


def _prep_w_in(w_in):
    o = [0]
    for s in (D_ATT, 3 * KV_BRANCH, 3 * N_HEADS_A, D_ATT, CONV_DIM, N_HEADS_B, N_HEADS_B, D_GDN):
        o.append(o[-1] + s)
    q_a, kv_a, g_a, z_a, qkv_b, a_b, b_b, z_b = (w_in[:, o[i]:o[i + 1]] for i in range(8))
    w_main = jnp.concatenate([q_a, z_a, z_b, qkv_b, kv_a], axis=1).astype(bf16)
    pad = jnp.zeros((w_in.shape[0], LANES - (3 * N_HEADS_A + 2 * N_HEADS_B)), w_in.dtype)
    w_small = jnp.concatenate([g_a, a_b, b_b, pad], axis=1).astype(bf16)
    return w_main, w_small


CMP_UNROLL = 4


def _compress_accumulate(load_rows, pe_ref, w_ref, nbk):
    def body(i, accs):
        new = []
        for c in range(2):
            halves = []
            for dl in range(2):
                l = 2 * i + dl
                pe_row = pe_ref[c, pl.ds(l, 1), :]
                halves.append(jnp.concatenate(
                    [load_rows(l, c * N_KV_A + h) + pe_row for h in range(N_KV_A)], axis=0))
            lhs = jnp.concatenate(halves, axis=1).astype(bf16)
            new.append(accs[c] + _dot(lhs, w_ref[c, i]))
        return tuple(new)

    z = jnp.zeros((N_KV_A * nbk, HEAD_DIM), f32)
    return lax.fori_loop(0, CMP_BLOCK // 2, body, (z, z), unroll=CMP_UNROLL)


def _compress_kernel(x_ref, pe_ref, w_ref, o_ref, *, nbk):
    accs = _compress_accumulate(
        lambda l, ch: x_ref[pl.ds(l * N_CH + ch, nbk, stride=CMP_BLOCK * N_CH), :], pe_ref, w_ref, nbk)
    for c in range(2):
        for h in range(N_KV_A):
            o_ref[0, c * N_KV_A + h] = accs[c][h * nbk:(h + 1) * nbk]


def _compress(rows4, nbatch, pe2d, w_cmp_bf, nbk):
    nblk = rows4.shape[0] // (nbatch * CMP_BLOCK * N_CH)
    nj = nblk // nbk
    return pl.pallas_call(
        functools.partial(_compress_kernel, nbk=nbk),
        grid=(nbatch, nj),
        in_specs=[
            pl.BlockSpec((nbk * CMP_BLOCK * N_CH, HEAD_DIM), lambda b, j: (b * nj + j, 0)),
            pl.BlockSpec((2, CMP_BLOCK, HEAD_DIM), lambda b, j: (0, 0, 0)),
            pl.BlockSpec((2, CMP_BLOCK // 2, 2 * HEAD_DIM, HEAD_DIM), lambda b, j: (0, 0, 0, 0)),
        ],
        out_specs=pl.BlockSpec((1, N_CH, nbk, HEAD_DIM), lambda b, j: (b, 0, j, 0)),
        out_shape=jax.ShapeDtypeStruct((nbatch, N_CH, nblk, HEAD_DIM), f32),
        compiler_params=_cparams(("parallel", "parallel")),
        name="compress",
    )(rows4, pe2d, w_cmp_bf)


def _topk_mask(sc, nsel):
    ncp = sc.shape[0]
    jblk = lax.broadcasted_iota(jnp.int32, (ncp, 1), 0)
    work = sc
    for _ in range(nsel):
        mx = jnp.max(work, axis=0, keepdims=True)
        first = jnp.min(jnp.where(work == mx, jblk, ncp), axis=0, keepdims=True)
        work = jnp.where(jblk == first, -2.0, work)
    return jnp.where((work == -2.0) & (sc >= 0.0), 1.0, 0.0)


def _topk_kernel(sc_ref, sel_ref, *, nsel):
    sel_ref[0] = _topk_mask(sc_ref[0], nsel)


def _topk(scores, nsel):
    ng, ncp, nq = scores.shape
    return pl.pallas_call(
        functools.partial(_topk_kernel, nsel=nsel),
        grid=(ng, nq // LANES),
        in_specs=[pl.BlockSpec((1, ncp, LANES), lambda g, c: (g, 0, c))],
        out_specs=pl.BlockSpec((1, ncp, LANES), lambda g, c: (g, 0, c)),
        out_shape=jax.ShapeDtypeStruct(scores.shape, f32),
        compiler_params=_cparams(("parallel", "parallel")),
        name="topk",
    )(scores)


def _cmp_kernel(slopes_ref, q_ref, kc_ref, vc_ref, o_ref, sel_ref, *, tq, tqp, ncp, pos0, nsel):
    kvh = pl.program_id(1)
    qt = pl.program_id(2)
    q = q_ref[...]
    if tqp > tq:
        q = jnp.concatenate([q, jnp.zeros((tqp - tq, q.shape[1]), f32)], axis=0)
    kc = kc_ref[0, 0].astype(bf16)
    vc = vc_ref[0, 0].astype(bf16)
    qpos = pos0 + qt * tq + lax.broadcasted_iota(jnp.int32, (1, tqp), 1)
    jblk = lax.broadcasted_iota(jnp.int32, (ncp, 1), 0)
    d = qpos - ((jblk + 1) * CMP_BLOCK - 1)
    ok = d >= 0
    df = d.astype(f32)
    imp = jnp.zeros((ncp, tqp), f32)
    for g in range(GQA):
        qg = q[:, g * HEAD_DIM:(g + 1) * HEAD_DIM].astype(bf16)
        s = _dot_nt(kc, qg) * SCALE
        s = jnp.where(ok, s - slopes_ref[kvh, g] * df, NEG_INF)
        mx = jnp.max(s, axis=0, keepdims=True)
        e = jnp.exp(s - mx)
        p = jnp.where(ok, e / jnp.sum(e, axis=0, keepdims=True), 0.0)
        imp = imp + p
        og = _dot_tn(p.astype(bf16), vc)
        o_ref[:, g * HEAD_DIM:(g + 1) * HEAD_DIM] = og[:tq]
    cur = qpos // CMP_BLOCK
    forced = (jblk == cur) | (jblk == 0)
    score = jnp.where(forced, FORCE_SCORE, jnp.where(jblk <= cur, imp, -1.0))
    if nsel is None:
        sel_ref[0, 0] = score
        return
    for cg in range(tqp // LANES):
        sel = _topk_mask(score[:, cg * LANES:(cg + 1) * LANES], nsel)
        lo = cg * LANES
        hi = min(tq, lo + LANES)
        sel_ref[0, 0, lo:hi, :] = sel.T[:hi - lo]


def _cmp_attend(slopes, qz, kcv, nbatch, t, tq, pos0, nsel):
    ncp = kcv.shape[2]
    tqp = max(tq, LANES)
    nqt = t // tq
    gw = GQA * HEAD_DIM
    packed = tq < LANES and (nbatch * t) % LANES == 0
    sel_spec = (pl.BlockSpec((1, 1, ncp, tqp), lambda b, h, i: (b, h, 0, i)) if packed
                else pl.BlockSpec((1, 1, tq, ncp), lambda b, h, i: (b, h, i, 0)))
    sel_shape = (nbatch, N_KV_A, ncp, nqt * tqp) if packed else (nbatch, N_KV_A, t, ncp)
    o_cmp, sel = pl.pallas_call(
        functools.partial(_cmp_kernel, tq=tq, tqp=tqp, ncp=ncp, pos0=pos0, nsel=None if packed else nsel),
        grid=(nbatch, N_KV_A, nqt),
        in_specs=[
            pl.BlockSpec(memory_space=pltpu.SMEM),
            pl.BlockSpec((tq, gw), lambda b, h, i: (b * nqt + i, h)),
            pl.BlockSpec((1, 1, ncp, HEAD_DIM), lambda b, h, i: (b, h, 0, 0)),
            pl.BlockSpec((1, 1, ncp, HEAD_DIM), lambda b, h, i: (b, N_KV_A + h, 0, 0)),
        ],
        out_specs=[pl.BlockSpec((tq, gw), lambda b, h, i: (b * nqt + i, h)), sel_spec],
        out_shape=[jax.ShapeDtypeStruct((nbatch * t, D_ATT), f32), jax.ShapeDtypeStruct(sel_shape, f32)],
        compiler_params=_cparams(("parallel", "parallel", "parallel")),
        name="cmp_attend",
    )(slopes, qz, kcv, kcv)
    if packed:
        scores = sel.reshape(nbatch, N_KV_A, ncp, nqt, tqp)[..., :tq].reshape(nbatch, N_KV_A, ncp, t)
        scores = jnp.transpose(scores, (1, 2, 0, 3)).reshape(N_KV_A, ncp, nbatch * t)
        mask = _topk(scores, nsel).reshape(N_KV_A, ncp, nbatch, t)
        sel = jnp.transpose(mask, (2, 0, 3, 1))
    return o_cmp, sel


def _lane_repeat(x, n):
    return x if n == 1 else jnp.concatenate([x] * n, axis=1)


def _flash_init(m_ref, l_ref, acc_ref):
    m_ref[...] = jnp.full(m_ref.shape, NEG_INF, f32)
    l_ref[...] = jnp.zeros(l_ref.shape, f32)
    acc_ref[...] = jnp.zeros(acc_ref.shape, f32)


LOG2E = 1.4426950408889634
MASK_DIST = 1e33
SLC_ROWS = 128


def _attn_prompt_body(kvh, qt, slopes_ref, q_ref, k_ref, v_ref, o_ref, qb_ref, dm_ref, s_ref, mn_ref,
                      m_ref, l_ref, acc_ref, *, tq, tk, kt_lo, kt_hi, tile_dist, active):
    _flash_init(m_ref, l_ref, acc_ref)
    for g in range(GQA):
        qb_ref[g * tq:(g + 1) * tq, :] = (q_ref[:, g * HEAD_DIM:(g + 1) * HEAD_DIM] * (SCALE * LOG2E)).astype(bf16)
    nrep = tk // LANES
    ngrp = tq // SLC_ROWS

    def group_update(groups, k, v):
        units = [(g, r) for r in groups for g in range(GQA)]
        for g, r in units:
            rows = pl.ds(g * tq + r * SLC_ROWS, SLC_ROWS)
            t2 = (_dot_nt(qb_ref[rows, :], k)
                  - (slopes_ref[kvh, g] * LOG2E) * dm_ref[r * SLC_ROWS:(r + 1) * SLC_ROWS, :])
            s_ref[rows, :] = t2
            mx = jnp.max(t2, axis=-1, keepdims=True)
            mn_ref[rows, :] = jnp.maximum(m_ref[rows, :], jnp.broadcast_to(mx, (SLC_ROWS, LANES)))
        for g, r in units:
            rows = pl.ds(g * tq + r * SLC_ROWS, SLC_ROWS)
            m_new = mn_ref[rows, :]
            p = jnp.exp2(s_ref[rows, :] - _lane_repeat(m_new, nrep))
            alpha = jnp.exp2(m_ref[rows, :] - m_new)
            psum = p[:, 0:LANES]
            for c in range(1, nrep):
                psum = psum + p[:, c * LANES:(c + 1) * LANES]
            l_ref[rows, :] = alpha * l_ref[rows, :] + psum
            acc_ref[rows, :] = alpha * acc_ref[rows, :] + _dot(p.astype(bf16), v)
            m_ref[rows, :] = m_new

    def tile_update(kt, acts):
        ks = pl.multiple_of(kt * tk, tk)
        k = k_ref[pl.ds(ks, tk), :]
        v = v_ref[pl.ds(ks, tk), :]
        dm_ref[...] = tile_dist(ks)
        if acts is None:
            group_update(tuple(range(ngrp)), k, v)
        elif ngrp == 2:
            pl.when(acts[0] & acts[1])(functools.partial(group_update, (0, 1), k, v))
            pl.when(acts[0] & jnp.logical_not(acts[1]))(functools.partial(group_update, (0,), k, v))
            pl.when(acts[1] & jnp.logical_not(acts[0]))(functools.partial(group_update, (1,), k, v))
        else:
            for r in range(ngrp):
                pl.when(acts[r])(functools.partial(group_update, (r,), k, v))

    def kt_body(kt, carry):
        if active is None:
            tile_update(kt, None)
        else:
            acts = [active(kt, r) for r in range(ngrp)]
            any_act = acts[0]
            for a in acts[1:]:
                any_act = any_act | a
            pl.when(any_act)(functools.partial(tile_update, kt, acts))
        return carry

    lax.fori_loop(kt_lo, kt_hi, kt_body, 0)
    for g in range(GQA):
        rows = pl.ds(g * tq, tq)
        o_ref[:, g * HEAD_DIM:(g + 1) * HEAD_DIM] = (
            acc_ref[rows, :] / jnp.sum(l_ref[rows, :], axis=-1, keepdims=True))


def _slc_kernel(flags_ref, slopes_ref, q_ref, k_ref, v_ref, sel_ref, e_ref, o_ref, *scratch, tq, tk, t):
    b = pl.program_id(0)
    kvh = pl.program_id(1)
    qt = pl.program_id(2)
    selb = sel_ref[0, 0].astype(bf16)
    qpos = qt * tq + lax.broadcasted_iota(jnp.int32, (tq, 1), 0)
    kofs = lax.broadcasted_iota(jnp.int32, (1, tk), 1)
    ngrp = tq // SLC_ROWS
    nkt = t // tk

    def tile_dist(ks):
        maskf = _dot(selb, e_ref[:, pl.ds(ks, tk)])
        d = qpos - (ks + kofs)
        return jnp.where((maskf > 0.5) & (d >= 0), d.astype(f32), MASK_DIST)

    def active(kt, r):
        row_group = (b * N_KV_A + kvh) * (t // SLC_ROWS) + qt * ngrp + r
        return flags_ref[row_group * nkt + kt] > 0

    _attn_prompt_body(kvh, qt, slopes_ref, q_ref, k_ref, v_ref, o_ref, *scratch, tq=tq, tk=tk,
                      kt_lo=0, kt_hi=(qt * tq + tq - 1) // tk + 1, tile_dist=tile_dist, active=active)


def _attn_scratch(tq, tk):
    return [pltpu.VMEM((GQA * tq, HEAD_DIM), bf16), pltpu.VMEM((tq, tk), f32),
            pltpu.VMEM((GQA * tq, tk), f32), pltpu.VMEM((GQA * tq, LANES), f32),
            pltpu.VMEM((GQA * tq, LANES), f32), pltpu.VMEM((GQA * tq, LANES), f32),
            pltpu.VMEM((GQA * tq, HEAD_DIM), f32)]


def _slc_prompt(slopes, qz, kv_slc, sel, expand, nbatch, t, tq, tk):
    nqt = t // tq
    gw = GQA * HEAD_DIM
    ncp = sel.shape[-1]
    bpt = tk // CMP_BLOCK
    flags = sel.reshape(nbatch, N_KV_A, t // SLC_ROWS, SLC_ROWS, ncp // bpt, bpt).max(axis=(3, 5))
    flags = (flags[..., :t // tk] > 0.5).astype(jnp.int32).reshape(-1)
    gs = pltpu.PrefetchScalarGridSpec(
        num_scalar_prefetch=1,
        grid=(nbatch, N_KV_A, nqt),
        in_specs=[
            pl.BlockSpec(memory_space=pltpu.SMEM),
            pl.BlockSpec((tq, gw), lambda b, h, i, fl: (b * nqt + i, h)),
            pl.BlockSpec((t, HEAD_DIM), lambda b, h, i, fl: (b, h)),
            pl.BlockSpec((t, HEAD_DIM), lambda b, h, i, fl: (b, N_KV_A + h)),
            pl.BlockSpec((1, 1, tq, ncp), lambda b, h, i, fl: (b, h, i, 0)),
            pl.BlockSpec((ncp, t), lambda b, h, i, fl: (0, 0)),
        ],
        out_specs=pl.BlockSpec((tq, gw), lambda b, h, i, fl: (b * nqt + i, h)),
        scratch_shapes=_attn_scratch(tq, tk),
    )
    return pl.pallas_call(
        functools.partial(_slc_kernel, tq=tq, tk=tk, t=t),
        grid_spec=gs,
        out_shape=jax.ShapeDtypeStruct((nbatch * t, D_ATT), f32),
        compiler_params=_cparams(("parallel", "parallel", "parallel")),
        name="slc_prompt",
    )(flags, slopes, qz, kv_slc, kv_slc, sel, expand)


def _swa_kernel(slopes_ref, q_ref, k_ref, v_ref, o_ref, *scratch, tq, tk):
    kvh = pl.program_id(1)
    qt = pl.program_id(2)
    qpos = qt * tq + lax.broadcasted_iota(jnp.int32, (tq, 1), 0)
    kofs = lax.broadcasted_iota(jnp.int32, (1, tk), 1)

    def tile_dist(ks):
        d = qpos - (ks + kofs)
        return jnp.where((d >= 0) & (d < WINDOW), d.astype(f32), MASK_DIST)

    _attn_prompt_body(kvh, qt, slopes_ref, q_ref, k_ref, v_ref, o_ref, *scratch, tq=tq, tk=tk,
                      kt_lo=jnp.maximum(qt * tq - (WINDOW - 1), 0) // tk, kt_hi=(qt * tq + tq - 1) // tk + 1,
                      tile_dist=tile_dist, active=None)


def _swa_prompt(slopes, qz, kv_swa, nbatch, t, tq, tk):
    nqt = t // tq
    gw = GQA * HEAD_DIM
    return pl.pallas_call(
        functools.partial(_swa_kernel, tq=tq, tk=tk),
        grid=(nbatch, N_KV_A, nqt),
        in_specs=[
            pl.BlockSpec(memory_space=pltpu.SMEM),
            pl.BlockSpec((tq, gw), lambda b, h, i: (b * nqt + i, h)),
            pl.BlockSpec((t, HEAD_DIM), lambda b, h, i: (b, h)),
            pl.BlockSpec((t, HEAD_DIM), lambda b, h, i: (b, N_KV_A + h)),
        ],
        out_specs=pl.BlockSpec((tq, gw), lambda b, h, i: (b * nqt + i, h)),
        out_shape=jax.ShapeDtypeStruct((nbatch * t, D_ATT), f32),
        scratch_shapes=_attn_scratch(tq, tk),
        compiler_params=_cparams(("parallel", "parallel", "parallel")),
        name="swa_prompt",
    )(slopes, qz, kv_swa, kv_swa)


def _softplus(x):
    return jnp.maximum(x, 0.0) + jnp.log1p(jnp.exp(-jnp.abs(x)))


def _sigmoid(x):
    return 1.0 / (1.0 + jnp.exp(-x))


def _silu(x):
    return x * _sigmoid(x)


def _dot_hi(a, b):
    return jnp.dot(a, b, precision=lax.Precision.HIGHEST, preferred_element_type=f32)


INV_BASE = 16


def _bdot(a, b):
    return _dot(a.astype(bf16), b.astype(bf16))


def _gdn_kernel(x_ref, small_ref, conv0_ref, s0_ref, cw_ref, alog_ref, dtb_ref, nw_ref,
                o_ref, s_out_ref, xp_ref, st_ref, gam_ref, n_ref, t_ref, pw_ref, uw_ref,
                k_ref, kb_ref, q_ref, qg_ref, kg_ref, qk_ref, vn_ref, rhs_ref, *, c, tv, halo):
    ci = pl.program_id(1)
    nci = pl.num_programs(1)

    @pl.when(ci == 0)
    def _():
        xp_ref[...] = jnp.zeros(xp_ref.shape, f32)
        xp_ref[halo - (CONV_W - 1):halo, :] = conv0_ref[0]
        st_ref[...] = s0_ref[0]

    xp_ref[halo:halo + tv, :] = x_ref[...]
    y = xp_ref[pl.ds(halo, c), :] * cw_ref[CONV_W - 1:CONV_W, :]
    for i in range(CONV_W - 2, -1, -1):
        y = y + xp_ref[pl.ds(halo - (CONV_W - 1) + i, c), :] * cw_ref[i:i + 1, :]
    y = _silu(y)
    xp_ref[halo - (CONV_W - 1):halo, :] = xp_ref[halo + tv - (CONV_W - 1):halo + tv, :]

    ri = lax.broadcasted_iota(jnp.int32, (c, 1), 0)
    rowi = lax.broadcasted_iota(jnp.int32, (c, c), 0)
    coli = lax.broadcasted_iota(jnp.int32, (c, c), 1)
    lower = rowi >= coli
    strict = rowi > coli
    tril = lower.astype(f32)
    eye = (rowi == coli).astype(f32)
    same_blk = {}
    s = INV_BASE
    while s <= c:
        same_blk[s] = (rowi // s) == (coli // s)
        s *= 2

    sm = small_ref[...]
    if tv < c:
        sm = jnp.concatenate([sm, jnp.zeros((c - tv, LANES), f32)], axis=0)
    g_all = -jnp.exp(alog_ref[...]) * _softplus(sm + dtb_ref[...])
    beta_all = _sigmoid(sm)
    if tv < c:
        live = ri < tv
        g_all = jnp.where(live, g_all, 0.0)
        beta_all = jnp.where(live, beta_all, 0.0)
    gc_all = _dot_hi(tril, g_all)
    gc_rows = gc_all.T
    heads = range(N_HEADS_B)
    decay = []

    for h in heads:
        yq = y[:, h * HEAD_DIM:(h + 1) * HEAD_DIM]
        yk = y[:, D_GDN + h * HEAD_DIM:D_GDN + (h + 1) * HEAD_DIM]
        v = y[:, 2 * D_GDN + h * HEAD_DIM:2 * D_GDN + (h + 1) * HEAD_DIM]
        q = yq * lax.rsqrt(jnp.sum(yq * yq, axis=-1, keepdims=True) + NORM_EPS) * SCALE
        k = yk * lax.rsqrt(jnp.sum(yk * yk, axis=-1, keepdims=True) + NORM_EPS)
        if tv < c:
            q = jnp.where(live, q, 0.0)
            k = jnp.where(live, k, 0.0)
            v = jnp.where(live, v, 0.0)
        beta = beta_all[:, B_OFF + h:B_OFF + h + 1]
        gcol = gc_all[:, A_OFF + h:A_OFF + h + 1]
        grow = gc_rows[A_OFF + h:A_OFF + h + 1, :]
        gam_ref[h] = jnp.where(lower, jnp.exp(jnp.where(lower, gcol - grow, 0.0)), 0.0)
        eg = jnp.exp(gcol)
        g_last = gc_all[c - 1:c, A_OFF + h:A_OFF + h + 1]
        decay.append(jnp.exp(g_last))
        kb = k * beta
        k_ref[h] = k.astype(bf16)
        kb_ref[h] = kb.astype(bf16)
        q_ref[h] = q.astype(bf16)
        qg_ref[h] = (q * eg).astype(bf16)
        kg_ref[h] = (k * jnp.exp(g_last - gcol)).astype(bf16)
        rhs_ref[h, :, 0:HEAD_DIM] = (v * beta).astype(bf16)
        rhs_ref[h, :, HEAD_DIM:2 * HEAD_DIM] = (kb * eg).astype(bf16)

    for h in heads:
        gam = gam_ref[h]
        nmat = jnp.where(strict, _dot_nt(kb_ref[h], k_ref[h]) * gam, 0.0)
        n_ref[h] = nmat
        nd = jnp.where(same_blk[INV_BASE], nmat, 0.0)
        t_ref[h] = eye - nd
        pw_ref[h] = nd.astype(bf16)
        qk_ref[h] = (_dot_nt(q_ref[h], k_ref[h]) * gam).astype(bf16)

    for _ in range(int(math.log2(INV_BASE)) - 1):
        for h in heads:
            pw = pw_ref[h]
            pw_ref[h] = _dot(pw, pw).astype(bf16)
        for h in heads:
            t = t_ref[h]
            t_ref[h] = t + _dot(t.astype(bf16), pw_ref[h])

    s = INV_BASE if tv > INV_BASE else c
    while s < c:
        pair = same_blk[2 * s] & jnp.logical_not(same_blk[s])
        for h in heads:
            off = jnp.where(pair, n_ref[h], 0.0)
            uw_ref[h, :, 0:HEAD_DIM] = _bdot(t_ref[h], off)
        for h in heads:
            t = t_ref[h]
            t_ref[h] = t - _bdot(uw_ref[h, :, 0:HEAD_DIM], t)
        s *= 2

    for h in heads:
        uw_ref[h] = _dot(t_ref[h].astype(bf16), rhs_ref[h])

    for h in heads:
        sb = st_ref[h].astype(bf16)
        v_new = uw_ref[h, :, 0:HEAD_DIM] - _dot(uw_ref[h, :, HEAD_DIM:2 * HEAD_DIM].astype(bf16), sb)
        vn_ref[h] = v_new.astype(bf16)
        uw_ref[h, :, 0:HEAD_DIM] = _dot(qg_ref[h], sb)
    for h in heads:
        vn = vn_ref[h]
        o = uw_ref[h, :, 0:HEAD_DIM] + _dot(qk_ref[h], vn)
        st_ref[h] = st_ref[h] * decay[h] + _dot_tn(kg_ref[h], vn)
        on = o * lax.rsqrt(jnp.mean(o * o, axis=-1, keepdims=True) + NORM_EPS) * nw_ref[...]
        o_ref[:, h * HEAD_DIM:(h + 1) * HEAD_DIM] = on[:tv]

    @pl.when(ci == nci - 1)
    def _():
        s_out_ref[0] = st_ref[...]


def _gdn(qkv_b, small, conv0, s0, conv_w, alog_row, dtb_row, gdn_norm_w, nbatch, t, c):
    tv = min(c, t)
    nci = t // tv
    halo = 8
    return pl.pallas_call(
        functools.partial(_gdn_kernel, c=c, tv=tv, halo=halo),
        grid=(nbatch, nci),
        in_specs=[
            pl.BlockSpec((tv, CONV_DIM), lambda b, i: (b * nci + i, 0)),
            pl.BlockSpec((tv, LANES), lambda b, i: (b * nci + i, 0)),
            pl.BlockSpec((1, CONV_W - 1, CONV_DIM), lambda b, i: (b, 0, 0)),
            pl.BlockSpec((1, N_HEADS_B, HEAD_DIM, HEAD_DIM), lambda b, i: (b, 0, 0, 0)),
            pl.BlockSpec((CONV_W, CONV_DIM), lambda b, i: (0, 0)),
            pl.BlockSpec((1, LANES), lambda b, i: (0, 0)),
            pl.BlockSpec((1, LANES), lambda b, i: (0, 0)),
            pl.BlockSpec((1, HEAD_DIM), lambda b, i: (0, 0)),
        ],
        out_specs=[
            pl.BlockSpec((tv, D_GDN), lambda b, i: (b * nci + i, 0)),
            pl.BlockSpec((1, N_HEADS_B, HEAD_DIM, HEAD_DIM), lambda b, i: (b, 0, 0, 0)),
        ],
        out_shape=[jax.ShapeDtypeStruct((nbatch * t, D_GDN), f32),
                   jax.ShapeDtypeStruct((nbatch, N_HEADS_B, HEAD_DIM, HEAD_DIM), f32)],
        scratch_shapes=[pltpu.VMEM((halo + c, CONV_DIM), f32),
                        pltpu.VMEM((N_HEADS_B, HEAD_DIM, HEAD_DIM), f32)]
        + [pltpu.VMEM((N_HEADS_B, c, c), f32)] * 3
        + [pltpu.VMEM((N_HEADS_B, c, c), bf16)]
        + [pltpu.VMEM((N_HEADS_B, c, 2 * HEAD_DIM), f32)]
        + [pltpu.VMEM((N_HEADS_B, c, HEAD_DIM), bf16)] * 5
        + [pltpu.VMEM((N_HEADS_B, c, c), bf16)]
        + [pltpu.VMEM((N_HEADS_B, c, HEAD_DIM), bf16)]
        + [pltpu.VMEM((N_HEADS_B, c, 2 * HEAD_DIM), bf16)],
        compiler_params=_cparams(("parallel", "arbitrary")),
        name="gdn",
    )(qkv_b, small, conv0, s0, conv_w, alog_row, dtb_row, gdn_norm_w.reshape(1, HEAD_DIM))


def _merge_kernel(x_ref, oc_ref, os_ref, ow_ref, small_ref, za_ref, zb_ref, ob_ref, w_ref, fw_ref, y_ref, mix_ref):
    gates = _sigmoid(small_ref[...])
    for h in range(N_HEADS_A):
        cs = slice(h * HEAD_DIM, (h + 1) * HEAD_DIM)
        o_a = (gates[:, G_OFF + h:G_OFF + h + 1] * oc_ref[:, cs]
               + gates[:, G_OFF + N_HEADS_A + h:G_OFF + N_HEADS_A + h + 1] * os_ref[:, cs]
               + gates[:, G_OFF + 2 * N_HEADS_A + h:G_OFF + 2 * N_HEADS_A + h + 1] * ow_ref[:, cs])
        mix_ref[:, cs] = (o_a * _silu(za_ref[:, cs])).astype(bf16)
    mix_ref[:, D_ATT:] = (ob_ref[...] * _silu(zb_ref[...])).astype(bf16)
    hres = x_ref[...] + _dot(mix_ref[...], w_ref[...])
    ms = jnp.mean(hres * hres, axis=-1, keepdims=True)
    y_ref[...] = hres * lax.rsqrt(ms + NORM_EPS) * fw_ref[...]


def _merge_out(x2d, o_cmp, o_slc, o_swa, small, qz, o_b, w_out_bf, final_norm_w, tm):
    m = x2d.shape[0]
    row = lambda i: (i, 0)
    return pl.pallas_call(
        _merge_kernel,
        grid=(m // tm,),
        in_specs=[
            pl.BlockSpec((tm, D_MODEL), row),
            pl.BlockSpec((tm, D_ATT), row),
            pl.BlockSpec((tm, D_ATT), row),
            pl.BlockSpec((tm, D_ATT), row),
            pl.BlockSpec((tm, LANES), row),
            pl.BlockSpec((tm, D_ATT), lambda i: (i, 1)),
            pl.BlockSpec((tm, D_GDN), lambda i: (i, 2)),
            pl.BlockSpec((tm, D_GDN), row),
            pl.BlockSpec((D_ATT + D_GDN, D_MODEL), lambda i: (0, 0), pipeline_mode=pl.Buffered(1)),
            pl.BlockSpec((1, D_MODEL), lambda i: (0, 0)),
        ],
        out_specs=pl.BlockSpec((tm, D_MODEL), row),
        out_shape=jax.ShapeDtypeStruct((m, D_MODEL), f32),
        scratch_shapes=[pltpu.VMEM((tm, D_ATT + D_GDN), bf16)],
        compiler_params=_cparams(("parallel",)),
        name="merge_out",
    )(x2d, o_cmp, o_slc, o_swa, small, qz, qz, o_b, w_out_bf, final_norm_w.reshape(1, D_MODEL))


BLOCK_ROWS4 = CMP_BLOCK * N_CH


def _page_copies(cache_ref, buf_ref, sem_ref, pt_ref, b, j, slot, i, pps, by_row):
    page = pt_ref[b, j * pps + i]
    if not by_row:
        return [pltpu.make_async_copy(
            cache_ref.at[pl.ds(page * ROWS_PER_PAGE4, ROWS_PER_PAGE4), :],
            buf_ref.at[slot, pl.ds(i * ROWS_PER_PAGE4, ROWS_PER_PAGE4), :],
            sem_ref.at[slot])]
    bpp = PAGE_SIZE // CMP_BLOCK
    return [pltpu.make_async_copy(
        cache_ref.at[pl.ds(page * ROWS_PER_PAGE4 + n * BLOCK_ROWS4, BLOCK_ROWS4), :],
        buf_ref.at[slot, :, i * bpp + n, :],
        sem_ref.at[slot]) for n in range(bpp)]


def _paged_fetch(cache_ref, buf_ref, sem_ref, pt_ref, pps, nj_pages, need=None, by_row=False):
    b = pl.program_id(0)
    j = pl.program_id(1)
    nb = pl.num_programs(0)
    nj = pl.num_programs(1)
    step = b * nj_pages + jnp.minimum(j, nj_pages - 1)
    slot = step % 2

    def guarded(bb, jj, i, fn):
        if need is None:
            fn()
        else:
            pl.when(need(bb, jj, i))(fn)

    def run(what, bb, jj, sl, i):
        for cp in _page_copies(cache_ref, buf_ref, sem_ref, pt_ref, bb, jj, sl, i, pps, by_row):
            getattr(cp, what)()

    def start(bb, jj, sl):
        for i in range(pps):
            guarded(bb, jj, i, functools.partial(run, "start", bb, jj, sl, i))

    @pl.when((b == 0) & (j == 0))
    def _():
        start(0, 0, 0)

    @pl.when(j < nj_pages)
    def _():
        last_j = j == nj_pages - 1
        nb_ = jnp.where(last_j, b + 1, b)
        nj_ = jnp.where(last_j, 0, j + 1)

        @pl.when(nb_ < nb)
        def _():
            start(nb_, nj_, 1 - slot)

        for i in range(pps):
            guarded(b, j, i, functools.partial(run, "wait", b, j, slot, i))

    return slot


def _compress_paged_kernel(pt_ref, cache_ref, pe_ref, w_ref, o_ref, buf_ref, sem_ref, *, pps):
    slot = _paged_fetch(cache_ref, buf_ref, sem_ref, pt_ref, pps, pl.num_programs(1), by_row=True)
    nbk = pps * (PAGE_SIZE // CMP_BLOCK)
    accs = _compress_accumulate(lambda l, ch: buf_ref[slot, l * N_CH + ch], pe_ref, w_ref, nbk)
    for c in range(2):
        for h in range(N_KV_A):
            o_ref[0, c * N_KV_A + h] = accs[c][h * nbk:(h + 1) * nbk]


def _compress_paged(page_table, cache4, pe2d, w_cmp_bf, pps):
    nbatch, n_pages = page_table.shape
    nj = n_pages // pps
    nbk = pps * (PAGE_SIZE // CMP_BLOCK)
    gs = pltpu.PrefetchScalarGridSpec(
        num_scalar_prefetch=1,
        grid=(nbatch, nj),
        in_specs=[
            pl.BlockSpec(memory_space=pl.ANY),
            pl.BlockSpec((2, CMP_BLOCK, HEAD_DIM), lambda b, j, pt: (0, 0, 0)),
            pl.BlockSpec((2, CMP_BLOCK // 2, 2 * HEAD_DIM, HEAD_DIM), lambda b, j, pt: (0, 0, 0, 0)),
        ],
        out_specs=pl.BlockSpec((1, N_CH, nbk, HEAD_DIM), lambda b, j, pt: (b, 0, j, 0)),
        scratch_shapes=[pltpu.VMEM((2, BLOCK_ROWS4, nbk, HEAD_DIM), f32), pltpu.SemaphoreType.DMA((2,))],
    )
    return pl.pallas_call(
        functools.partial(_compress_paged_kernel, pps=pps),
        grid_spec=gs,
        out_shape=jax.ShapeDtypeStruct((nbatch, N_CH, nj * nbk, HEAD_DIM), f32),
        compiler_params=_cparams(("arbitrary", "arbitrary")),
        name="compress_paged",
    )(page_table, cache4, pe2d, w_cmp_bf)


def _two_phase_attend(qs, slope2, qpos, kpos0, load_k, load_v, load_mask, nchunks, chunk,
                      s_ref, rows, m_ref, l_ref, acc_ref, active=None):
    nrow = qs.shape[0]
    nrep = chunk // LANES
    kofs = lax.broadcasted_iota(jnp.int32, (1, chunk), 1)
    unroll = math.gcd(nchunks, 4)

    def cols(ci):
        return pl.ds(pl.multiple_of(ci * chunk, chunk), chunk)

    def skippable(ci, fn, carry):
        if active is None:
            return fn(carry)
        return lax.cond(active(ci), fn, lambda c: c, carry)

    def phase1(ci, mrun):
        def update(mrun):
            d = qpos - (kpos0 + ci * chunk + kofs)
            dmask = jnp.where((load_mask(ci) > 0.5) & (d >= 0), d.astype(f32), MASK_DIST)
            t2 = _dot_nt(qs, load_k(ci)) * (SCALE * LOG2E) - slope2 * dmask
            s_ref[rows, cols(ci)] = t2
            for c in range(nrep):
                mrun = jnp.maximum(mrun, t2[:, c * LANES:(c + 1) * LANES])
            return mrun

        return skippable(ci, update, mrun)

    mrun = lax.fori_loop(0, nchunks, phase1, jnp.full((nrow, LANES), NEG_INF, f32), unroll=unroll)
    m_prev = m_ref[rows, :]
    m_new = jnp.maximum(m_prev, jnp.broadcast_to(jnp.max(mrun, axis=-1, keepdims=True), (nrow, LANES)))

    def phase2(ci, carry):
        def update(carry):
            lp, acc = carry
            p = jnp.exp2(s_ref[rows, cols(ci)] - _lane_repeat(m_new, nrep))
            for c in range(nrep):
                lp = lp + p[:, c * LANES:(c + 1) * LANES]
            return lp, acc + _dot(p.astype(bf16), load_v(ci))

        return skippable(ci, update, carry)

    z = jnp.zeros((nrow, LANES), f32)
    lp, acc = lax.fori_loop(0, nchunks, phase2, (z, z), unroll=unroll)
    alpha = jnp.exp2(m_prev - m_new)
    l_ref[rows, :] = alpha * l_ref[rows, :] + lp
    acc_ref[rows, :] = alpha * acc_ref[rows, :] + acc
    m_ref[rows, :] = m_new


def _slc_paged_kernel(pt_ref, chunk_any_ref, chunk_kvh_ref, slopes_ref, cache_ref, q_ref, tail_ref, sel_ref,
                      e_ref, o_ref, buf_ref, sem_ref, s_ref, m_ref, l_ref, acc_ref, *, pps, t, sub, pos0):
    b = pl.program_id(0)
    j = pl.program_id(1)
    nj_pages = pl.num_programs(1) - 1
    ppc = sub // PAGE_SIZE
    cps = pps // ppc
    slot = _paged_fetch(cache_ref, buf_ref, sem_ref, pt_ref, pps, nj_pages,
                        need=lambda bb, jj, i: chunk_any_ref[bb, jj * cps + i // ppc] > 0)
    nrow = GQA * t

    @pl.when(j == 0)
    def _():
        _flash_init(m_ref, l_ref, acc_ref)

    tpos = lax.broadcasted_iota(jnp.int32, (t, 1), 0)
    qpos = pos0 + jnp.concatenate([tpos] * GQA, axis=0)

    def attend(kvh, load_k, load_v, nchunks, chunk, kpos0, active=None):
        q = q_ref[:, kvh * GQA * HEAD_DIM:(kvh + 1) * GQA * HEAD_DIM]
        qs = jnp.concatenate([q[:, g * HEAD_DIM:(g + 1) * HEAD_DIM] for g in range(GQA)], axis=0).astype(bf16)
        slope2 = jnp.concatenate([jnp.full((t, 1), slopes_ref[kvh, g] * LOG2E, f32) for g in range(GQA)], axis=0)
        selb = sel_ref[0, kvh].astype(bf16)

        def load_mask(ci):
            maskf = _dot(selb, e_ref[:, pl.ds(pl.multiple_of(ci * chunk, chunk), chunk)])
            return jnp.concatenate([maskf] * GQA, axis=0)

        _two_phase_attend(qs, slope2, qpos, kpos0, load_k, load_v, load_mask, nchunks, chunk,
                          s_ref, pl.ds(kvh * nrow, nrow), m_ref, l_ref, acc_ref, active)

    @pl.when(j < nj_pages)
    def _():
        for kvh in range(N_KV_A):
            def load(ci, ch):
                return buf_ref[slot, pl.ds(ci * (sub * N_CH) + ch, sub, stride=N_CH), :].astype(bf16)

            first = ((b * N_KV_A + kvh) * nj_pages + j) * cps
            attend(kvh, functools.partial(load, ch=kvh), functools.partial(load, ch=N_KV_A + kvh),
                   cps, sub, j * pps * PAGE_SIZE, lambda ci, first=first: chunk_kvh_ref[first + ci] > 0)

    @pl.when(j == nj_pages)
    def _():
        ntail = tail_ref.shape[1]
        for kvh in range(N_KV_A):
            k = tail_ref[0, :, kvh * HEAD_DIM:(kvh + 1) * HEAD_DIM].astype(bf16)
            v = tail_ref[0, :, (N_KV_A + kvh) * HEAD_DIM:(N_KV_A + kvh + 1) * HEAD_DIM].astype(bf16)
            attend(kvh, lambda ci, k=k: k, lambda ci, v=v: v, 1, ntail, nj_pages * pps * PAGE_SIZE)
        for kvh in range(N_KV_A):
            for g in range(GQA):
                r = pl.ds(kvh * nrow + g * t, t)
                o_ref[:, (kvh * GQA + g) * HEAD_DIM:(kvh * GQA + g + 1) * HEAD_DIM] = (
                    acc_ref[r, :] / jnp.sum(l_ref[r, :], axis=-1, keepdims=True))


def _slc_paged(page_table, slopes, cache4, qz, tail, sel, expand, t, pps, pos0):
    nbatch, n_pages = page_table.shape
    njp = n_pages // pps
    bps = pps * (PAGE_SIZE // CMP_BLOCK)
    sub = min(512, pps * PAGE_SIZE)
    bpc = sub // CMP_BLOCK
    n_pb = n_pages * (PAGE_SIZE // CMP_BLOCK)
    chunk_sel = sel[..., :n_pb].reshape(nbatch, N_KV_A, t, n_pb // bpc, bpc).max(axis=(2, 4)) > 0.5
    chunk_kvh = chunk_sel.astype(jnp.int32).reshape(-1)
    chunk_any = chunk_sel.any(axis=1).astype(jnp.int32)
    gs = pltpu.PrefetchScalarGridSpec(
        num_scalar_prefetch=3,
        grid=(nbatch, njp + 1),
        in_specs=[
            pl.BlockSpec(memory_space=pltpu.SMEM),
            pl.BlockSpec(memory_space=pl.ANY),
            pl.BlockSpec((t, D_ATT), lambda b, j, *_: (b, 0)),
            pl.BlockSpec((1, LANES, KV_BRANCH), lambda b, j, *_: (b, 0, 0)),
            pl.BlockSpec((1, N_KV_A, t, bps), lambda b, j, *_: (b, 0, 0, j)),
            pl.BlockSpec((bps, pps * PAGE_SIZE), lambda b, j, *_: (0, 0)),
        ],
        out_specs=pl.BlockSpec((t, D_ATT), lambda b, j, *_: (b, 0)),
        scratch_shapes=[pltpu.VMEM((2, pps * ROWS_PER_PAGE4, HEAD_DIM), f32), pltpu.SemaphoreType.DMA((2,)),
                        pltpu.VMEM((N_KV_A * GQA * t, pps * PAGE_SIZE), f32),
                        pltpu.VMEM((N_KV_A * GQA * t, LANES), f32), pltpu.VMEM((N_KV_A * GQA * t, LANES), f32),
                        pltpu.VMEM((N_KV_A * GQA * t, HEAD_DIM), f32)],
    )
    return pl.pallas_call(
        functools.partial(_slc_paged_kernel, pps=pps, t=t, sub=sub, pos0=pos0),
        grid_spec=gs,
        out_shape=jax.ShapeDtypeStruct((nbatch * t, D_ATT), f32),
        compiler_params=_cparams(("arbitrary", "arbitrary")),
        name="slc_paged",
    )(page_table, chunk_any, chunk_kvh, slopes, cache4, qz, tail, sel, expand)


def _swa_sample_kernel(slopes_ref, q_ref, win_ref, tail_ref, o_ref, *, t, wbuf, pos0):
    tpos = lax.broadcasted_iota(jnp.int32, (t, 1), 0)
    qpos = pos0 + jnp.concatenate([tpos] * GQA, axis=0)
    nk = wbuf + CMP_BLOCK
    kpos = pos0 - wbuf + lax.broadcasted_iota(jnp.int32, (1, nk), 1)
    d = qpos - kpos
    ok = (d >= 0) & (d < WINDOW)
    df = d.astype(f32)
    for kvh in range(N_KV_A):
        k = jnp.concatenate([win_ref[pl.ds(kvh, wbuf, stride=N_CH), :].astype(bf16),
                             tail_ref[0, :, kvh * HEAD_DIM:(kvh + 1) * HEAD_DIM]], axis=0)
        v = jnp.concatenate([win_ref[pl.ds(N_KV_A + kvh, wbuf, stride=N_CH), :].astype(bf16),
                             tail_ref[0, :, (N_KV_A + kvh) * HEAD_DIM:(N_KV_A + kvh + 1) * HEAD_DIM]], axis=0)
        q = q_ref[:, kvh * GQA * HEAD_DIM:(kvh + 1) * GQA * HEAD_DIM]
        qs = jnp.concatenate([q[:, g * HEAD_DIM:(g + 1) * HEAD_DIM] for g in range(GQA)], axis=0).astype(bf16)
        slope = jnp.concatenate([jnp.full((t, 1), slopes_ref[kvh, g], f32) for g in range(GQA)], axis=0)
        s = _dot_nt(qs, k) * SCALE - slope * df
        s = jnp.where(ok, s, NEG_INF)
        e = jnp.exp(s - jnp.max(s, axis=-1, keepdims=True))
        p = jnp.where(ok, e / jnp.sum(e, axis=-1, keepdims=True), 0.0)
        o = _dot(p.astype(bf16), v)
        for g in range(GQA):
            o_ref[:, (kvh * GQA + g) * HEAD_DIM:(kvh * GQA + g + 1) * HEAD_DIM] = o[g * t:(g + 1) * t]


def _swa_sample(slopes, qz, win4, tail, nbatch, t, wbuf, pos0):
    return pl.pallas_call(
        functools.partial(_swa_sample_kernel, t=t, wbuf=wbuf, pos0=pos0),
        grid=(nbatch,),
        in_specs=[
            pl.BlockSpec(memory_space=pltpu.SMEM),
            pl.BlockSpec((t, D_ATT), lambda b: (b, 0)),
            pl.BlockSpec((wbuf * N_CH, HEAD_DIM), lambda b: (b, 0)),
            pl.BlockSpec((1, CMP_BLOCK, KV_BRANCH), lambda b: (b, 0, 0)),
        ],
        out_specs=pl.BlockSpec((t, D_ATT), lambda b: (b, 0)),
        out_shape=jax.ShapeDtypeStruct((nbatch * t, D_ATT), f32),
        compiler_params=_cparams(("parallel",)),
        name="swa_sample",
    )(slopes, qz, win4, tail)


def _round_up(a, b):
    return -(-a // b) * b


def _pad_blocks(kcv, ncp):
    return jnp.pad(kcv, ((0, 0), (0, 0), (0, ncp - kcv.shape[2]), (0, 0)))


def _expand_matrix(nblocks, nkeys):
    return (jnp.arange(nkeys)[None, :] // CMP_BLOCK == jnp.arange(nblocks)[:, None]).astype(bf16)


def _kv6(rows2d, nbatch, t):
    return rows2d.reshape(nbatch, t, 2, N_KV_A, HEAD_DIM)


def _layer_prompt(x, prm):
    nbatch, t, _ = x.shape
    m = nbatch * t
    x2d = x.reshape(m, D_MODEL)
    tm = min(1024, m)
    qz, qkvb, kv_slc, kv_swa, cmp4, slc4, swa4, small = _in_proj(
        x2d, prm["norm_w"], prm["w_main"], prm["w_small"], tm)
    nc = t // CMP_BLOCK
    kcv = _compress(cmp4, nbatch, prm["pe2d"], prm["w_cmp"], min(64, nc))
    kcv = _pad_blocks(kcv, _round_up(nc, LANES))
    tq = min(256, t)
    o_cmp, sel = _cmp_attend(prm["slopes"], qz, kcv, nbatch, t, tq, 0, min(N_SELECT, nc))
    tk = min(256, t)
    o_slc = _slc_prompt(prm["slopes"], qz, kv_slc, sel, _expand_matrix(sel.shape[-1], t), nbatch, t, tq, tk)
    o_swa = _swa_prompt(prm["slopes"], qz, kv_swa, nbatch, t, tq, tk)
    conv0 = jnp.zeros((nbatch, CONV_W - 1, CONV_DIM), f32)
    s0 = jnp.zeros((nbatch, N_HEADS_B, HEAD_DIM, HEAD_DIM), f32)
    o_b, s_new = _gdn(qkvb, small, conv0, s0, prm["conv_w"], prm["alog_row"], prm["dtb_row"],
                      prm["gdn_norm_w"], nbatch, t, 128)
    y = _merge_out(x2d, o_cmp, o_slc, o_swa, small, qz, o_b, prm["w_out"], prm["final_norm_w"], min(512, m))
    w = min(WINDOW, t)
    conv_new = qkvb.reshape(nbatch, t, CONV_DIM)[:, t - (CONV_W - 1):]
    return (y.reshape(nbatch, t, D_MODEL), _kv6(cmp4, nbatch, t), _kv6(slc4, nbatch, t),
            _kv6(swa4, nbatch, t)[:, t - w:], conv_new, s_new)


def _layer_sample(x, cache_cmp, cache_slc, cache_swa, conv_buf, s0, page_table, prm):
    nbatch, t, _ = x.shape
    m = nbatch * t
    n_pages = page_table.shape[1]
    past_len = n_pages * PAGE_SIZE
    wbuf = cache_swa.shape[1]
    x2d = x.reshape(m, D_MODEL)
    qz, qkvb, kv_slc, kv_swa, cmp4, slc4, swa4, small = _in_proj(
        x2d, prm["norm_w"], prm["w_main"], prm["w_small"], m)
    pad_tail = lambda r: jnp.pad(r.reshape(nbatch, t, KV_BRANCH), ((0, 0), (0, CMP_BLOCK - t), (0, 0)))
    pps = min(32, n_pages)
    kc_past = _compress_paged(page_table, cache_cmp.reshape(-1, HEAD_DIM), prm["pe2d"], prm["w_cmp"], pps)
    kc_tail = _compress(pad_tail(cmp4).reshape(nbatch * CMP_BLOCK * N_CH, HEAD_DIM), 1, prm["pe2d"], prm["w_cmp"], nbatch)
    kc_tail = jnp.swapaxes(kc_tail[0], 0, 1)[:, :, None, :]
    n_pb = past_len // CMP_BLOCK
    pps_s = min(64, n_pages)
    bps = pps_s * (PAGE_SIZE // CMP_BLOCK)
    ncp = _round_up(n_pb + 1, max(LANES, bps))
    kcv = _pad_blocks(jnp.concatenate([kc_past, kc_tail], axis=2), ncp)
    o_cmp, sel = _cmp_attend(prm["slopes"], qz, kcv, nbatch, t, t, past_len, min(N_SELECT, n_pb + 1))
    o_slc = _slc_paged(page_table, prm["slopes"], cache_slc.reshape(-1, HEAD_DIM), qz,
                       jnp.pad(kv_slc.reshape(nbatch, t, KV_BRANCH), ((0, 0), (0, LANES - t), (0, 0))), sel,
                       _expand_matrix(bps, pps_s * PAGE_SIZE), t, pps_s, past_len)
    o_swa = _swa_sample(prm["slopes"], qz, cache_swa.reshape(-1, HEAD_DIM), pad_tail(kv_swa), nbatch, t, wbuf, past_len)
    o_b, s_new = _gdn(qkvb, small, conv_buf, s0, prm["conv_w"], prm["alog_row"], prm["dtb_row"],
                      prm["gdn_norm_w"], nbatch, t, 128)
    y = _merge_out(x2d, o_cmp, o_slc, o_swa, small, qz, o_b, prm["w_out"], prm["final_norm_w"], min(512, m))
    kv_win = jnp.concatenate([cache_swa, _kv6(swa4, nbatch, t)], axis=1)[:, t:]
    conv_new = jnp.concatenate([conv_buf, qkvb.reshape(nbatch, t, CONV_DIM)], axis=1)[:, t:]
    return (y.reshape(nbatch, t, D_MODEL), _kv6(cmp4, nbatch, t), _kv6(slc4, nbatch, t), kv_win, conv_new, s_new)


def kernel(x_prompt, x_sample, cache_cmp, cache_slc, cache_swa, state_conv, state_gdn, page_table,
           norm_w, w_in, pe_cmp, w_cmp, conv_w, a_log, dt_bias, gdn_norm_w, w_out, final_norm_w):
    depth = norm_w.shape[0]
    assert depth == 1, "the final norm is fused into the single layer's output projection"
    l = 0
    w_main, w_small = _prep_w_in(w_in[l])
    head = jnp.arange(1, N_HEADS_A + 1, dtype=f32)
    lane_row = lambda vals: jnp.zeros((1, LANES), f32).at[0, A_OFF:A_OFF + N_HEADS_B].set(vals.astype(f32))
    prm = {
        "norm_w": norm_w[l], "w_main": w_main, "w_small": w_small,
        "pe2d": jnp.swapaxes(pe_cmp[l], 0, 1),
        "w_cmp": jnp.swapaxes(w_cmp[l], 0, 1).reshape(2, CMP_BLOCK // 2, 2 * HEAD_DIM, HEAD_DIM).astype(bf16),
        "conv_w": conv_w[l], "alog_row": lane_row(a_log[l]), "dtb_row": lane_row(dt_bias[l]),
        "gdn_norm_w": gdn_norm_w[l], "w_out": w_out[l].astype(bf16), "final_norm_w": final_norm_w,
        "slopes": jnp.exp2(-8.0 * head / N_HEADS_A).reshape(N_KV_A, GQA),
    }
    yp, p_cmp, p_slc, p_swa, p_conv, p_gdn = _layer_prompt(x_prompt, prm)
    ys, s_cmp, s_slc, s_swa, s_conv, s_gdn = _layer_sample(
        x_sample, cache_cmp[l], cache_slc[l], cache_swa[l], state_conv[l], state_gdn[l], page_table, prm)
    st = lambda a: a[None]
    return (yp, ys, st(p_cmp), st(p_slc), st(p_swa), st(p_conv), st(p_gdn),
            st(s_cmp), st(s_slc), st(s_swa), st(s_conv), st(s_gdn))
```

```python
import functools
import math

import jax
import jax.numpy as jnp
from jax import lax
from jax.experimental import pallas as pl
from jax.experimental.pallas import tpu as pltpu

f32 = jnp.float32
bf16 = jnp.bfloat16

D_MODEL = 2048
HEAD_DIM = 128
N_HEADS_A = 8
N_KV_A = 2
GQA = 4
D_ATT = N_HEADS_A * HEAD_DIM
CMP_BLOCK = 64
N_SELECT = 16
WINDOW = 512
N_HEADS_B = 8
D_GDN = N_HEADS_B * HEAD_DIM
CONV_W = 4
CONV_DIM = 3 * D_GDN
PAGE_SIZE = 128
NORM_EPS = 1e-6
NEG_INF = -1e30
FORCE_SCORE = 1e4
KV_BRANCH = 2 * N_KV_A * HEAD_DIM
N_CH = 2 * N_KV_A
ROWS_PER_PAGE4 = PAGE_SIZE * N_CH
SCALE = HEAD_DIM ** -0.5
G_OFF, A_OFF, B_OFF = 0, 3 * N_HEADS_A, 3 * N_HEADS_A + N_HEADS_B
LANES = 128
VMEM_LIMIT = 56 * 1024 * 1024
IN_PROJ_VMEM_LIMIT = 60 * 1024 * 1024

IN_TN = 512
N_QZ_TILES = 3 * D_ATT // IN_TN
N_B_TILES = CONV_DIM // IN_TN
N_IN_TILES = N_QZ_TILES + N_B_TILES + 3


def _cparams(sem):
    return pltpu.CompilerParams(dimension_semantics=sem, vmem_limit_bytes=VMEM_LIMIT)


def _dot(a, b):
    return jnp.dot(a, b, preferred_element_type=f32)


def _dot_nt(a, b):
    return lax.dot_general(a, b, (((1,), (1,)), ((), ())), preferred_element_type=f32)


def _dot_tn(a, b):
    return lax.dot_general(a, b, (((0,), (0,)), ((), ())), preferred_element_type=f32)


def _store_row4(o4_ref, acc):
    for ch in range(N_CH):
        o4_ref[pl.ds(ch, acc.shape[0], stride=N_CH), :] = acc[:, ch * HEAD_DIM:(ch + 1) * HEAD_DIM]


def _in_proj_kernel(x_ref, nw_ref, w_ref, ws_ref, qz_ref, b_ref, slc_ref, swa_ref, cmp4_ref, slc4_ref, swa4_ref,
                    small_ref, xn_ref):
    n = pl.program_id(1)

    @pl.when(n == 0)
    def _():
        x = x_ref[...]
        ms = jnp.mean(x * x, axis=-1, keepdims=True)
        xn = (x * lax.rsqrt(ms + NORM_EPS) * nw_ref[...]).astype(bf16)
        xn_ref[...] = xn
        small_ref[...] = _dot(xn, ws_ref[...])

    acc = _dot(xn_ref[...], w_ref[...])

    @pl.when(n < N_QZ_TILES)
    def _():
        qz_ref[...] = acc

    @pl.when((n >= N_QZ_TILES) & (n < N_QZ_TILES + N_B_TILES))
    def _():
        b_ref[...] = acc

    @pl.when(n == N_QZ_TILES + N_B_TILES)
    def _():
        _store_row4(cmp4_ref, acc)

    @pl.when(n == N_QZ_TILES + N_B_TILES + 1)
    def _():
        slc_ref[...] = acc.astype(bf16)
        _store_row4(slc4_ref, acc)

    @pl.when(n == N_QZ_TILES + N_B_TILES + 2)
    def _():
        swa_ref[...] = acc.astype(bf16)
        _store_row4(swa4_ref, acc)


def _in_proj(x2d, norm_w, w_main, w_small, tm):
    m = x2d.shape[0]
    S = jax.ShapeDtypeStruct
    nb0 = N_QZ_TILES
    return pl.pallas_call(
        _in_proj_kernel,
        grid=(m // tm, N_IN_TILES),
        in_specs=[
            pl.BlockSpec((tm, D_MODEL), lambda i, n: (i, 0)),
            pl.BlockSpec((1, D_MODEL), lambda i, n: (0, 0)),
            pl.BlockSpec((D_MODEL, IN_TN), lambda i, n: (0, n)),
            pl.BlockSpec((D_MODEL, LANES), lambda i, n: (0, 0)),
        ],
        out_specs=[
            pl.BlockSpec((tm, IN_TN), lambda i, n: (i, jnp.minimum(n, nb0 - 1))),
            pl.BlockSpec((tm, IN_TN), lambda i, n: (i, jnp.clip(n - nb0, 0, N_B_TILES - 1))),
            pl.BlockSpec((tm, KV_BRANCH), lambda i, n: (i, 0)),
            pl.BlockSpec((tm, KV_BRANCH), lambda i, n: (i, 0)),
            pl.BlockSpec((tm * N_CH, HEAD_DIM), lambda i, n: (i, 0)),
            pl.BlockSpec((tm * N_CH, HEAD_DIM), lambda i, n: (i, 0)),
            pl.BlockSpec((tm * N_CH, HEAD_DIM), lambda i, n: (i, 0)),
            pl.BlockSpec((tm, LANES), lambda i, n: (i, 0)),
        ],
        out_shape=[S((m, 3 * D_ATT), f32), S((m, CONV_DIM), f32), S((m, KV_BRANCH), bf16), S((m, KV_BRANCH), bf16),
                   S((m * N_CH, HEAD_DIM), f32), S((m * N_CH, HEAD_DIM), f32), S((m * N_CH, HEAD_DIM), f32),
                   S((m, LANES), f32)],
        scratch_shapes=[pltpu.VMEM((tm, D_MODEL), bf16)],
        compiler_params=pltpu.CompilerParams(dimension_semantics=("parallel", "arbitrary"),
                                             vmem_limit_bytes=IN_PROJ_VMEM_LIMIT),
        name="in_proj",
    )(x2d, norm_w.reshape(1, D_MODEL), w_main, w_small)


def _prep_w_in(w_in):
    o = [0]
    for s in (D_ATT, 3 * KV_BRANCH, 3 * N_HEADS_A, D_ATT, CONV_DIM, N_HEADS_B, N_HEADS_B, D_GDN):
        o.append(o[-1] + s)
    q_a, kv_a, g_a, z_a, qkv_b, a_b, b_b, z_b = (w_in[:, o[i]:o[i + 1]] for i in range(8))
    w_main = jnp.concatenate([q_a, z_a, z_b, qkv_b, kv_a], axis=1).astype(bf16)
    pad = jnp.zeros((w_in.shape[0], LANES - (3 * N_HEADS_A + 2 * N_HEADS_B)), w_in.dtype)
    w_small = jnp.concatenate([g_a, a_b, b_b, pad], axis=1).astype(bf16)
    return w_main, w_small


CMP_UNROLL = 4


def _compress_accumulate(load_rows, pe_ref, w_ref, nbk):
    def body(i, accs):
        new = []
        for c in range(2):
            halves = []
            for dl in range(2):
                l = 2 * i + dl
                pe_row = pe_ref[c, pl.ds(l, 1), :]
                halves.append(jnp.concatenate(
                    [load_rows(l, c * N_KV_A + h) + pe_row for h in range(N_KV_A)], axis=0))
            lhs = jnp.concatenate(halves, axis=1).astype(bf16)
            new.append(accs[c] + _dot(lhs, w_ref[c, i]))
        return tuple(new)

    z = jnp.zeros((N_KV_A * nbk, HEAD_DIM), f32)
    return lax.fori_loop(0, CMP_BLOCK // 2, body, (z, z), unroll=CMP_UNROLL)


def _compress_kernel(x_ref, pe_ref, w_ref, o_ref, *, nbk):
    accs = _compress_accumulate(
        lambda l, ch: x_ref[pl.ds(l * N_CH + ch, nbk, stride=CMP_BLOCK * N_CH), :], pe_ref, w_ref, nbk)
    for c in range(2):
        for h in range(N_KV_A):
            o_ref[0, c * N_KV_A + h] = accs[c][h * nbk:(h + 1) * nbk]


def _compress(rows4, nbatch, pe2d, w_cmp_bf, nbk):
    nblk = rows4.shape[0] // (nbatch * CMP_BLOCK * N_CH)
    nj = nblk // nbk
    return pl.pallas_call(
        functools.partial(_compress_kernel, nbk=nbk),
        grid=(nbatch, nj),
        in_specs=[
            pl.BlockSpec((nbk * CMP_BLOCK * N_CH, HEAD_DIM), lambda b, j: (b * nj + j, 0)),
            pl.BlockSpec((2, CMP_BLOCK, HEAD_DIM), lambda b, j: (0, 0, 0)),
            pl.BlockSpec((2, CMP_BLOCK // 2, 2 * HEAD_DIM, HEAD_DIM), lambda b, j: (0, 0, 0, 0)),
        ],
        out_specs=pl.BlockSpec((1, N_CH, nbk, HEAD_DIM), lambda b, j: (b, 0, j, 0)),
        out_shape=jax.ShapeDtypeStruct((nbatch, N_CH, nblk, HEAD_DIM), f32),
        compiler_params=_cparams(("parallel", "parallel")),
        name="compress",
    )(rows4, pe2d, w_cmp_bf)


def _topk_mask(sc, nsel):
    ncp = sc.shape[0]
    jblk = lax.broadcasted_iota(jnp.int32, (ncp, 1), 0)
    work = sc
    for _ in range(nsel):
        mx = jnp.max(work, axis=0, keepdims=True)
        first = jnp.min(jnp.where(work == mx, jblk, ncp), axis=0, keepdims=True)
        work = jnp.where(jblk == first, -2.0, work)
    return jnp.where((work == -2.0) & (sc >= 0.0), 1.0, 0.0)


def _topk_kernel(sc_ref, sel_ref, *, nsel):
    sel_ref[0] = _topk_mask(sc_ref[0], nsel)


def _topk(scores, nsel):
    ng, ncp, nq = scores.shape
    return pl.pallas_call(
        functools.partial(_topk_kernel, nsel=nsel),
        grid=(ng, nq // LANES),
        in_specs=[pl.BlockSpec((1, ncp, LANES), lambda g, c: (g, 0, c))],
        out_specs=pl.BlockSpec((1, ncp, LANES), lambda g, c: (g, 0, c)),
        out_shape=jax.ShapeDtypeStruct(scores.shape, f32),
        compiler_params=_cparams(("parallel", "parallel")),
        name="topk",
    )(scores)


def _cmp_kernel(slopes_ref, q_ref, kc_ref, vc_ref, o_ref, sel_ref, *, tq, tqp, ncp, pos0, nsel):
    kvh = pl.program_id(1)
    qt = pl.program_id(2)
    q = q_ref[...]
    if tqp > tq:
        q = jnp.concatenate([q, jnp.zeros((tqp - tq, q.shape[1]), f32)], axis=0)
    kc = kc_ref[0, 0].astype(bf16)
    vc = vc_ref[0, 0].astype(bf16)
    qpos = pos0 + qt * tq + lax.broadcasted_iota(jnp.int32, (1, tqp), 1)
    jblk = lax.broadcasted_iota(jnp.int32, (ncp, 1), 0)
    d = qpos - ((jblk + 1) * CMP_BLOCK - 1)
    ok = d >= 0
    df = d.astype(f32)
    imp = jnp.zeros((ncp, tqp), f32)
    for g in range(GQA):
        qg = q[:, g * HEAD_DIM:(g + 1) * HEAD_DIM].astype(bf16)
        s = _dot_nt(kc, qg) * SCALE
        s = jnp.where(ok, s - slopes_ref[kvh, g] * df, NEG_INF)
        mx = jnp.max(s, axis=0, keepdims=True)
        e = jnp.exp(s - mx)
        p = jnp.where(ok, e / jnp.sum(e, axis=0, keepdims=True), 0.0)
        imp = imp + p
        og = _dot_tn(p.astype(bf16), vc)
        o_ref[:, g * HEAD_DIM:(g + 1) * HEAD_DIM] = og[:tq]
    cur = qpos // CMP_BLOCK
    forced = (jblk == cur) | (jblk == 0)
    score = jnp.where(forced, FORCE_SCORE, jnp.where(jblk <= cur, imp, -1.0))
    if nsel is None:
        sel_ref[0, 0] = score
        return
    for cg in range(tqp // LANES):
        sel = _topk_mask(score[:, cg * LANES:(cg + 1) * LANES], nsel)
        lo = cg * LANES
        hi = min(tq, lo + LANES)
        sel_ref[0, 0, lo:hi, :] = sel.T[:hi - lo]


def _cmp_attend(slopes, qz, kcv, nbatch, t, tq, pos0, nsel):
    ncp = kcv.shape[2]
    tqp = max(tq, LANES)
    nqt = t // tq
    gw = GQA * HEAD_DIM
    packed = tq < LANES and (nbatch * t) % LANES == 0
    sel_spec = (pl.BlockSpec((1, 1, ncp, tqp), lambda b, h, i: (b, h, 0, i)) if packed
                else pl.BlockSpec((1, 1, tq, ncp), lambda b, h, i: (b, h, i, 0)))
    sel_shape = (nbatch, N_KV_A, ncp, nqt * tqp) if packed else (nbatch, N_KV_A, t, ncp)
    o_cmp, sel = pl.pallas_call(
        functools.partial(_cmp_kernel, tq=tq, tqp=tqp, ncp=ncp, pos0=pos0, nsel=None if packed else nsel),
        grid=(nbatch, N_KV_A, nqt),
        in_specs=[
            pl.BlockSpec(memory_space=pltpu.SMEM),
            pl.BlockSpec((tq, gw), lambda b, h, i: (b * nqt + i, h)),
            pl.BlockSpec((1, 1, ncp, HEAD_DIM), lambda b, h, i: (b, h, 0, 0)),
            pl.BlockSpec((1, 1, ncp, HEAD_DIM), lambda b, h, i: (b, N_KV_A + h, 0, 0)),
        ],
        out_specs=[pl.BlockSpec((tq, gw), lambda b, h, i: (b * nqt + i, h)), sel_spec],
        out_shape=[jax.ShapeDtypeStruct((nbatch * t, D_ATT), f32), jax.ShapeDtypeStruct(sel_shape, f32)],
        compiler_params=_cparams(("parallel", "parallel", "parallel")),
        name="cmp_attend",
    )(slopes, qz, kcv, kcv)
    if packed:
        scores = sel.reshape(nbatch, N_KV_A, ncp, nqt, tqp)[..., :tq].reshape(nbatch, N_KV_A, ncp, t)
        scores = jnp.transpose(scores, (1, 2, 0, 3)).reshape(N_KV_A, ncp, nbatch * t)
        mask = _topk(scores, nsel).reshape(N_KV_A, ncp, nbatch, t)
        sel = jnp.transpose(mask, (2, 0, 3, 1))
    return o_cmp, sel


def _lane_repeat(x, n):
    return x if n == 1 else jnp.concatenate([x] * n, axis=1)


def _flash_init(m_ref, l_ref, acc_ref):
    m_ref[...] = jnp.full(m_ref.shape, NEG_INF, f32)
    l_ref[...] = jnp.zeros(l_ref.shape, f32)
    acc_ref[...] = jnp.zeros(acc_ref.shape, f32)


LOG2E = 1.4426950408889634
MASK_DIST = 1e33
SLC_ROWS = 128


def _attn_prompt_body(kvh, qt, slopes_ref, q_ref, k_ref, v_ref, o_ref, qb_ref, dm_ref, s_ref, mn_ref,
                      m_ref, l_ref, acc_ref, *, tq, tk, kt_lo, kt_hi, tile_dist, active):
    _flash_init(m_ref, l_ref, acc_ref)
    for g in range(GQA):
        qb_ref[g * tq:(g + 1) * tq, :] = (q_ref[:, g * HEAD_DIM:(g + 1) * HEAD_DIM] * (SCALE * LOG2E)).astype(bf16)
    nrep = tk // LANES
    ngrp = tq // SLC_ROWS

    def group_update(groups, k, v):
        units = [(g, r) for r in groups for g in range(GQA)]
        for g, r in units:
            rows = pl.ds(g * tq + r * SLC_ROWS, SLC_ROWS)
            t2 = (_dot_nt(qb_ref[rows, :], k)
                  - (slopes_ref[kvh, g] * LOG2E) * dm_ref[r * SLC_ROWS:(r + 1) * SLC_ROWS, :])
            s_ref[rows, :] = t2
            mx = jnp.max(t2, axis=-1, keepdims=True)
            mn_ref[rows, :] = jnp.maximum(m_ref[rows, :], jnp.broadcast_to(mx, (SLC_ROWS, LANES)))
        for g, r in units:
            rows = pl.ds(g * tq + r * SLC_ROWS, SLC_ROWS)
            m_new = mn_ref[rows, :]
            p = jnp.exp2(s_ref[rows, :] - _lane_repeat(m_new, nrep))
            alpha = jnp.exp2(m_ref[rows, :] - m_new)
            psum = p[:, 0:LANES]
            for c in range(1, nrep):
                psum = psum + p[:, c * LANES:(c + 1) * LANES]
            l_ref[rows, :] = alpha * l_ref[rows, :] + psum
            acc_ref[rows, :] = alpha * acc_ref[rows, :] + _dot(p.astype(bf16), v)
            m_ref[rows, :] = m_new

    def tile_update(kt, acts):
        ks = pl.multiple_of(kt * tk, tk)
        k = k_ref[pl.ds(ks, tk), :]
        v = v_ref[pl.ds(ks, tk), :]
        dm_ref[...] = tile_dist(ks)
        if acts is None:
            group_update(tuple(range(ngrp)), k, v)
        elif ngrp == 2:
            pl.when(acts[0] & acts[1])(functools.partial(group_update, (0, 1), k, v))
            pl.when(acts[0] & jnp.logical_not(acts[1]))(functools.partial(group_update, (0,), k, v))
            pl.when(acts[1] & jnp.logical_not(acts[0]))(functools.partial(group_update, (1,), k, v))
        else:
            for r in range(ngrp):
                pl.when(acts[r])(functools.partial(group_update, (r,), k, v))

    def kt_body(kt, carry):
        if active is None:
            tile_update(kt, None)
        else:
            acts = [active(kt, r) for r in range(ngrp)]
            any_act = acts[0]
            for a in acts[1:]:
                any_act = any_act | a
            pl.when(any_act)(functools.partial(tile_update, kt, acts))
        return carry

    lax.fori_loop(kt_lo, kt_hi, kt_body, 0)
    for g in range(GQA):
        rows = pl.ds(g * tq, tq)
        o_ref[:, g * HEAD_DIM:(g + 1) * HEAD_DIM] = (
            acc_ref[rows, :] / jnp.sum(l_ref[rows, :], axis=-1, keepdims=True))


def _slc_kernel(flags_ref, slopes_ref, q_ref, k_ref, v_ref, sel_ref, e_ref, o_ref, *scratch, tq, tk, t):
    b = pl.program_id(0)
    kvh = pl.program_id(1)
    qt = pl.program_id(2)
    selb = sel_ref[0, 0].astype(bf16)
    qpos = qt * tq + lax.broadcasted_iota(jnp.int32, (tq, 1), 0)
    kofs = lax.broadcasted_iota(jnp.int32, (1, tk), 1)
    ngrp = tq // SLC_ROWS
    nkt = t // tk

    def tile_dist(ks):
        maskf = _dot(selb, e_ref[:, pl.ds(ks, tk)])
        d = qpos - (ks + kofs)
        return jnp.where((maskf > 0.5) & (d >= 0), d.astype(f32), MASK_DIST)

    def active(kt, r):
        row_group = (b * N_KV_A + kvh) * (t // SLC_ROWS) + qt * ngrp + r
        return flags_ref[row_group * nkt + kt] > 0

    _attn_prompt_body(kvh, qt, slopes_ref, q_ref, k_ref, v_ref, o_ref, *scratch, tq=tq, tk=tk,
                      kt_lo=0, kt_hi=(qt * tq + tq - 1) // tk + 1, tile_dist=tile_dist, active=active)


def _attn_scratch(tq, tk):
    return [pltpu.VMEM((GQA * tq, HEAD_DIM), bf16), pltpu.VMEM((tq, tk), f32),
            pltpu.VMEM((GQA * tq, tk), f32), pltpu.VMEM((GQA * tq, LANES), f32),
            pltpu.VMEM((GQA * tq, LANES), f32), pltpu.VMEM((GQA * tq, LANES), f32),
            pltpu.VMEM((GQA * tq, HEAD_DIM), f32)]


def _slc_prompt(slopes, qz, kv_slc, sel, expand, nbatch, t, tq, tk):
    nqt = t // tq
    gw = GQA * HEAD_DIM
    ncp = sel.shape[-1]
    bpt = tk // CMP_BLOCK
    flags = sel.reshape(nbatch, N_KV_A, t // SLC_ROWS, SLC_ROWS, ncp // bpt, bpt).max(axis=(3, 5))
    flags = (flags[..., :t // tk] > 0.5).astype(jnp.int32).reshape(-1)
    gs = pltpu.PrefetchScalarGridSpec(
        num_scalar_prefetch=1,
        grid=(nbatch, N_KV_A, nqt),
        in_specs=[
            pl.BlockSpec(memory_space=pltpu.SMEM),
            pl.BlockSpec((tq, gw), lambda b, h, i, fl: (b * nqt + i, h)),
            pl.BlockSpec((t, HEAD_DIM), lambda b, h, i, fl: (b, h)),
            pl.BlockSpec((t, HEAD_DIM), lambda b, h, i, fl: (b, N_KV_A + h)),
            pl.BlockSpec((1, 1, tq, ncp), lambda b, h, i, fl: (b, h, i, 0)),
            pl.BlockSpec((ncp, t), lambda b, h, i, fl: (0, 0)),
        ],
        out_specs=pl.BlockSpec((tq, gw), lambda b, h, i, fl: (b * nqt + i, h)),
        scratch_shapes=_attn_scratch(tq, tk),
    )
    return pl.pallas_call(
        functools.partial(_slc_kernel, tq=tq, tk=tk, t=t),
        grid_spec=gs,
        out_shape=jax.ShapeDtypeStruct((nbatch * t, D_ATT), f32),
        compiler_params=_cparams(("parallel", "parallel", "parallel")),
        name="slc_prompt",
    )(flags, slopes, qz, kv_slc, kv_slc, sel, expand)


def _swa_kernel(slopes_ref, q_ref, k_ref, v_ref, o_ref, *scratch, tq, tk):
    kvh = pl.program_id(1)
    qt = pl.program_id(2)
    qpos = qt * tq + lax.broadcasted_iota(jnp.int32, (tq, 1), 0)
    kofs = lax.broadcasted_iota(jnp.int32, (1, tk), 1)

    def tile_dist(ks):
        d = qpos - (ks + kofs)
        return jnp.where((d >= 0) & (d < WINDOW), d.astype(f32), MASK_DIST)

    _attn_prompt_body(kvh, qt, slopes_ref, q_ref, k_ref, v_ref, o_ref, *scratch, tq=tq, tk=tk,
                      kt_lo=jnp.maximum(qt * tq - (WINDOW - 1), 0) // tk, kt_hi=(qt * tq + tq - 1) // tk + 1,
                      tile_dist=tile_dist, active=None)


def _swa_prompt(slopes, qz, kv_swa, nbatch, t, tq, tk):
    nqt = t // tq
    gw = GQA * HEAD_DIM
    return pl.pallas_call(
        functools.partial(_swa_kernel, tq=tq, tk=tk),
        grid=(nbatch, N_KV_A, nqt),
        in_specs=[
            pl.BlockSpec(memory_space=pltpu.SMEM),
            pl.BlockSpec((tq, gw), lambda b, h, i: (b * nqt + i, h)),
            pl.BlockSpec((t, HEAD_DIM), lambda b, h, i: (b, h)),
            pl.BlockSpec((t, HEAD_DIM), lambda b, h, i: (b, N_KV_A + h)),
        ],
        out_specs=pl.BlockSpec((tq, gw), lambda b, h, i: (b * nqt + i, h)),
        out_shape=jax.ShapeDtypeStruct((nbatch * t, D_ATT), f32),
        scratch_shapes=_attn_scratch(tq, tk),
        compiler_params=_cparams(("parallel", "parallel", "parallel")),
        name="swa_prompt",
    )(slopes, qz, kv_swa, kv_swa)


def _softplus(x):
    return jnp.maximum(x, 0.0) + jnp.log1p(jnp.exp(-jnp.abs(x)))


def _sigmoid(x):
    return 1.0 / (1.0 + jnp.exp(-x))


def _silu(x):
    return x * _sigmoid(x)


def _dot_hi(a, b):
    return jnp.dot(a, b, precision=lax.Precision.HIGHEST, preferred_element_type=f32)


INV_BASE = 16


def _bdot(a, b):
    return _dot(a.astype(bf16), b.astype(bf16))


def _gdn_kernel(x_ref, small_ref, conv0_ref, s0_ref, cw_ref, alog_ref, dtb_ref, nw_ref,
                o_ref, s_out_ref, xp_ref, st_ref, gam_ref, n_ref, t_ref, pw_ref, uw_ref,
                k_ref, kb_ref, q_ref, qg_ref, kg_ref, qk_ref, vn_ref, rhs_ref, *, c, tv, halo):
    ci = pl.program_id(1)
    nci = pl.num_programs(1)

    @pl.when(ci == 0)
    def _():
        xp_ref[...] = jnp.zeros(xp_ref.shape, f32)
        xp_ref[halo - (CONV_W - 1):halo, :] = conv0_ref[0]
        st_ref[...] = s0_ref[0]

    xp_ref[halo:halo + tv, :] = x_ref[...]
    y = xp_ref[pl.ds(halo, c), :] * cw_ref[CONV_W - 1:CONV_W, :]
    for i in range(CONV_W - 2, -1, -1):
        y = y + xp_ref[pl.ds(halo - (CONV_W - 1) + i, c), :] * cw_ref[i:i + 1, :]
    y = _silu(y)
    xp_ref[halo - (CONV_W - 1):halo, :] = xp_ref[halo + tv - (CONV_W - 1):halo + tv, :]

    ri = lax.broadcasted_iota(jnp.int32, (c, 1), 0)
    rowi = lax.broadcasted_iota(jnp.int32, (c, c), 0)
    coli = lax.broadcasted_iota(jnp.int32, (c, c), 1)
    lower = rowi >= coli
    strict = rowi > coli
    tril = lower.astype(f32)
    eye = (rowi == coli).astype(f32)
    same_blk = {}
    s = INV_BASE
    while s <= c:
        same_blk[s] = (rowi // s) == (coli // s)
        s *= 2

    sm = small_ref[...]
    if tv < c:
        sm = jnp.concatenate([sm, jnp.zeros((c - tv, LANES), f32)], axis=0)
    g_all = -jnp.exp(alog_ref[...]) * _softplus(sm + dtb_ref[...])
    beta_all = _sigmoid(sm)
    if tv < c:
        live = ri < tv
        g_all = jnp.where(live, g_all, 0.0)
        beta_all = jnp.where(live, beta_all, 0.0)
    gc_all = _dot_hi(tril, g_all)
    gc_rows = gc_all.T
    heads = range(N_HEADS_B)
    decay = []

    for h in heads:
        yq = y[:, h * HEAD_DIM:(h + 1) * HEAD_DIM]
        yk = y[:, D_GDN + h * HEAD_DIM:D_GDN + (h + 1) * HEAD_DIM]
        v = y[:, 2 * D_GDN + h * HEAD_DIM:2 * D_GDN + (h + 1) * HEAD_DIM]
        q = yq * lax.rsqrt(jnp.sum(yq * yq, axis=-1, keepdims=True) + NORM_EPS) * SCALE
        k = yk * lax.rsqrt(jnp.sum(yk * yk, axis=-1, keepdims=True) + NORM_EPS)
        if tv < c:
            q = jnp.where(live, q, 0.0)
            k = jnp.where(live, k, 0.0)
            v = jnp.where(live, v, 0.0)
        beta = beta_all[:, B_OFF + h:B_OFF + h + 1]
        gcol = gc_all[:, A_OFF + h:A_OFF + h + 1]
        grow = gc_rows[A_OFF + h:A_OFF + h + 1, :]
        gam_ref[h] = jnp.where(lower, jnp.exp(jnp.where(lower, gcol - grow, 0.0)), 0.0)
        eg = jnp.exp(gcol)
        g_last = gc_all[c - 1:c, A_OFF + h:A_OFF + h + 1]
        decay.append(jnp.exp(g_last))
        kb = k * beta
        k_ref[h] = k.astype(bf16)
        kb_ref[h] = kb.astype(bf16)
        q_ref[h] = q.astype(bf16)
        qg_ref[h] = (q * eg).astype(bf16)
        kg_ref[h] = (k * jnp.exp(g_last - gcol)).astype(bf16)
        rhs_ref[h, :, 0:HEAD_DIM] = (v * beta).astype(bf16)
        rhs_ref[h, :, HEAD_DIM:2 * HEAD_DIM] = (kb * eg).astype(bf16)

    for h in heads:
        gam = gam_ref[h]
        nmat = jnp.where(strict, _dot_nt(kb_ref[h], k_ref[h]) * gam, 0.0)
        n_ref[h] = nmat
        nd = jnp.where(same_blk[INV_BASE], nmat, 0.0)
        t_ref[h] = eye - nd
        pw_ref[h] = nd.astype(bf16)
        qk_ref[h] = (_dot_nt(q_ref[h], k_ref[h]) * gam).astype(bf16)

    for _ in range(int(math.log2(INV_BASE)) - 1):
        for h in heads:
            pw = pw_ref[h]
            pw_ref[h] = _dot(pw, pw).astype(bf16)
        for h in heads:
            t = t_ref[h]
            t_ref[h] = t + _dot(t.astype(bf16), pw_ref[h])

    s = INV_BASE if tv > INV_BASE else c
    while s < c:
        pair = same_blk[2 * s] & jnp.logical_not(same_blk[s])
        for h in heads:
            off = jnp.where(pair, n_ref[h], 0.0)
            uw_ref[h, :, 0:HEAD_DIM] = _bdot(t_ref[h], off)
        for h in heads:
            t = t_ref[h]
            t_ref[h] = t - _bdot(uw_ref[h, :, 0:HEAD_DIM], t)
        s *= 2

    for h in heads:
        uw_ref[h] = _dot(t_ref[h].astype(bf16), rhs_ref[h])

    for h in heads:
        sb = st_ref[h].astype(bf16)
        v_new = uw_ref[h, :, 0:HEAD_DIM] - _dot(uw_ref[h, :, HEAD_DIM:2 * HEAD_DIM].astype(bf16), sb)
        vn_ref[h] = v_new.astype(bf16)
        uw_ref[h, :, 0:HEAD_DIM] = _dot(qg_ref[h], sb)
    for h in heads:
        vn = vn_ref[h]
        o = uw_ref[h, :, 0:HEAD_DIM] + _dot(qk_ref[h], vn)
        st_ref[h] = st_ref[h] * decay[h] + _dot_tn(kg_ref[h], vn)
        on = o * lax.rsqrt(jnp.mean(o * o, axis=-1, keepdims=True) + NORM_EPS) * nw_ref[...]
        o_ref[:, h * HEAD_DIM:(h + 1) * HEAD_DIM] = on[:tv]

    @pl.when(ci == nci - 1)
    def _():
        s_out_ref[0] = st_ref[...]


def _gdn(qkv_b, small, conv0, s0, conv_w, alog_row, dtb_row, gdn_norm_w, nbatch, t, c):
    tv = min(c, t)
    nci = t // tv
    halo = 8
    return pl.pallas_call(
        functools.partial(_gdn_kernel, c=c, tv=tv, halo=halo),
        grid=(nbatch, nci),
        in_specs=[
            pl.BlockSpec((tv, CONV_DIM), lambda b, i: (b * nci + i, 0)),
            pl.BlockSpec((tv, LANES), lambda b, i: (b * nci + i, 0)),
            pl.BlockSpec((1, CONV_W - 1, CONV_DIM), lambda b, i: (b, 0, 0)),
            pl.BlockSpec((1, N_HEADS_B, HEAD_DIM, HEAD_DIM), lambda b, i: (b, 0, 0, 0)),
            pl.BlockSpec((CONV_W, CONV_DIM), lambda b, i: (0, 0)),
            pl.BlockSpec((1, LANES), lambda b, i: (0, 0)),
            pl.BlockSpec((1, LANES), lambda b, i: (0, 0)),
            pl.BlockSpec((1, HEAD_DIM), lambda b, i: (0, 0)),
        ],
        out_specs=[
            pl.BlockSpec((tv, D_GDN), lambda b, i: (b * nci + i, 0)),
            pl.BlockSpec((1, N_HEADS_B, HEAD_DIM, HEAD_DIM), lambda b, i: (b, 0, 0, 0)),
        ],
        out_shape=[jax.ShapeDtypeStruct((nbatch * t, D_GDN), f32),
                   jax.ShapeDtypeStruct((nbatch, N_HEADS_B, HEAD_DIM, HEAD_DIM), f32)],
        scratch_shapes=[pltpu.VMEM((halo + c, CONV_DIM), f32),
                        pltpu.VMEM((N_HEADS_B, HEAD_DIM, HEAD_DIM), f32)]
        + [pltpu.VMEM((N_HEADS_B, c, c), f32)] * 3
        + [pltpu.VMEM((N_HEADS_B, c, c), bf16)]
        + [pltpu.VMEM((N_HEADS_B, c, 2 * HEAD_DIM), f32)]
        + [pltpu.VMEM((N_HEADS_B, c, HEAD_DIM), bf16)] * 5
        + [pltpu.VMEM((N_HEADS_B, c, c), bf16)]
        + [pltpu.VMEM((N_HEADS_B, c, HEAD_DIM), bf16)]
        + [pltpu.VMEM((N_HEADS_B, c, 2 * HEAD_DIM), bf16)],
        compiler_params=_cparams(("parallel", "arbitrary")),
        name="gdn",
    )(qkv_b, small, conv0, s0, conv_w, alog_row, dtb_row, gdn_norm_w.reshape(1, HEAD_DIM))


MERGE_ROWS = 256


def _merge_kernel(x_ref, oc_ref, os_ref, ow_ref, small_ref, za_ref, zb_ref, ob_ref, w_ref, fw_ref, y_ref, mix_ref):
    tm = x_ref.shape[0]
    sub = min(MERGE_ROWS, tm)
    for r in range(tm // sub):
        rs = slice(r * sub, (r + 1) * sub)
        gates = _sigmoid(small_ref[rs, :])
        for h in range(N_HEADS_A):
            cs = slice(h * HEAD_DIM, (h + 1) * HEAD_DIM)
            o_a = (gates[:, G_OFF + h:G_OFF + h + 1] * oc_ref[rs, cs]
                   + gates[:, G_OFF + N_HEADS_A + h:G_OFF + N_HEADS_A + h + 1] * os_ref[rs, cs]
                   + gates[:, G_OFF + 2 * N_HEADS_A + h:G_OFF + 2 * N_HEADS_A + h + 1] * ow_ref[rs, cs])
            mix_ref[rs, cs] = (o_a * _silu(za_ref[rs, cs])).astype(bf16)
        mix_ref[rs, D_ATT:] = (ob_ref[rs, :] * _silu(zb_ref[rs, :])).astype(bf16)
        hres = x_ref[rs, :] + _dot(mix_ref[rs, :], w_ref[...])
        ms = jnp.mean(hres * hres, axis=-1, keepdims=True)
        y_ref[rs, :] = hres * lax.rsqrt(ms + NORM_EPS) * fw_ref[...]


def _merge_out(x2d, o_cmp, o_slc, o_swa, small, qz, o_b, w_out_bf, final_norm_w, tm):
    m = x2d.shape[0]
    row = lambda i: (i, 0)
    return pl.pallas_call(
        _merge_kernel,
        grid=(m // tm,),
        in_specs=[
            pl.BlockSpec((tm, D_MODEL), row),
            pl.BlockSpec((tm, D_ATT), row),
            pl.BlockSpec((tm, D_ATT), row),
            pl.BlockSpec((tm, D_ATT), row),
            pl.BlockSpec((tm, LANES), row),
            pl.BlockSpec((tm, D_ATT), lambda i: (i, 1)),
            pl.BlockSpec((tm, D_GDN), lambda i: (i, 2)),
            pl.BlockSpec((tm, D_GDN), row),
            pl.BlockSpec((D_ATT + D_GDN, D_MODEL), lambda i: (0, 0), pipeline_mode=pl.Buffered(1)),
            pl.BlockSpec((1, D_MODEL), lambda i: (0, 0)),
        ],
        out_specs=pl.BlockSpec((tm, D_MODEL), row),
        out_shape=jax.ShapeDtypeStruct((m, D_MODEL), f32),
        scratch_shapes=[pltpu.VMEM((tm, D_ATT + D_GDN), bf16)],
        compiler_params=_cparams(("parallel",)),
        name="merge_out",
    )(x2d, o_cmp, o_slc, o_swa, small, qz, qz, o_b, w_out_bf, final_norm_w.reshape(1, D_MODEL))


BLOCK_ROWS4 = CMP_BLOCK * N_CH


def _page_copies(cache_ref, buf_ref, sem_ref, pt_ref, b, j, slot, i, pps, by_row):
    page = pt_ref[b, j * pps + i]
    if not by_row:
        return [pltpu.make_async_copy(
            cache_ref.at[pl.ds(page * ROWS_PER_PAGE4, ROWS_PER_PAGE4), :],
            buf_ref.at[slot, pl.ds(i * ROWS_PER_PAGE4, ROWS_PER_PAGE4), :],
            sem_ref.at[slot])]
    bpp = PAGE_SIZE // CMP_BLOCK
    return [pltpu.make_async_copy(
        cache_ref.at[pl.ds(page * ROWS_PER_PAGE4 + n * BLOCK_ROWS4, BLOCK_ROWS4), :],
        buf_ref.at[slot, :, i * bpp + n, :],
        sem_ref.at[slot]) for n in range(bpp)]


def _paged_fetch(cache_ref, buf_ref, sem_ref, pt_ref, pps, nj_pages, need=None, by_row=False):
    b = pl.program_id(0)
    j = pl.program_id(1)
    nb = pl.num_programs(0)
    nj = pl.num_programs(1)
    step = b * nj_pages + jnp.minimum(j, nj_pages - 1)
    slot = step % 2

    def guarded(bb, jj, i, fn):
        if need is None:
            fn()
        else:
            pl.when(need(bb, jj, i))(fn)

    def run(what, bb, jj, sl, i):
        for cp in _page_copies(cache_ref, buf_ref, sem_ref, pt_ref, bb, jj, sl, i, pps, by_row):
            getattr(cp, what)()

    def start(bb, jj, sl):
        for i in range(pps):
            guarded(bb, jj, i, functools.partial(run, "start", bb, jj, sl, i))

    @pl.when((b == 0) & (j == 0))
    def _():
        start(0, 0, 0)

    @pl.when(j < nj_pages)
    def _():
        last_j = j == nj_pages - 1
        nb_ = jnp.where(last_j, b + 1, b)
        nj_ = jnp.where(last_j, 0, j + 1)

        @pl.when(nb_ < nb)
        def _():
            start(nb_, nj_, 1 - slot)

        for i in range(pps):
            guarded(b, j, i, functools.partial(run, "wait", b, j, slot, i))

    return slot


def _compress_paged_kernel(pt_ref, cache_ref, pe_ref, w_ref, o_ref, buf_ref, sem_ref, *, pps):
    slot = _paged_fetch(cache_ref, buf_ref, sem_ref, pt_ref, pps, pl.num_programs(1), by_row=True)
    nbk = pps * (PAGE_SIZE // CMP_BLOCK)
    accs = _compress_accumulate(lambda l, ch: buf_ref[slot, l * N_CH + ch], pe_ref, w_ref, nbk)
    for c in range(2):
        for h in range(N_KV_A):
            o_ref[0, c * N_KV_A + h] = accs[c][h * nbk:(h + 1) * nbk]


def _compress_paged(page_table, cache4, pe2d, w_cmp_bf, pps):
    nbatch, n_pages = page_table.shape
    nj = n_pages // pps
    nbk = pps * (PAGE_SIZE // CMP_BLOCK)
    gs = pltpu.PrefetchScalarGridSpec(
        num_scalar_prefetch=1,
        grid=(nbatch, nj),
        in_specs=[
            pl.BlockSpec(memory_space=pl.ANY),
            pl.BlockSpec((2, CMP_BLOCK, HEAD_DIM), lambda b, j, pt: (0, 0, 0)),
            pl.BlockSpec((2, CMP_BLOCK // 2, 2 * HEAD_DIM, HEAD_DIM), lambda b, j, pt: (0, 0, 0, 0)),
        ],
        out_specs=pl.BlockSpec((1, N_CH, nbk, HEAD_DIM), lambda b, j, pt: (b, 0, j, 0)),
        scratch_shapes=[pltpu.VMEM((2, BLOCK_ROWS4, nbk, HEAD_DIM), f32), pltpu.SemaphoreType.DMA((2,))],
    )
    return pl.pallas_call(
        functools.partial(_compress_paged_kernel, pps=pps),
        grid_spec=gs,
        out_shape=jax.ShapeDtypeStruct((nbatch, N_CH, nj * nbk, HEAD_DIM), f32),
        compiler_params=_cparams(("arbitrary", "arbitrary")),
        name="compress_paged",
    )(page_table, cache4, pe2d, w_cmp_bf)


def _two_phase_attend(qs, slope2, qpos, kpos0, load_k, load_v, load_mask, nchunks, chunk,
                      s_ref, rows, m_ref, l_ref, acc_ref, active=None):
    nrow = qs.shape[0]
    nrep = chunk // LANES
    kofs = lax.broadcasted_iota(jnp.int32, (1, chunk), 1)
    unroll = math.gcd(nchunks, 4)

    def cols(ci):
        return pl.ds(pl.multiple_of(ci * chunk, chunk), chunk)

    def skippable(ci, fn, carry):
        if active is None:
            return fn(carry)
        return lax.cond(active(ci), fn, lambda c: c, carry)

    def phase1(ci, mrun):
        def update(mrun):
            d = qpos - (kpos0 + ci * chunk + kofs)
            dmask = jnp.where((load_mask(ci) > 0.5) & (d >= 0), d.astype(f32), MASK_DIST)
            t2 = _dot_nt(qs, load_k(ci)) * (SCALE * LOG2E) - slope2 * dmask
            s_ref[rows, cols(ci)] = t2
            for c in range(nrep):
                mrun = jnp.maximum(mrun, t2[:, c * LANES:(c + 1) * LANES])
            return mrun

        return skippable(ci, update, mrun)

    mrun = lax.fori_loop(0, nchunks, phase1, jnp.full((nrow, LANES), NEG_INF, f32), unroll=unroll)
    m_prev = m_ref[rows, :]
    m_new = jnp.maximum(m_prev, jnp.broadcast_to(jnp.max(mrun, axis=-1, keepdims=True), (nrow, LANES)))

    def phase2(ci, carry):
        def update(carry):
            lp, acc = carry
            p = jnp.exp2(s_ref[rows, cols(ci)] - _lane_repeat(m_new, nrep))
            for c in range(nrep):
                lp = lp + p[:, c * LANES:(c + 1) * LANES]
            return lp, acc + _dot(p.astype(bf16), load_v(ci))

        return skippable(ci, update, carry)

    z = jnp.zeros((nrow, LANES), f32)
    lp, acc = lax.fori_loop(0, nchunks, phase2, (z, z), unroll=unroll)
    alpha = jnp.exp2(m_prev - m_new)
    l_ref[rows, :] = alpha * l_ref[rows, :] + lp
    acc_ref[rows, :] = alpha * acc_ref[rows, :] + acc
    m_ref[rows, :] = m_new


def _slc_paged_kernel(pt_ref, chunk_any_ref, chunk_kvh_ref, slopes_ref, cache_ref, q_ref, tail_ref, sel_ref,
                      e_ref, o_ref, buf_ref, sem_ref, s_ref, m_ref, l_ref, acc_ref, *, pps, t, sub, pos0):
    b = pl.program_id(0)
    j = pl.program_id(1)
    nj_pages = pl.num_programs(1) - 1
    ppc = sub // PAGE_SIZE
    cps = pps // ppc
    slot = _paged_fetch(cache_ref, buf_ref, sem_ref, pt_ref, pps, nj_pages,
                        need=lambda bb, jj, i: chunk_any_ref[bb, jj * cps + i // ppc] > 0)
    nrow = GQA * t

    @pl.when(j == 0)
    def _():
        _flash_init(m_ref, l_ref, acc_ref)

    tpos = lax.broadcasted_iota(jnp.int32, (t, 1), 0)
    qpos = pos0 + jnp.concatenate([tpos] * GQA, axis=0)

    def attend(kvh, load_k, load_v, nchunks, chunk, kpos0, active=None):
        q = q_ref[:, kvh * GQA * HEAD_DIM:(kvh + 1) * GQA * HEAD_DIM]
        qs = jnp.concatenate([q[:, g * HEAD_DIM:(g + 1) * HEAD_DIM] for g in range(GQA)], axis=0).astype(bf16)
        slope2 = jnp.concatenate([jnp.full((t, 1), slopes_ref[kvh, g] * LOG2E, f32) for g in range(GQA)], axis=0)
        selb = sel_ref[0, kvh].astype(bf16)

        def load_mask(ci):
            maskf = _dot(selb, e_ref[:, pl.ds(pl.multiple_of(ci * chunk, chunk), chunk)])
            return jnp.concatenate([maskf] * GQA, axis=0)

        _two_phase_attend(qs, slope2, qpos, kpos0, load_k, load_v, load_mask, nchunks, chunk,
                          s_ref, pl.ds(kvh * nrow, nrow), m_ref, l_ref, acc_ref, active)

    @pl.when(j < nj_pages)
    def _():
        for kvh in range(N_KV_A):
            def load(ci, ch):
                return buf_ref[slot, pl.ds(ci * (sub * N_CH) + ch, sub, stride=N_CH), :].astype(bf16)

            first = ((b * N_KV_A + kvh) * nj_pages + j) * cps
            attend(kvh, functools.partial(load, ch=kvh), functools.partial(load, ch=N_KV_A + kvh),
                   cps, sub, j * pps * PAGE_SIZE, lambda ci, first=first: chunk_kvh_ref[first + ci] > 0)

    @pl.when(j == nj_pages)
    def _():
        ntail = tail_ref.shape[1]
        for kvh in range(N_KV_A):
            k = tail_ref[0, :, kvh * HEAD_DIM:(kvh + 1) * HEAD_DIM].astype(bf16)
            v = tail_ref[0, :, (N_KV_A + kvh) * HEAD_DIM:(N_KV_A + kvh + 1) * HEAD_DIM].astype(bf16)
            attend(kvh, lambda ci, k=k: k, lambda ci, v=v: v, 1, ntail, nj_pages * pps * PAGE_SIZE)
        for kvh in range(N_KV_A):
            for g in range(GQA):
                r = pl.ds(kvh * nrow + g * t, t)
                o_ref[:, (kvh * GQA + g) * HEAD_DIM:(kvh * GQA + g + 1) * HEAD_DIM] = (
                    acc_ref[r, :] / jnp.sum(l_ref[r, :], axis=-1, keepdims=True))


def _slc_paged(page_table, slopes, cache4, qz, tail, sel, expand, t, pps, pos0):
    nbatch, n_pages = page_table.shape
    njp = n_pages // pps
    bps = pps * (PAGE_SIZE // CMP_BLOCK)
    sub = min(512, pps * PAGE_SIZE)
    bpc = sub // CMP_BLOCK
    n_pb = n_pages * (PAGE_SIZE // CMP_BLOCK)
    chunk_sel = sel[..., :n_pb].reshape(nbatch, N_KV_A, t, n_pb // bpc, bpc).max(axis=(2, 4)) > 0.5
    chunk_kvh = chunk_sel.astype(jnp.int32).reshape(-1)
    chunk_any = chunk_sel.any(axis=1).astype(jnp.int32)
    gs = pltpu.PrefetchScalarGridSpec(
        num_scalar_prefetch=3,
        grid=(nbatch, njp + 1),
        in_specs=[
            pl.BlockSpec(memory_space=pltpu.SMEM),
            pl.BlockSpec(memory_space=pl.ANY),
            pl.BlockSpec((t, D_ATT), lambda b, j, *_: (b, 0)),
            pl.BlockSpec((1, LANES, KV_BRANCH), lambda b, j, *_: (b, 0, 0)),
            pl.BlockSpec((1, N_KV_A, t, bps), lambda b, j, *_: (b, 0, 0, j)),
            pl.BlockSpec((bps, pps * PAGE_SIZE), lambda b, j, *_: (0, 0)),
        ],
        out_specs=pl.BlockSpec((t, D_ATT), lambda b, j, *_: (b, 0)),
        scratch_shapes=[pltpu.VMEM((2, pps * ROWS_PER_PAGE4, HEAD_DIM), f32), pltpu.SemaphoreType.DMA((2,)),
                        pltpu.VMEM((N_KV_A * GQA * t, pps * PAGE_SIZE), f32),
                        pltpu.VMEM((N_KV_A * GQA * t, LANES), f32), pltpu.VMEM((N_KV_A * GQA * t, LANES), f32),
                        pltpu.VMEM((N_KV_A * GQA * t, HEAD_DIM), f32)],
    )
    return pl.pallas_call(
        functools.partial(_slc_paged_kernel, pps=pps, t=t, sub=sub, pos0=pos0),
        grid_spec=gs,
        out_shape=jax.ShapeDtypeStruct((nbatch * t, D_ATT), f32),
        compiler_params=_cparams(("arbitrary", "arbitrary")),
        name="slc_paged",
    )(page_table, chunk_any, chunk_kvh, slopes, cache4, qz, tail, sel, expand)


def _swa_sample_kernel(slopes_ref, q_ref, win_ref, tail_ref, o_ref, *, t, wbuf, pos0):
    tpos = lax.broadcasted_iota(jnp.int32, (t, 1), 0)
    qpos = pos0 + jnp.concatenate([tpos] * GQA, axis=0)
    nk = wbuf + CMP_BLOCK
    kpos = pos0 - wbuf + lax.broadcasted_iota(jnp.int32, (1, nk), 1)
    d = qpos - kpos
    ok = (d >= 0) & (d < WINDOW)
    df = d.astype(f32)
    for kvh in range(N_KV_A):
        k = jnp.concatenate([win_ref[pl.ds(kvh, wbuf, stride=N_CH), :].astype(bf16),
                             tail_ref[0, :, kvh * HEAD_DIM:(kvh + 1) * HEAD_DIM]], axis=0)
        v = jnp.concatenate([win_ref[pl.ds(N_KV_A + kvh, wbuf, stride=N_CH), :].astype(bf16),
                             tail_ref[0, :, (N_KV_A + kvh) * HEAD_DIM:(N_KV_A + kvh + 1) * HEAD_DIM]], axis=0)
        q = q_ref[:, kvh * GQA * HEAD_DIM:(kvh + 1) * GQA * HEAD_DIM]
        qs = jnp.concatenate([q[:, g * HEAD_DIM:(g + 1) * HEAD_DIM] for g in range(GQA)], axis=0).astype(bf16)
        slope = jnp.concatenate([jnp.full((t, 1), slopes_ref[kvh, g], f32) for g in range(GQA)], axis=0)
        s = _dot_nt(qs, k) * SCALE - slope * df
        s = jnp.where(ok, s, NEG_INF)
        e = jnp.exp(s - jnp.max(s, axis=-1, keepdims=True))
        p = jnp.where(ok, e / jnp.sum(e, axis=-1, keepdims=True), 0.0)
        o = _dot(p.astype(bf16), v)
        for g in range(GQA):
            o_ref[:, (kvh * GQA + g) * HEAD_DIM:(kvh * GQA + g + 1) * HEAD_DIM] = o[g * t:(g + 1) * t]


def _swa_sample(slopes, qz, win4, tail, nbatch, t, wbuf, pos0):
    return pl.pallas_call(
        functools.partial(_swa_sample_kernel, t=t, wbuf=wbuf, pos0=pos0),
        grid=(nbatch,),
        in_specs=[
            pl.BlockSpec(memory_space=pltpu.SMEM),
            pl.BlockSpec((t, D_ATT), lambda b: (b, 0)),
            pl.BlockSpec((wbuf * N_CH, HEAD_DIM), lambda b: (b, 0)),
            pl.BlockSpec((1, CMP_BLOCK, KV_BRANCH), lambda b: (b, 0, 0)),
        ],
        out_specs=pl.BlockSpec((t, D_ATT), lambda b: (b, 0)),
        out_shape=jax.ShapeDtypeStruct((nbatch * t, D_ATT), f32),
        compiler_params=_cparams(("parallel",)),
        name="swa_sample",
    )(slopes, qz, win4, tail)


def _round_up(a, b):
    return -(-a // b) * b


def _pad_blocks(kcv, ncp):
    return jnp.pad(kcv, ((0, 0), (0, 0), (0, ncp - kcv.shape[2]), (0, 0)))


def _expand_matrix(nblocks, nkeys):
    return (jnp.arange(nkeys)[None, :] // CMP_BLOCK == jnp.arange(nblocks)[:, None]).astype(bf16)


def _kv6(rows2d, nbatch, t):
    return rows2d.reshape(nbatch, t, 2, N_KV_A, HEAD_DIM)


def _layer_prompt(x, prm):
    nbatch, t, _ = x.shape
    m = nbatch * t
    x2d = x.reshape(m, D_MODEL)
    tm = min(1024, m)
    qz, qkvb, kv_slc, kv_swa, cmp4, slc4, swa4, small = _in_proj(
        x2d, prm["norm_w"], prm["w_main"], prm["w_small"], tm)
    nc = t // CMP_BLOCK
    kcv = _compress(cmp4, nbatch, prm["pe2d"], prm["w_cmp"], min(64, nc))
    kcv = _pad_blocks(kcv, _round_up(nc, LANES))
    tq = min(256, t)
    o_cmp, sel = _cmp_attend(prm["slopes"], qz, kcv, nbatch, t, tq, 0, min(N_SELECT, nc))
    tk = min(256, t)
    o_slc = _slc_prompt(prm["slopes"], qz, kv_slc, sel, _expand_matrix(sel.shape[-1], t), nbatch, t, tq, tk)
    o_swa = _swa_prompt(prm["slopes"], qz, kv_swa, nbatch, t, tq, tk)
    conv0 = jnp.zeros((nbatch, CONV_W - 1, CONV_DIM), f32)
    s0 = jnp.zeros((nbatch, N_HEADS_B, HEAD_DIM, HEAD_DIM), f32)
    o_b, s_new = _gdn(qkvb, small, conv0, s0, prm["conv_w"], prm["alog_row"], prm["dtb_row"],
                      prm["gdn_norm_w"], nbatch, t, 128)
    y = _merge_out(x2d, o_cmp, o_slc, o_swa, small, qz, o_b, prm["w_out"], prm["final_norm_w"], min(512, m))
    w = min(WINDOW, t)
    conv_new = qkvb.reshape(nbatch, t, CONV_DIM)[:, t - (CONV_W - 1):]
    return (y.reshape(nbatch, t, D_MODEL), _kv6(cmp4, nbatch, t), _kv6(slc4, nbatch, t),
            _kv6(swa4, nbatch, t)[:, t - w:], conv_new, s_new)


def _layer_sample(x, cache_cmp, cache_slc, cache_swa, conv_buf, s0, page_table, prm):
    nbatch, t, _ = x.shape
    m = nbatch * t
    n_pages = page_table.shape[1]
    past_len = n_pages * PAGE_SIZE
    wbuf = cache_swa.shape[1]
    x2d = x.reshape(m, D_MODEL)
    qz, qkvb, kv_slc, kv_swa, cmp4, slc4, swa4, small = _in_proj(
        x2d, prm["norm_w"], prm["w_main"], prm["w_small"], m)
    pad_tail = lambda r: jnp.pad(r.reshape(nbatch, t, KV_BRANCH), ((0, 0), (0, CMP_BLOCK - t), (0, 0)))
    pps = min(32, n_pages)
    kc_past = _compress_paged(page_table, cache_cmp.reshape(-1, HEAD_DIM), prm["pe2d"], prm["w_cmp"], pps)
    kc_tail = _compress(pad_tail(cmp4).reshape(nbatch * CMP_BLOCK * N_CH, HEAD_DIM), 1, prm["pe2d"], prm["w_cmp"], nbatch)
    kc_tail = jnp.swapaxes(kc_tail[0], 0, 1)[:, :, None, :]
    n_pb = past_len // CMP_BLOCK
    pps_s = min(64, n_pages)
    bps = pps_s * (PAGE_SIZE // CMP_BLOCK)
    ncp = _round_up(n_pb + 1, max(LANES, bps))
    kcv = _pad_blocks(jnp.concatenate([kc_past, kc_tail], axis=2), ncp)
    o_cmp, sel = _cmp_attend(prm["slopes"], qz, kcv, nbatch, t, t, past_len, min(N_SELECT, n_pb + 1))
    o_slc = _slc_paged(page_table, prm["slopes"], cache_slc.reshape(-1, HEAD_DIM), qz,
                       jnp.pad(kv_slc.reshape(nbatch, t, KV_BRANCH), ((0, 0), (0, LANES - t), (0, 0))), sel,
                       _expand_matrix(bps, pps_s * PAGE_SIZE), t, pps_s, past_len)
    o_swa = _swa_sample(prm["slopes"], qz, cache_swa.reshape(-1, HEAD_DIM), pad_tail(kv_swa), nbatch, t, wbuf, past_len)
    o_b, s_new = _gdn(qkvb, small, conv_buf, s0, prm["conv_w"], prm["alog_row"], prm["dtb_row"],
                      prm["gdn_norm_w"], nbatch, t, 128)
    y = _merge_out(x2d, o_cmp, o_slc, o_swa, small, qz, o_b, prm["w_out"], prm["final_norm_w"], min(512, m))
    kv_win = jnp.concatenate([cache_swa, _kv6(swa4, nbatch, t)], axis=1)[:, t:]
    conv_new = jnp.concatenate([conv_buf, qkvb.reshape(nbatch, t, CONV_DIM)], axis=1)[:, t:]
    return (y.reshape(nbatch, t, D_MODEL), _kv6(cmp4, nbatch, t), _kv6(slc4, nbatch, t), kv_win, conv_new, s_new)


def kernel(x_prompt, x_sample, cache_cmp, cache_slc, cache_swa, state_conv, state_gdn, page_table,
           norm_w, w_in, pe_cmp, w_cmp, conv_w, a_log, dt_bias, gdn_norm_w, w_out, final_norm_w):
    depth = norm_w.shape[0]
    assert depth == 1, "the final norm is fused into the single layer's output projection"
    l = 0
    w_main, w_small = _prep_w_in(w_in[l])
    head = jnp.arange(1, N_HEADS_A + 1, dtype=f32)
    lane_row = lambda vals: jnp.zeros((1, LANES), f32).at[0, A_OFF:A_OFF + N_HEADS_B].set(vals.astype(f32))
    prm = {
        "norm_w": norm_w[l], "w_main": w_main, "w_small": w_small,
        "pe2d": jnp.swapaxes(pe_cmp[l], 0, 1),
        "w_cmp": jnp.swapaxes(w_cmp[l], 0, 1).reshape(2, CMP_BLOCK // 2, 2 * HEAD_DIM, HEAD_DIM).astype(bf16),
        "conv_w": conv_w[l], "alog_row": lane_row(a_log[l]), "dtb_row": lane_row(dt_bias[l]),
        "gdn_norm_w": gdn_norm_w[l], "w_out": w_out[l].astype(bf16), "final_norm_w": final_norm_w,
        "slopes": jnp.exp2(-8.0 * head / N_HEADS_A).reshape(N_KV_A, GQA),
    }
    yp, p_cmp, p_slc, p_swa, p_conv, p_gdn = _layer_prompt(x_prompt, prm)
    ys, s_cmp, s_slc, s_swa, s_conv, s_gdn = _layer_sample(
        x_sample, cache_cmp[l], cache_slc[l], cache_swa[l], state_conv[l], state_gdn[l], page_table, prm)
    st = lambda a: a[None]
    return (yp, ys, st(p_cmp), st(p_slc), st(p_swa), st(p_conv), st(p_gdn),
            st(s_cmp), st(s_slc), st(s_swa), st(s_conv), st(s_gdn))
```

```python
import functools
import math

import jax
import jax.numpy as jnp
from jax import lax
from jax.experimental import pallas as pl
from jax.experimental.pallas import tpu as pltpu

f32 = jnp.float32
bf16 = jnp.bfloat16

D_MODEL = 2048
HEAD_DIM = 128
N_HEADS_A = 8
N_KV_A = 2
GQA = 4
D_ATT = N_HEADS_A * HEAD_DIM
CMP_BLOCK = 64
N_SELECT = 16
WINDOW = 512
N_HEADS_B = 8
D_GDN = N_HEADS_B * HEAD_DIM
CONV_W = 4
CONV_DIM = 3 * D_GDN
PAGE_SIZE = 128
NORM_EPS = 1e-6
NEG_INF = -1e30
FORCE_SCORE = 1e4
KV_BRANCH = 2 * N_KV_A * HEAD_DIM
N_CH = 2 * N_KV_A
ROWS_PER_PAGE4 = PAGE_SIZE * N_CH
SCALE = HEAD_DIM ** -0.5
G_OFF, A_OFF, B_OFF = 0, 3 * N_HEADS_A, 3 * N_HEADS_A + N_HEADS_B
LANES = 128
VMEM_LIMIT = 56 * 1024 * 1024
IN_PROJ_VMEM_LIMIT = 60 * 1024 * 1024

IN_TN = 512
N_QZ_TILES = 3 * D_ATT // IN_TN
N_B_TILES = CONV_DIM // IN_TN
N_IN_TILES = N_QZ_TILES + N_B_TILES + 3


def _cparams(sem):
    return pltpu.CompilerParams(dimension_semantics=sem, vmem_limit_bytes=VMEM_LIMIT)


def _dot(a, b):
    return jnp.dot(a, b, preferred_element_type=f32)


def _dot_nt(a, b):
    return lax.dot_general(a, b, (((1,), (1,)), ((), ())), preferred_element_type=f32)


def _dot_tn(a, b):
    return lax.dot_general(a, b, (((0,), (0,)), ((), ())), preferred_element_type=f32)


def _store_row4(o4_ref, acc):
    for ch in range(N_CH):
        o4_ref[pl.ds(ch, acc.shape[0], stride=N_CH), :] = acc[:, ch * HEAD_DIM:(ch + 1) * HEAD_DIM]


def _in_proj_kernel(x_ref, nw_ref, w_ref, ws_ref, qz_ref, b_ref, slc_ref, swa_ref, cmp4_ref, slc4_ref, swa4_ref,
                    small_ref, xn_ref):
    n = pl.program_id(1)

    @pl.when(n == 0)
    def _():
        x = x_ref[...]
        ms = jnp.mean(x * x, axis=-1, keepdims=True)
        xn = (x * lax.rsqrt(ms + NORM_EPS) * nw_ref[...]).astype(bf16)
        xn_ref[...] = xn
        small_ref[...] = _dot(xn, ws_ref[...])

    acc = _dot(xn_ref[...], w_ref[...])

    @pl.when(n < N_QZ_TILES)
    def _():
        qz_ref[...] = acc

    @pl.when((n >= N_QZ_TILES) & (n < N_QZ_TILES + N_B_TILES))
    def _():
        b_ref[...] = acc

    @pl.when(n == N_QZ_TILES + N_B_TILES)
    def _():
        _store_row4(cmp4_ref, acc)

    @pl.when(n == N_QZ_TILES + N_B_TILES + 1)
    def _():
        slc_ref[...] = acc.astype(bf16)
        _store_row4(slc4_ref, acc)

    @pl.when(n == N_QZ_TILES + N_B_TILES + 2)
    def _():
        swa_ref[...] = acc.astype(bf16)
        _store_row4(swa4_ref, acc)


def _in_proj(x2d, norm_w, w_main, w_small, tm):
    m = x2d.shape[0]
    S = jax.ShapeDtypeStruct
    nb0 = N_QZ_TILES
    return pl.pallas_call(
        _in_proj_kernel,
        grid=(m // tm, N_IN_TILES),
        in_specs=[
            pl.BlockSpec((tm, D_MODEL), lambda i, n: (i, 0)),
            pl.BlockSpec((1, D_MODEL), lambda i, n: (0, 0)),
            pl.BlockSpec((D_MODEL, IN_TN), lambda i, n: (0, n)),
            pl.BlockSpec((D_MODEL, LANES), lambda i, n: (0, 0)),
        ],
        out_specs=[
            pl.BlockSpec((tm, IN_TN), lambda i, n: (i, jnp.minimum(n, nb0 - 1))),
            pl.BlockSpec((tm, IN_TN), lambda i, n: (i, jnp.clip(n - nb0, 0, N_B_TILES - 1))),
            pl.BlockSpec((tm, KV_BRANCH), lambda i, n: (i, 0)),
            pl.BlockSpec((tm, KV_BRANCH), lambda i, n: (i, 0)),
            pl.BlockSpec((tm * N_CH, HEAD_DIM), lambda i, n: (i, 0)),
            pl.BlockSpec((tm * N_CH, HEAD_DIM), lambda i, n: (i, 0)),
            pl.BlockSpec((tm * N_CH, HEAD_DIM), lambda i, n: (i, 0)),
            pl.BlockSpec((tm, LANES), lambda i, n: (i, 0)),
        ],
        out_shape=[S((m, 3 * D_ATT), f32), S((m, CONV_DIM), f32), S((m, KV_BRANCH), bf16), S((m, KV_BRANCH), bf16),
                   S((m * N_CH, HEAD_DIM), f32), S((m * N_CH, HEAD_DIM), f32), S((m * N_CH, HEAD_DIM), f32),
                   S((m, LANES), f32)],
        scratch_shapes=[pltpu.VMEM((tm, D_MODEL), bf16)],
        compiler_params=pltpu.CompilerParams(dimension_semantics=("parallel", "arbitrary"),
                                             vmem_limit_bytes=IN_PROJ_VMEM_LIMIT),
        name="in_proj",
    )(x2d, norm_w.reshape(1, D_MODEL), w_main, w_small)


def _prep_w_in(w_in):
    o = [0]
    for s in (D_ATT, 3 * KV_BRANCH, 3 * N_HEADS_A, D_ATT, CONV_DIM, N_HEADS_B, N_HEADS_B, D_GDN):
        o.append(o[-1] + s)
    q_a, kv_a, g_a, z_a, qkv_b, a_b, b_b, z_b = (w_in[:, o[i]:o[i + 1]] for i in range(8))
    w_main = jnp.concatenate([q_a, z_a, z_b, qkv_b, kv_a], axis=1).astype(bf16)
    pad = jnp.zeros((w_in.shape[0], LANES - (3 * N_HEADS_A + 2 * N_HEADS_B)), w_in.dtype)
    w_small = jnp.concatenate([g_a, a_b, b_b, pad], axis=1).astype(bf16)
    return w_main, w_small


CMP_UNROLL = 16


def _compress_accumulate(load_rows, pe_ref, w_ref, nbk):
    def body(i, accs):
        new = []
        for c in range(2):
            halves = []
            for dl in range(2):
                l = 2 * i + dl
                pe_row = pe_ref[c, pl.ds(l, 1), :]
                halves.append(jnp.concatenate(
                    [load_rows(l, c * N_KV_A + h) + pe_row for h in range(N_KV_A)], axis=0))
            lhs = jnp.concatenate(halves, axis=1).astype(bf16)
            new.append(accs[c] + _dot(lhs, w_ref[c, i]))
        return tuple(new)

    z = jnp.zeros((N_KV_A * nbk, HEAD_DIM), f32)
    return lax.fori_loop(0, CMP_BLOCK // 2, body, (z, z), unroll=CMP_UNROLL)


def _compress_kernel(x_ref, pe_ref, w_ref, o_ref, *, nbk):
    accs = _compress_accumulate(
        lambda l, ch: x_ref[pl.ds(l * N_CH + ch, nbk, stride=CMP_BLOCK * N_CH), :], pe_ref, w_ref, nbk)
    for c in range(2):
        for h in range(N_KV_A):
            o_ref[0, c * N_KV_A + h] = accs[c][h * nbk:(h + 1) * nbk]


def _compress(rows4, nbatch, pe2d, w_cmp_bf, nbk):
    nblk = rows4.shape[0] // (nbatch * CMP_BLOCK * N_CH)
    nj = nblk // nbk
    return pl.pallas_call(
        functools.partial(_compress_kernel, nbk=nbk),
        grid=(nbatch, nj),
        in_specs=[
            pl.BlockSpec((nbk * CMP_BLOCK * N_CH, HEAD_DIM), lambda b, j: (b * nj + j, 0)),
            pl.BlockSpec((2, CMP_BLOCK, HEAD_DIM), lambda b, j: (0, 0, 0)),
            pl.BlockSpec((2, CMP_BLOCK // 2, 2 * HEAD_DIM, HEAD_DIM), lambda b, j: (0, 0, 0, 0)),
        ],
        out_specs=pl.BlockSpec((1, N_CH, nbk, HEAD_DIM), lambda b, j: (b, 0, j, 0)),
        out_shape=jax.ShapeDtypeStruct((nbatch, N_CH, nblk, HEAD_DIM), f32),
        compiler_params=_cparams(("parallel", "parallel")),
        name="compress",
    )(rows4, pe2d, w_cmp_bf)


def _topk_mask(sc, nsel):
    ncp = sc.shape[0]
    jblk = lax.broadcasted_iota(jnp.int32, (ncp, 1), 0)
    work = sc
    for _ in range(nsel):
        mx = jnp.max(work, axis=0, keepdims=True)
        first = jnp.min(jnp.where(work == mx, jblk, ncp), axis=0, keepdims=True)
        work = jnp.where(jblk == first, -2.0, work)
    return jnp.where((work == -2.0) & (sc >= 0.0), 1.0, 0.0)


def _topk_kernel(sc_ref, sel_ref, *, nsel):
    sel_ref[0] = _topk_mask(sc_ref[0], nsel)


def _topk(scores, nsel):
    ng, ncp, nq = scores.shape
    return pl.pallas_call(
        functools.partial(_topk_kernel, nsel=nsel),
        grid=(ng, nq // LANES),
        in_specs=[pl.BlockSpec((1, ncp, LANES), lambda g, c: (g, 0, c))],
        out_specs=pl.BlockSpec((1, ncp, LANES), lambda g, c: (g, 0, c)),
        out_shape=jax.ShapeDtypeStruct(scores.shape, f32),
        compiler_params=_cparams(("parallel", "parallel")),
        name="topk",
    )(scores)


def _cmp_kernel(slopes_ref, q_ref, kc_ref, vc_ref, o_ref, sel_ref, *, tq, tqp, ncp, pos0, nsel):
    kvh = pl.program_id(1)
    qt = pl.program_id(2)
    q = q_ref[...]
    if tqp > tq:
        q = jnp.concatenate([q, jnp.zeros((tqp - tq, q.shape[1]), f32)], axis=0)
    kc = kc_ref[0, 0].astype(bf16)
    vc = vc_ref[0, 0].astype(bf16)
    qpos = pos0 + qt * tq + lax.broadcasted_iota(jnp.int32, (1, tqp), 1)
    jblk = lax.broadcasted_iota(jnp.int32, (ncp, 1), 0)
    d = qpos - ((jblk + 1) * CMP_BLOCK - 1)
    ok = d >= 0
    df = d.astype(f32)
    imp = jnp.zeros((ncp, tqp), f32)
    for g in range(GQA):
        qg = q[:, g * HEAD_DIM:(g + 1) * HEAD_DIM].astype(bf16)
        s = _dot_nt(kc, qg) * SCALE
        s = jnp.where(ok, s - slopes_ref[kvh, g] * df, NEG_INF)
        mx = jnp.max(s, axis=0, keepdims=True)
        e = jnp.exp(s - mx)
        p = jnp.where(ok, e / jnp.sum(e, axis=0, keepdims=True), 0.0)
        imp = imp + p
        og = _dot_tn(p.astype(bf16), vc)
        o_ref[:, g * HEAD_DIM:(g + 1) * HEAD_DIM] = og[:tq]
    cur = qpos // CMP_BLOCK
    forced = (jblk == cur) | (jblk == 0)
    score = jnp.where(forced, FORCE_SCORE, jnp.where(jblk <= cur, imp, -1.0))
    if nsel is None:
        sel_ref[0, 0] = score
        return
    for cg in range(tqp // LANES):
        sel = _topk_mask(score[:, cg * LANES:(cg + 1) * LANES], nsel)
        lo = cg * LANES
        hi = min(tq, lo + LANES)
        sel_ref[0, 0, lo:hi, :] = sel.T[:hi - lo]


def _cmp_attend(slopes, qz, kcv, nbatch, t, tq, pos0, nsel):
    ncp = kcv.shape[2]
    tqp = max(tq, LANES)
    nqt = t // tq
    gw = GQA * HEAD_DIM
    packed = tq < LANES and (nbatch * t) % LANES == 0
    sel_spec = (pl.BlockSpec((1, 1, ncp, tqp), lambda b, h, i: (b, h, 0, i)) if packed
                else pl.BlockSpec((1, 1, tq, ncp), lambda b, h, i: (b, h, i, 0)))
    sel_shape = (nbatch, N_KV_A, ncp, nqt * tqp) if packed else (nbatch, N_KV_A, t, ncp)
    o_cmp, sel = pl.pallas_call(
        functools.partial(_cmp_kernel, tq=tq, tqp=tqp, ncp=ncp, pos0=pos0, nsel=None if packed else nsel),
        grid=(nbatch, N_KV_A, nqt),
        in_specs=[
            pl.BlockSpec(memory_space=pltpu.SMEM),
            pl.BlockSpec((tq, gw), lambda b, h, i: (b * nqt + i, h)),
            pl.BlockSpec((1, 1, ncp, HEAD_DIM), lambda b, h, i: (b, h, 0, 0)),
            pl.BlockSpec((1, 1, ncp, HEAD_DIM), lambda b, h, i: (b, N_KV_A + h, 0, 0)),
        ],
        out_specs=[pl.BlockSpec((tq, gw), lambda b, h, i: (b * nqt + i, h)), sel_spec],
        out_shape=[jax.ShapeDtypeStruct((nbatch * t, D_ATT), f32), jax.ShapeDtypeStruct(sel_shape, f32)],
        compiler_params=_cparams(("parallel", "parallel", "parallel")),
        name="cmp_attend",
    )(slopes, qz, kcv, kcv)
    if packed:
        scores = sel.reshape(nbatch, N_KV_A, ncp, nqt, tqp)[..., :tq].reshape(nbatch, N_KV_A, ncp, t)
        scores = jnp.transpose(scores, (1, 2, 0, 3)).reshape(N_KV_A, ncp, nbatch * t)
        mask = _topk(scores, nsel).reshape(N_KV_A, ncp, nbatch, t)
        sel = jnp.transpose(mask, (2, 0, 3, 1))
    return o_cmp, sel


def _lane_repeat(x, n):
    return x if n == 1 else jnp.concatenate([x] * n, axis=1)


def _flash_init(m_ref, l_ref, acc_ref):
    m_ref[...] = jnp.full(m_ref.shape, NEG_INF, f32)
    l_ref[...] = jnp.zeros(l_ref.shape, f32)
    acc_ref[...] = jnp.zeros(acc_ref.shape, f32)


LOG2E = 1.4426950408889634
MASK_DIST = 1e33
SLC_ROWS = 128


def _attn_prompt_body(kvh, qt, slopes_ref, q_ref, k_ref, v_ref, o_ref, qb_ref, dm_ref, s_ref, mn_ref,
                      m_ref, l_ref, acc_ref, *, tq, tk, kt_lo, kt_hi, tile_dist, active):
    _flash_init(m_ref, l_ref, acc_ref)
    for g in range(GQA):
        qb_ref[g * tq:(g + 1) * tq, :] = (q_ref[:, g * HEAD_DIM:(g + 1) * HEAD_DIM] * (SCALE * LOG2E)).astype(bf16)
    nrep = tk // LANES
    ngrp = tq // SLC_ROWS

    def group_update(groups, k, v):
        units = [(g, r) for r in groups for g in range(GQA)]
        for g, r in units:
            rows = pl.ds(g * tq + r * SLC_ROWS, SLC_ROWS)
            t2 = (_dot_nt(qb_ref[rows, :], k)
                  - (slopes_ref[kvh, g] * LOG2E) * dm_ref[r * SLC_ROWS:(r + 1) * SLC_ROWS, :])
            s_ref[rows, :] = t2
            mx = jnp.max(t2, axis=-1, keepdims=True)
            mn_ref[rows, :] = jnp.maximum(m_ref[rows, :], jnp.broadcast_to(mx, (SLC_ROWS, LANES)))
        for g, r in units:
            rows = pl.ds(g * tq + r * SLC_ROWS, SLC_ROWS)
            m_new = mn_ref[rows, :]
            p = jnp.exp2(s_ref[rows, :] - _lane_repeat(m_new, nrep))
            alpha = jnp.exp2(m_ref[rows, :] - m_new)
            psum = p[:, 0:LANES]
            for c in range(1, nrep):
                psum = psum + p[:, c * LANES:(c + 1) * LANES]
            l_ref[rows, :] = alpha * l_ref[rows, :] + psum
            acc_ref[rows, :] = alpha * acc_ref[rows, :] + _dot(p.astype(bf16), v)
            m_ref[rows, :] = m_new

    def tile_update(kt, acts):
        ks = pl.multiple_of(kt * tk, tk)
        k = k_ref[pl.ds(ks, tk), :]
        v = v_ref[pl.ds(ks, tk), :]
        dm_ref[...] = tile_dist(ks)
        if acts is None:
            group_update(tuple(range(ngrp)), k, v)
        elif ngrp == 2:
            pl.when(acts[0] & acts[1])(functools.partial(group_update, (0, 1), k, v))
            pl.when(acts[0] & jnp.logical_not(acts[1]))(functools.partial(group_update, (0,), k, v))
            pl.when(acts[1] & jnp.logical_not(acts[0]))(functools.partial(group_update, (1,), k, v))
        else:
            for r in range(ngrp):
                pl.when(acts[r])(functools.partial(group_update, (r,), k, v))

    def kt_body(kt, carry):
        if active is None:
            tile_update(kt, None)
        else:
            acts = [active(kt, r) for r in range(ngrp)]
            any_act = acts[0]
            for a in acts[1:]:
                any_act = any_act | a
            pl.when(any_act)(functools.partial(tile_update, kt, acts))
        return carry

    lax.fori_loop(kt_lo, kt_hi, kt_body, 0)
    for g in range(GQA):
        rows = pl.ds(g * tq, tq)
        o_ref[:, g * HEAD_DIM:(g + 1) * HEAD_DIM] = (
            acc_ref[rows, :] / jnp.sum(l_ref[rows, :], axis=-1, keepdims=True))


def _slc_kernel(flags_ref, slopes_ref, q_ref, k_ref, v_ref, sel_ref, e_ref, o_ref, *scratch, tq, tk, t):
    b = pl.program_id(0)
    kvh = pl.program_id(1)
    qt = pl.program_id(2)
    selb = sel_ref[0, 0].astype(bf16)
    qpos = qt * tq + lax.broadcasted_iota(jnp.int32, (tq, 1), 0)
    kofs = lax.broadcasted_iota(jnp.int32, (1, tk), 1)
    ngrp = tq // SLC_ROWS
    nkt = t // tk

    def tile_dist(ks):
        maskf = _dot(selb, e_ref[:, pl.ds(ks, tk)])
        d = qpos - (ks + kofs)
        return jnp.where((maskf > 0.5) & (d >= 0), d.astype(f32), MASK_DIST)

    def active(kt, r):
        row_group = (b * N_KV_A + kvh) * (t // SLC_ROWS) + qt * ngrp + r
        return flags_ref[row_group * nkt + kt] > 0

    _attn_prompt_body(kvh, qt, slopes_ref, q_ref, k_ref, v_ref, o_ref, *scratch, tq=tq, tk=tk,
                      kt_lo=0, kt_hi=(qt * tq + tq - 1) // tk + 1, tile_dist=tile_dist, active=active)


def _attn_scratch(tq, tk):
    return [pltpu.VMEM((GQA * tq, HEAD_DIM), bf16), pltpu.VMEM((tq, tk), f32),
            pltpu.VMEM((GQA * tq, tk), f32), pltpu.VMEM((GQA * tq, LANES), f32),
            pltpu.VMEM((GQA * tq, LANES), f32), pltpu.VMEM((GQA * tq, LANES), f32),
            pltpu.VMEM((GQA * tq, HEAD_DIM), f32)]


def _slc_prompt(slopes, qz, kv_slc, sel, expand, nbatch, t, tq, tk):
    nqt = t // tq
    gw = GQA * HEAD_DIM
    ncp = sel.shape[-1]
    bpt = tk // CMP_BLOCK
    flags = sel.reshape(nbatch, N_KV_A, t // SLC_ROWS, SLC_ROWS, ncp // bpt, bpt).max(axis=(3, 5))
    flags = (flags[..., :t // tk] > 0.5).astype(jnp.int32).reshape(-1)
    gs = pltpu.PrefetchScalarGridSpec(
        num_scalar_prefetch=1,
        grid=(nbatch, N_KV_A, nqt),
        in_specs=[
            pl.BlockSpec(memory_space=pltpu.SMEM),
            pl.BlockSpec((tq, gw), lambda b, h, i, fl: (b * nqt + i, h)),
            pl.BlockSpec((t, HEAD_DIM), lambda b, h, i, fl: (b, h)),
            pl.BlockSpec((t, HEAD_DIM), lambda b, h, i, fl: (b, N_KV_A + h)),
            pl.BlockSpec((1, 1, tq, ncp), lambda b, h, i, fl: (b, h, i, 0)),
            pl.BlockSpec((ncp, t), lambda b, h, i, fl: (0, 0)),
        ],
        out_specs=pl.BlockSpec((tq, gw), lambda b, h, i, fl: (b * nqt + i, h)),
        scratch_shapes=_attn_scratch(tq, tk),
    )
    return pl.pallas_call(
        functools.partial(_slc_kernel, tq=tq, tk=tk, t=t),
        grid_spec=gs,
        out_shape=jax.ShapeDtypeStruct((nbatch * t, D_ATT), f32),
        compiler_params=_cparams(("parallel", "parallel", "parallel")),
        name="slc_prompt",
    )(flags, slopes, qz, kv_slc, kv_slc, sel, expand)


def _swa_kernel(slopes_ref, q_ref, k_ref, v_ref, o_ref, *scratch, tq, tk):
    kvh = pl.program_id(1)
    qt = pl.program_id(2)
    qpos = qt * tq + lax.broadcasted_iota(jnp.int32, (tq, 1), 0)
    kofs = lax.broadcasted_iota(jnp.int32, (1, tk), 1)

    def tile_dist(ks):
        d = qpos - (ks + kofs)
        return jnp.where((d >= 0) & (d < WINDOW), d.astype(f32), MASK_DIST)

    _attn_prompt_body(kvh, qt, slopes_ref, q_ref, k_ref, v_ref, o_ref, *scratch, tq=tq, tk=tk,
                      kt_lo=jnp.maximum(qt * tq - (WINDOW - 1), 0) // tk, kt_hi=(qt * tq + tq - 1) // tk + 1,
                      tile_dist=tile_dist, active=None)


def _swa_prompt(slopes, qz, kv_swa, nbatch, t, tq, tk):
    nqt = t // tq
    gw = GQA * HEAD_DIM
    return pl.pallas_call(
        functools.partial(_swa_kernel, tq=tq, tk=tk),
        grid=(nbatch, N_KV_A, nqt),
        in_specs=[
            pl.BlockSpec(memory_space=pltpu.SMEM),
            pl.BlockSpec((tq, gw), lambda b, h, i: (b * nqt + i, h)),
            pl.BlockSpec((t, HEAD_DIM), lambda b, h, i: (b, h)),
            pl.BlockSpec((t, HEAD_DIM), lambda b, h, i: (b, N_KV_A + h)),
        ],
        out_specs=pl.BlockSpec((tq, gw), lambda b, h, i: (b * nqt + i, h)),
        out_shape=jax.ShapeDtypeStruct((nbatch * t, D_ATT), f32),
        scratch_shapes=_attn_scratch(tq, tk),
        compiler_params=_cparams(("parallel", "parallel", "parallel")),
        name="swa_prompt",
    )(slopes, qz, kv_swa, kv_swa)


def _softplus(x):
    return jnp.maximum(x, 0.0) + jnp.log1p(jnp.exp(-jnp.abs(x)))


def _sigmoid(x):
    return 1.0 / (1.0 + jnp.exp(-x))


def _silu(x):
    return x * _sigmoid(x)


def _dot_hi(a, b):
    return jnp.dot(a, b, precision=lax.Precision.HIGHEST, preferred_element_type=f32)


INV_BASE = 16


def _bdot(a, b):
    return _dot(a.astype(bf16), b.astype(bf16))


def _gdn_kernel(x_ref, small_ref, conv0_ref, s0_ref, cw_ref, alog_ref, dtb_ref, nw_ref,
                o_ref, s_out_ref, xp_ref, st_ref, gam_ref, n_ref, t_ref, pw_ref, uw_ref,
                k_ref, kb_ref, q_ref, qg_ref, kg_ref, qk_ref, vn_ref, rhs_ref, *, c, tv, halo):
    ci = pl.program_id(1)
    nci = pl.num_programs(1)

    @pl.when(ci == 0)
    def _():
        xp_ref[...] = jnp.zeros(xp_ref.shape, f32)
        xp_ref[halo - (CONV_W - 1):halo, :] = conv0_ref[0]
        st_ref[...] = s0_ref[0]

    xp_ref[halo:halo + tv, :] = x_ref[...]
    y = xp_ref[pl.ds(halo, c), :] * cw_ref[CONV_W - 1:CONV_W, :]
    for i in range(CONV_W - 2, -1, -1):
        y = y + xp_ref[pl.ds(halo - (CONV_W - 1) + i, c), :] * cw_ref[i:i + 1, :]
    y = _silu(y)
    xp_ref[halo - (CONV_W - 1):halo, :] = xp_ref[halo + tv - (CONV_W - 1):halo + tv, :]

    ri = lax.broadcasted_iota(jnp.int32, (c, 1), 0)
    rowi = lax.broadcasted_iota(jnp.int32, (c, c), 0)
    coli = lax.broadcasted_iota(jnp.int32, (c, c), 1)
    lower = rowi >= coli
    strict = rowi > coli
    tril = lower.astype(f32)
    eye = (rowi == coli).astype(f32)
    same_blk = {}
    s = INV_BASE
    while s <= c:
        same_blk[s] = (rowi // s) == (coli // s)
        s *= 2

    sm = small_ref[...]
    if tv < c:
        sm = jnp.concatenate([sm, jnp.zeros((c - tv, LANES), f32)], axis=0)
    g_all = -jnp.exp(alog_ref[...]) * _softplus(sm + dtb_ref[...])
    beta_all = _sigmoid(sm)
    if tv < c:
        live = ri < tv
        g_all = jnp.where(live, g_all, 0.0)
        beta_all = jnp.where(live, beta_all, 0.0)
    gc_all = _dot_hi(tril, g_all)
    gc_rows = gc_all.T
    heads = range(N_HEADS_B)
    decay = []

    for h in heads:
        yq = y[:, h * HEAD_DIM:(h + 1) * HEAD_DIM]
        yk = y[:, D_GDN + h * HEAD_DIM:D_GDN + (h + 1) * HEAD_DIM]
        v = y[:, 2 * D_GDN + h * HEAD_DIM:2 * D_GDN + (h + 1) * HEAD_DIM]
        q = yq * lax.rsqrt(jnp.sum(yq * yq, axis=-1, keepdims=True) + NORM_EPS) * SCALE
        k = yk * lax.rsqrt(jnp.sum(yk * yk, axis=-1, keepdims=True) + NORM_EPS)
        if tv < c:
            q = jnp.where(live, q, 0.0)
            k = jnp.where(live, k, 0.0)
            v = jnp.where(live, v, 0.0)
        beta = beta_all[:, B_OFF + h:B_OFF + h + 1]
        gcol = gc_all[:, A_OFF + h:A_OFF + h + 1]
        grow = gc_rows[A_OFF + h:A_OFF + h + 1, :]
        gam_ref[h] = jnp.where(lower, jnp.exp(jnp.where(lower, gcol - grow, 0.0)), 0.0)
        eg = jnp.exp(gcol)
        g_last = gc_all[c - 1:c, A_OFF + h:A_OFF + h + 1]
        decay.append(jnp.exp(g_last))
        kb = k * beta
        k_ref[h] = k.astype(bf16)
        kb_ref[h] = kb.astype(bf16)
        q_ref[h] = q.astype(bf16)
        qg_ref[h] = (q * eg).astype(bf16)
        kg_ref[h] = (k * jnp.exp(g_last - gcol)).astype(bf16)
        rhs_ref[h, :, 0:HEAD_DIM] = (v * beta).astype(bf16)
        rhs_ref[h, :, HEAD_DIM:2 * HEAD_DIM] = (kb * eg).astype(bf16)

    for h in heads:
        gam = gam_ref[h]
        nmat = jnp.where(strict, _dot_nt(kb_ref[h], k_ref[h]) * gam, 0.0)
        n_ref[h] = nmat
        nd = jnp.where(same_blk[INV_BASE], nmat, 0.0)
        t_ref[h] = eye - nd
        pw_ref[h] = nd.astype(bf16)
        qk_ref[h] = (_dot_nt(q_ref[h], k_ref[h]) * gam).astype(bf16)

    for _ in range(int(math.log2(INV_BASE)) - 1):
        for h in heads:
            pw = pw_ref[h]
            pw_ref[h] = _dot(pw, pw).astype(bf16)
        for h in heads:
            t = t_ref[h]
            t_ref[h] = t + _dot(t.astype(bf16), pw_ref[h])

    s = INV_BASE if tv > INV_BASE else c
    while s < c:
        pair = same_blk[2 * s] & jnp.logical_not(same_blk[s])
        for h in heads:
            off = jnp.where(pair, n_ref[h], 0.0)
            uw_ref[h, :, 0:HEAD_DIM] = _bdot(t_ref[h], off)
        for h in heads:
            t = t_ref[h]
            t_ref[h] = t - _bdot(uw_ref[h, :, 0:HEAD_DIM], t)
        s *= 2

    for h in heads:
        uw_ref[h] = _dot(t_ref[h].astype(bf16), rhs_ref[h])

    for h in heads:
        sb = st_ref[h].astype(bf16)
        v_new = uw_ref[h, :, 0:HEAD_DIM] - _dot(uw_ref[h, :, HEAD_DIM:2 * HEAD_DIM].astype(bf16), sb)
        vn_ref[h] = v_new.astype(bf16)
        uw_ref[h, :, 0:HEAD_DIM] = _dot(qg_ref[h], sb)
    for h in heads:
        vn = vn_ref[h]
        o = uw_ref[h, :, 0:HEAD_DIM] + _dot(qk_ref[h], vn)
        st_ref[h] = st_ref[h] * decay[h] + _dot_tn(kg_ref[h], vn)
        on = o * lax.rsqrt(jnp.mean(o * o, axis=-1, keepdims=True) + NORM_EPS) * nw_ref[...]
        o_ref[:, h * HEAD_DIM:(h + 1) * HEAD_DIM] = on[:tv]

    @pl.when(ci == nci - 1)
    def _():
        s_out_ref[0] = st_ref[...]


def _gdn(qkv_b, small, conv0, s0, conv_w, alog_row, dtb_row, gdn_norm_w, nbatch, t, c):
    tv = min(c, t)
    nci = t // tv
    halo = 8
    return pl.pallas_call(
        functools.partial(_gdn_kernel, c=c, tv=tv, halo=halo),
        grid=(nbatch, nci),
        in_specs=[
            pl.BlockSpec((tv, CONV_DIM), lambda b, i: (b * nci + i, 0)),
            pl.BlockSpec((tv, LANES), lambda b, i: (b * nci + i, 0)),
            pl.BlockSpec((1, CONV_W - 1, CONV_DIM), lambda b, i: (b, 0, 0)),
            pl.BlockSpec((1, N_HEADS_B, HEAD_DIM, HEAD_DIM), lambda b, i: (b, 0, 0, 0)),
            pl.BlockSpec((CONV_W, CONV_DIM), lambda b, i: (0, 0)),
            pl.BlockSpec((1, LANES), lambda b, i: (0, 0)),
            pl.BlockSpec((1, LANES), lambda b, i: (0, 0)),
            pl.BlockSpec((1, HEAD_DIM), lambda b, i: (0, 0)),
        ],
        out_specs=[
            pl.BlockSpec((tv, D_GDN), lambda b, i: (b * nci + i, 0)),
            pl.BlockSpec((1, N_HEADS_B, HEAD_DIM, HEAD_DIM), lambda b, i: (b, 0, 0, 0)),
        ],
        out_shape=[jax.ShapeDtypeStruct((nbatch * t, D_GDN), f32),
                   jax.ShapeDtypeStruct((nbatch, N_HEADS_B, HEAD_DIM, HEAD_DIM), f32)],
        scratch_shapes=[pltpu.VMEM((halo + c, CONV_DIM), f32),
                        pltpu.VMEM((N_HEADS_B, HEAD_DIM, HEAD_DIM), f32)]
        + [pltpu.VMEM((N_HEADS_B, c, c), f32)] * 3
        + [pltpu.VMEM((N_HEADS_B, c, c), bf16)]
        + [pltpu.VMEM((N_HEADS_B, c, 2 * HEAD_DIM), f32)]
        + [pltpu.VMEM((N_HEADS_B, c, HEAD_DIM), bf16)] * 5
        + [pltpu.VMEM((N_HEADS_B, c, c), bf16)]
        + [pltpu.VMEM((N_HEADS_B, c, HEAD_DIM), bf16)]
        + [pltpu.VMEM((N_HEADS_B, c, 2 * HEAD_DIM), bf16)],
        compiler_params=_cparams(("parallel", "arbitrary")),
        name="gdn",
    )(qkv_b, small, conv0, s0, conv_w, alog_row, dtb_row, gdn_norm_w.reshape(1, HEAD_DIM))


MERGE_ROWS = 256


def _merge_kernel(x_ref, oc_ref, os_ref, ow_ref, small_ref, za_ref, zb_ref, ob_ref, w_ref, fw_ref, y_ref, mix_ref):
    tm = x_ref.shape[0]
    sub = min(MERGE_ROWS, tm)
    for r in range(tm // sub):
        rs = slice(r * sub, (r + 1) * sub)
        gates = _sigmoid(small_ref[rs, :])
        for h in range(N_HEADS_A):
            cs = slice(h * HEAD_DIM, (h + 1) * HEAD_DIM)
            o_a = (gates[:, G_OFF + h:G_OFF + h + 1] * oc_ref[rs, cs]
                   + gates[:, G_OFF + N_HEADS_A + h:G_OFF + N_HEADS_A + h + 1] * os_ref[rs, cs]
                   + gates[:, G_OFF + 2 * N_HEADS_A + h:G_OFF + 2 * N_HEADS_A + h + 1] * ow_ref[rs, cs])
            mix_ref[rs, cs] = (o_a * _silu(za_ref[rs, cs])).astype(bf16)
        mix_ref[rs, D_ATT:] = (ob_ref[rs, :] * _silu(zb_ref[rs, :])).astype(bf16)
        hres = x_ref[rs, :] + _dot(mix_ref[rs, :], w_ref[...])
        ms = jnp.mean(hres * hres, axis=-1, keepdims=True)
        y_ref[rs, :] = hres * lax.rsqrt(ms + NORM_EPS) * fw_ref[...]


def _merge_out(x2d, o_cmp, o_slc, o_swa, small, qz, o_b, w_out_bf, final_norm_w, tm):
    m = x2d.shape[0]
    row = lambda i: (i, 0)
    return pl.pallas_call(
        _merge_kernel,
        grid=(m // tm,),
        in_specs=[
            pl.BlockSpec((tm, D_MODEL), row),
            pl.BlockSpec((tm, D_ATT), row),
            pl.BlockSpec((tm, D_ATT), row),
            pl.BlockSpec((tm, D_ATT), row),
            pl.BlockSpec((tm, LANES), row),
            pl.BlockSpec((tm, D_ATT), lambda i: (i, 1)),
            pl.BlockSpec((tm, D_GDN), lambda i: (i, 2)),
            pl.BlockSpec((tm, D_GDN), row),
            pl.BlockSpec((D_ATT + D_GDN, D_MODEL), lambda i: (0, 0), pipeline_mode=pl.Buffered(1)),
            pl.BlockSpec((1, D_MODEL), lambda i: (0, 0)),
        ],
        out_specs=pl.BlockSpec((tm, D_MODEL), row),
        out_shape=jax.ShapeDtypeStruct((m, D_MODEL), f32),
        scratch_shapes=[pltpu.VMEM((tm, D_ATT + D_GDN), bf16)],
        compiler_params=_cparams(("parallel",)),
        name="merge_out",
    )(x2d, o_cmp, o_slc, o_swa, small, qz, qz, o_b, w_out_bf, final_norm_w.reshape(1, D_MODEL))


BLOCK_ROWS4 = CMP_BLOCK * N_CH


def _page_copies(cache_ref, buf_ref, sem_ref, pt_ref, b, j, slot, i, pps, by_row):
    page = pt_ref[b, j * pps + i]
    if not by_row:
        return [pltpu.make_async_copy(
            cache_ref.at[pl.ds(page * ROWS_PER_PAGE4, ROWS_PER_PAGE4), :],
            buf_ref.at[slot, pl.ds(i * ROWS_PER_PAGE4, ROWS_PER_PAGE4), :],
            sem_ref.at[slot])]
    bpp = PAGE_SIZE // CMP_BLOCK
    return [pltpu.make_async_copy(
        cache_ref.at[pl.ds(page * ROWS_PER_PAGE4 + n * BLOCK_ROWS4, BLOCK_ROWS4), :],
        buf_ref.at[slot, :, i * bpp + n, :],
        sem_ref.at[slot]) for n in range(bpp)]


def _paged_fetch(cache_ref, buf_ref, sem_ref, pt_ref, pps, nj_pages, need=None, by_row=False):
    b = pl.program_id(0)
    j = pl.program_id(1)
    nb = pl.num_programs(0)
    nj = pl.num_programs(1)
    step = b * nj_pages + jnp.minimum(j, nj_pages - 1)
    slot = step % 2

    def guarded(bb, jj, i, fn):
        if need is None:
            fn()
        else:
            pl.when(need(bb, jj, i))(fn)

    def run(what, bb, jj, sl, i):
        for cp in _page_copies(cache_ref, buf_ref, sem_ref, pt_ref, bb, jj, sl, i, pps, by_row):
            getattr(cp, what)()

    def start(bb, jj, sl):
        for i in range(pps):
            guarded(bb, jj, i, functools.partial(run, "start", bb, jj, sl, i))

    @pl.when((b == 0) & (j == 0))
    def _():
        start(0, 0, 0)

    @pl.when(j < nj_pages)
    def _():
        last_j = j == nj_pages - 1
        nb_ = jnp.where(last_j, b + 1, b)
        nj_ = jnp.where(last_j, 0, j + 1)

        @pl.when(nb_ < nb)
        def _():
            start(nb_, nj_, 1 - slot)

        for i in range(pps):
            guarded(b, j, i, functools.partial(run, "wait", b, j, slot, i))

    return slot


def _compress_paged_kernel(pt_ref, cache_ref, pe_ref, w_ref, o_ref, buf_ref, sem_ref, *, pps):
    slot = _paged_fetch(cache_ref, buf_ref, sem_ref, pt_ref, pps, pl.num_programs(1), by_row=True)
    nbk = pps * (PAGE_SIZE // CMP_BLOCK)
    accs = _compress_accumulate(lambda l, ch: buf_ref[slot, l * N_CH + ch], pe_ref, w_ref, nbk)
    for c in range(2):
        for h in range(N_KV_A):
            o_ref[0, c * N_KV_A + h] = accs[c][h * nbk:(h + 1) * nbk]


def _compress_paged(page_table, cache4, pe2d, w_cmp_bf, pps):
    nbatch, n_pages = page_table.shape
    nj = n_pages // pps
    nbk = pps * (PAGE_SIZE // CMP_BLOCK)
    gs = pltpu.PrefetchScalarGridSpec(
        num_scalar_prefetch=1,
        grid=(nbatch, nj),
        in_specs=[
            pl.BlockSpec(memory_space=pl.ANY),
            pl.BlockSpec((2, CMP_BLOCK, HEAD_DIM), lambda b, j, pt: (0, 0, 0)),
            pl.BlockSpec((2, CMP_BLOCK // 2, 2 * HEAD_DIM, HEAD_DIM), lambda b, j, pt: (0, 0, 0, 0)),
        ],
        out_specs=pl.BlockSpec((1, N_CH, nbk, HEAD_DIM), lambda b, j, pt: (b, 0, j, 0)),
        scratch_shapes=[pltpu.VMEM((2, BLOCK_ROWS4, nbk, HEAD_DIM), f32), pltpu.SemaphoreType.DMA((2,))],
    )
    return pl.pallas_call(
        functools.partial(_compress_paged_kernel, pps=pps),
        grid_spec=gs,
        out_shape=jax.ShapeDtypeStruct((nbatch, N_CH, nj * nbk, HEAD_DIM), f32),
        compiler_params=_cparams(("arbitrary", "arbitrary")),
        name="compress_paged",
    )(page_table, cache4, pe2d, w_cmp_bf)


def _two_phase_attend(qs, slope2, qpos, kpos0, load_k, load_v, load_mask, nchunks, chunk,
                      s_ref, rows, m_ref, l_ref, acc_ref, active=None):
    nrow = qs.shape[0]
    nrep = chunk // LANES
    kofs = lax.broadcasted_iota(jnp.int32, (1, chunk), 1)
    unroll = math.gcd(nchunks, 4)

    def cols(ci):
        return pl.ds(pl.multiple_of(ci * chunk, chunk), chunk)

    def skippable(ci, fn, carry):
        if active is None:
            return fn(carry)
        return lax.cond(active(ci), fn, lambda c: c, carry)

    def phase1(ci, mrun):
        def update(mrun):
            d = qpos - (kpos0 + ci * chunk + kofs)
            dmask = jnp.where((load_mask(ci) > 0.5) & (d >= 0), d.astype(f32), MASK_DIST)
            t2 = _dot_nt(qs, load_k(ci)) * (SCALE * LOG2E) - slope2 * dmask
            s_ref[rows, cols(ci)] = t2
            for c in range(nrep):
                mrun = jnp.maximum(mrun, t2[:, c * LANES:(c + 1) * LANES])
            return mrun

        return skippable(ci, update, mrun)

    mrun = lax.fori_loop(0, nchunks, phase1, jnp.full((nrow, LANES), NEG_INF, f32), unroll=unroll)
    m_prev = m_ref[rows, :]
    m_new = jnp.maximum(m_prev, jnp.broadcast_to(jnp.max(mrun, axis=-1, keepdims=True), (nrow, LANES)))

    def phase2(ci, carry):
        def update(carry):
            lp, acc = carry
            p = jnp.exp2(s_ref[rows, cols(ci)] - _lane_repeat(m_new, nrep))
            for c in range(nrep):
                lp = lp + p[:, c * LANES:(c + 1) * LANES]
            return lp, acc + _dot(p.astype(bf16), load_v(ci))

        return skippable(ci, update, carry)

    z = jnp.zeros((nrow, LANES), f32)
    lp, acc = lax.fori_loop(0, nchunks, phase2, (z, z), unroll=unroll)
    alpha = jnp.exp2(m_prev - m_new)
    l_ref[rows, :] = alpha * l_ref[rows, :] + lp
    acc_ref[rows, :] = alpha * acc_ref[rows, :] + acc
    m_ref[rows, :] = m_new


def _slc_paged_kernel(pt_ref, chunk_any_ref, chunk_kvh_ref, slopes_ref, cache_ref, q_ref, tail_ref, sel_ref,
                      e_ref, o_ref, buf_ref, sem_ref, s_ref, m_ref, l_ref, acc_ref, *, pps, t, sub, pos0):
    b = pl.program_id(0)
    j = pl.program_id(1)
    nj_pages = pl.num_programs(1) - 1
    ppc = sub // PAGE_SIZE
    cps = pps // ppc
    slot = _paged_fetch(cache_ref, buf_ref, sem_ref, pt_ref, pps, nj_pages,
                        need=lambda bb, jj, i: chunk_any_ref[bb, jj * cps + i // ppc] > 0)
    nrow = GQA * t

    @pl.when(j == 0)
    def _():
        _flash_init(m_ref, l_ref, acc_ref)

    tpos = lax.broadcasted_iota(jnp.int32, (t, 1), 0)
    qpos = pos0 + jnp.concatenate([tpos] * GQA, axis=0)

    def attend(kvh, load_k, load_v, nchunks, chunk, kpos0, active=None):
        q = q_ref[:, kvh * GQA * HEAD_DIM:(kvh + 1) * GQA * HEAD_DIM]
        qs = jnp.concatenate([q[:, g * HEAD_DIM:(g + 1) * HEAD_DIM] for g in range(GQA)], axis=0).astype(bf16)
        slope2 = jnp.concatenate([jnp.full((t, 1), slopes_ref[kvh, g] * LOG2E, f32) for g in range(GQA)], axis=0)
        selb = sel_ref[0, kvh].astype(bf16)

        def load_mask(ci):
            maskf = _dot(selb, e_ref[:, pl.ds(pl.multiple_of(ci * chunk, chunk), chunk)])
            return jnp.concatenate([maskf] * GQA, axis=0)

        _two_phase_attend(qs, slope2, qpos, kpos0, load_k, load_v, load_mask, nchunks, chunk,
                          s_ref, pl.ds(kvh * nrow, nrow), m_ref, l_ref, acc_ref, active)

    @pl.when(j < nj_pages)
    def _():
        for kvh in range(N_KV_A):
            def load(ci, ch):
                return buf_ref[slot, pl.ds(ci * (sub * N_CH) + ch, sub, stride=N_CH), :].astype(bf16)

            first = ((b * N_KV_A + kvh) * nj_pages + j) * cps
            attend(kvh, functools.partial(load, ch=kvh), functools.partial(load, ch=N_KV_A + kvh),
                   cps, sub, j * pps * PAGE_SIZE, lambda ci, first=first: chunk_kvh_ref[first + ci] > 0)

    @pl.when(j == nj_pages)
    def _():
        ntail = tail_ref.shape[1]
        for kvh in range(N_KV_A):
            k = tail_ref[0, :, kvh * HEAD_DIM:(kvh + 1) * HEAD_DIM].astype(bf16)
            v = tail_ref[0, :, (N_KV_A + kvh) * HEAD_DIM:(N_KV_A + kvh + 1) * HEAD_DIM].astype(bf16)
            attend(kvh, lambda ci, k=k: k, lambda ci, v=v: v, 1, ntail, nj_pages * pps * PAGE_SIZE)
        for kvh in range(N_KV_A):
            for g in range(GQA):
                r = pl.ds(kvh * nrow + g * t, t)
                o_ref[:, (kvh * GQA + g) * HEAD_DIM:(kvh * GQA + g + 1) * HEAD_DIM] = (
                    acc_ref[r, :] / jnp.sum(l_ref[r, :], axis=-1, keepdims=True))


def _slc_paged(page_table, slopes, cache4, qz, tail, sel, expand, t, pps, pos0):
    nbatch, n_pages = page_table.shape
    njp = n_pages // pps
    bps = pps * (PAGE_SIZE // CMP_BLOCK)
    sub = min(512, pps * PAGE_SIZE)
    bpc = sub // CMP_BLOCK
    n_pb = n_pages * (PAGE_SIZE // CMP_BLOCK)
    chunk_sel = sel[..., :n_pb].reshape(nbatch, N_KV_A, t, n_pb // bpc, bpc).max(axis=(2, 4)) > 0.5
    chunk_kvh = chunk_sel.astype(jnp.int32).reshape(-1)
    chunk_any = chunk_sel.any(axis=1).astype(jnp.int32)
    gs = pltpu.PrefetchScalarGridSpec(
        num_scalar_prefetch=3,
        grid=(nbatch, njp + 1),
        in_specs=[
            pl.BlockSpec(memory_space=pltpu.SMEM),
            pl.BlockSpec(memory_space=pl.ANY),
            pl.BlockSpec((t, D_ATT), lambda b, j, *_: (b, 0)),
            pl.BlockSpec((1, LANES, KV_BRANCH), lambda b, j, *_: (b, 0, 0)),
            pl.BlockSpec((1, N_KV_A, t, bps), lambda b, j, *_: (b, 0, 0, j)),
            pl.BlockSpec((bps, pps * PAGE_SIZE), lambda b, j, *_: (0, 0)),
        ],
        out_specs=pl.BlockSpec((t, D_ATT), lambda b, j, *_: (b, 0)),
        scratch_shapes=[pltpu.VMEM((2, pps * ROWS_PER_PAGE4, HEAD_DIM), f32), pltpu.SemaphoreType.DMA((2,)),
                        pltpu.VMEM((N_KV_A * GQA * t, pps * PAGE_SIZE), f32),
                        pltpu.VMEM((N_KV_A * GQA * t, LANES), f32), pltpu.VMEM((N_KV_A * GQA * t, LANES), f32),
                        pltpu.VMEM((N_KV_A * GQA * t, HEAD_DIM), f32)],
    )
    return pl.pallas_call(
        functools.partial(_slc_paged_kernel, pps=pps, t=t, sub=sub, pos0=pos0),
        grid_spec=gs,
        out_shape=jax.ShapeDtypeStruct((nbatch * t, D_ATT), f32),
        compiler_params=_cparams(("arbitrary", "arbitrary")),
        name="slc_paged",
    )(page_table, chunk_any, chunk_kvh, slopes, cache4, qz, tail, sel, expand)


def _swa_sample_kernel(slopes_ref, q_ref, win_ref, tail_ref, o_ref, *, t, wbuf, pos0):
    tpos = lax.broadcasted_iota(jnp.int32, (t, 1), 0)
    qpos = pos0 + jnp.concatenate([tpos] * GQA, axis=0)
    nk = wbuf + CMP_BLOCK
    kpos = pos0 - wbuf + lax.broadcasted_iota(jnp.int32, (1, nk), 1)
    d = qpos - kpos
    ok = (d >= 0) & (d < WINDOW)
    df = d.astype(f32)
    for kvh in range(N_KV_A):
        k = jnp.concatenate([win_ref[pl.ds(kvh, wbuf, stride=N_CH), :].astype(bf16),
                             tail_ref[0, :, kvh * HEAD_DIM:(kvh + 1) * HEAD_DIM]], axis=0)
        v = jnp.concatenate([win_ref[pl.ds(N_KV_A + kvh, wbuf, stride=N_CH), :].astype(bf16),
                             tail_ref[0, :, (N_KV_A + kvh) * HEAD_DIM:(N_KV_A + kvh + 1) * HEAD_DIM]], axis=0)
        q = q_ref[:, kvh * GQA * HEAD_DIM:(kvh + 1) * GQA * HEAD_DIM]
        qs = jnp.concatenate([q[:, g * HEAD_DIM:(g + 1) * HEAD_DIM] for g in range(GQA)], axis=0).astype(bf16)
        slope = jnp.concatenate([jnp.full((t, 1), slopes_ref[kvh, g], f32) for g in range(GQA)], axis=0)
        s = _dot_nt(qs, k) * SCALE - slope * df
        s = jnp.where(ok, s, NEG_INF)
        e = jnp.exp(s - jnp.max(s, axis=-1, keepdims=True))
        p = jnp.where(ok, e / jnp.sum(e, axis=-1, keepdims=True), 0.0)
        o = _dot(p.astype(bf16), v)
        for g in range(GQA):
            o_ref[:, (kvh * GQA + g) * HEAD_DIM:(kvh * GQA + g + 1) * HEAD_DIM] = o[g * t:(g + 1) * t]


def _swa_sample(slopes, qz, win4, tail, nbatch, t, wbuf, pos0):
    return pl.pallas_call(
        functools.partial(_swa_sample_kernel, t=t, wbuf=wbuf, pos0=pos0),
        grid=(nbatch,),
        in_specs=[
            pl.BlockSpec(memory_space=pltpu.SMEM),
            pl.BlockSpec((t, D_ATT), lambda b: (b, 0)),
            pl.BlockSpec((wbuf * N_CH, HEAD_DIM), lambda b: (b, 0)),
            pl.BlockSpec((1, CMP_BLOCK, KV_BRANCH), lambda b: (b, 0, 0)),
        ],
        out_specs=pl.BlockSpec((t, D_ATT), lambda b: (b, 0)),
        out_shape=jax.ShapeDtypeStruct((nbatch * t, D_ATT), f32),
        compiler_params=_cparams(("parallel",)),
        name="swa_sample",
    )(slopes, qz, win4, tail)


def _round_up(a, b):
    return -(-a // b) * b


def _pad_blocks(kcv, ncp):
    return jnp.pad(kcv, ((0, 0), (0, 0), (0, ncp - kcv.shape[2]), (0, 0)))


def _expand_matrix(nblocks, nkeys):
    return (jnp.arange(nkeys)[None, :] // CMP_BLOCK == jnp.arange(nblocks)[:, None]).astype(bf16)


def _kv6(rows2d, nbatch, t):
    return rows2d.reshape(nbatch, t, 2, N_KV_A, HEAD_DIM)


def _layer_prompt(x, prm):
    nbatch, t, _ = x.shape
    m = nbatch * t
    x2d = x.reshape(m, D_MODEL)
    tm = min(1024, m)
    qz, qkvb, kv_slc, kv_swa, cmp4, slc4, swa4, small = _in_proj(
        x2d, prm["norm_w"], prm["w_main"], prm["w_small"], tm)
    nc = t // CMP_BLOCK
    kcv = _compress(cmp4, nbatch, prm["pe2d"], prm["w_cmp"], min(64, nc))
    kcv = _pad_blocks(kcv, _round_up(nc, LANES))
    tq = min(256, t)
    o_cmp, sel = _cmp_attend(prm["slopes"], qz, kcv, nbatch, t, tq, 0, min(N_SELECT, nc))
    tk = min(256, t)
    o_slc = _slc_prompt(prm["slopes"], qz, kv_slc, sel, _expand_matrix(sel.shape[-1], t), nbatch, t, tq, tk)
    o_swa = _swa_prompt(prm["slopes"], qz, kv_swa, nbatch, t, tq, tk)
    conv0 = jnp.zeros((nbatch, CONV_W - 1, CONV_DIM), f32)
    s0 = jnp.zeros((nbatch, N_HEADS_B, HEAD_DIM, HEAD_DIM), f32)
    o_b, s_new = _gdn(qkvb, small, conv0, s0, prm["conv_w"], prm["alog_row"], prm["dtb_row"],
                      prm["gdn_norm_w"], nbatch, t, 128)
    y = _merge_out(x2d, o_cmp, o_slc, o_swa, small, qz, o_b, prm["w_out"], prm["final_norm_w"], min(512, m))
    w = min(WINDOW, t)
    conv_new = qkvb.reshape(nbatch, t, CONV_DIM)[:, t - (CONV_W - 1):]
    return (y.reshape(nbatch, t, D_MODEL), _kv6(cmp4, nbatch, t), _kv6(slc4, nbatch, t),
            _kv6(swa4, nbatch, t)[:, t - w:], conv_new, s_new)


def _layer_sample(x, cache_cmp, cache_slc, cache_swa, conv_buf, s0, page_table, prm):
    nbatch, t, _ = x.shape
    m = nbatch * t
    n_pages = page_table.shape[1]
    past_len = n_pages * PAGE_SIZE
    wbuf = cache_swa.shape[1]
    x2d = x.reshape(m, D_MODEL)
    qz, qkvb, kv_slc, kv_swa, cmp4, slc4, swa4, small = _in_proj(
        x2d, prm["norm_w"], prm["w_main"], prm["w_small"], m)
    pad_tail = lambda r: jnp.pad(r.reshape(nbatch, t, KV_BRANCH), ((0, 0), (0, CMP_BLOCK - t), (0, 0)))
    pps = min(32, n_pages)
    kc_past = _compress_paged(page_table, cache_cmp.reshape(-1, HEAD_DIM), prm["pe2d"], prm["w_cmp"], pps)
    kc_tail = _compress(pad_tail(cmp4).reshape(nbatch * CMP_BLOCK * N_CH, HEAD_DIM), 1, prm["pe2d"], prm["w_cmp"], nbatch)
    kc_tail = jnp.swapaxes(kc_tail[0], 0, 1)[:, :, None, :]
    n_pb = past_len // CMP_BLOCK
    pps_s = min(64, n_pages)
    bps = pps_s * (PAGE_SIZE // CMP_BLOCK)
    ncp = _round_up(n_pb + 1, max(LANES, bps))
    kcv = _pad_blocks(jnp.concatenate([kc_past, kc_tail], axis=2), ncp)
    o_cmp, sel = _cmp_attend(prm["slopes"], qz, kcv, nbatch, t, t, past_len, min(N_SELECT, n_pb + 1))
    o_slc = _slc_paged(page_table, prm["slopes"], cache_slc.reshape(-1, HEAD_DIM), qz,
                       jnp.pad(kv_slc.reshape(nbatch, t, KV_BRANCH), ((0, 0), (0, LANES - t), (0, 0))), sel,
                       _expand_matrix(bps, pps_s * PAGE_SIZE), t, pps_s, past_len)
    o_swa = _swa_sample(prm["slopes"], qz, cache_swa.reshape(-1, HEAD_DIM), pad_tail(kv_swa), nbatch, t, wbuf, past_len)
    o_b, s_new = _gdn(qkvb, small, conv_buf, s0, prm["conv_w"], prm["alog_row"], prm["dtb_row"],
                      prm["gdn_norm_w"], nbatch, t, 128)
    y = _merge_out(x2d, o_cmp, o_slc, o_swa, small, qz, o_b, prm["w_out"], prm["final_norm_w"], min(512, m))
    kv_win = jnp.concatenate([cache_swa, _kv6(swa4, nbatch, t)], axis=1)[:, t:]
    conv_new = jnp.concatenate([conv_buf, qkvb.reshape(nbatch, t, CONV_DIM)], axis=1)[:, t:]
    return (y.reshape(nbatch, t, D_MODEL), _kv6(cmp4, nbatch, t), _kv6(slc4, nbatch, t), kv_win, conv_new, s_new)


def kernel(x_prompt, x_sample, cache_cmp, cache_slc, cache_swa, state_conv, state_gdn, page_table,
           norm_w, w_in, pe_cmp, w_cmp, conv_w, a_log, dt_bias, gdn_norm_w, w_out, final_norm_w):
    depth = norm_w.shape[0]
    assert depth == 1, "the final norm is fused into the single layer's output projection"
    l = 0
    w_main, w_small = _prep_w_in(w_in[l])
    head = jnp.arange(1, N_HEADS_A + 1, dtype=f32)
    lane_row = lambda vals: jnp.zeros((1, LANES), f32).at[0, A_OFF:A_OFF + N_HEADS_B].set(vals.astype(f32))
    prm = {
        "norm_w": norm_w[l], "w_main": w_main, "w_small": w_small,
        "pe2d": jnp.swapaxes(pe_cmp[l], 0, 1),
        "w_cmp": jnp.swapaxes(w_cmp[l], 0, 1).reshape(2, CMP_BLOCK // 2, 2 * HEAD_DIM, HEAD_DIM).astype(bf16),
        "conv_w": conv_w[l], "alog_row": lane_row(a_log[l]), "dtb_row": lane_row(dt_bias[l]),
        "gdn_norm_w": gdn_norm_w[l], "w_out": w_out[l].astype(bf16), "final_norm_w": final_norm_w,
        "slopes": jnp.exp2(-8.0 * head / N_HEADS_A).reshape(N_KV_A, GQA),
    }
    yp, p_cmp, p_slc, p_swa, p_conv, p_gdn = _layer_prompt(x_prompt, prm)
    ys, s_cmp, s_slc, s_swa, s_conv, s_gdn = _layer_sample(
        x_sample, cache_cmp[l], cache_slc[l], cache_swa[l], state_conv[l], state_gdn[l], page_table, prm)
    st = lambda a: a[None]
    return (yp, ys, st(p_cmp), st(p_slc), st(p_swa), st(p_conv), st(p_gdn),
            st(s_cmp), st(s_slc), st(s_swa), st(s_conv), st(s_gdn))
```

```python
import functools
import math

import jax
import jax.numpy as jnp
from jax import lax
from jax.experimental import pallas as pl
from jax.experimental.pallas import tpu as pltpu

f32 = jnp.float32
bf16 = jnp.bfloat16

D_MODEL = 2048
HEAD_DIM = 128
N_HEADS_A = 8
N_KV_A = 2
GQA = 4
D_ATT = N_HEADS_A * HEAD_DIM
CMP_BLOCK = 64
N_SELECT = 16
WINDOW = 512
N_HEADS_B = 8
D_GDN = N_HEADS_B * HEAD_DIM
CONV_W = 4
CONV_DIM = 3 * D_GDN
PAGE_SIZE = 128
NORM_EPS = 1e-6
NEG_INF = -1e30
FORCE_SCORE = 1e4
KV_BRANCH = 2 * N_KV_A * HEAD_DIM
N_CH = 2 * N_KV_A
ROWS_PER_PAGE4 = PAGE_SIZE * N_CH
SCALE = HEAD_DIM ** -0.5
G_OFF, A_OFF, B_OFF = 0, 3 * N_HEADS_A, 3 * N_HEADS_A + N_HEADS_B
LANES = 128
VMEM_LIMIT = 56 * 1024 * 1024
IN_PROJ_VMEM_LIMIT = 60 * 1024 * 1024

IN_TN = 512
N_QZ_TILES = 3 * D_ATT // IN_TN
N_B_TILES = CONV_DIM // IN_TN
N_IN_TILES = N_QZ_TILES + N_B_TILES + 3


def _cparams(sem):
    return pltpu.CompilerParams(dimension_semantics=sem, vmem_limit_bytes=VMEM_LIMIT)


def _dot(a, b):
    return jnp.dot(a, b, preferred_element_type=f32)


def _dot_nt(a, b):
    return lax.dot_general(a, b, (((1,), (1,)), ((), ())), preferred_element_type=f32)


def _dot_tn(a, b):
    return lax.dot_general(a, b, (((0,), (0,)), ((), ())), preferred_element_type=f32)


def _store_row4(o4_ref, acc):
    for ch in range(N_CH):
        o4_ref[pl.ds(ch, acc.shape[0], stride=N_CH), :] = acc[:, ch * HEAD_DIM:(ch + 1) * HEAD_DIM]


def _in_proj_kernel(x_ref, nw_ref, w_ref, ws_ref, qz_ref, b_ref, slc_ref, swa_ref, cmp4_ref, slc4_ref, swa4_ref,
                    small_ref, xn_ref):
    n = pl.program_id(1)

    @pl.when(n == 0)
    def _():
        x = x_ref[...]
        ms = jnp.mean(x * x, axis=-1, keepdims=True)
        xn = (x * lax.rsqrt(ms + NORM_EPS) * nw_ref[...]).astype(bf16)
        xn_ref[...] = xn
        small_ref[...] = _dot(xn, ws_ref[...])

    acc = _dot(xn_ref[...], w_ref[...])

    @pl.when(n < N_QZ_TILES)
    def _():
        qz_ref[...] = acc

    @pl.when((n >= N_QZ_TILES) & (n < N_QZ_TILES + N_B_TILES))
    def _():
        b_ref[...] = acc

    @pl.when(n == N_QZ_TILES + N_B_TILES)
    def _():
        _store_row4(cmp4_ref, acc)

    @pl.when(n == N_QZ_TILES + N_B_TILES + 1)
    def _():
        slc_ref[...] = acc.astype(bf16)
        _store_row4(slc4_ref, acc)

    @pl.when(n == N_QZ_TILES + N_B_TILES + 2)
    def _():
        swa_ref[...] = acc.astype(bf16)
        _store_row4(swa4_ref, acc)


def _in_proj(x2d, norm_w, w_main, w_small, tm):
    m = x2d.shape[0]
    S = jax.ShapeDtypeStruct
    nb0 = N_QZ_TILES
    return pl.pallas_call(
        _in_proj_kernel,
        grid=(m // tm, N_IN_TILES),
        in_specs=[
            pl.BlockSpec((tm, D_MODEL), lambda i, n: (i, 0)),
            pl.BlockSpec((1, D_MODEL), lambda i, n: (0, 0)),
            pl.BlockSpec((D_MODEL, IN_TN), lambda i, n: (0, n)),
            pl.BlockSpec((D_MODEL, LANES), lambda i, n: (0, 0)),
        ],
        out_specs=[
            pl.BlockSpec((tm, IN_TN), lambda i, n: (i, jnp.minimum(n, nb0 - 1))),
            pl.BlockSpec((tm, IN_TN), lambda i, n: (i, jnp.clip(n - nb0, 0, N_B_TILES - 1))),
            pl.BlockSpec((tm, KV_BRANCH), lambda i, n: (i, 0)),
            pl.BlockSpec((tm, KV_BRANCH), lambda i, n: (i, 0)),
            pl.BlockSpec((tm * N_CH, HEAD_DIM), lambda i, n: (i, 0)),
            pl.BlockSpec((tm * N_CH, HEAD_DIM), lambda i, n: (i, 0)),
            pl.BlockSpec((tm * N_CH, HEAD_DIM), lambda i, n: (i, 0)),
            pl.BlockSpec((tm, LANES), lambda i, n: (i, 0)),
        ],
        out_shape=[S((m, 3 * D_ATT), f32), S((m, CONV_DIM), f32), S((m, KV_BRANCH), bf16), S((m, KV_BRANCH), bf16),
                   S((m * N_CH, HEAD_DIM), f32), S((m * N_CH, HEAD_DIM), f32), S((m * N_CH, HEAD_DIM), f32),
                   S((m, LANES), f32)],
        scratch_shapes=[pltpu.VMEM((tm, D_MODEL), bf16)],
        compiler_params=pltpu.CompilerParams(dimension_semantics=("parallel", "arbitrary"),
                                             vmem_limit_bytes=IN_PROJ_VMEM_LIMIT),
        name="in_proj",
    )(x2d, norm_w.reshape(1, D_MODEL), w_main, w_small)


def _prep_w_in(w_in):
    o = [0]
    for s in (D_ATT, 3 * KV_BRANCH, 3 * N_HEADS_A, D_ATT, CONV_DIM, N_HEADS_B, N_HEADS_B, D_GDN):
        o.append(o[-1] + s)
    q_a, kv_a, g_a, z_a, qkv_b, a_b, b_b, z_b = (w_in[:, o[i]:o[i + 1]] for i in range(8))
    w_main = jnp.concatenate([q_a, z_a, z_b, qkv_b, kv_a], axis=1).astype(bf16)
    pad = jnp.zeros((w_in.shape[0], LANES - (3 * N_HEADS_A + 2 * N_HEADS_B)), w_in.dtype)
    w_small = jnp.concatenate([g_a, a_b, b_b, pad], axis=1).astype(bf16)
    return w_main, w_small


CMP_UNROLL = 16


def _compress_accumulate(load_rows, pe_ref, w_ref, nbk):
    def body(i, accs):
        new = []
        for c in range(2):
            halves = []
            for dl in range(2):
                l = 2 * i + dl
                pe_row = pe_ref[c, pl.ds(l, 1), :]
                halves.append(jnp.concatenate(
                    [load_rows(l, c * N_KV_A + h) + pe_row for h in range(N_KV_A)], axis=0))
            lhs = jnp.concatenate(halves, axis=1).astype(bf16)
            new.append(accs[c] + _dot(lhs, w_ref[c, i]))
        return tuple(new)

    z = jnp.zeros((N_KV_A * nbk, HEAD_DIM), f32)
    return lax.fori_loop(0, CMP_BLOCK // 2, body, (z, z), unroll=CMP_UNROLL)


def _compress_kernel(x_ref, pe_ref, w_ref, o_ref, *, nbk):
    accs = _compress_accumulate(
        lambda l, ch: x_ref[pl.ds(l * N_CH + ch, nbk, stride=CMP_BLOCK * N_CH), :], pe_ref, w_ref, nbk)
    for c in range(2):
        for h in range(N_KV_A):
            o_ref[0, c * N_KV_A + h] = accs[c][h * nbk:(h + 1) * nbk]


def _compress(rows4, nbatch, pe2d, w_cmp_bf, nbk):
    nblk = rows4.shape[0] // (nbatch * CMP_BLOCK * N_CH)
    nj = nblk // nbk
    return pl.pallas_call(
        functools.partial(_compress_kernel, nbk=nbk),
        grid=(nbatch, nj),
        in_specs=[
            pl.BlockSpec((nbk * CMP_BLOCK * N_CH, HEAD_DIM), lambda b, j: (b * nj + j, 0)),
            pl.BlockSpec((2, CMP_BLOCK, HEAD_DIM), lambda b, j: (0, 0, 0)),
            pl.BlockSpec((2, CMP_BLOCK // 2, 2 * HEAD_DIM, HEAD_DIM), lambda b, j: (0, 0, 0, 0)),
        ],
        out_specs=pl.BlockSpec((1, N_CH, nbk, HEAD_DIM), lambda b, j: (b, 0, j, 0)),
        out_shape=jax.ShapeDtypeStruct((nbatch, N_CH, nblk, HEAD_DIM), f32),
        compiler_params=_cparams(("parallel", "parallel")),
        name="compress",
    )(rows4, pe2d, w_cmp_bf)


def _topk_mask(sc, nsel):
    ncp = sc.shape[0]
    jblk = lax.broadcasted_iota(jnp.int32, (ncp, 1), 0)
    work = sc
    for _ in range(nsel):
        mx = jnp.max(work, axis=0, keepdims=True)
        first = jnp.min(jnp.where(work == mx, jblk, ncp), axis=0, keepdims=True)
        work = jnp.where(jblk == first, -2.0, work)
    return jnp.where((work == -2.0) & (sc >= 0.0), 1.0, 0.0)


def _topk_kernel(sc_ref, sel_ref, *, nsel):
    sel_ref[0] = _topk_mask(sc_ref[0], nsel)


def _topk(scores, nsel):
    ng, ncp, nq = scores.shape
    return pl.pallas_call(
        functools.partial(_topk_kernel, nsel=nsel),
        grid=(ng, nq // LANES),
        in_specs=[pl.BlockSpec((1, ncp, LANES), lambda g, c: (g, 0, c))],
        out_specs=pl.BlockSpec((1, ncp, LANES), lambda g, c: (g, 0, c)),
        out_shape=jax.ShapeDtypeStruct(scores.shape, f32),
        compiler_params=_cparams(("parallel", "parallel")),
        name="topk",
    )(scores)


def _cmp_kernel(slopes_ref, q_ref, kc_ref, vc_ref, o_ref, sel_ref, *, tq, tqp, ncp, pos0, nsel):
    kvh = pl.program_id(1)
    qt = pl.program_id(2)
    q = q_ref[...]
    if tqp > tq:
        q = jnp.concatenate([q, jnp.zeros((tqp - tq, q.shape[1]), f32)], axis=0)
    kc = kc_ref[0, 0].astype(bf16)
    vc = vc_ref[0, 0].astype(bf16)
    qpos = pos0 + qt * tq + lax.broadcasted_iota(jnp.int32, (1, tqp), 1)
    jblk = lax.broadcasted_iota(jnp.int32, (ncp, 1), 0)
    d = qpos - ((jblk + 1) * CMP_BLOCK - 1)
    ok = d >= 0
    df = d.astype(f32)
    imp = jnp.zeros((ncp, tqp), f32)
    for g in range(GQA):
        qg = q[:, g * HEAD_DIM:(g + 1) * HEAD_DIM].astype(bf16)
        s = _dot_nt(kc, qg) * SCALE
        s = jnp.where(ok, s - slopes_ref[kvh, g] * df, NEG_INF)
        mx = jnp.max(s, axis=0, keepdims=True)
        e = jnp.exp(s - mx)
        p = jnp.where(ok, e / jnp.sum(e, axis=0, keepdims=True), 0.0)
        imp = imp + p
        og = _dot_tn(p.astype(bf16), vc)
        o_ref[:, g * HEAD_DIM:(g + 1) * HEAD_DIM] = og[:tq]
    cur = qpos // CMP_BLOCK
    forced = (jblk == cur) | (jblk == 0)
    score = jnp.where(forced, FORCE_SCORE, jnp.where(jblk <= cur, imp, -1.0))
    if nsel is None:
        sel_ref[0, 0] = score
        return
    for cg in range(tqp // LANES):
        sel = _topk_mask(score[:, cg * LANES:(cg + 1) * LANES], nsel)
        lo = cg * LANES
        hi = min(tq, lo + LANES)
        sel_ref[0, 0, lo:hi, :] = sel.T[:hi - lo]


def _cmp_attend(slopes, qz, kcv, nbatch, t, tq, pos0, nsel):
    ncp = kcv.shape[2]
    tqp = max(tq, LANES)
    nqt = t // tq
    gw = GQA * HEAD_DIM
    packed = tq < LANES and (nbatch * t) % LANES == 0
    sel_spec = (pl.BlockSpec((1, 1, ncp, tqp), lambda b, h, i: (b, h, 0, i)) if packed
                else pl.BlockSpec((1, 1, tq, ncp), lambda b, h, i: (b, h, i, 0)))
    sel_shape = (nbatch, N_KV_A, ncp, nqt * tqp) if packed else (nbatch, N_KV_A, t, ncp)
    o_cmp, sel = pl.pallas_call(
        functools.partial(_cmp_kernel, tq=tq, tqp=tqp, ncp=ncp, pos0=pos0, nsel=None if packed else nsel),
        grid=(nbatch, N_KV_A, nqt),
        in_specs=[
            pl.BlockSpec(memory_space=pltpu.SMEM),
            pl.BlockSpec((tq, gw), lambda b, h, i: (b * nqt + i, h)),
            pl.BlockSpec((1, 1, ncp, HEAD_DIM), lambda b, h, i: (b, h, 0, 0)),
            pl.BlockSpec((1, 1, ncp, HEAD_DIM), lambda b, h, i: (b, N_KV_A + h, 0, 0)),
        ],
        out_specs=[pl.BlockSpec((tq, gw), lambda b, h, i: (b * nqt + i, h)), sel_spec],
        out_shape=[jax.ShapeDtypeStruct((nbatch * t, D_ATT), f32), jax.ShapeDtypeStruct(sel_shape, f32)],
        compiler_params=_cparams(("parallel", "parallel", "parallel")),
        name="cmp_attend",
    )(slopes, qz, kcv, kcv)
    if packed:
        scores = sel.reshape(nbatch, N_KV_A, ncp, nqt, tqp)[..., :tq].reshape(nbatch, N_KV_A, ncp, t)
        scores = jnp.transpose(scores, (1, 2, 0, 3)).reshape(N_KV_A, ncp, nbatch * t)
        mask = _topk(scores, nsel).reshape(N_KV_A, ncp, nbatch, t)
        sel = jnp.transpose(mask, (2, 0, 3, 1))
    return o_cmp, sel


def _lane_repeat(x, n):
    return x if n == 1 else jnp.concatenate([x] * n, axis=1)


def _flash_init(m_ref, l_ref, acc_ref):
    m_ref[...] = jnp.full(m_ref.shape, NEG_INF, f32)
    l_ref[...] = jnp.zeros(l_ref.shape, f32)
    acc_ref[...] = jnp.zeros(acc_ref.shape, f32)


LOG2E = 1.4426950408889634
MASK_DIST = 1e33
SLC_ROWS = 128


def _attn_prompt_body(kvh, qt, slopes_ref, q_ref, k_ref, v_ref, o_ref, qb_ref, dm_ref, s_ref, mn_ref,
                      m_ref, l_ref, acc_ref, *, tq, tk, kt_lo, kt_hi, tile_dist, active):
    _flash_init(m_ref, l_ref, acc_ref)
    for g in range(GQA):
        qb_ref[g * tq:(g + 1) * tq, :] = (q_ref[:, g * HEAD_DIM:(g + 1) * HEAD_DIM] * (SCALE * LOG2E)).astype(bf16)
    nrep = tk // LANES
    ngrp = tq // SLC_ROWS

    def group_update(groups, k, v):
        units = [(g, r) for r in groups for g in range(GQA)]
        for g, r in units:
            rows = pl.ds(g * tq + r * SLC_ROWS, SLC_ROWS)
            t2 = (_dot_nt(qb_ref[rows, :], k)
                  - (slopes_ref[kvh, g] * LOG2E) * dm_ref[r * SLC_ROWS:(r + 1) * SLC_ROWS, :])
            s_ref[rows, :] = t2
            mx = jnp.max(t2, axis=-1, keepdims=True)
            mn_ref[rows, :] = jnp.maximum(m_ref[rows, :], jnp.broadcast_to(mx, (SLC_ROWS, LANES)))
        for g, r in units:
            rows = pl.ds(g * tq + r * SLC_ROWS, SLC_ROWS)
            m_new = mn_ref[rows, :]
            p = jnp.exp2(s_ref[rows, :] - _lane_repeat(m_new, nrep))
            alpha = jnp.exp2(m_ref[rows, :] - m_new)
            psum = p[:, 0:LANES]
            for c in range(1, nrep):
                psum = psum + p[:, c * LANES:(c + 1) * LANES]
            l_ref[rows, :] = alpha * l_ref[rows, :] + psum
            acc_ref[rows, :] = alpha * acc_ref[rows, :] + _dot(p.astype(bf16), v)
            m_ref[rows, :] = m_new

    def tile_update(kt, acts):
        ks = pl.multiple_of(kt * tk, tk)
        k = k_ref[pl.ds(ks, tk), :]
        v = v_ref[pl.ds(ks, tk), :]
        dm_ref[...] = tile_dist(ks)
        if acts is None:
            group_update(tuple(range(ngrp)), k, v)
        elif ngrp == 2:
            pl.when(acts[0] & acts[1])(functools.partial(group_update, (0, 1), k, v))
            pl.when(acts[0] & jnp.logical_not(acts[1]))(functools.partial(group_update, (0,), k, v))
            pl.when(acts[1] & jnp.logical_not(acts[0]))(functools.partial(group_update, (1,), k, v))
        else:
            for r in range(ngrp):
                pl.when(acts[r])(functools.partial(group_update, (r,), k, v))

    def kt_body(kt, carry):
        if active is None:
            tile_update(kt, None)
        else:
            acts = [active(kt, r) for r in range(ngrp)]
            any_act = acts[0]
            for a in acts[1:]:
                any_act = any_act | a
            pl.when(any_act)(functools.partial(tile_update, kt, acts))
        return carry

    lax.fori_loop(kt_lo, kt_hi, kt_body, 0)
    for g in range(GQA):
        rows = pl.ds(g * tq, tq)
        o_ref[:, g * HEAD_DIM:(g + 1) * HEAD_DIM] = (
            acc_ref[rows, :] / jnp.sum(l_ref[rows, :], axis=-1, keepdims=True))


def _slc_kernel(flags_ref, slopes_ref, q_ref, k_ref, v_ref, sel_ref, e_ref, o_ref, *scratch, tq, tk, t):
    b = pl.program_id(0)
    kvh = pl.program_id(1)
    qt = pl.program_id(2)
    selb = sel_ref[0, 0].astype(bf16)
    qpos = qt * tq + lax.broadcasted_iota(jnp.int32, (tq, 1), 0)
    kofs = lax.broadcasted_iota(jnp.int32, (1, tk), 1)
    ngrp = tq // SLC_ROWS
    nkt = t // tk

    def tile_dist(ks):
        maskf = _dot(selb, e_ref[:, pl.ds(ks, tk)])
        d = qpos - (ks + kofs)
        return jnp.where((maskf > 0.5) & (d >= 0), d.astype(f32), MASK_DIST)

    def active(kt, r):
        row_group = (b * N_KV_A + kvh) * (t // SLC_ROWS) + qt * ngrp + r
        return flags_ref[row_group * nkt + kt] > 0

    _attn_prompt_body(kvh, qt, slopes_ref, q_ref, k_ref, v_ref, o_ref, *scratch, tq=tq, tk=tk,
                      kt_lo=0, kt_hi=(qt * tq + tq - 1) // tk + 1, tile_dist=tile_dist, active=active)


def _attn_scratch(tq, tk):
    return [pltpu.VMEM((GQA * tq, HEAD_DIM), bf16), pltpu.VMEM((tq, tk), f32),
            pltpu.VMEM((GQA * tq, tk), f32), pltpu.VMEM((GQA * tq, LANES), f32),
            pltpu.VMEM((GQA * tq, LANES), f32), pltpu.VMEM((GQA * tq, LANES), f32),
            pltpu.VMEM((GQA * tq, HEAD_DIM), f32)]


def _slc_prompt(slopes, qz, kv_slc, sel, expand, nbatch, t, tq, tk):
    nqt = t // tq
    gw = GQA * HEAD_DIM
    ncp = sel.shape[-1]
    bpt = tk // CMP_BLOCK
    flags = sel.reshape(nbatch, N_KV_A, t // SLC_ROWS, SLC_ROWS, ncp // bpt, bpt).max(axis=(3, 5))
    flags = (flags[..., :t // tk] > 0.5).astype(jnp.int32).reshape(-1)
    gs = pltpu.PrefetchScalarGridSpec(
        num_scalar_prefetch=1,
        grid=(nbatch, N_KV_A, nqt),
        in_specs=[
            pl.BlockSpec(memory_space=pltpu.SMEM),
            pl.BlockSpec((tq, gw), lambda b, h, i, fl: (b * nqt + i, h)),
            pl.BlockSpec((t, HEAD_DIM), lambda b, h, i, fl: (b, h)),
            pl.BlockSpec((t, HEAD_DIM), lambda b, h, i, fl: (b, N_KV_A + h)),
            pl.BlockSpec((1, 1, tq, ncp), lambda b, h, i, fl: (b, h, i, 0)),
            pl.BlockSpec((ncp, t), lambda b, h, i, fl: (0, 0)),
        ],
        out_specs=pl.BlockSpec((tq, gw), lambda b, h, i, fl: (b * nqt + i, h)),
        scratch_shapes=_attn_scratch(tq, tk),
    )
    return pl.pallas_call(
        functools.partial(_slc_kernel, tq=tq, tk=tk, t=t),
        grid_spec=gs,
        out_shape=jax.ShapeDtypeStruct((nbatch * t, D_ATT), f32),
        compiler_params=_cparams(("parallel", "parallel", "parallel")),
        name="slc_prompt",
    )(flags, slopes, qz, kv_slc, kv_slc, sel, expand)


def _swa_kernel(slopes_ref, q_ref, k_ref, v_ref, o_ref, *scratch, tq, tk):
    kvh = pl.program_id(1)
    qt = pl.program_id(2)
    qpos = qt * tq + lax.broadcasted_iota(jnp.int32, (tq, 1), 0)
    kofs = lax.broadcasted_iota(jnp.int32, (1, tk), 1)

    def tile_dist(ks):
        d = qpos - (ks + kofs)
        return jnp.where((d >= 0) & (d < WINDOW), d.astype(f32), MASK_DIST)

    _attn_prompt_body(kvh, qt, slopes_ref, q_ref, k_ref, v_ref, o_ref, *scratch, tq=tq, tk=tk,
                      kt_lo=jnp.maximum(qt * tq - (WINDOW - 1), 0) // tk, kt_hi=(qt * tq + tq - 1) // tk + 1,
                      tile_dist=tile_dist, active=None)


def _swa_prompt(slopes, qz, kv_swa, nbatch, t, tq, tk):
    nqt = t // tq
    gw = GQA * HEAD_DIM
    return pl.pallas_call(
        functools.partial(_swa_kernel, tq=tq, tk=tk),
        grid=(nbatch, N_KV_A, nqt),
        in_specs=[
            pl.BlockSpec(memory_space=pltpu.SMEM),
            pl.BlockSpec((tq, gw), lambda b, h, i: (b * nqt + i, h)),
            pl.BlockSpec((t, HEAD_DIM), lambda b, h, i: (b, h)),
            pl.BlockSpec((t, HEAD_DIM), lambda b, h, i: (b, N_KV_A + h)),
        ],
        out_specs=pl.BlockSpec((tq, gw), lambda b, h, i: (b * nqt + i, h)),
        out_shape=jax.ShapeDtypeStruct((nbatch * t, D_ATT), f32),
        scratch_shapes=_attn_scratch(tq, tk),
        compiler_params=_cparams(("parallel", "parallel", "parallel")),
        name="swa_prompt",
    )(slopes, qz, kv_swa, kv_swa)


def _softplus(x):
    return jnp.maximum(x, 0.0) + jnp.log1p(jnp.exp(-jnp.abs(x)))


def _sigmoid(x):
    return 1.0 / (1.0 + jnp.exp(-x))


def _silu(x):
    return x * _sigmoid(x)


def _dot_hi(a, b):
    return jnp.dot(a, b, precision=lax.Precision.HIGHEST, preferred_element_type=f32)


INV_BASE = 16


def _bdot(a, b):
    return _dot(a.astype(bf16), b.astype(bf16))


def _gdn_kernel(x_ref, small_ref, conv0_ref, s0_ref, cw_ref, alog_ref, dtb_ref, nw_ref,
                o_ref, s_out_ref, xp_ref, st_ref, gam_ref, n_ref, t_ref, pw_ref, uw_ref,
                k_ref, kb_ref, q_ref, qg_ref, kg_ref, qk_ref, vn_ref, rhs_ref, *, c, tv, halo):
    ci = pl.program_id(1)
    nci = pl.num_programs(1)

    @pl.when(ci == 0)
    def _():
        xp_ref[...] = jnp.zeros(xp_ref.shape, f32)
        xp_ref[halo - (CONV_W - 1):halo, :] = conv0_ref[0]
        st_ref[...] = s0_ref[0]

    xp_ref[halo:halo + tv, :] = x_ref[...]
    y = xp_ref[pl.ds(halo, c), :] * cw_ref[CONV_W - 1:CONV_W, :]
    for i in range(CONV_W - 2, -1, -1):
        y = y + xp_ref[pl.ds(halo - (CONV_W - 1) + i, c), :] * cw_ref[i:i + 1, :]
    y = _silu(y)
    xp_ref[halo - (CONV_W - 1):halo, :] = xp_ref[halo + tv - (CONV_W - 1):halo + tv, :]

    ri = lax.broadcasted_iota(jnp.int32, (c, 1), 0)
    rowi = lax.broadcasted_iota(jnp.int32, (c, c), 0)
    coli = lax.broadcasted_iota(jnp.int32, (c, c), 1)
    lower = rowi >= coli
    strict = rowi > coli
    tril = lower.astype(f32)
    eye = (rowi == coli).astype(f32)
    same_blk = {}
    s = INV_BASE
    while s <= c:
        same_blk[s] = (rowi // s) == (coli // s)
        s *= 2

    sm = small_ref[...]
    if tv < c:
        sm = jnp.concatenate([sm, jnp.zeros((c - tv, LANES), f32)], axis=0)
    g_all = -jnp.exp(alog_ref[...]) * _softplus(sm + dtb_ref[...])
    beta_all = _sigmoid(sm)
    if tv < c:
        live = ri < tv
        g_all = jnp.where(live, g_all, 0.0)
        beta_all = jnp.where(live, beta_all, 0.0)
    gc_all = _dot_hi(tril, g_all)
    gc_rows = gc_all.T
    heads = range(N_HEADS_B)
    decay = []

    for h in heads:
        yq = y[:, h * HEAD_DIM:(h + 1) * HEAD_DIM]
        yk = y[:, D_GDN + h * HEAD_DIM:D_GDN + (h + 1) * HEAD_DIM]
        v = y[:, 2 * D_GDN + h * HEAD_DIM:2 * D_GDN + (h + 1) * HEAD_DIM]
        q = yq * lax.rsqrt(jnp.sum(yq * yq, axis=-1, keepdims=True) + NORM_EPS) * SCALE
        k = yk * lax.rsqrt(jnp.sum(yk * yk, axis=-1, keepdims=True) + NORM_EPS)
        if tv < c:
            q = jnp.where(live, q, 0.0)
            k = jnp.where(live, k, 0.0)
            v = jnp.where(live, v, 0.0)
        beta = beta_all[:, B_OFF + h:B_OFF + h + 1]
        gcol = gc_all[:, A_OFF + h:A_OFF + h + 1]
        grow = gc_rows[A_OFF + h:A_OFF + h + 1, :]
        gam_ref[h] = jnp.where(lower, jnp.exp(jnp.where(lower, gcol - grow, 0.0)), 0.0)
        eg = jnp.exp(gcol)
        g_last = gc_all[c - 1:c, A_OFF + h:A_OFF + h + 1]
        decay.append(jnp.exp(g_last))
        kb = k * beta
        k_ref[h] = k.astype(bf16)
        kb_ref[h] = kb.astype(bf16)
        q_ref[h] = q.astype(bf16)
        qg_ref[h] = (q * eg).astype(bf16)
        kg_ref[h] = (k * jnp.exp(g_last - gcol)).astype(bf16)
        rhs_ref[h, :, 0:HEAD_DIM] = (v * beta).astype(bf16)
        rhs_ref[h, :, HEAD_DIM:2 * HEAD_DIM] = (kb * eg).astype(bf16)

    for h in heads:
        gam = gam_ref[h]
        nmat = jnp.where(strict, _dot_nt(kb_ref[h], k_ref[h]) * gam, 0.0)
        n_ref[h] = nmat
        nd = jnp.where(same_blk[INV_BASE], nmat, 0.0)
        t_ref[h] = eye - nd
        pw_ref[h] = nd.astype(bf16)
        qk_ref[h] = (_dot_nt(q_ref[h], k_ref[h]) * gam).astype(bf16)

    for _ in range(int(math.log2(INV_BASE)) - 1):
        for h in heads:
            pw = pw_ref[h]
            pw_ref[h] = _dot(pw, pw).astype(bf16)
        for h in heads:
            t = t_ref[h]
            t_ref[h] = t + _dot(t.astype(bf16), pw_ref[h])

    s = INV_BASE if tv > INV_BASE else c
    while s < c:
        pair = same_blk[2 * s] & jnp.logical_not(same_blk[s])
        for h in heads:
            off = jnp.where(pair, n_ref[h], 0.0)
            uw_ref[h, :, 0:HEAD_DIM] = _bdot(t_ref[h], off)
        for h in heads:
            t = t_ref[h]
            t_ref[h] = t - _bdot(uw_ref[h, :, 0:HEAD_DIM], t)
        s *= 2

    for h in heads:
        uw_ref[h] = _dot(t_ref[h].astype(bf16), rhs_ref[h])

    for h in heads:
        sb = st_ref[h].astype(bf16)
        v_new = uw_ref[h, :, 0:HEAD_DIM] - _dot(uw_ref[h, :, HEAD_DIM:2 * HEAD_DIM].astype(bf16), sb)
        vn_ref[h] = v_new.astype(bf16)
        uw_ref[h, :, 0:HEAD_DIM] = _dot(qg_ref[h], sb)
    for h in heads:
        vn = vn_ref[h]
        o = uw_ref[h, :, 0:HEAD_DIM] + _dot(qk_ref[h], vn)
        st_ref[h] = st_ref[h] * decay[h] + _dot_tn(kg_ref[h], vn)
        on = o * lax.rsqrt(jnp.mean(o * o, axis=-1, keepdims=True) + NORM_EPS) * nw_ref[...]
        o_ref[:, h * HEAD_DIM:(h + 1) * HEAD_DIM] = on[:tv]

    @pl.when(ci == nci - 1)
    def _():
        s_out_ref[0] = st_ref[...]


def _gdn(qkv_b, small, conv0, s0, conv_w, alog_row, dtb_row, gdn_norm_w, nbatch, t, c):
    tv = min(c, t)
    nci = t // tv
    halo = 8
    return pl.pallas_call(
        functools.partial(_gdn_kernel, c=c, tv=tv, halo=halo),
        grid=(nbatch, nci),
        in_specs=[
            pl.BlockSpec((tv, CONV_DIM), lambda b, i: (b * nci + i, 0)),
            pl.BlockSpec((tv, LANES), lambda b, i: (b * nci + i, 0)),
            pl.BlockSpec((1, CONV_W - 1, CONV_DIM), lambda b, i: (b, 0, 0)),
            pl.BlockSpec((1, N_HEADS_B, HEAD_DIM, HEAD_DIM), lambda b, i: (b, 0, 0, 0)),
            pl.BlockSpec((CONV_W, CONV_DIM), lambda b, i: (0, 0)),
            pl.BlockSpec((1, LANES), lambda b, i: (0, 0)),
            pl.BlockSpec((1, LANES), lambda b, i: (0, 0)),
            pl.BlockSpec((1, HEAD_DIM), lambda b, i: (0, 0)),
        ],
        out_specs=[
            pl.BlockSpec((tv, D_GDN), lambda b, i: (b * nci + i, 0)),
            pl.BlockSpec((1, N_HEADS_B, HEAD_DIM, HEAD_DIM), lambda b, i: (b, 0, 0, 0)),
        ],
        out_shape=[jax.ShapeDtypeStruct((nbatch * t, D_GDN), f32),
                   jax.ShapeDtypeStruct((nbatch, N_HEADS_B, HEAD_DIM, HEAD_DIM), f32)],
        scratch_shapes=[pltpu.VMEM((halo + c, CONV_DIM), f32),
                        pltpu.VMEM((N_HEADS_B, HEAD_DIM, HEAD_DIM), f32)]
        + [pltpu.VMEM((N_HEADS_B, c, c), f32)] * 3
        + [pltpu.VMEM((N_HEADS_B, c, c), bf16)]
        + [pltpu.VMEM((N_HEADS_B, c, 2 * HEAD_DIM), f32)]
        + [pltpu.VMEM((N_HEADS_B, c, HEAD_DIM), bf16)] * 5
        + [pltpu.VMEM((N_HEADS_B, c, c), bf16)]
        + [pltpu.VMEM((N_HEADS_B, c, HEAD_DIM), bf16)]
        + [pltpu.VMEM((N_HEADS_B, c, 2 * HEAD_DIM), bf16)],
        compiler_params=_cparams(("parallel", "arbitrary")),
        name="gdn",
    )(qkv_b, small, conv0, s0, conv_w, alog_row, dtb_row, gdn_norm_w.reshape(1, HEAD_DIM))


MERGE_ROWS = 256


def _merge_kernel(x_ref, oc_ref, os_ref, ow_ref, small_ref, za_ref, zb_ref, ob_ref, w_ref, fw_ref, y_ref, mix_ref):
    tm = x_ref.shape[0]
    sub = min(MERGE_ROWS, tm)
    for r in range(tm // sub):
        rs = slice(r * sub, (r + 1) * sub)
        gates = _sigmoid(small_ref[rs, :])
        for h in range(N_HEADS_A):
            cs = slice(h * HEAD_DIM, (h + 1) * HEAD_DIM)
            o_a = (gates[:, G_OFF + h:G_OFF + h + 1] * oc_ref[rs, cs]
                   + gates[:, G_OFF + N_HEADS_A + h:G_OFF + N_HEADS_A + h + 1] * os_ref[rs, cs]
                   + gates[:, G_OFF + 2 * N_HEADS_A + h:G_OFF + 2 * N_HEADS_A + h + 1] * ow_ref[rs, cs])
            mix_ref[rs, cs] = (o_a * _silu(za_ref[rs, cs])).astype(bf16)
        mix_ref[rs, D_ATT:] = (ob_ref[rs, :] * _silu(zb_ref[rs, :])).astype(bf16)
        hres = x_ref[rs, :] + _dot(mix_ref[rs, :], w_ref[...])
        ms = jnp.mean(hres * hres, axis=-1, keepdims=True)
        y_ref[rs, :] = hres * lax.rsqrt(ms + NORM_EPS) * fw_ref[...]


def _merge_out(x2d, o_cmp, o_slc, o_swa, small, qz, o_b, w_out_bf, final_norm_w, tm):
    m = x2d.shape[0]
    row = lambda i: (i, 0)
    return pl.pallas_call(
        _merge_kernel,
        grid=(m // tm,),
        in_specs=[
            pl.BlockSpec((tm, D_MODEL), row),
            pl.BlockSpec((tm, D_ATT), row),
            pl.BlockSpec((tm, D_ATT), row),
            pl.BlockSpec((tm, D_ATT), row),
            pl.BlockSpec((tm, LANES), row),
            pl.BlockSpec((tm, D_ATT), lambda i: (i, 1)),
            pl.BlockSpec((tm, D_GDN), lambda i: (i, 2)),
            pl.BlockSpec((tm, D_GDN), row),
            pl.BlockSpec((D_ATT + D_GDN, D_MODEL), lambda i: (0, 0), pipeline_mode=pl.Buffered(1)),
            pl.BlockSpec((1, D_MODEL), lambda i: (0, 0)),
        ],
        out_specs=pl.BlockSpec((tm, D_MODEL), row),
        out_shape=jax.ShapeDtypeStruct((m, D_MODEL), f32),
        scratch_shapes=[pltpu.VMEM((tm, D_ATT + D_GDN), bf16)],
        compiler_params=_cparams(("parallel",)),
        name="merge_out",
    )(x2d, o_cmp, o_slc, o_swa, small, qz, qz, o_b, w_out_bf, final_norm_w.reshape(1, D_MODEL))


BLOCK_ROWS4 = CMP_BLOCK * N_CH


def _page_copies(cache_ref, buf_ref, sem_ref, pt_ref, b, j, slot, i, pps, by_row):
    page = pt_ref[b, j * pps + i]
    if not by_row:
        return [pltpu.make_async_copy(
            cache_ref.at[pl.ds(page * ROWS_PER_PAGE4, ROWS_PER_PAGE4), :],
            buf_ref.at[slot, pl.ds(i * ROWS_PER_PAGE4, ROWS_PER_PAGE4), :],
            sem_ref.at[slot])]
    bpp = PAGE_SIZE // CMP_BLOCK
    return [pltpu.make_async_copy(
        cache_ref.at[pl.ds(page * ROWS_PER_PAGE4 + n * BLOCK_ROWS4, BLOCK_ROWS4), :],
        buf_ref.at[slot, :, i * bpp + n, :],
        sem_ref.at[slot]) for n in range(bpp)]


def _paged_fetch(cache_ref, buf_ref, sem_ref, pt_ref, pps, nj_pages, need=None, by_row=False):
    b = pl.program_id(0)
    j = pl.program_id(1)
    nb = pl.num_programs(0)
    nj = pl.num_programs(1)
    step = b * nj_pages + jnp.minimum(j, nj_pages - 1)
    slot = step % 2

    def guarded(bb, jj, i, fn):
        if need is None:
            fn()
        else:
            pl.when(need(bb, jj, i))(fn)

    def run(what, bb, jj, sl, i):
        copies = _page_copies(cache_ref, buf_ref, sem_ref, pt_ref, bb, jj, sl, i, pps, by_row)
        for n, cp in enumerate(copies):
            if what == "start":
                cp.start(priority=(i * len(copies) + n) % 2)
            else:
                cp.wait()

    def start(bb, jj, sl):
        for i in range(pps):
            guarded(bb, jj, i, functools.partial(run, "start", bb, jj, sl, i))

    @pl.when((b == 0) & (j == 0))
    def _():
        start(0, 0, 0)

    @pl.when(j < nj_pages)
    def _():
        last_j = j == nj_pages - 1
        nb_ = jnp.where(last_j, b + 1, b)
        nj_ = jnp.where(last_j, 0, j + 1)

        @pl.when(nb_ < nb)
        def _():
            start(nb_, nj_, 1 - slot)

        for i in range(pps):
            guarded(b, j, i, functools.partial(run, "wait", b, j, slot, i))

    return slot


def _compress_paged_kernel(pt_ref, cache_ref, pe_ref, w_ref, o_ref, buf_ref, sem_ref, *, pps):
    slot = _paged_fetch(cache_ref, buf_ref, sem_ref, pt_ref, pps, pl.num_programs(1), by_row=True)
    nbk = pps * (PAGE_SIZE // CMP_BLOCK)
    accs = _compress_accumulate(lambda l, ch: buf_ref[slot, l * N_CH + ch], pe_ref, w_ref, nbk)
    for c in range(2):
        for h in range(N_KV_A):
            o_ref[0, c * N_KV_A + h] = accs[c][h * nbk:(h + 1) * nbk]


def _compress_paged(page_table, cache4, pe2d, w_cmp_bf, pps):
    nbatch, n_pages = page_table.shape
    nj = n_pages // pps
    nbk = pps * (PAGE_SIZE // CMP_BLOCK)
    gs = pltpu.PrefetchScalarGridSpec(
        num_scalar_prefetch=1,
        grid=(nbatch, nj),
        in_specs=[
            pl.BlockSpec(memory_space=pl.ANY),
            pl.BlockSpec((2, CMP_BLOCK, HEAD_DIM), lambda b, j, pt: (0, 0, 0)),
            pl.BlockSpec((2, CMP_BLOCK // 2, 2 * HEAD_DIM, HEAD_DIM), lambda b, j, pt: (0, 0, 0, 0)),
        ],
        out_specs=pl.BlockSpec((1, N_CH, nbk, HEAD_DIM), lambda b, j, pt: (b, 0, j, 0)),
        scratch_shapes=[pltpu.VMEM((2, BLOCK_ROWS4, nbk, HEAD_DIM), f32), pltpu.SemaphoreType.DMA((2,))],
    )
    return pl.pallas_call(
        functools.partial(_compress_paged_kernel, pps=pps),
        grid_spec=gs,
        out_shape=jax.ShapeDtypeStruct((nbatch, N_CH, nj * nbk, HEAD_DIM), f32),
        compiler_params=_cparams(("arbitrary", "arbitrary")),
        name="compress_paged",
    )(page_table, cache4, pe2d, w_cmp_bf)


def _two_phase_attend(qs, slope2, qpos, kpos0, load_k, load_v, load_mask, nchunks, chunk,
                      s_ref, rows, m_ref, l_ref, acc_ref, active=None):
    nrow = qs.shape[0]
    nrep = chunk // LANES
    kofs = lax.broadcasted_iota(jnp.int32, (1, chunk), 1)
    unroll = math.gcd(nchunks, 4)

    def cols(ci):
        return pl.ds(pl.multiple_of(ci * chunk, chunk), chunk)

    def skippable(ci, fn, carry):
        if active is None:
            return fn(carry)
        return lax.cond(active(ci), fn, lambda c: c, carry)

    def phase1(ci, mrun):
        def update(mrun):
            d = qpos - (kpos0 + ci * chunk + kofs)
            dmask = jnp.where((load_mask(ci) > 0.5) & (d >= 0), d.astype(f32), MASK_DIST)
            t2 = _dot_nt(qs, load_k(ci)) * (SCALE * LOG2E) - slope2 * dmask
            s_ref[rows, cols(ci)] = t2
            for c in range(nrep):
                mrun = jnp.maximum(mrun, t2[:, c * LANES:(c + 1) * LANES])
            return mrun

        return skippable(ci, update, mrun)

    mrun = lax.fori_loop(0, nchunks, phase1, jnp.full((nrow, LANES), NEG_INF, f32), unroll=unroll)
    m_prev = m_ref[rows, :]
    m_new = jnp.maximum(m_prev, jnp.broadcast_to(jnp.max(mrun, axis=-1, keepdims=True), (nrow, LANES)))

    def phase2(ci, carry):
        def update(carry):
            lp, acc = carry
            p = jnp.exp2(s_ref[rows, cols(ci)] - _lane_repeat(m_new, nrep))
            for c in range(nrep):
                lp = lp + p[:, c * LANES:(c + 1) * LANES]
            return lp, acc + _dot(p.astype(bf16), load_v(ci))

        return skippable(ci, update, carry)

    z = jnp.zeros((nrow, LANES), f32)
    lp, acc = lax.fori_loop(0, nchunks, phase2, (z, z), unroll=unroll)
    alpha = jnp.exp2(m_prev - m_new)
    l_ref[rows, :] = alpha * l_ref[rows, :] + lp
    acc_ref[rows, :] = alpha * acc_ref[rows, :] + acc
    m_ref[rows, :] = m_new


def _slc_paged_kernel(pt_ref, chunk_any_ref, chunk_kvh_ref, slopes_ref, cache_ref, q_ref, tail_ref, sel_ref,
                      e_ref, o_ref, buf_ref, sem_ref, s_ref, m_ref, l_ref, acc_ref, *, pps, t, sub, pos0):
    b = pl.program_id(0)
    j = pl.program_id(1)
    nj_pages = pl.num_programs(1) - 1
    ppc = sub // PAGE_SIZE
    cps = pps // ppc
    slot = _paged_fetch(cache_ref, buf_ref, sem_ref, pt_ref, pps, nj_pages,
                        need=lambda bb, jj, i: chunk_any_ref[bb, jj * cps + i // ppc] > 0)
    nrow = GQA * t

    @pl.when(j == 0)
    def _():
        _flash_init(m_ref, l_ref, acc_ref)

    tpos = lax.broadcasted_iota(jnp.int32, (t, 1), 0)
    qpos = pos0 + jnp.concatenate([tpos] * GQA, axis=0)

    def attend(kvh, load_k, load_v, nchunks, chunk, kpos0, active=None):
        q = q_ref[:, kvh * GQA * HEAD_DIM:(kvh + 1) * GQA * HEAD_DIM]
        qs = jnp.concatenate([q[:, g * HEAD_DIM:(g + 1) * HEAD_DIM] for g in range(GQA)], axis=0).astype(bf16)
        slope2 = jnp.concatenate([jnp.full((t, 1), slopes_ref[kvh, g] * LOG2E, f32) for g in range(GQA)], axis=0)
        selb = sel_ref[0, kvh].astype(bf16)

        def load_mask(ci):
            maskf = _dot(selb, e_ref[:, pl.ds(pl.multiple_of(ci * chunk, chunk), chunk)])
            return jnp.concatenate([maskf] * GQA, axis=0)

        _two_phase_attend(qs, slope2, qpos, kpos0, load_k, load_v, load_mask, nchunks, chunk,
                          s_ref, pl.ds(kvh * nrow, nrow), m_ref, l_ref, acc_ref, active)

    @pl.when(j < nj_pages)
    def _():
        for kvh in range(N_KV_A):
            def load(ci, ch):
                return buf_ref[slot, pl.ds(ci * (sub * N_CH) + ch, sub, stride=N_CH), :].astype(bf16)

            first = ((b * N_KV_A + kvh) * nj_pages + j) * cps
            attend(kvh, functools.partial(load, ch=kvh), functools.partial(load, ch=N_KV_A + kvh),
                   cps, sub, j * pps * PAGE_SIZE, lambda ci, first=first: chunk_kvh_ref[first + ci] > 0)

    @pl.when(j == nj_pages)
    def _():
        ntail = tail_ref.shape[1]
        for kvh in range(N_KV_A):
            k = tail_ref[0, :, kvh * HEAD_DIM:(kvh + 1) * HEAD_DIM].astype(bf16)
            v = tail_ref[0, :, (N_KV_A + kvh) * HEAD_DIM:(N_KV_A + kvh + 1) * HEAD_DIM].astype(bf16)
            attend(kvh, lambda ci, k=k: k, lambda ci, v=v: v, 1, ntail, nj_pages * pps * PAGE_SIZE)
        for kvh in range(N_KV_A):
            for g in range(GQA):
                r = pl.ds(kvh * nrow + g * t, t)
                o_ref[:, (kvh * GQA + g) * HEAD_DIM:(kvh * GQA + g + 1) * HEAD_DIM] = (
                    acc_ref[r, :] / jnp.sum(l_ref[r, :], axis=-1, keepdims=True))


def _slc_paged(page_table, slopes, cache4, qz, tail, sel, expand, t, pps, pos0):
    nbatch, n_pages = page_table.shape
    njp = n_pages // pps
    bps = pps * (PAGE_SIZE // CMP_BLOCK)
    sub = min(512, pps * PAGE_SIZE)
    bpc = sub // CMP_BLOCK
    n_pb = n_pages * (PAGE_SIZE // CMP_BLOCK)
    chunk_sel = sel[..., :n_pb].reshape(nbatch, N_KV_A, t, n_pb // bpc, bpc).max(axis=(2, 4)) > 0.5
    chunk_kvh = chunk_sel.astype(jnp.int32).reshape(-1)
    chunk_any = chunk_sel.any(axis=1).astype(jnp.int32)
    gs = pltpu.PrefetchScalarGridSpec(
        num_scalar_prefetch=3,
        grid=(nbatch, njp + 1),
        in_specs=[
            pl.BlockSpec(memory_space=pltpu.SMEM),
            pl.BlockSpec(memory_space=pl.ANY),
            pl.BlockSpec((t, D_ATT), lambda b, j, *_: (b, 0)),
            pl.BlockSpec((1, LANES, KV_BRANCH), lambda b, j, *_: (b, 0, 0)),
            pl.BlockSpec((1, N_KV_A, t, bps), lambda b, j, *_: (b, 0, 0, j)),
            pl.BlockSpec((bps, pps * PAGE_SIZE), lambda b, j, *_: (0, 0)),
        ],
        out_specs=pl.BlockSpec((t, D_ATT), lambda b, j, *_: (b, 0)),
        scratch_shapes=[pltpu.VMEM((2, pps * ROWS_PER_PAGE4, HEAD_DIM), f32), pltpu.SemaphoreType.DMA((2,)),
                        pltpu.VMEM((N_KV_A * GQA * t, pps * PAGE_SIZE), f32),
                        pltpu.VMEM((N_KV_A * GQA * t, LANES), f32), pltpu.VMEM((N_KV_A * GQA * t, LANES), f32),
                        pltpu.VMEM((N_KV_A * GQA * t, HEAD_DIM), f32)],
    )
    return pl.pallas_call(
        functools.partial(_slc_paged_kernel, pps=pps, t=t, sub=sub, pos0=pos0),
        grid_spec=gs,
        out_shape=jax.ShapeDtypeStruct((nbatch * t, D_ATT), f32),
        compiler_params=_cparams(("arbitrary", "arbitrary")),
        name="slc_paged",
    )(page_table, chunk_any, chunk_kvh, slopes, cache4, qz, tail, sel, expand)


def _swa_sample_kernel(slopes_ref, q_ref, win_ref, tail_ref, o_ref, *, t, wbuf, pos0):
    tpos = lax.broadcasted_iota(jnp.int32, (t, 1), 0)
    qpos = pos0 + jnp.concatenate([tpos] * GQA, axis=0)
    nk = wbuf + CMP_BLOCK
    kpos = pos0 - wbuf + lax.broadcasted_iota(jnp.int32, (1, nk), 1)
    d = qpos - kpos
    ok = (d >= 0) & (d < WINDOW)
    df = d.astype(f32)
    for kvh in range(N_KV_A):
        k = jnp.concatenate([win_ref[pl.ds(kvh, wbuf, stride=N_CH), :].astype(bf16),
                             tail_ref[0, :, kvh * HEAD_DIM:(kvh + 1) * HEAD_DIM]], axis=0)
        v = jnp.concatenate([win_ref[pl.ds(N_KV_A + kvh, wbuf, stride=N_CH), :].astype(bf16),
                             tail_ref[0, :, (N_KV_A + kvh) * HEAD_DIM:(N_KV_A + kvh + 1) * HEAD_DIM]], axis=0)
        q = q_ref[:, kvh * GQA * HEAD_DIM:(kvh + 1) * GQA * HEAD_DIM]
        qs = jnp.concatenate([q[:, g * HEAD_DIM:(g + 1) * HEAD_DIM] for g in range(GQA)], axis=0).astype(bf16)
        slope = jnp.concatenate([jnp.full((t, 1), slopes_ref[kvh, g], f32) for g in range(GQA)], axis=0)
        s = _dot_nt(qs, k) * SCALE - slope * df
        s = jnp.where(ok, s, NEG_INF)
        e = jnp.exp(s - jnp.max(s, axis=-1, keepdims=True))
        p = jnp.where(ok, e / jnp.sum(e, axis=-1, keepdims=True), 0.0)
        o = _dot(p.astype(bf16), v)
        for g in range(GQA):
            o_ref[:, (kvh * GQA + g) * HEAD_DIM:(kvh * GQA + g + 1) * HEAD_DIM] = o[g * t:(g + 1) * t]


def _swa_sample(slopes, qz, win4, tail, nbatch, t, wbuf, pos0):
    return pl.pallas_call(
        functools.partial(_swa_sample_kernel, t=t, wbuf=wbuf, pos0=pos0),
        grid=(nbatch,),
        in_specs=[
            pl.BlockSpec(memory_space=pltpu.SMEM),
            pl.BlockSpec((t, D_ATT), lambda b: (b, 0)),
            pl.BlockSpec((wbuf * N_CH, HEAD_DIM), lambda b: (b, 0)),
            pl.BlockSpec((1, CMP_BLOCK, KV_BRANCH), lambda b: (b, 0, 0)),
        ],
        out_specs=pl.BlockSpec((t, D_ATT), lambda b: (b, 0)),
        out_shape=jax.ShapeDtypeStruct((nbatch * t, D_ATT), f32),
        compiler_params=_cparams(("parallel",)),
        name="swa_sample",
    )(slopes, qz, win4, tail)


def _round_up(a, b):
    return -(-a // b) * b


def _pad_blocks(kcv, ncp):
    return jnp.pad(kcv, ((0, 0), (0, 0), (0, ncp - kcv.shape[2]), (0, 0)))


def _expand_matrix(nblocks, nkeys):
    return (jnp.arange(nkeys)[None, :] // CMP_BLOCK == jnp.arange(nblocks)[:, None]).astype(bf16)


def _kv6(rows2d, nbatch, t):
    return rows2d.reshape(nbatch, t, 2, N_KV_A, HEAD_DIM)


def _layer_prompt(x, prm):
    nbatch, t, _ = x.shape
    m = nbatch * t
    x2d = x.reshape(m, D_MODEL)
    tm = min(1024, m)
    qz, qkvb, kv_slc, kv_swa, cmp4, slc4, swa4, small = _in_proj(
        x2d, prm["norm_w"], prm["w_main"], prm["w_small"], tm)
    nc = t // CMP_BLOCK
    kcv = _compress(cmp4, nbatch, prm["pe2d"], prm["w_cmp"], min(64, nc))
    kcv = _pad_blocks(kcv, _round_up(nc, LANES))
    tq = min(256, t)
    o_cmp, sel = _cmp_attend(prm["slopes"], qz, kcv, nbatch, t, tq, 0, min(N_SELECT, nc))
    tk = min(256, t)
    o_slc = _slc_prompt(prm["slopes"], qz, kv_slc, sel, _expand_matrix(sel.shape[-1], t), nbatch, t, tq, tk)
    o_swa = _swa_prompt(prm["slopes"], qz, kv_swa, nbatch, t, tq, tk)
    conv0 = jnp.zeros((nbatch, CONV_W - 1, CONV_DIM), f32)
    s0 = jnp.zeros((nbatch, N_HEADS_B, HEAD_DIM, HEAD_DIM), f32)
    o_b, s_new = _gdn(qkvb, small, conv0, s0, prm["conv_w"], prm["alog_row"], prm["dtb_row"],
                      prm["gdn_norm_w"], nbatch, t, 128)
    y = _merge_out(x2d, o_cmp, o_slc, o_swa, small, qz, o_b, prm["w_out"], prm["final_norm_w"], min(512, m))
    w = min(WINDOW, t)
    conv_new = qkvb.reshape(nbatch, t, CONV_DIM)[:, t - (CONV_W - 1):]
    return (y.reshape(nbatch, t, D_MODEL), _kv6(cmp4, nbatch, t), _kv6(slc4, nbatch, t),
            _kv6(swa4, nbatch, t)[:, t - w:], conv_new, s_new)


def _layer_sample(x, cache_cmp, cache_slc, cache_swa, conv_buf, s0, page_table, prm):
    nbatch, t, _ = x.shape
    m = nbatch * t
    n_pages = page_table.shape[1]
    past_len = n_pages * PAGE_SIZE
    wbuf = cache_swa.shape[1]
    x2d = x.reshape(m, D_MODEL)
    qz, qkvb, kv_slc, kv_swa, cmp4, slc4, swa4, small = _in_proj(
        x2d, prm["norm_w"], prm["w_main"], prm["w_small"], m)
    pad_tail = lambda r: jnp.pad(r.reshape(nbatch, t, KV_BRANCH), ((0, 0), (0, CMP_BLOCK - t), (0, 0)))
    pps = min(32, n_pages)
    kc_past = _compress_paged(page_table, cache_cmp.reshape(-1, HEAD_DIM), prm["pe2d"], prm["w_cmp"], pps)
    kc_tail = _compress(pad_tail(cmp4).reshape(nbatch * CMP_BLOCK * N_CH, HEAD_DIM), 1, prm["pe2d"], prm["w_cmp"], nbatch)
    kc_tail = jnp.swapaxes(kc_tail[0], 0, 1)[:, :, None, :]
    n_pb = past_len // CMP_BLOCK
    pps_s = min(64, n_pages)
    bps = pps_s * (PAGE_SIZE // CMP_BLOCK)
    ncp = _round_up(n_pb + 1, max(LANES, bps))
    kcv = _pad_blocks(jnp.concatenate([kc_past, kc_tail], axis=2), ncp)
    o_cmp, sel = _cmp_attend(prm["slopes"], qz, kcv, nbatch, t, t, past_len, min(N_SELECT, n_pb + 1))
    o_slc = _slc_paged(page_table, prm["slopes"], cache_slc.reshape(-1, HEAD_DIM), qz,
                       jnp.pad(kv_slc.reshape(nbatch, t, KV_BRANCH), ((0, 0), (0, LANES - t), (0, 0))), sel,
                       _expand_matrix(bps, pps_s * PAGE_SIZE), t, pps_s, past_len)
    o_swa = _swa_sample(prm["slopes"], qz, cache_swa.reshape(-1, HEAD_DIM), pad_tail(kv_swa), nbatch, t, wbuf, past_len)
    o_b, s_new = _gdn(qkvb, small, conv_buf, s0, prm["conv_w"], prm["alog_row"], prm["dtb_row"],
                      prm["gdn_norm_w"], nbatch, t, 128)
    y = _merge_out(x2d, o_cmp, o_slc, o_swa, small, qz, o_b, prm["w_out"], prm["final_norm_w"], min(512, m))
    kv_win = jnp.concatenate([cache_swa, _kv6(swa4, nbatch, t)], axis=1)[:, t:]
    conv_new = jnp.concatenate([conv_buf, qkvb.reshape(nbatch, t, CONV_DIM)], axis=1)[:, t:]
    return (y.reshape(nbatch, t, D_MODEL), _kv6(cmp4, nbatch, t), _kv6(slc4, nbatch, t), kv_win, conv_new, s_new)


def kernel(x_prompt, x_sample, cache_cmp, cache_slc, cache_swa, state_conv, state_gdn, page_table,
           norm_w, w_in, pe_cmp, w_cmp, conv_w, a_log, dt_bias, gdn_norm_w, w_out, final_norm_w):
    depth = norm_w.shape[0]
    assert depth == 1, "the final norm is fused into the single layer's output projection"
    l = 0
    w_main, w_small = _prep_w_in(w_in[l])
    head = jnp.arange(1, N_HEADS_A + 1, dtype=f32)
    lane_row = lambda vals: jnp.zeros((1, LANES), f32).at[0, A_OFF:A_OFF + N_HEADS_B].set(vals.astype(f32))
    prm = {
        "norm_w": norm_w[l], "w_main": w_main, "w_small": w_small,
        "pe2d": jnp.swapaxes(pe_cmp[l], 0, 1),
        "w_cmp": jnp.swapaxes(w_cmp[l], 0, 1).reshape(2, CMP_BLOCK // 2, 2 * HEAD_DIM, HEAD_DIM).astype(bf16),
        "conv_w": conv_w[l], "alog_row": lane_row(a_log[l]), "dtb_row": lane_row(dt_bias[l]),
        "gdn_norm_w": gdn_norm_w[l], "w_out": w_out[l].astype(bf16), "final_norm_w": final_norm_w,
        "slopes": jnp.exp2(-8.0 * head / N_HEADS_A).reshape(N_KV_A, GQA),
    }
    yp, p_cmp, p_slc, p_swa, p_conv, p_gdn = _layer_prompt(x_prompt, prm)
    ys, s_cmp, s_slc, s_swa, s_conv, s_gdn = _layer_sample(
        x_sample, cache_cmp[l], cache_slc[l], cache_swa[l], state_conv[l], state_gdn[l], page_table, prm)
    st = lambda a: a[None]
    return (yp, ys, st(p_cmp), st(p_slc), st(p_swa), st(p_conv), st(p_gdn),
            st(s_cmp), st(s_slc), st(s_swa), st(s_conv), st(s_gdn))
```
